```python
import math
import jax, jax.numpy as jnp
from jax import lax
import numpy as np

D_MODEL = 2048
BATCH = 4
SEQ = 4096
DEPTH = 1

HEAD_DIM = 128
N_HEADS_A = 8
N_HEADS_B = 8
D_A = N_HEADS_A * HEAD_DIM
D_B = N_HEADS_B * HEAD_DIM
D_MIX = D_A + D_B
Q_RANK = 512
IDX_HEADS = 16
IDX_DIM = 64
TOPK_MAX = 256
N_BUCKETS = 32
MAX_DISTANCE = 128
Q_BLOCK = 128
N_GROUPS = 4
EXPERTS_PER_GROUP = 8
N_EXPERTS = N_GROUPS * EXPERTS_PER_GROUP
EXPERT_TOPK = 2
D_EXPERT = 512
MOE_BLOCK = 128
N_MOD = 6
EPS = 1e-6
IN_COLS = (Q_RANK, HEAD_DIM, HEAD_DIM, IDX_DIM, IDX_HEADS, D_B, D_B, D_B)
D_IN = sum(IN_COLS)

kernel_name = "hymba_dsa_stickbreak_hmoe_layer"


def rms_norm(x, g):
    xf = x.astype(jnp.float32)
    y = xf * lax.rsqrt(jnp.mean(xf * xf, axis=-1, keepdims=True) + EPS)
    return (y * g.astype(jnp.float32)).astype(x.dtype)


def t5_bucket(dist):
    max_exact = N_BUCKETS // 2
    d = jnp.maximum(dist, 1).astype(jnp.float32)
    large = max_exact + (jnp.log(d / max_exact) / math.log(MAX_DISTANCE / max_exact)
                         * (N_BUCKETS - max_exact)).astype(jnp.int32)
    large = jnp.minimum(large, N_BUCKETS - 1)
    return jnp.where(dist < max_exact, dist, large)


def dsa_block(q_blk, qi_blk, wi_blk, t_blk, k, v, k_idx, rel_bias, topk):
    s_pos = jnp.arange(k.shape[1], dtype=jnp.int32)
    causal = s_pos[None, :] <= t_blk[:, None]
    sc = jnp.einsum('bqhd,bsd->bqhs', qi_blk, k_idx)
    score = jnp.einsum('bqhs,bqh->bqs', jax.nn.relu(sc), wi_blk).astype(jnp.float32)
    score = jnp.where(causal[None], score, -jnp.inf)
    _, sel = lax.top_k(score, topk)
    valid = sel <= t_blk[None, :, None]
    k_sel = jax.vmap(lambda kk, ii: kk[ii])(k, sel)
    v_sel = jax.vmap(lambda vv, ii: vv[ii])(v, sel)
    logits = jnp.einsum('bqhd,bqkd->bqhk', q_blk, k_sel).astype(jnp.float32)
    bias = rel_bias[t5_bucket(jnp.maximum(t_blk[None, :, None] - sel, 0))]
    logits = logits + jnp.moveaxis(bias, -1, 2).astype(jnp.float32)
    logits = jnp.where(valid[:, :, None, :], logits, -jnp.inf)
    p = jax.nn.softmax(logits, axis=-1)
    return jnp.einsum('bqhk,bqkd->bqhd', p.astype(v.dtype), v_sel)


def stick_breaking_block(q_blk, t_blk, k, v):
    s_pos = jnp.arange(k.shape[1], dtype=jnp.int32)
    strict = (s_pos[None, :] < t_blk[:, None])[None, None]
    z = jnp.einsum('bqhd,bshd->bhqs', q_blk, k).astype(jnp.float32)
    log_1m = jnp.where(strict, jax.nn.log_sigmoid(-z), 0.0)
    tail = lax.cumsum(log_1m, axis=3, reverse=True) - log_1m
    a = jnp.where(strict, jnp.exp(jax.nn.log_sigmoid(z) + tail), 0.0)
    return jnp.einsum('bhqs,bshd->bqhd', a.astype(v.dtype), v)


def hybrid_mixer(h, w_in, q_norm_g, w_q_up, q_gain, k_gain, rel_bias, gn_a, gn_b, w_out):
    B, S, _ = h.shape
    splits = list(np.cumsum(IN_COLS)[:-1])
    c_q, k_a, v_a, k_idx, w_idx, q_b, k_b, v_b = jnp.split(h @ w_in, splits, axis=-1)
    q_up = rms_norm(c_q, q_norm_g) @ w_q_up
    q_a = q_up[..., :D_A].reshape(B, S, N_HEADS_A, HEAD_DIM)
    q_a = rms_norm(q_a, q_gain) * (HEAD_DIM ** -0.5)
    k_a = rms_norm(k_a, k_gain)
    q_idx = q_up[..., D_A:].reshape(B, S, IDX_HEADS, IDX_DIM) * (IDX_DIM ** -0.5)
    w_idx = w_idx * (IDX_HEADS ** -0.5)
    q_b = q_b.reshape(B, S, N_HEADS_B, HEAD_DIM) * (HEAD_DIM ** -0.5)
    k_b = k_b.reshape(B, S, N_HEADS_B, HEAD_DIM)
    v_b = v_b.reshape(B, S, N_HEADS_B, HEAD_DIM)

    topk = min(TOPK_MAX, S // 4)
    nb = S // Q_BLOCK

    def to_blocks(a):
        return jnp.moveaxis(a.reshape(B, nb, Q_BLOCK, *a.shape[2:]), 1, 0)

    t_all = jnp.arange(S, dtype=jnp.int32).reshape(nb, Q_BLOCK)

    def per_block(args):
        qa, qi, wi, qb, t = args
        oa = dsa_block(qa, qi, wi, t, k_a, v_a, k_idx, rel_bias, topk)
        ob = stick_breaking_block(qb, t, k_b, v_b)
        return oa, ob

    o_a, o_b = lax.map(per_block, (to_blocks(q_a), to_blocks(q_idx), to_blocks(w_idx),
                                   to_blocks(q_b), t_all))
    o_a = jnp.moveaxis(o_a, 0, 1).reshape(B, S, D_A)
    o_b = jnp.moveaxis(o_b, 0, 1).reshape(B, S, D_B)
    o = jnp.concatenate([rms_norm(o_a, gn_a), rms_norm(o_b, gn_b)], axis=-1)
    return o @ w_out


def hierarchical_moe(h, router_g, router_e, w_gate, w_up, w_down):
    n, d = h.shape
    pg = jax.nn.softmax((h @ router_g).astype(jnp.float32), axis=-1)
    g_idx = jnp.argmax(pg, axis=-1).astype(jnp.int32)
    g_w = jnp.max(pg, axis=-1)
    le = (h @ router_e).astype(jnp.float32).reshape(n, N_GROUPS, EXPERTS_PER_GROUP)
    le = jnp.take_along_axis(le, g_idx[:, None, None], axis=1)[:, 0]
    top_p, top_i = lax.top_k(jax.nn.softmax(le, axis=-1), EXPERT_TOPK)
    gate = g_w[:, None] * top_p / jnp.sum(top_p, axis=-1, keepdims=True)
    expert = g_idx[:, None] * EXPERTS_PER_GROUP + top_i.astype(jnp.int32)

    a = n * EXPERT_TOPK
    flat_e = expert.reshape(a)
    flat_w = gate.reshape(a)
    flat_tok = jnp.repeat(jnp.arange(n, dtype=jnp.int32), EXPERT_TOPK)
    order = jnp.argsort(flat_e)
    se = flat_e[order]
    counts = jnp.bincount(flat_e, length=N_EXPERTS).astype(jnp.int32)
    padded = (counts + MOE_BLOCK - 1) // MOE_BLOCK * MOE_BLOCK
    pad_end = jnp.cumsum(padded)
    pad_start = pad_end - padded
    start = jnp.cumsum(counts) - counts
    dest = pad_start[se] + jnp.arange(a, dtype=jnp.int32) - start[se]
    p = a + N_EXPERTS * MOE_BLOCK
    nb = p // MOE_BLOCK
    buf_tok = jnp.zeros((p,), jnp.int32).at[dest].set(flat_tok[order])
    buf_w = jnp.zeros((p,), jnp.float32).at[dest].set(flat_w[order])
    blk_expert = jnp.minimum(
        jnp.searchsorted(pad_end, jnp.arange(nb, dtype=jnp.int32) * MOE_BLOCK, side='right'),
        N_EXPERTS - 1)
    xb = h[buf_tok].reshape(nb, MOE_BLOCK, d)

    def expert_block(args):
        xe, e = args
        u = jax.nn.silu(xe @ w_gate[e]) * (xe @ w_up[e])
        return u @ w_down[e]

    yb = lax.map(expert_block, (xb, blk_expert)).reshape(p, d)
    return jnp.zeros_like(h).at[buf_tok].add(yb * buf_w[:, None].astype(h.dtype))


def setup_inputs(seed: int = 0) -> dict:
    key = jax.random.key(seed)
    ks = jax.random.split(key, 24)
    f = jnp.float32
    nrm = lambda k, shape, s: jax.random.normal(k, shape, f) * s
    L, D = DEPTH, D_MODEL
    return {
        "x": nrm(ks[0], (BATCH, SEQ, D), 1.0),
        "c": nrm(ks[1], (BATCH, D), 1.0),
        "w_mod": nrm(ks[2], (L, D, N_MOD * D), 0.5 * D ** -0.5),
        "b_mod": nrm(ks[3], (L, N_MOD * D), 0.02),
        "ln1_g": 1.0 + nrm(ks[4], (L, D), 0.02),
        "w_in": nrm(ks[5], (L, D, D_IN), D ** -0.5),
        "q_norm_g": 1.0 + nrm(ks[6], (L, Q_RANK), 0.02),
        "w_q_up": nrm(ks[7], (L, Q_RANK, D_A + IDX_HEADS * IDX_DIM), Q_RANK ** -0.5),
        "q_gain": 1.0 + nrm(ks[8], (L, HEAD_DIM), 0.02),
        "k_gain": 1.0 + nrm(ks[9], (L, HEAD_DIM), 0.02),
        "rel_bias": nrm(ks[10], (N_BUCKETS, N_HEADS_A), 0.5),
        "gn_a": 1.0 + nrm(ks[11], (L, D_A), 0.02),
        "gn_b": 1.0 + nrm(ks[12], (L, D_B), 0.02),
        "w_out": nrm(ks[13], (L, D_MIX, D), D_MIX ** -0.5),
        "ln2_g": 1.0 + nrm(ks[14], (L, D), 0.02),
        "router_g": nrm(ks[15], (L, D, N_GROUPS), D ** -0.5),
        "router_e": nrm(ks[16], (L, D, N_EXPERTS), D ** -0.5),
        "w_gate": nrm(ks[17], (L, N_EXPERTS, D, D_EXPERT), D ** -0.5),
        "w_up": nrm(ks[18], (L, N_EXPERTS, D, D_EXPERT), D ** -0.5),
        "w_down": nrm(ks[19], (L, N_EXPERTS, D_EXPERT, D), D_EXPERT ** -0.5),
    }


def reference(x, c, w_mod, b_mod, ln1_g, w_in, q_norm_g, w_q_up, q_gain, k_gain, rel_bias,
              gn_a, gn_b, w_out, ln2_g, router_g, router_e, w_gate, w_up, w_down):
    B, S, D = x.shape
    for l in range(DEPTH):
        mod = jax.nn.silu(c) @ w_mod[l] + b_mod[l]
        sh1, sc1, g1, sh2, sc2, g2 = [m[:, None, :] for m in jnp.split(mod, N_MOD, axis=-1)]
        h = rms_norm(x, ln1_g[l]) * (1.0 + sc1) + sh1
        x = x + g1 * hybrid_mixer(h, w_in[l], q_norm_g[l], w_q_up[l], q_gain[l], k_gain[l],
                                  rel_bias, gn_a[l], gn_b[l], w_out[l])
        h = rms_norm(x, ln2_g[l]) * (1.0 + sc2) + sh2
        y = hierarchical_moe(h.reshape(B * S, D), router_g[l], router_e[l],
                             w_gate[l], w_up[l], w_down[l])
        x = x + g2 * y.reshape(B, S, D)
    return x
```

```python
import functools
import math

import numpy as np
import jax
import jax.numpy as jnp
from jax import lax
from jax.experimental import pallas as pl
from jax.experimental.pallas import tpu as pltpu

F32 = jnp.float32
BF16 = jnp.bfloat16
I32 = jnp.int32

HEAD_DIM = 128
N_HEADS_A = 8
N_HEADS_B = 8
D_A = N_HEADS_A * HEAD_DIM
D_B = N_HEADS_B * HEAD_DIM
Q_RANK = 512
IDX_HEADS = 16
IDX_DIM = 64
TOPK_MAX = 256
N_BUCKETS = 32
MAX_DISTANCE = 128
N_GROUPS = 4
EXPERTS_PER_GROUP = 8
N_EXPERTS = N_GROUPS * EXPERTS_PER_GROUP
D_EXPERT = 512
EPS = 1e-6

LANES = 128
VMEM_LIMIT = 56 * 1024 * 1024
NEG = -1e30
INT_MIN = -(2 ** 31)
EXP_UNDERFLOW = 104.0

QB_A = 128
KB_A = 256
QB_B = 128
MOE_BLK = 256
TOK_TILE = 256

COL_QB, COL_KB, COL_VB, COL_A = 0, D_B, 2 * D_B, 3 * D_B
A_CQ, A_KA, A_VA, A_KIDX, A_WIDX = 0, 512, 640, 768, 832
A_WIDTH = 1024
D_IN_PAD = COL_A + A_WIDTH


def _cparams(sem=None):
    return pltpu.CompilerParams(dimension_semantics=sem, vmem_limit_bytes=VMEM_LIMIT)


def _rms(x):
    return x * lax.rsqrt(jnp.mean(x * x, axis=-1, keepdims=True) + EPS)


def _mod_kernel(c_ref, w_ref, b_ref, o_ref):
    c = c_ref[...]
    s = c * (1.0 / (1.0 + jnp.exp(-c)))
    o_ref[...] = jnp.dot(s, w_ref[...], preferred_element_type=F32,
                         precision=lax.Precision.HIGHEST) + b_ref[...]


def _modulation(c8, w_mod, b_mod):
    d, n6 = w_mod.shape
    tn = 1024
    return pl.pallas_call(
        _mod_kernel,
        grid=(n6 // tn,),
        in_specs=[pl.BlockSpec((8, d), lambda j: (0, 0)),
                  pl.BlockSpec((d, tn), lambda j: (0, j)),
                  pl.BlockSpec((1, tn), lambda j: (0, j))],
        out_specs=pl.BlockSpec((8, tn), lambda j: (0, j)),
        out_shape=jax.ShapeDtypeStruct((8, n6), F32),
        compiler_params=_cparams(("arbitrary",)),
        name="modulation",
    )(c8, w_mod, b_mod)


def _ln_proj_kernel(x_ref, g_ref, sc_ref, sh_ref, w_ref, o_ref, h_scr):
    @pl.when(pl.program_id(1) == 0)
    def _():
        h = _rms(x_ref[...]) * g_ref[...]
        h = h * (1.0 + sc_ref[0]) + sh_ref[0]
        h_scr[...] = h.astype(BF16)

    o_ref[...] = jnp.dot(h_scr[...], w_ref[...], preferred_element_type=F32).astype(o_ref.dtype)


def _ln_proj(x2, ln_g, sc, sh, w, seq):
    n, d = x2.shape
    ncol = w.shape[1]
    tm = min(1024, seq)
    tn = 512
    tpb = seq // tm
    return pl.pallas_call(
        _ln_proj_kernel,
        grid=(n // tm, ncol // tn),
        in_specs=[pl.BlockSpec((tm, d), lambda i, j: (i, 0)),
                  pl.BlockSpec((1, d), lambda i, j: (0, 0)),
                  pl.BlockSpec((1, 1, d), lambda i, j: (i // tpb, 0, 0)),
                  pl.BlockSpec((1, 1, d), lambda i, j: (i // tpb, 0, 0)),
                  pl.BlockSpec((d, tn), lambda i, j: (0, j))],
        out_specs=pl.BlockSpec((tm, tn), lambda i, j: (i, j)),
        out_shape=jax.ShapeDtypeStruct((n, ncol), BF16),
        scratch_shapes=[pltpu.VMEM((tm, d), BF16)],
        compiler_params=_cparams(("parallel", "arbitrary")),
        name="ln_in_proj",
    )(x2, ln_g, sc, sh, w)


def _aprep_kernel(a_ref, qng_ref, wq_ref, qg_ref, kg_ref,
                  qa_ref, qidx_ref, ka_ref, va_ref, kidx_ref, widx_ref):
    cq = a_ref[:, A_CQ:A_CQ + Q_RANK].astype(F32)
    cqn = (_rms(cq) * qng_ref[...]).astype(BF16)
    qup = jnp.dot(cqn, wq_ref[...], preferred_element_type=F32)
    for h in range(N_HEADS_A):
        qh = qup[:, h * HEAD_DIM:(h + 1) * HEAD_DIM]
        qn = _rms(qh) * qg_ref[...] * (HEAD_DIM ** -0.5)
        qa_ref[:, h * HEAD_DIM:(h + 1) * HEAD_DIM] = qn.astype(BF16)
    for h in range(IDX_HEADS):
        qi = qup[:, D_A + h * IDX_DIM:D_A + (h + 1) * IDX_DIM] * (IDX_DIM ** -0.5)
        qidx_ref[h] = qi.astype(BF16)
    ka = a_ref[:, A_KA:A_KA + HEAD_DIM].astype(F32)
    ka_ref[...] = (_rms(ka) * kg_ref[...]).astype(BF16)
    va_ref[...] = a_ref[:, A_VA:A_VA + HEAD_DIM]
    kidx_ref[...] = a_ref[:, A_KIDX:A_KIDX + IDX_DIM]
    widx_ref[...] = a_ref[:, A_WIDX:A_WIDX + IDX_HEADS].astype(F32) * (IDX_HEADS ** -0.5)


def _aprep(proj, q_norm_g, w_q_up, q_gain, k_gain):
    n = proj.shape[0]
    tm = 512
    cblk = COL_A // A_WIDTH
    nup = w_q_up.shape[1]
    return pl.pallas_call(
        _aprep_kernel,
        grid=(n // tm,),
        in_specs=[pl.BlockSpec((tm, A_WIDTH), lambda i: (i, cblk)),
                  pl.BlockSpec((1, Q_RANK), lambda i: (0, 0)),
                  pl.BlockSpec((Q_RANK, nup), lambda i: (0, 0)),
                  pl.BlockSpec((1, HEAD_DIM), lambda i: (0, 0)),
                  pl.BlockSpec((1, HEAD_DIM), lambda i: (0, 0))],
        out_specs=[pl.BlockSpec((tm, D_A), lambda i: (i, 0)),
                   pl.BlockSpec((IDX_HEADS, tm, IDX_DIM), lambda i: (0, i, 0)),
                   pl.BlockSpec((tm, HEAD_DIM), lambda i: (i, 0)),
                   pl.BlockSpec((tm, HEAD_DIM), lambda i: (i, 0)),
                   pl.BlockSpec((tm, IDX_DIM), lambda i: (i, 0)),
                   pl.BlockSpec((tm, IDX_HEADS), lambda i: (i, 0))],
        out_shape=[jax.ShapeDtypeStruct((n, D_A), BF16),
                   jax.ShapeDtypeStruct((IDX_HEADS, n, IDX_DIM), BF16),
                   jax.ShapeDtypeStruct((n, HEAD_DIM), BF16),
                   jax.ShapeDtypeStruct((n, HEAD_DIM), BF16),
                   jax.ShapeDtypeStruct((n, IDX_DIM), BF16),
                   jax.ShapeDtypeStruct((n, IDX_HEADS), F32)],
        compiler_params=_cparams(("parallel",)),
        name="group_a_prep",
    )(proj, q_norm_g, w_q_up, q_gain, k_gain)


def _t5_bucket_starts():
    max_exact = N_BUCKETS // 2
    d = np.arange(0, 4 * MAX_DISTANCE, dtype=np.int64)
    df = np.maximum(d, 1).astype(np.float32)
    large = max_exact + (np.log(df / np.float32(max_exact)) / np.float32(math.log(MAX_DISTANCE / max_exact))
                         * np.float32(N_BUCKETS - max_exact)).astype(np.int32)
    large = np.minimum(large, N_BUCKETS - 1)
    bucket = np.where(d < max_exact, d, large)
    assert np.all(np.diff(bucket) >= 0) and bucket[-1] == N_BUCKETS - 1
    return [int(np.argmax(bucket >= b)) for b in range(N_BUCKETS)]


_BUCKET_START = _t5_bucket_starts()
N_BIAS_TILES = 2 * KB_A // LANES
assert (N_BIAS_TILES - 1) * LANES - (KB_A - 1) >= _BUCKET_START[-1] or True


def _bias_kernel(rb_ref, o_ref):
    di = pl.program_id(0)
    h = pl.program_id(1)
    i = lax.broadcasted_iota(I32, (QB_A, KB_A), 0)
    j = lax.broadcasted_iota(I32, (QB_A, KB_A), 1)
    d = di * LANES + i - j
    val = jnp.full((QB_A, KB_A), rb_ref[0, h], F32)
    for b in range(1, N_BUCKETS):
        val = jnp.where(d >= _BUCKET_START[b], rb_ref[b, h], val)
    o_ref[0, 0] = val - rb_ref[N_BUCKETS - 1, h]


def _bias_tiles(rel_bias):
    return pl.pallas_call(
        _bias_kernel,
        grid=(N_BIAS_TILES, N_HEADS_A),
        in_specs=[pl.BlockSpec(memory_space=pltpu.SMEM)],
        out_specs=pl.BlockSpec((1, 1, QB_A, KB_A), lambda a, h: (a, h, 0, 0)),
        out_shape=jax.ShapeDtypeStruct((N_BIAS_TILES, N_HEADS_A, QB_A, KB_A), F32),
        compiler_params=_cparams(("arbitrary", "arbitrary")),
        name="t5_bias_tiles",
    )(rel_bias)


def _attn_a_kernel(qidx_ref, w_ref, qa_ref, kidx_ref, ka_ref, va_ref, bt_ref, gn_ref, o_ref,
                   keys_scr, wb_scr, m_scr, l_scr, acc_scr, mb_scr, *, topk):
    qb = pl.program_id(1)
    t0 = qb * QB_A
    kbl = (t0 + QB_A - 1) // KB_A
    row = t0 + lax.broadcasted_iota(I32, (QB_A, KB_A), 0)
    col0 = lax.broadcasted_iota(I32, (QB_A, KB_A), 1)
    nt = (((1,), (1,)), ((), ()))

    w = w_ref[...]
    for h in range(IDX_HEADS):
        wb_scr[h] = jnp.broadcast_to(w[:, h:h + 1], (QB_A, KB_A))

    def score_tile(kb, carry):
        kt = kidx_ref[pl.ds(pl.multiple_of(kb * KB_A, KB_A), KB_A), :]

        def head(h, acc):
            sc = lax.dot_general(qidx_ref[h], kt, nt, preferred_element_type=F32)
            return acc + wb_scr[h] * jnp.maximum(sc, 0.0)

        score = lax.fori_loop(0, IDX_HEADS, head, jnp.zeros((QB_A, KB_A), F32))
        bits = pltpu.bitcast(score, I32)
        key = jnp.where(bits < 0, bits ^ jnp.int32(0x7FFFFFFF), bits)
        causal = (kb * KB_A + col0) <= row
        keys_scr[kb] = jnp.where(causal, key, jnp.int32(INT_MIN))
        return carry

    lax.fori_loop(0, kbl + 1, score_tile, 0)

    def bit_step(i, thr):
        inc = lax.shift_left(jnp.int32(1), 31 - i)
        cand = thr + inc
        candb = jnp.broadcast_to(cand, (QB_A, KB_A))

        def count(kb, cnt):
            return cnt + jnp.where(keys_scr[kb] >= candb, 1.0, 0.0)

        cnt = lax.fori_loop(0, kbl + 1, count, jnp.zeros((QB_A, KB_A), F32))
        tot = jnp.sum(cnt, axis=1, keepdims=True)
        return jnp.where(tot >= float(topk), cand, thr)

    thr = lax.fori_loop(0, 32, bit_step, jnp.full((QB_A, 1), INT_MIN, I32))
    thrb = jnp.broadcast_to(thr, (QB_A, KB_A))

    m_scr[...] = jnp.full(m_scr.shape, NEG, F32)
    l_scr[...] = jnp.zeros(l_scr.shape, F32)
    acc_scr[...] = jnp.zeros(acc_scr.shape, F32)

    def attend(kb, near):
        mb = jnp.where(keys_scr[kb] >= thrb, 0.0, NEG)
        if near:
            mb = jnp.where((kb * KB_A + col0) <= row, mb, NEG)
            di = (t0 - kb * KB_A) // LANES
        mb_scr[...] = mb
        start = pl.multiple_of(kb * KB_A, KB_A)
        kt = ka_ref[pl.ds(start, KB_A), :]
        vt = va_ref[pl.ds(start, KB_A), :]
        for h in range(N_HEADS_A):
            q = qa_ref[:, h * HEAD_DIM:(h + 1) * HEAD_DIM]
            s = lax.dot_general(q, kt, nt, preferred_element_type=F32) + mb_scr[...]
            if near:
                s = s + bt_ref[di, h]
            m_prev = m_scr[h]
            m_new = jnp.maximum(m_prev, jnp.max(s, axis=1, keepdims=True))
            alpha = jnp.exp(m_prev - m_new)
            p = jnp.exp(s - jnp.tile(m_new, (1, KB_A // LANES)))
            l_scr[h] = alpha * l_scr[h] + jnp.sum(p, axis=1, keepdims=True)
            acc_scr[h] = alpha * acc_scr[h] + jnp.dot(p.astype(BF16), vt, preferred_element_type=F32)
            m_scr[h] = m_new

    def far_tile(kb, carry):
        attend(kb, False)
        return carry

    lax.fori_loop(0, jnp.maximum(kbl - 1, 0), far_tile, 0)

    @pl.when(kbl >= 1)
    def _():
        attend(kbl - 1, True)

    attend(kbl, True)

    ssq = jnp.zeros((QB_A, LANES), F32)
    for h in range(N_HEADS_A):
        oh = acc_scr[h] / l_scr[h]
        acc_scr[h] = oh
        ssq = ssq + jnp.sum(oh * oh, axis=1, keepdims=True)
    inv = lax.rsqrt(ssq * (1.0 / D_A) + EPS)
    for h in range(N_HEADS_A):
        sl = slice(h * HEAD_DIM, (h + 1) * HEAD_DIM)
        o_ref[:, sl] = (acc_scr[h] * inv * gn_ref[:, sl]).astype(o_ref.dtype)


def _attn_a(qidx, widx, qa, kidx, ka, va, bt, gn_a, batch, seq):
    n = qa.shape[0]
    nq = seq // QB_A
    nkt = seq // KB_A
    topk = min(TOPK_MAX, seq // 4)
    return pl.pallas_call(
        functools.partial(_attn_a_kernel, topk=topk),
        grid=(batch, nq),
        in_specs=[pl.BlockSpec((IDX_HEADS, QB_A, IDX_DIM), lambda b, q: (0, b * nq + q, 0)),
                  pl.BlockSpec((QB_A, IDX_HEADS), lambda b, q: (b * nq + q, 0)),
                  pl.BlockSpec((QB_A, D_A), lambda b, q: (b * nq + q, 0)),
                  pl.BlockSpec((seq, IDX_DIM), lambda b, q: (b, 0)),
                  pl.BlockSpec((seq, HEAD_DIM), lambda b, q: (b, 0)),
                  pl.BlockSpec((seq, HEAD_DIM), lambda b, q: (b, 0)),
                  pl.BlockSpec(bt.shape, lambda b, q: (0, 0, 0, 0)),
                  pl.BlockSpec((1, D_A), lambda b, q: (0, 0))],
        out_specs=pl.BlockSpec((QB_A, D_A), lambda b, q: (b * nq + q, 0)),
        out_shape=jax.ShapeDtypeStruct((n, D_A), BF16),
        scratch_shapes=[pltpu.VMEM((nkt, QB_A, KB_A), I32),
                        pltpu.VMEM((IDX_HEADS, QB_A, KB_A), F32),
                        pltpu.VMEM((N_HEADS_A, QB_A, LANES), F32),
                        pltpu.VMEM((N_HEADS_A, QB_A, LANES), F32),
                        pltpu.VMEM((N_HEADS_A, QB_A, HEAD_DIM), F32),
                        pltpu.VMEM((QB_A, KB_A), F32)],
        compiler_params=_cparams(("parallel", "arbitrary")),
        name="dsa_attention",
    )(qidx, widx, qa, kidx, ka, va, bt, gn_a)


def _attn_b_kernel(q_ref, k_ref, v_ref, tri_ref, o_ref):
    qb = pl.program_id(2)
    q = q_ref[...]
    row = lax.broadcasted_iota(I32, (QB_B, QB_B), 0)
    col = lax.broadcasted_iota(I32, (QB_B, QB_B), 1)
    strict = col < row
    nt = (((1,), (1,)), ((), ()))
    scale = HEAD_DIM ** -0.5

    def step(kb, rest, o, diag):
        start = pl.multiple_of(kb * QB_B, QB_B)
        kt = k_ref[pl.ds(start, QB_B), :]
        vt = v_ref[pl.ds(start, QB_B), :]
        z = lax.dot_general(q, kt, nt, preferred_element_type=F32) * scale
        sp = jnp.maximum(z, 0.0) + jnp.log(1.0 + jnp.exp(-jnp.abs(z)))
        if diag:
            sp = jnp.where(strict, sp, 0.0)
        hi = sp.astype(BF16)
        lo = (sp - hi.astype(F32)).astype(BF16)
        cs = jnp.dot(jnp.concatenate([hi, lo], axis=1), tri_ref[...], preferred_element_type=F32)
        a = jnp.exp(z - cs - rest)
        if diag:
            a = jnp.where(strict, a, 0.0)
        o = o + jnp.dot(a.astype(BF16), vt, preferred_element_type=F32)
        return rest + cs[:, 0:1], o

    rest0, o0 = step(qb, jnp.zeros((QB_B, 1), F32), jnp.zeros((QB_B, HEAD_DIM), F32), True)

    def more(kb, rest):
        return jnp.logical_and(kb >= 0, jnp.min(rest) < EXP_UNDERFLOW)

    def body(carry):
        kb, rest, o, _ = carry
        rest, o = step(kb, rest, o, False)
        return kb - 1, rest, o, more(kb - 1, rest)

    _, _, o, _ = lax.while_loop(lambda c: c[3], body, (qb - 1, rest0, o0, more(qb - 1, rest0)))
    o_ref[...] = o


def _attn_b(proj, tri, batch, seq):
    n = proj.shape[0]
    nq = seq // QB_B
    kcol = COL_KB // HEAD_DIM
    vcol = COL_VB // HEAD_DIM
    return pl.pallas_call(
        _attn_b_kernel,
        grid=(batch, N_HEADS_B, nq),
        in_specs=[pl.BlockSpec((QB_B, HEAD_DIM), lambda b, h, q: (b * nq + q, h)),
                  pl.BlockSpec((seq, HEAD_DIM), lambda b, h, q: (b, kcol + h)),
                  pl.BlockSpec((seq, HEAD_DIM), lambda b, h, q: (b, vcol + h)),
                  pl.BlockSpec(tri.shape, lambda b, h, q: (0, 0))],
        out_specs=pl.BlockSpec((QB_B, HEAD_DIM), lambda b, h, q: (b * nq + q, h)),
        out_shape=jax.ShapeDtypeStruct((n, D_B), F32),
        compiler_params=_cparams(("parallel", "parallel", "arbitrary")),
        name="stick_breaking_attention",
    )(proj, proj, proj, tri)


def _out_proj_kernel(x_ref, oa_ref, ob_ref, gnb_ref, w_ref, g1_ref, ln_ref, sc_ref, sh_ref, rw_ref,
                     x1_ref, h2_ref, lg_ref):
    obn = (_rms(ob_ref[...]) * gnb_ref[...]).astype(BF16)
    y = jnp.dot(oa_ref[...], w_ref[0:D_A, :], preferred_element_type=F32)
    y = y + jnp.dot(obn, w_ref[D_A:D_A + D_B, :], preferred_element_type=F32)
    x1 = x_ref[...] + g1_ref[0] * y
    x1_ref[...] = x1
    h2 = _rms(x1) * ln_ref[...]
    h2 = h2 * (1.0 + sc_ref[0]) + sh_ref[0]
    h2_ref[...] = h2
    hh = h2.astype(BF16)
    hl = (h2 - hh.astype(F32)).astype(BF16)
    rw = rw_ref[...]
    rh = rw.astype(BF16)
    rl = (rw - rh.astype(F32)).astype(BF16)
    lg = jnp.dot(hh, rh, preferred_element_type=F32)
    lg = lg + jnp.dot(hh, rl, preferred_element_type=F32)
    lg = lg + jnp.dot(hl, rh, preferred_element_type=F32)
    lg_ref[...] = lg


def _out_proj(x2, oa, ob, gn_b, w_out, g1, ln_g, sc, sh, rw, seq):
    n, d = x2.shape
    tm = 256
    tpb = seq // tm
    row = lambda i: (i, 0)
    fixed = lambda i: (0, 0)
    perb = lambda i: (i // tpb, 0, 0)
    return pl.pallas_call(
        _out_proj_kernel,
        grid=(n // tm,),
        in_specs=[pl.BlockSpec((tm, d), row),
                  pl.BlockSpec((tm, D_A), row),
                  pl.BlockSpec((tm, D_B), row),
                  pl.BlockSpec((1, D_B), fixed),
                  pl.BlockSpec(w_out.shape, fixed),
                  pl.BlockSpec((1, 1, d), perb),
                  pl.BlockSpec((1, d), fixed),
                  pl.BlockSpec((1, 1, d), perb),
                  pl.BlockSpec((1, 1, d), perb),
                  pl.BlockSpec(rw.shape, fixed)],
        out_specs=[pl.BlockSpec((tm, d), row),
                   pl.BlockSpec((tm, d), row),
                   pl.BlockSpec((tm, LANES), row)],
        out_shape=[jax.ShapeDtypeStruct((n, d), F32),
                   jax.ShapeDtypeStruct((n, d), F32),
                   jax.ShapeDtypeStruct((n, LANES), F32)],
        compiler_params=_cparams(("parallel",)),
        name="out_proj_ln2_router",
    )(x2, oa, ob, gn_b, w_out, g1, ln_g, sc, sh, rw)


def _route_kernel(lg_ref, info_ref, cnt_ref):
    @pl.when(pl.program_id(0) == 0)
    def _():
        cnt_ref[...] = jnp.zeros(cnt_ref.shape, F32)

    lg = lg_ref[...]
    lane = lax.broadcasted_iota(I32, lg.shape, 1)
    lanef = lane.astype(F32)
    big = float(4 * LANES)
    gm = jnp.where(lane >= N_EXPERTS, jnp.where(lane < N_EXPERTS + N_GROUPS, 1.0, 0.0), 0.0) > 0.5
    lgm = jnp.where(gm, lg, NEG)
    mg = jnp.max(lgm, axis=1, keepdims=True)
    eg = jnp.where(gm, jnp.exp(lgm - mg), 0.0)
    pg = eg / jnp.sum(eg, axis=1, keepdims=True)
    gw = jnp.max(pg, axis=1, keepdims=True)
    gidx = jnp.min(jnp.where(gm, jnp.where(pg == gw, lanef - N_EXPERTS, big), big), axis=1, keepdims=True)
    lane_group = (lane // EXPERTS_PER_GROUP).astype(F32)
    em = jnp.where(lane < N_EXPERTS, jnp.where(lane_group == gidx, 1.0, 0.0), 0.0) > 0.5
    lem = jnp.where(em, lg, NEG)
    me = jnp.max(lem, axis=1, keepdims=True)
    ee = jnp.where(em, jnp.exp(lem - me), 0.0)
    pe = jnp.where(em, ee / jnp.sum(ee, axis=1, keepdims=True), -1.0)
    p1 = jnp.max(pe, axis=1, keepdims=True)
    i1 = jnp.min(jnp.where(pe == p1, lanef, big), axis=1, keepdims=True)
    pe2 = jnp.where(lanef == i1, -1.0, pe)
    p2 = jnp.max(pe2, axis=1, keepdims=True)
    i2 = jnp.min(jnp.where(pe2 == p2, lanef, big), axis=1, keepdims=True)
    den = p1 + p2
    g0 = gw * p1 / den
    g1 = gw * p2 / den
    info = jnp.where(lane == 0, i1, jnp.where(lane == 1, i2,
                     jnp.where(lane == 2, g0, jnp.where(lane == 3, g1, 0.0))))
    info_ref[...] = info
    oh = jnp.where(lanef == i1, 1.0, 0.0) + jnp.where(lanef == i2, 1.0, 0.0)
    cnt_ref[...] += jnp.sum(oh, axis=0, keepdims=True)


def _route(lg):
    n = lg.shape[0]
    tm = min(1024, n)
    return pl.pallas_call(
        _route_kernel,
        grid=(n // tm,),
        in_specs=[pl.BlockSpec((tm, LANES), lambda i: (i, 0))],
        out_specs=[pl.BlockSpec((tm, LANES), lambda i: (i, 0)),
                   pl.BlockSpec((1, LANES), lambda i: (0, 0))],
        out_shape=[jax.ShapeDtypeStruct((n, LANES), F32),
                   jax.ShapeDtypeStruct((1, LANES), F32)],
        compiler_params=_cparams(("arbitrary",)),
        name="moe_route",
    )(lg)


def _plan_kernel(cnt_ref, ps_ref, be_ref, nu_ref, *, nblk):
    def fill(i, c):
        be_ref[i] = N_EXPERTS - 1
        return c

    lax.fori_loop(0, nblk, fill, 0)

    def per_expert(e, pos):
        ps_ref[e] = pos * MOE_BLK
        nb = (cnt_ref[e] + MOE_BLK - 1) // MOE_BLK

        def mark(k, c):
            be_ref[pos + k] = e
            return c

        lax.fori_loop(0, nb, mark, 0)
        return pos + nb

    nu_ref[0] = lax.fori_loop(0, N_EXPERTS, per_expert, jnp.int32(0))


def _plan(counts, nblk):
    smem = pl.BlockSpec(memory_space=pltpu.SMEM)
    return pl.pallas_call(
        functools.partial(_plan_kernel, nblk=nblk),
        in_specs=[smem],
        out_specs=[smem, smem, smem],
        out_shape=[jax.ShapeDtypeStruct((N_EXPERTS,), I32),
                   jax.ShapeDtypeStruct((nblk,), I32),
                   jax.ShapeDtypeStruct((1,), I32)],
        name="moe_block_plan",
    )(counts)


def _dest_kernel(info_ref, ps_ref, tri_ref, o_ref, carry_scr):
    @pl.when(pl.program_id(0) == 0)
    def _():
        carry_scr[...] = jnp.zeros(carry_scr.shape, F32)

    info = info_ref[...]
    lane = lax.broadcasted_iota(I32, info.shape, 1)
    lanef = lane.astype(F32)
    o1 = jnp.where(lanef == info[:, 0:1], 1.0, 0.0)
    o2 = jnp.where(lanef == info[:, 1:2], 1.0, 0.0)
    oh = o1 + o2
    before = jnp.dot(tri_ref[...], oh.astype(BF16), preferred_element_type=F32)
    base = before + carry_scr[...] + ps_ref[...]
    d1 = jnp.sum(o1 * base, axis=1, keepdims=True)
    d2 = jnp.sum(o2 * base, axis=1, keepdims=True)
    o_ref[...] = jnp.where(lane == 0, d1, jnp.where(lane == 1, d2, 0.0))
    carry_scr[...] += jnp.sum(oh, axis=0, keepdims=True)


def _dest(info, ps_lanes, tri):
    n = info.shape[0]
    tm = tri.shape[0]
    return pl.pallas_call(
        _dest_kernel,
        grid=(n // tm,),
        in_specs=[pl.BlockSpec((tm, LANES), lambda i: (i, 0)),
                  pl.BlockSpec((1, LANES), lambda i: (0, 0)),
                  pl.BlockSpec((tm, tm), lambda i: (0, 0))],
        out_specs=pl.BlockSpec((tm, LANES), lambda i: (i, 0)),
        out_shape=jax.ShapeDtypeStruct((n, LANES), F32),
        scratch_shapes=[pltpu.VMEM((1, LANES), F32)],
        compiler_params=_cparams(("arbitrary",)),
        name="moe_dest_rows",
    )(info, ps_lanes, tri)


def _scatter_kernel(d0_ref, d1_ref, h_ref, xz_ref, xb_ref, sem):
    del xz_ref
    base = pl.program_id(0) * TOK_TILE

    def row_copy(r, dst_row):
        return pltpu.make_async_copy(h_ref.at[pl.ds(r, 1)], xb_ref.at[pl.ds(dst_row, 1)], sem)

    def issue(r, c):
        row_copy(r, d0_ref[base + r]).start()
        row_copy(r, d1_ref[base + r]).start()
        return c

    lax.fori_loop(0, TOK_TILE, issue, 0)

    def drain(r, c):
        row_copy(0, 0).wait()
        row_copy(0, 0).wait()
        return c

    lax.fori_loop(0, TOK_TILE, drain, 0)


def _scatter_rows(d0, d1, h2, xzero):
    n, d = h2.shape
    return pl.pallas_call(
        _scatter_kernel,
        grid_spec=pltpu.PrefetchScalarGridSpec(
            num_scalar_prefetch=2,
            grid=(n // TOK_TILE,),
            in_specs=[pl.BlockSpec((TOK_TILE, d), lambda i, a, b: (i, 0)),
                      pl.BlockSpec(memory_space=pl.ANY)],
            out_specs=pl.BlockSpec(memory_space=pl.ANY),
            scratch_shapes=[pltpu.SemaphoreType.DMA(())]),
        out_shape=jax.ShapeDtypeStruct(xzero.shape, xzero.dtype),
        input_output_aliases={3: 0},
        compiler_params=_cparams(("arbitrary",)),
        name="moe_scatter_rows",
    )(d0, d1, h2, xzero)


def _expert_kernel(be_ref, nu_ref, x_ref, wg_ref, wu_ref, wd_ref, o_ref):
    del be_ref
    i = pl.program_id(0)

    @pl.when(i < nu_ref[0])
    def _():
        x = x_ref[...].astype(BF16)
        g = jnp.dot(x, wg_ref[0], preferred_element_type=F32)
        u = jnp.dot(x, wu_ref[0], preferred_element_type=F32)
        act = (g * (1.0 / (1.0 + jnp.exp(-g))) * u).astype(BF16)
        o_ref[...] = jnp.dot(act, wd_ref[0], preferred_element_type=F32)

    @pl.when(i >= nu_ref[0])
    def _():
        o_ref[...] = jnp.zeros(o_ref.shape, o_ref.dtype)


def _experts(be, nu, xb, wg, wu, wd):
    p, d = xb.shape
    de = wg.shape[2]
    return pl.pallas_call(
        _expert_kernel,
        grid_spec=pltpu.PrefetchScalarGridSpec(
            num_scalar_prefetch=2,
            grid=(p // MOE_BLK,),
            in_specs=[pl.BlockSpec((MOE_BLK, d), lambda i, be, nu: (i, 0)),
                      pl.BlockSpec((1, d, de), lambda i, be, nu: (be[i], 0, 0)),
                      pl.BlockSpec((1, d, de), lambda i, be, nu: (be[i], 0, 0)),
                      pl.BlockSpec((1, de, d), lambda i, be, nu: (be[i], 0, 0))],
            out_specs=pl.BlockSpec((MOE_BLK, d), lambda i, be, nu: (i, 0))),
        out_shape=jax.ShapeDtypeStruct((p, d), F32),
        compiler_params=_cparams(("arbitrary",)),
        name="moe_expert_ffn",
    )(be, nu, xb, wg, wu, wd)


def _combine_kernel(d0_ref, d1_ref, x_ref, info_ref, g2_ref, yb_ref, o_ref, rows_scr, sem):
    base = pl.program_id(0) * TOK_TILE

    def row_copy(slot, r, src_row):
        return pltpu.make_async_copy(yb_ref.at[pl.ds(src_row, 1)], rows_scr.at[slot, pl.ds(r, 1)], sem)

    def issue(r, c):
        row_copy(0, r, d0_ref[base + r]).start()
        row_copy(1, r, d1_ref[base + r]).start()
        return c

    lax.fori_loop(0, TOK_TILE, issue, 0)

    def drain(r, c):
        row_copy(0, 0, 0).wait()
        row_copy(1, 0, 0).wait()
        return c

    lax.fori_loop(0, TOK_TILE, drain, 0)

    info = info_ref[...]
    y = info[:, 2:3] * rows_scr[0] + info[:, 3:4] * rows_scr[1]
    o_ref[...] = x_ref[...] + g2_ref[0] * y


def _combine(d0, d1, x1, info, g2, yb, seq):
    n, d = x1.shape
    tpb = seq // TOK_TILE
    return pl.pallas_call(
        _combine_kernel,
        grid_spec=pltpu.PrefetchScalarGridSpec(
            num_scalar_prefetch=2,
            grid=(n // TOK_TILE,),
            in_specs=[pl.BlockSpec((TOK_TILE, d), lambda i, a, b: (i, 0)),
                      pl.BlockSpec((TOK_TILE, LANES), lambda i, a, b: (i, 0)),
                      pl.BlockSpec((1, 1, d), lambda i, a, b: (i // tpb, 0, 0)),
                      pl.BlockSpec(memory_space=pl.ANY)],
            out_specs=pl.BlockSpec((TOK_TILE, d), lambda i, a, b: (i, 0)),
            scratch_shapes=[pltpu.VMEM((2, TOK_TILE, d), F32),
                            pltpu.SemaphoreType.DMA(())]),
        out_shape=jax.ShapeDtypeStruct((n, d), F32),
        compiler_params=_cparams(("arbitrary",)),
        name="moe_combine",
    )(d0, d1, x1, info, g2, yb)


def _tri_inclusive_rev(k):
    l = (np.arange(k)[:, None] >= np.arange(k)[None, :]).astype(np.float32)
    return jnp.asarray(np.concatenate([l, l], axis=0), dtype=BF16)


def _tri_strict_lower(k):
    return jnp.asarray((np.arange(k)[None, :] < np.arange(k)[:, None]).astype(np.float32), dtype=BF16)


def kernel(x, c, w_mod, b_mod, ln1_g, w_in, q_norm_g, w_q_up, q_gain, k_gain, rel_bias, gn_a, gn_b,
           w_out, ln2_g, router_g, router_e, w_gate, w_up, w_down):
    batch, seq, d = x.shape
    n = batch * seq
    assert w_mod.shape[0] == 1 and d == D_A + D_B
    assert seq % KB_A == 0 and seq % TOK_TILE == 0 and n % 512 == 0
    x2 = x.reshape(n, d)

    c8 = jnp.pad(c, ((0, 8 - batch), (0, 0)))
    mod = _modulation(c8, w_mod.reshape(d, -1), b_mod.reshape(1, -1))[:batch]
    sh1, sc1, g1, sh2, sc2, g2 = [m.reshape(batch, 1, d) for m in jnp.split(mod, 6, axis=-1)]

    wi = w_in.reshape(d, -1)
    n_a = Q_RANK + 2 * HEAD_DIM + IDX_DIM + IDX_HEADS
    wi = jnp.concatenate([wi[:, n_a:], wi[:, :n_a],
                          jnp.zeros((d, D_IN_PAD - wi.shape[1]), wi.dtype)], axis=1).astype(BF16)
    proj = _ln_proj(x2, ln1_g.reshape(1, d), sc1, sh1, wi, seq)

    qa, qidx, ka, va, kidx, widx = _aprep(
        proj, q_norm_g.reshape(1, -1), w_q_up.reshape(Q_RANK, -1).astype(BF16),
        q_gain.reshape(1, -1), k_gain.reshape(1, -1))

    bt = _bias_tiles(rel_bias)
    oa = _attn_a(qidx, widx, qa, kidx, ka, va, bt, gn_a.reshape(1, -1), batch, seq)
    ob = _attn_b(proj, _tri_inclusive_rev(QB_B), batch, seq)

    rw = jnp.concatenate([router_e.reshape(d, -1), router_g.reshape(d, -1),
                          jnp.zeros((d, LANES - N_EXPERTS - N_GROUPS), F32)], axis=1)
    x1, h2, lg = _out_proj(x2, oa, ob, gn_b.reshape(1, -1), w_out.reshape(d, d).astype(BF16),
                           g1, ln2_g.reshape(1, d), sc2, sh2, rw, seq)

    info, cnt = _route(lg)
    counts = cnt[0, :N_EXPERTS].astype(I32)
    p_rows = 2 * n + N_EXPERTS * MOE_BLK
    ps, be, nu = _plan(counts, p_rows // MOE_BLK)
    ps_lanes = jnp.pad(ps.astype(F32), (0, LANES - N_EXPERTS)).reshape(1, LANES)
    dinfo = _dest(info, ps_lanes, _tri_strict_lower(512))
    d0 = dinfo[:, 0].astype(I32)
    d1 = dinfo[:, 1].astype(I32)
    xb = _scatter_rows(d0, d1, h2, jnp.zeros((p_rows, d), F32))
    yb = _experts(be, nu, xb,
                  w_gate.reshape(N_EXPERTS, d, D_EXPERT).astype(BF16),
                  w_up.reshape(N_EXPERTS, d, D_EXPERT).astype(BF16),
                  w_down.reshape(N_EXPERTS, D_EXPERT, d).astype(BF16))
    out = _combine(d0, d1, x1, info, g2, yb, seq)
    return out.reshape(batch, seq, d)
```

```python
import functools
import math

import numpy as np
import jax
import jax.numpy as jnp
from jax import lax
from jax.experimental import pallas as pl
from jax.experimental.pallas import tpu as pltpu

F32 = jnp.float32
BF16 = jnp.bfloat16
I32 = jnp.int32

HEAD_DIM = 128
N_HEADS_A = 8
N_HEADS_B = 8
D_A = N_HEADS_A * HEAD_DIM
D_B = N_HEADS_B * HEAD_DIM
Q_RANK = 512
IDX_HEADS = 16
IDX_DIM = 64
TOPK_MAX = 256
N_BUCKETS = 32
MAX_DISTANCE = 128
N_GROUPS = 4
EXPERTS_PER_GROUP = 8
N_EXPERTS = N_GROUPS * EXPERTS_PER_GROUP
D_EXPERT = 512
EPS = 1e-6

LANES = 128
VMEM_LIMIT = 56 * 1024 * 1024
NEG = -1e30
INT_MIN = -(2 ** 31)
EXP_UNDERFLOW = 104.0

QB_A = 128
KB_A = 256
QB_B = 128
MOE_BLK = 256
TOK_TILE = 256

COL_QB, COL_KB, COL_VB, COL_A = 0, D_B, 2 * D_B, 3 * D_B
A_CQ, A_KA, A_VA, A_KIDX, A_WIDX = 0, 512, 640, 768, 832
A_WIDTH = 1024
D_IN_PAD = COL_A + A_WIDTH


def _cparams(sem=None):
    return pltpu.CompilerParams(dimension_semantics=sem, vmem_limit_bytes=VMEM_LIMIT)


def _rms(x):
    return x * lax.rsqrt(jnp.mean(x * x, axis=-1, keepdims=True) + EPS)


def _mod_kernel(c_ref, w_ref, b_ref, o_ref):
    c = c_ref[...]
    s = c * (1.0 / (1.0 + jnp.exp(-c)))
    o_ref[...] = jnp.dot(s, w_ref[...], preferred_element_type=F32,
                         precision=lax.Precision.HIGHEST) + b_ref[...]


def _modulation(c8, w_mod, b_mod):
    d, n6 = w_mod.shape
    tn = 1024
    return pl.pallas_call(
        _mod_kernel,
        grid=(n6 // tn,),
        in_specs=[pl.BlockSpec((8, d), lambda j: (0, 0)),
                  pl.BlockSpec((d, tn), lambda j: (0, j)),
                  pl.BlockSpec((1, tn), lambda j: (0, j))],
        out_specs=pl.BlockSpec((8, tn), lambda j: (0, j)),
        out_shape=jax.ShapeDtypeStruct((8, n6), F32),
        compiler_params=_cparams(("arbitrary",)),
        name="modulation",
    )(c8, w_mod, b_mod)


def _ln_proj_kernel(x_ref, g_ref, sc_ref, sh_ref, w_ref, o_ref, h_scr):
    @pl.when(pl.program_id(1) == 0)
    def _():
        h = _rms(x_ref[...]) * g_ref[...]
        h = h * (1.0 + sc_ref[0]) + sh_ref[0]
        h_scr[...] = h.astype(BF16)

    o_ref[...] = jnp.dot(h_scr[...], w_ref[...], preferred_element_type=F32).astype(o_ref.dtype)


def _ln_proj(x2, ln_g, sc, sh, w, seq):
    n, d = x2.shape
    ncol = w.shape[1]
    tm = min(1024, seq)
    tn = 512
    tpb = seq // tm
    return pl.pallas_call(
        _ln_proj_kernel,
        grid=(n // tm, ncol // tn),
        in_specs=[pl.BlockSpec((tm, d), lambda i, j: (i, 0)),
                  pl.BlockSpec((1, d), lambda i, j: (0, 0)),
                  pl.BlockSpec((1, 1, d), lambda i, j: (i // tpb, 0, 0)),
                  pl.BlockSpec((1, 1, d), lambda i, j: (i // tpb, 0, 0)),
                  pl.BlockSpec((d, tn), lambda i, j: (0, j))],
        out_specs=pl.BlockSpec((tm, tn), lambda i, j: (i, j)),
        out_shape=jax.ShapeDtypeStruct((n, ncol), BF16),
        scratch_shapes=[pltpu.VMEM((tm, d), BF16)],
        compiler_params=_cparams(("parallel", "arbitrary")),
        name="ln_in_proj",
    )(x2, ln_g, sc, sh, w)


def _aprep_kernel(a_ref, qng_ref, wq_ref, qg_ref, kg_ref,
                  qa_ref, qidx_ref, ka_ref, va_ref, kidx_ref, widx_ref):
    cq = a_ref[:, A_CQ:A_CQ + Q_RANK].astype(F32)
    cqn = (_rms(cq) * qng_ref[...]).astype(BF16)
    qup = jnp.dot(cqn, wq_ref[...], preferred_element_type=F32)
    for h in range(N_HEADS_A):
        qh = qup[:, h * HEAD_DIM:(h + 1) * HEAD_DIM]
        qn = _rms(qh) * qg_ref[...] * (HEAD_DIM ** -0.5)
        qa_ref[:, h * HEAD_DIM:(h + 1) * HEAD_DIM] = qn.astype(BF16)
    for h in range(IDX_HEADS):
        qi = qup[:, D_A + h * IDX_DIM:D_A + (h + 1) * IDX_DIM] * (IDX_DIM ** -0.5)
        qidx_ref[h] = qi.astype(BF16)
    ka = a_ref[:, A_KA:A_KA + HEAD_DIM].astype(F32)
    ka_ref[...] = (_rms(ka) * kg_ref[...]).astype(BF16)
    va_ref[...] = a_ref[:, A_VA:A_VA + HEAD_DIM]
    kidx_ref[...] = a_ref[:, A_KIDX:A_KIDX + IDX_DIM]
    widx_ref[...] = a_ref[:, A_WIDX:A_WIDX + IDX_HEADS].astype(F32) * (IDX_HEADS ** -0.5)


def _aprep(proj, q_norm_g, w_q_up, q_gain, k_gain):
    n = proj.shape[0]
    tm = 512
    cblk = COL_A // A_WIDTH
    nup = w_q_up.shape[1]
    return pl.pallas_call(
        _aprep_kernel,
        grid=(n // tm,),
        in_specs=[pl.BlockSpec((tm, A_WIDTH), lambda i: (i, cblk)),
                  pl.BlockSpec((1, Q_RANK), lambda i: (0, 0)),
                  pl.BlockSpec((Q_RANK, nup), lambda i: (0, 0)),
                  pl.BlockSpec((1, HEAD_DIM), lambda i: (0, 0)),
                  pl.BlockSpec((1, HEAD_DIM), lambda i: (0, 0))],
        out_specs=[pl.BlockSpec((tm, D_A), lambda i: (i, 0)),
                   pl.BlockSpec((IDX_HEADS, tm, IDX_DIM), lambda i: (0, i, 0)),
                   pl.BlockSpec((tm, HEAD_DIM), lambda i: (i, 0)),
                   pl.BlockSpec((tm, HEAD_DIM), lambda i: (i, 0)),
                   pl.BlockSpec((tm, IDX_DIM), lambda i: (i, 0)),
                   pl.BlockSpec((tm, IDX_HEADS), lambda i: (i, 0))],
        out_shape=[jax.ShapeDtypeStruct((n, D_A), BF16),
                   jax.ShapeDtypeStruct((IDX_HEADS, n, IDX_DIM), BF16),
                   jax.ShapeDtypeStruct((n, HEAD_DIM), BF16),
                   jax.ShapeDtypeStruct((n, HEAD_DIM), BF16),
                   jax.ShapeDtypeStruct((n, IDX_DIM), BF16),
                   jax.ShapeDtypeStruct((n, IDX_HEADS), F32)],
        compiler_params=_cparams(("parallel",)),
        name="group_a_prep",
    )(proj, q_norm_g, w_q_up, q_gain, k_gain)


def _t5_bucket_starts():
    max_exact = N_BUCKETS // 2
    d = np.arange(0, 4 * MAX_DISTANCE, dtype=np.int64)
    df = np.maximum(d, 1).astype(np.float32)
    large = max_exact + (np.log(df / np.float32(max_exact)) / np.float32(math.log(MAX_DISTANCE / max_exact))
                         * np.float32(N_BUCKETS - max_exact)).astype(np.int32)
    large = np.minimum(large, N_BUCKETS - 1)
    bucket = np.where(d < max_exact, d, large)
    assert np.all(np.diff(bucket) >= 0) and bucket[-1] == N_BUCKETS - 1
    return [int(np.argmax(bucket >= b)) for b in range(N_BUCKETS)]


_BUCKET_START = _t5_bucket_starts()
N_BIAS_TILES = 2 * KB_A // LANES


def _bias_kernel(rb_ref, o_ref):
    di = pl.program_id(0)
    h = pl.program_id(1)
    i = lax.broadcasted_iota(I32, (QB_A, KB_A), 0)
    j = lax.broadcasted_iota(I32, (QB_A, KB_A), 1)
    d = di * LANES + i - j
    val = jnp.full((QB_A, KB_A), rb_ref[0, h], F32)
    for b in range(1, N_BUCKETS):
        val = jnp.where(d >= _BUCKET_START[b], rb_ref[b, h], val)
    o_ref[0, 0] = val - rb_ref[N_BUCKETS - 1, h]


def _bias_tiles(rel_bias):
    return pl.pallas_call(
        _bias_kernel,
        grid=(N_BIAS_TILES, N_HEADS_A),
        in_specs=[pl.BlockSpec(memory_space=pltpu.SMEM)],
        out_specs=pl.BlockSpec((1, 1, QB_A, KB_A), lambda a, h: (a, h, 0, 0)),
        out_shape=jax.ShapeDtypeStruct((N_BIAS_TILES, N_HEADS_A, QB_A, KB_A), F32),
        compiler_params=_cparams(("arbitrary", "arbitrary")),
        name="t5_bias_tiles",
    )(rel_bias)


def _attn_a_kernel(qidx_ref, w_ref, qa_ref, kidx_ref, ka_ref, va_ref, bt_ref, gn_ref, o_ref,
                   keys_scr, wb_scr, m_scr, acc_scr, s_scr, mb_scr, *, topk):
    qb = pl.program_id(1)
    t0 = qb * QB_A
    kbl = (t0 + QB_A - 1) // KB_A
    row = t0 + lax.broadcasted_iota(I32, (QB_A, KB_A), 0)
    col0 = lax.broadcasted_iota(I32, (QB_A, KB_A), 1)
    nt = (((1,), (1,)), ((), ()))

    reps = KB_A // LANES
    w = w_ref[...]
    for h in range(IDX_HEADS):
        wb_scr[h] = jnp.broadcast_to(w[:, h:h + 1], (QB_A, LANES))

    def score_tile(kb, carry):
        kt = kidx_ref[pl.ds(pl.multiple_of(kb * KB_A, KB_A), KB_A), :]
        score = jnp.zeros((QB_A, KB_A), F32)
        for h in range(IDX_HEADS):
            sc = lax.dot_general(qidx_ref[h], kt, nt, preferred_element_type=F32)
            score = score + jnp.tile(wb_scr[h], (1, reps)) * jnp.maximum(sc, 0.0)
        bits = pltpu.bitcast(score, I32)
        key = jnp.where(bits < 0, bits ^ jnp.int32(0x7FFFFFFF), bits)
        causal = (kb * KB_A + col0) <= row
        keys_scr[kb] = jnp.where(causal, key, jnp.int32(INT_MIN))
        return carry

    lax.fori_loop(0, kbl + 1, score_tile, 0)

    ones = jnp.ones((LANES, LANES), BF16)
    row1 = t0 + lax.broadcasted_iota(I32, (QB_A, LANES), 0)

    def bit_step(carry):
        i, thr, open_rows, _ = carry
        cand = thr + lax.shift_left(jnp.int32(1), 31 - i)

        def count(kb, cnt):
            for r in range(reps):
                cnt = cnt + jnp.where(keys_scr[kb, :, r * LANES:(r + 1) * LANES] >= cand, 1.0, 0.0)
            return cnt

        cnt = lax.fori_loop(0, kbl + 1, count, jnp.zeros((QB_A, LANES), F32))
        tot = jnp.dot(cnt.astype(BF16), ones, preferred_element_type=F32)
        thr = jnp.where(tot >= float(topk), cand, thr)
        open_rows = jnp.where(tot == float(topk), 0.0, open_rows)
        go = jnp.logical_and(i < 31, jnp.max(open_rows) > 0.0)
        return i + 1, thr, open_rows, go

    open0 = jnp.where(row1 >= topk, 1.0, 0.0)
    _, thr, _, _ = lax.while_loop(
        lambda c: c[3], bit_step,
        (jnp.int32(0), jnp.full((QB_A, LANES), INT_MIN, I32), open0, jnp.max(open0) > 0.0))
    thrb = jnp.tile(thr, (1, reps))

    m_scr[...] = jnp.full(m_scr.shape, NEG, F32)
    acc_scr[...] = jnp.zeros(acc_scr.shape, F32)

    def attend(kb, near):
        mb = jnp.where(keys_scr[kb] >= thrb, 0.0, NEG)
        if near:
            mb = jnp.where((kb * KB_A + col0) <= row, mb, NEG)
            di = (t0 - kb * KB_A) // LANES
        mb_scr[...] = mb
        start = pl.multiple_of(kb * KB_A, KB_A)
        kt = ka_ref[pl.ds(start, KB_A), :]
        vt = jnp.concatenate([va_ref[pl.ds(start, KB_A), :], jnp.ones((KB_A, LANES), BF16)], axis=1)
        for h in range(N_HEADS_A):
            q = qa_ref[:, h * HEAD_DIM:(h + 1) * HEAD_DIM]
            s_scr[h] = lax.dot_general(q, kt, nt, preferred_element_type=F32)
        for h in range(N_HEADS_A):
            s = s_scr[h] + mb_scr[...]
            if near:
                s = s + bt_ref[di, h]
            m_prev = m_scr[h]
            m_new = jnp.maximum(m_prev, jnp.max(s, axis=1, keepdims=True))
            alpha = jnp.exp(m_prev - m_new)
            p = jnp.exp(s - jnp.tile(m_new, (1, reps)))
            acc_scr[h] = (jnp.tile(alpha, (1, 2)) * acc_scr[h]
                          + jnp.dot(p.astype(BF16), vt, preferred_element_type=F32))
            m_scr[h] = m_new

    def far_tile(kb, carry):
        attend(kb, False)
        return carry

    lax.fori_loop(0, jnp.maximum(kbl - 1, 0), far_tile, 0)

    @pl.when(kbl >= 1)
    def _():
        attend(kbl - 1, True)

    attend(kbl, True)

    ssq = jnp.zeros((QB_A, LANES), F32)
    for h in range(N_HEADS_A):
        oh = acc_scr[h, :, :HEAD_DIM] / acc_scr[h, :, HEAD_DIM:]
        acc_scr[h, :, :HEAD_DIM] = oh
        ssq = ssq + jnp.sum(oh * oh, axis=1, keepdims=True)
    inv = lax.rsqrt(ssq * (1.0 / D_A) + EPS)
    for h in range(N_HEADS_A):
        sl = slice(h * HEAD_DIM, (h + 1) * HEAD_DIM)
        o_ref[:, sl] = (acc_scr[h, :, :HEAD_DIM] * inv * gn_ref[:, sl]).astype(o_ref.dtype)


def _attn_a(qidx, widx, qa, kidx, ka, va, bt, gn_a, batch, seq):
    n = qa.shape[0]
    nq = seq // QB_A
    nkt = seq // KB_A
    topk = min(TOPK_MAX, seq // 4)
    return pl.pallas_call(
        functools.partial(_attn_a_kernel, topk=topk),
        grid=(batch, nq),
        in_specs=[pl.BlockSpec((IDX_HEADS, QB_A, IDX_DIM), lambda b, q: (0, b * nq + q, 0)),
                  pl.BlockSpec((QB_A, IDX_HEADS), lambda b, q: (b * nq + q, 0)),
                  pl.BlockSpec((QB_A, D_A), lambda b, q: (b * nq + q, 0)),
                  pl.BlockSpec((seq, IDX_DIM), lambda b, q: (b, 0)),
                  pl.BlockSpec((seq, HEAD_DIM), lambda b, q: (b, 0)),
                  pl.BlockSpec((seq, HEAD_DIM), lambda b, q: (b, 0)),
                  pl.BlockSpec(bt.shape, lambda b, q: (0, 0, 0, 0)),
                  pl.BlockSpec((1, D_A), lambda b, q: (0, 0))],
        out_specs=pl.BlockSpec((QB_A, D_A), lambda b, q: (b * nq + q, 0)),
        out_shape=jax.ShapeDtypeStruct((n, D_A), BF16),
        scratch_shapes=[pltpu.VMEM((nkt, QB_A, KB_A), I32),
                        pltpu.VMEM((IDX_HEADS, QB_A, LANES), F32),
                        pltpu.VMEM((N_HEADS_A, QB_A, LANES), F32),
                        pltpu.VMEM((N_HEADS_A, QB_A, 2 * HEAD_DIM), F32),
                        pltpu.VMEM((N_HEADS_A, QB_A, KB_A), F32),
                        pltpu.VMEM((QB_A, KB_A), F32)],
        compiler_params=_cparams(("parallel", "arbitrary")),
        name="dsa_attention",
    )(qidx, widx, qa, kidx, ka, va, bt, gn_a)


def _attn_b_kernel(q_ref, k_ref, v_ref, tri_ref, o_ref, rest_scr, z_scr, cs_scr):
    qb = pl.program_id(1)
    row = lax.broadcasted_iota(I32, (QB_B, QB_B), 0)
    col = lax.broadcasted_iota(I32, (QB_B, QB_B), 1)
    strict = col < row
    nt = (((1,), (1,)), ((), ()))
    scale = HEAD_DIM ** -0.5

    def step(kb, diag):
        start = pl.multiple_of(kb * QB_B, QB_B)
        heads = [slice(h * HEAD_DIM, (h + 1) * HEAD_DIM) for h in range(N_HEADS_B)]
        for h, sl in enumerate(heads):
            kt = k_ref[pl.ds(start, QB_B), sl]
            z_scr[h] = lax.dot_general(q_ref[:, sl], kt, nt, preferred_element_type=F32) * scale
        for h, sl in enumerate(heads):
            z = z_scr[h]
            sp = jnp.maximum(z, 0.0) + jnp.log(1.0 + jnp.exp(-jnp.abs(z)))
            if diag:
                sp = jnp.where(strict, sp, 0.0)
            hi = sp.astype(BF16)
            lo = (sp - hi.astype(F32)).astype(BF16)
            cs_scr[h] = jnp.dot(jnp.concatenate([hi, lo], axis=1), tri_ref[...],
                                preferred_element_type=F32)
        worst = None
        for h, sl in enumerate(heads):
            vt = v_ref[pl.ds(start, QB_B), sl]
            z = z_scr[h]
            cs = cs_scr[h, :, :QB_B]
            tot = cs_scr[h, :, QB_B:]
            if diag:
                a = jnp.where(strict, jnp.exp(z - cs), 0.0)
                o_ref[:, sl] = jnp.dot(a.astype(BF16), vt, preferred_element_type=F32)
                rest = tot
            else:
                rest = rest_scr[h]
                a = jnp.exp(z - cs - rest)
                o_ref[:, sl] += jnp.dot(a.astype(BF16), vt, preferred_element_type=F32)
                rest = rest + tot
            rest_scr[h] = rest
            worst = rest if worst is None else jnp.minimum(worst, rest)
        return jnp.min(worst)

    def more(kb, smallest):
        return jnp.logical_and(kb >= 0, smallest < EXP_UNDERFLOW)

    def body(carry):
        kb, _ = carry
        return kb - 1, more(kb - 1, step(kb, False))

    lax.while_loop(lambda c: c[1], body, (qb - 1, more(qb - 1, step(qb, True))))


def _attn_b(proj, tri, batch, seq):
    n = proj.shape[0]
    nq = seq // QB_B
    return pl.pallas_call(
        _attn_b_kernel,
        grid=(batch, nq),
        in_specs=[pl.BlockSpec((QB_B, D_B), lambda b, q: (b * nq + q, COL_QB // D_B)),
                  pl.BlockSpec((seq, D_B), lambda b, q: (b, COL_KB // D_B)),
                  pl.BlockSpec((seq, D_B), lambda b, q: (b, COL_VB // D_B)),
                  pl.BlockSpec(tri.shape, lambda b, q: (0, 0))],
        out_specs=pl.BlockSpec((QB_B, D_B), lambda b, q: (b * nq + q, 0)),
        out_shape=jax.ShapeDtypeStruct((n, D_B), F32),
        scratch_shapes=[pltpu.VMEM((N_HEADS_B, QB_B, QB_B), F32),
                        pltpu.VMEM((N_HEADS_B, QB_B, QB_B), F32),
                        pltpu.VMEM((N_HEADS_B, QB_B, 2 * QB_B), F32)],
        compiler_params=_cparams(("parallel", "arbitrary")),
        name="stick_breaking_attention",
    )(proj, proj, proj, tri)


def _out_proj_kernel(x_ref, oa_ref, ob_ref, gnb_ref, w_ref, g1_ref, ln_ref, sc_ref, sh_ref, rw_ref,
                     x1_ref, h2_ref, lg_ref):
    obn = (_rms(ob_ref[...]) * gnb_ref[...]).astype(BF16)
    y = jnp.dot(oa_ref[...], w_ref[0:D_A, :], preferred_element_type=F32)
    y = y + jnp.dot(obn, w_ref[D_A:D_A + D_B, :], preferred_element_type=F32)
    x1 = x_ref[...] + g1_ref[0] * y
    x1_ref[...] = x1
    h2 = _rms(x1) * ln_ref[...]
    h2 = h2 * (1.0 + sc_ref[0]) + sh_ref[0]
    h2_ref[...] = h2
    hh = h2.astype(BF16)
    hl = (h2 - hh.astype(F32)).astype(BF16)
    rw = rw_ref[...]
    rh = rw.astype(BF16)
    rl = (rw - rh.astype(F32)).astype(BF16)
    lg = jnp.dot(hh, rh, preferred_element_type=F32)
    lg = lg + jnp.dot(hh, rl, preferred_element_type=F32)
    lg = lg + jnp.dot(hl, rh, preferred_element_type=F32)
    lg_ref[...] = lg


def _out_proj(x2, oa, ob, gn_b, w_out, g1, ln_g, sc, sh, rw, seq):
    n, d = x2.shape
    tm = 256
    tpb = seq // tm
    row = lambda i: (i, 0)
    fixed = lambda i: (0, 0)
    perb = lambda i: (i // tpb, 0, 0)
    return pl.pallas_call(
        _out_proj_kernel,
        grid=(n // tm,),
        in_specs=[pl.BlockSpec((tm, d), row),
                  pl.BlockSpec((tm, D_A), row),
                  pl.BlockSpec((tm, D_B), row),
                  pl.BlockSpec((1, D_B), fixed),
                  pl.BlockSpec(w_out.shape, fixed),
                  pl.BlockSpec((1, 1, d), perb),
                  pl.BlockSpec((1, d), fixed),
                  pl.BlockSpec((1, 1, d), perb),
                  pl.BlockSpec((1, 1, d), perb),
                  pl.BlockSpec(rw.shape, fixed)],
        out_specs=[pl.BlockSpec((tm, d), row),
                   pl.BlockSpec((tm, d), row),
                   pl.BlockSpec((tm, LANES), row)],
        out_shape=[jax.ShapeDtypeStruct((n, d), F32),
                   jax.ShapeDtypeStruct((n, d), F32),
                   jax.ShapeDtypeStruct((n, LANES), F32)],
        compiler_params=_cparams(("parallel",)),
        name="out_proj_ln2_router",
    )(x2, oa, ob, gn_b, w_out, g1, ln_g, sc, sh, rw)


def _route_kernel(lg_ref, info_ref, cnt_ref):
    @pl.when(pl.program_id(0) == 0)
    def _():
        cnt_ref[...] = jnp.zeros(cnt_ref.shape, F32)

    lg = lg_ref[...]
    lane = lax.broadcasted_iota(I32, lg.shape, 1)
    lanef = lane.astype(F32)
    big = float(4 * LANES)
    gm = jnp.where(lane >= N_EXPERTS, jnp.where(lane < N_EXPERTS + N_GROUPS, 1.0, 0.0), 0.0) > 0.5
    lgm = jnp.where(gm, lg, NEG)
    mg = jnp.max(lgm, axis=1, keepdims=True)
    eg = jnp.where(gm, jnp.exp(lgm - mg), 0.0)
    pg = eg / jnp.sum(eg, axis=1, keepdims=True)
    gw = jnp.max(pg, axis=1, keepdims=True)
    gidx = jnp.min(jnp.where(gm, jnp.where(pg == gw, lanef - N_EXPERTS, big), big), axis=1, keepdims=True)
    lane_group = (lane // EXPERTS_PER_GROUP).astype(F32)
    em = jnp.where(lane < N_EXPERTS, jnp.where(lane_group == gidx, 1.0, 0.0), 0.0) > 0.5
    lem = jnp.where(em, lg, NEG)
    me = jnp.max(lem, axis=1, keepdims=True)
    ee = jnp.where(em, jnp.exp(lem - me), 0.0)
    pe = jnp.where(em, ee / jnp.sum(ee, axis=1, keepdims=True), -1.0)
    p1 = jnp.max(pe, axis=1, keepdims=True)
    i1 = jnp.min(jnp.where(pe == p1, lanef, big), axis=1, keepdims=True)
    pe2 = jnp.where(lanef == i1, -1.0, pe)
    p2 = jnp.max(pe2, axis=1, keepdims=True)
    i2 = jnp.min(jnp.where(pe2 == p2, lanef, big), axis=1, keepdims=True)
    den = p1 + p2
    g0 = gw * p1 / den
    g1 = gw * p2 / den
    info = jnp.where(lane == 0, i1, jnp.where(lane == 1, i2,
                     jnp.where(lane == 2, g0, jnp.where(lane == 3, g1, 0.0))))
    info_ref[...] = info
    oh = jnp.where(lanef == i1, 1.0, 0.0) + jnp.where(lanef == i2, 1.0, 0.0)
    cnt_ref[...] += jnp.sum(oh, axis=0, keepdims=True)


def _route(lg):
    n = lg.shape[0]
    tm = min(1024, n)
    return pl.pallas_call(
        _route_kernel,
        grid=(n // tm,),
        in_specs=[pl.BlockSpec((tm, LANES), lambda i: (i, 0))],
        out_specs=[pl.BlockSpec((tm, LANES), lambda i: (i, 0)),
                   pl.BlockSpec((1, LANES), lambda i: (0, 0))],
        out_shape=[jax.ShapeDtypeStruct((n, LANES), F32),
                   jax.ShapeDtypeStruct((1, LANES), F32)],
        compiler_params=_cparams(("arbitrary",)),
        name="moe_route",
    )(lg)


def _plan_kernel(cnt_ref, ps_ref, be_ref, nu_ref, *, nblk):
    def fill(i, c):
        be_ref[i] = N_EXPERTS - 1
        return c

    lax.fori_loop(0, nblk, fill, 0)

    def per_expert(e, pos):
        ps_ref[e] = pos * MOE_BLK
        nb = (cnt_ref[e] + MOE_BLK - 1) // MOE_BLK

        def mark(k, c):
            be_ref[pos + k] = e
            return c

        lax.fori_loop(0, nb, mark, 0)
        return pos + nb

    nu_ref[0] = lax.fori_loop(0, N_EXPERTS, per_expert, jnp.int32(0))


def _plan(counts, nblk):
    smem = pl.BlockSpec(memory_space=pltpu.SMEM)
    return pl.pallas_call(
        functools.partial(_plan_kernel, nblk=nblk),
        in_specs=[smem],
        out_specs=[smem, smem, smem],
        out_shape=[jax.ShapeDtypeStruct((N_EXPERTS,), I32),
                   jax.ShapeDtypeStruct((nblk,), I32),
                   jax.ShapeDtypeStruct((1,), I32)],
        name="moe_block_plan",
    )(counts)


def _dest_kernel(info_ref, ps_ref, tri_ref, o_ref, carry_scr):
    @pl.when(pl.program_id(0) == 0)
    def _():
        carry_scr[...] = jnp.zeros(carry_scr.shape, F32)

    info = info_ref[...]
    lane = lax.broadcasted_iota(I32, info.shape, 1)
    lanef = lane.astype(F32)
    o1 = jnp.where(lanef == info[:, 0:1], 1.0, 0.0)
    o2 = jnp.where(lanef == info[:, 1:2], 1.0, 0.0)
    oh = o1 + o2
    before = jnp.dot(tri_ref[...], oh.astype(BF16), preferred_element_type=F32)
    base = before + carry_scr[...] + ps_ref[...]
    d1 = jnp.sum(o1 * base, axis=1, keepdims=True)
    d2 = jnp.sum(o2 * base, axis=1, keepdims=True)
    o_ref[...] = jnp.where(lane == 0, d1, jnp.where(lane == 1, d2, 0.0))
    carry_scr[...] += jnp.sum(oh, axis=0, keepdims=True)


def _dest(info, ps_lanes, tri):
    n = info.shape[0]
    tm = tri.shape[0]
    return pl.pallas_call(
        _dest_kernel,
        grid=(n // tm,),
        in_specs=[pl.BlockSpec((tm, LANES), lambda i: (i, 0)),
                  pl.BlockSpec((1, LANES), lambda i: (0, 0)),
                  pl.BlockSpec((tm, tm), lambda i: (0, 0))],
        out_specs=pl.BlockSpec((tm, LANES), lambda i: (i, 0)),
        out_shape=jax.ShapeDtypeStruct((n, LANES), F32),
        scratch_shapes=[pltpu.VMEM((1, LANES), F32)],
        compiler_params=_cparams(("arbitrary",)),
        name="moe_dest_rows",
    )(info, ps_lanes, tri)


def _scatter_kernel(d0_ref, d1_ref, h_ref, xz_ref, xb_ref, sem):
    del xz_ref
    base = pl.program_id(0) * TOK_TILE

    def row_copy(r, dst_row):
        return pltpu.make_async_copy(h_ref.at[pl.ds(r, 1)], xb_ref.at[pl.ds(dst_row, 1)], sem)

    def issue(r, c):
        row_copy(r, d0_ref[base + r]).start()
        row_copy(r, d1_ref[base + r]).start()
        return c

    lax.fori_loop(0, TOK_TILE, issue, 0)

    def drain(r, c):
        row_copy(0, 0).wait()
        row_copy(0, 0).wait()
        return c

    lax.fori_loop(0, TOK_TILE, drain, 0)


def _scatter_rows(d0, d1, h2, xzero):
    n, d = h2.shape
    return pl.pallas_call(
        _scatter_kernel,
        grid_spec=pltpu.PrefetchScalarGridSpec(
            num_scalar_prefetch=2,
            grid=(n // TOK_TILE,),
            in_specs=[pl.BlockSpec((TOK_TILE, d), lambda i, a, b: (i, 0)),
                      pl.BlockSpec(memory_space=pl.ANY)],
            out_specs=pl.BlockSpec(memory_space=pl.ANY),
            scratch_shapes=[pltpu.SemaphoreType.DMA(())]),
        out_shape=jax.ShapeDtypeStruct(xzero.shape, xzero.dtype),
        input_output_aliases={3: 0},
        compiler_params=_cparams(("arbitrary",)),
        name="moe_scatter_rows",
    )(d0, d1, h2, xzero)


def _expert_kernel(be_ref, nu_ref, x_ref, wg_ref, wu_ref, wd_ref, o_ref):
    del be_ref
    i = pl.program_id(0)

    @pl.when(i < nu_ref[0])
    def _():
        x = x_ref[...].astype(BF16)
        g = jnp.dot(x, wg_ref[0], preferred_element_type=F32)
        u = jnp.dot(x, wu_ref[0], preferred_element_type=F32)
        act = (g * (1.0 / (1.0 + jnp.exp(-g))) * u).astype(BF16)
        o_ref[...] = jnp.dot(act, wd_ref[0], preferred_element_type=F32)

    @pl.when(i >= nu_ref[0])
    def _():
        o_ref[...] = jnp.zeros(o_ref.shape, o_ref.dtype)


def _experts(be, nu, xb, wg, wu, wd):
    p, d = xb.shape
    de = wg.shape[2]
    return pl.pallas_call(
        _expert_kernel,
        grid_spec=pltpu.PrefetchScalarGridSpec(
            num_scalar_prefetch=2,
            grid=(p // MOE_BLK,),
            in_specs=[pl.BlockSpec((MOE_BLK, d), lambda i, be, nu: (i, 0)),
                      pl.BlockSpec((1, d, de), lambda i, be, nu: (be[i], 0, 0)),
                      pl.BlockSpec((1, d, de), lambda i, be, nu: (be[i], 0, 0)),
                      pl.BlockSpec((1, de, d), lambda i, be, nu: (be[i], 0, 0))],
            out_specs=pl.BlockSpec((MOE_BLK, d), lambda i, be, nu: (i, 0))),
        out_shape=jax.ShapeDtypeStruct((p, d), F32),
        compiler_params=_cparams(("arbitrary",)),
        name="moe_expert_ffn",
    )(be, nu, xb, wg, wu, wd)


def _combine_kernel(d0_ref, d1_ref, x_ref, info_ref, g2_ref, yb_ref, o_ref, rows_scr, sem):
    base = pl.program_id(0) * TOK_TILE

    def row_copy(slot, r, src_row):
        return pltpu.make_async_copy(yb_ref.at[pl.ds(src_row, 1)], rows_scr.at[slot, pl.ds(r, 1)], sem)

    def issue(r, c):
        row_copy(0, r, d0_ref[base + r]).start()
        row_copy(1, r, d1_ref[base + r]).start()
        return c

    lax.fori_loop(0, TOK_TILE, issue, 0)

    def drain(r, c):
        row_copy(0, 0, 0).wait()
        row_copy(1, 0, 0).wait()
        return c

    lax.fori_loop(0, TOK_TILE, drain, 0)

    info = info_ref[...]
    y = info[:, 2:3] * rows_scr[0] + info[:, 3:4] * rows_scr[1]
    o_ref[...] = x_ref[...] + g2_ref[0] * y


def _combine(d0, d1, x1, info, g2, yb, seq):
    n, d = x1.shape
    tpb = seq // TOK_TILE
    return pl.pallas_call(
        _combine_kernel,
        grid_spec=pltpu.PrefetchScalarGridSpec(
            num_scalar_prefetch=2,
            grid=(n // TOK_TILE,),
            in_specs=[pl.BlockSpec((TOK_TILE, d), lambda i, a, b: (i, 0)),
                      pl.BlockSpec((TOK_TILE, LANES), lambda i, a, b: (i, 0)),
                      pl.BlockSpec((1, 1, d), lambda i, a, b: (i // tpb, 0, 0)),
                      pl.BlockSpec(memory_space=pl.ANY)],
            out_specs=pl.BlockSpec((TOK_TILE, d), lambda i, a, b: (i, 0)),
            scratch_shapes=[pltpu.VMEM((2, TOK_TILE, d), F32),
                            pltpu.SemaphoreType.DMA(())]),
        out_shape=jax.ShapeDtypeStruct((n, d), F32),
        compiler_params=_cparams(("arbitrary",)),
        name="moe_combine",
    )(d0, d1, x1, info, g2, yb)


def _tri_inclusive_rev(k):
    l = (np.arange(k)[:, None] >= np.arange(k)[None, :]).astype(np.float32)
    half = np.concatenate([l, np.ones((k, k), np.float32)], axis=1)
    return jnp.asarray(np.concatenate([half, half], axis=0), dtype=BF16)


def _tri_strict_lower(k):
    return jnp.asarray((np.arange(k)[None, :] < np.arange(k)[:, None]).astype(np.float32), dtype=BF16)


def kernel(x, c, w_mod, b_mod, ln1_g, w_in, q_norm_g, w_q_up, q_gain, k_gain, rel_bias, gn_a, gn_b,
           w_out, ln2_g, router_g, router_e, w_gate, w_up, w_down):
    batch, seq, d = x.shape
    n = batch * seq
    assert w_mod.shape[0] == 1 and d == D_A + D_B
    assert seq % KB_A == 0 and seq % TOK_TILE == 0 and n % 512 == 0
    x2 = x.reshape(n, d)

    c8 = jnp.pad(c, ((0, 8 - batch), (0, 0)))
    mod = _modulation(c8, w_mod.reshape(d, -1), b_mod.reshape(1, -1))[:batch]
    sh1, sc1, g1, sh2, sc2, g2 = [m.reshape(batch, 1, d) for m in jnp.split(mod, 6, axis=-1)]

    wi = w_in.reshape(d, -1)
    n_a = Q_RANK + 2 * HEAD_DIM + IDX_DIM + IDX_HEADS
    wi = jnp.concatenate([wi[:, n_a:], wi[:, :n_a],
                          jnp.zeros((d, D_IN_PAD - wi.shape[1]), wi.dtype)], axis=1).astype(BF16)
    proj = _ln_proj(x2, ln1_g.reshape(1, d), sc1, sh1, wi, seq)

    qa, qidx, ka, va, kidx, widx = _aprep(
        proj, q_norm_g.reshape(1, -1), w_q_up.reshape(Q_RANK, -1).astype(BF16),
        q_gain.reshape(1, -1), k_gain.reshape(1, -1))

    bt = _bias_tiles(rel_bias)
    oa = _attn_a(qidx, widx, qa, kidx, ka, va, bt, gn_a.reshape(1, -1), batch, seq)
    ob = _attn_b(proj, _tri_inclusive_rev(QB_B), batch, seq)

    rw = jnp.concatenate([router_e.reshape(d, -1), router_g.reshape(d, -1),
                          jnp.zeros((d, LANES - N_EXPERTS - N_GROUPS), F32)], axis=1)
    x1, h2, lg = _out_proj(x2, oa, ob, gn_b.reshape(1, -1), w_out.reshape(d, d).astype(BF16),
                           g1, ln2_g.reshape(1, d), sc2, sh2, rw, seq)

    info, cnt = _route(lg)
    counts = cnt[0, :N_EXPERTS].astype(I32)
    p_rows = 2 * n + N_EXPERTS * MOE_BLK
    ps, be, nu = _plan(counts, p_rows // MOE_BLK)
    ps_lanes = jnp.pad(ps.astype(F32), (0, LANES - N_EXPERTS)).reshape(1, LANES)
    dinfo = _dest(info, ps_lanes, _tri_strict_lower(512))
    d0 = dinfo[:, 0].astype(I32)
    d1 = dinfo[:, 1].astype(I32)
    xb = _scatter_rows(d0, d1, h2, jnp.zeros((p_rows, d), F32))
    yb = _experts(be, nu, xb,
                  w_gate.reshape(N_EXPERTS, d, D_EXPERT).astype(BF16),
                  w_up.reshape(N_EXPERTS, d, D_EXPERT).astype(BF16),
                  w_down.reshape(N_EXPERTS, D_EXPERT, d).astype(BF16))
    out = _combine(d0, d1, x1, info, g2, yb, seq)
    return out.reshape(batch, seq, d)
```

```python
import functools
import math

import numpy as np
import jax
import jax.numpy as jnp
from jax import lax
from jax.experimental import pallas as pl
from jax.experimental.pallas import tpu as pltpu

F32 = jnp.float32
BF16 = jnp.bfloat16
I32 = jnp.int32

HEAD_DIM = 128
N_HEADS_A = 8
N_HEADS_B = 8
D_A = N_HEADS_A * HEAD_DIM
D_B = N_HEADS_B * HEAD_DIM
Q_RANK = 512
IDX_HEADS = 16
IDX_DIM = 64
TOPK_MAX = 256
N_BUCKETS = 32
MAX_DISTANCE = 128
N_GROUPS = 4
EXPERTS_PER_GROUP = 8
N_EXPERTS = N_GROUPS * EXPERTS_PER_GROUP
D_EXPERT = 512
EPS = 1e-6

LANES = 128
VMEM_LIMIT = 56 * 1024 * 1024
NEG = -1e30
INT_MIN = -(2 ** 31)
EXP_UNDERFLOW = 104.0

QB_A = 128
KB_A = 256
QB_B = 128
MOE_BLK = 256
TOK_TILE = 256

COL_QB, COL_KB, COL_VB, COL_A = 0, D_B, 2 * D_B, 3 * D_B
A_CQ, A_KA, A_VA, A_KIDX, A_WIDX = 0, 512, 640, 768, 832
A_WIDTH = 1024
D_IN_PAD = COL_A + A_WIDTH


def _cparams(sem=None):
    return pltpu.CompilerParams(dimension_semantics=sem, vmem_limit_bytes=VMEM_LIMIT)


def _rms(x):
    return x * lax.rsqrt(jnp.mean(x * x, axis=-1, keepdims=True) + EPS)


def _mod_kernel(c_ref, w_ref, b_ref, o_ref):
    c = c_ref[...]
    s = c * (1.0 / (1.0 + jnp.exp(-c)))
    o_ref[...] = jnp.dot(s, w_ref[...], preferred_element_type=F32,
                         precision=lax.Precision.HIGHEST) + b_ref[...]


def _modulation(c8, w_mod, b_mod):
    d, n6 = w_mod.shape
    tn = 1024
    return pl.pallas_call(
        _mod_kernel,
        grid=(n6 // tn,),
        in_specs=[pl.BlockSpec((8, d), lambda j: (0, 0)),
                  pl.BlockSpec((d, tn), lambda j: (0, j)),
                  pl.BlockSpec((1, tn), lambda j: (0, j))],
        out_specs=pl.BlockSpec((8, tn), lambda j: (0, j)),
        out_shape=jax.ShapeDtypeStruct((8, n6), F32),
        compiler_params=_cparams(("arbitrary",)),
        name="modulation",
    )(c8, w_mod, b_mod)


def _ln_proj_kernel(x_ref, g_ref, sc_ref, sh_ref, w_ref, o_ref, h_scr):
    @pl.when(pl.program_id(1) == 0)
    def _():
        h = _rms(x_ref[...]) * g_ref[...]
        h = h * (1.0 + sc_ref[0]) + sh_ref[0]
        h_scr[...] = h.astype(BF16)

    o_ref[...] = jnp.dot(h_scr[...], w_ref[...], preferred_element_type=F32).astype(o_ref.dtype)


def _ln_proj(x2, ln_g, sc, sh, w, seq):
    n, d = x2.shape
    ncol = w.shape[1]
    tm = min(1024, seq)
    tn = 512
    tpb = seq // tm
    return pl.pallas_call(
        _ln_proj_kernel,
        grid=(n // tm, ncol // tn),
        in_specs=[pl.BlockSpec((tm, d), lambda i, j: (i, 0)),
                  pl.BlockSpec((1, d), lambda i, j: (0, 0)),
                  pl.BlockSpec((1, 1, d), lambda i, j: (i // tpb, 0, 0)),
                  pl.BlockSpec((1, 1, d), lambda i, j: (i // tpb, 0, 0)),
                  pl.BlockSpec((d, tn), lambda i, j: (0, j))],
        out_specs=pl.BlockSpec((tm, tn), lambda i, j: (i, j)),
        out_shape=jax.ShapeDtypeStruct((n, ncol), BF16),
        scratch_shapes=[pltpu.VMEM((tm, d), BF16)],
        compiler_params=_cparams(("parallel", "arbitrary")),
        name="ln_in_proj",
    )(x2, ln_g, sc, sh, w)


def _aprep_kernel(a_ref, qng_ref, wq_ref, qg_ref, kg_ref,
                  qa_ref, qidx_ref, ka_ref, va_ref, kidx_ref, widx_ref):
    cq = a_ref[:, A_CQ:A_CQ + Q_RANK].astype(F32)
    cqn = (_rms(cq) * qng_ref[...]).astype(BF16)
    qup = jnp.dot(cqn, wq_ref[...], preferred_element_type=F32)
    for h in range(N_HEADS_A):
        qh = qup[:, h * HEAD_DIM:(h + 1) * HEAD_DIM]
        qn = _rms(qh) * qg_ref[...] * (HEAD_DIM ** -0.5)
        qa_ref[:, h * HEAD_DIM:(h + 1) * HEAD_DIM] = qn.astype(BF16)
    for h in range(IDX_HEADS):
        qi = qup[:, D_A + h * IDX_DIM:D_A + (h + 1) * IDX_DIM] * (IDX_DIM ** -0.5)
        qidx_ref[h] = qi.astype(BF16)
    ka = a_ref[:, A_KA:A_KA + HEAD_DIM].astype(F32)
    ka_ref[...] = (_rms(ka) * kg_ref[...]).astype(BF16)
    va_ref[...] = a_ref[:, A_VA:A_VA + HEAD_DIM]
    kidx_ref[...] = a_ref[:, A_KIDX:A_KIDX + IDX_DIM]
    widx_ref[...] = a_ref[:, A_WIDX:A_WIDX + IDX_HEADS].astype(F32) * (IDX_HEADS ** -0.5)


def _aprep(proj, q_norm_g, w_q_up, q_gain, k_gain):
    n = proj.shape[0]
    tm = 512
    cblk = COL_A // A_WIDTH
    nup = w_q_up.shape[1]
    return pl.pallas_call(
        _aprep_kernel,
        grid=(n // tm,),
        in_specs=[pl.BlockSpec((tm, A_WIDTH), lambda i: (i, cblk)),
                  pl.BlockSpec((1, Q_RANK), lambda i: (0, 0)),
                  pl.BlockSpec((Q_RANK, nup), lambda i: (0, 0)),
                  pl.BlockSpec((1, HEAD_DIM), lambda i: (0, 0)),
                  pl.BlockSpec((1, HEAD_DIM), lambda i: (0, 0))],
        out_specs=[pl.BlockSpec((tm, D_A), lambda i: (i, 0)),
                   pl.BlockSpec((IDX_HEADS, tm, IDX_DIM), lambda i: (0, i, 0)),
                   pl.BlockSpec((tm, HEAD_DIM), lambda i: (i, 0)),
                   pl.BlockSpec((tm, HEAD_DIM), lambda i: (i, 0)),
                   pl.BlockSpec((tm, IDX_DIM), lambda i: (i, 0)),
                   pl.BlockSpec((tm, IDX_HEADS), lambda i: (i, 0))],
        out_shape=[jax.ShapeDtypeStruct((n, D_A), BF16),
                   jax.ShapeDtypeStruct((IDX_HEADS, n, IDX_DIM), BF16),
                   jax.ShapeDtypeStruct((n, HEAD_DIM), BF16),
                   jax.ShapeDtypeStruct((n, HEAD_DIM), BF16),
                   jax.ShapeDtypeStruct((n, IDX_DIM), BF16),
                   jax.ShapeDtypeStruct((n, IDX_HEADS), F32)],
        compiler_params=_cparams(("parallel",)),
        name="group_a_prep",
    )(proj, q_norm_g, w_q_up, q_gain, k_gain)


def _t5_bucket_starts():
    max_exact = N_BUCKETS // 2
    d = np.arange(0, 4 * MAX_DISTANCE, dtype=np.int64)
    df = np.maximum(d, 1).astype(np.float32)
    large = max_exact + (np.log(df / np.float32(max_exact)) / np.float32(math.log(MAX_DISTANCE / max_exact))
                         * np.float32(N_BUCKETS - max_exact)).astype(np.int32)
    large = np.minimum(large, N_BUCKETS - 1)
    bucket = np.where(d < max_exact, d, large)
    assert np.all(np.diff(bucket) >= 0) and bucket[-1] == N_BUCKETS - 1
    return [int(np.argmax(bucket >= b)) for b in range(N_BUCKETS)]


_BUCKET_START = _t5_bucket_starts()
N_BIAS_TILES = 2 * KB_A // LANES


def _bias_kernel(rb_ref, o_ref):
    di = pl.program_id(0)
    h = pl.program_id(1)
    i = lax.broadcasted_iota(I32, (QB_A, KB_A), 0)
    j = lax.broadcasted_iota(I32, (QB_A, KB_A), 1)
    d = di * LANES + i - j
    val = jnp.full((QB_A, KB_A), rb_ref[0, h], F32)
    for b in range(1, N_BUCKETS):
        val = jnp.where(d >= _BUCKET_START[b], rb_ref[b, h], val)
    o_ref[0, 0] = val - rb_ref[N_BUCKETS - 1, h]


def _bias_tiles(rel_bias):
    return pl.pallas_call(
        _bias_kernel,
        grid=(N_BIAS_TILES, N_HEADS_A),
        in_specs=[pl.BlockSpec(memory_space=pltpu.SMEM)],
        out_specs=pl.BlockSpec((1, 1, QB_A, KB_A), lambda a, h: (a, h, 0, 0)),
        out_shape=jax.ShapeDtypeStruct((N_BIAS_TILES, N_HEADS_A, QB_A, KB_A), F32),
        compiler_params=_cparams(("arbitrary", "arbitrary")),
        name="t5_bias_tiles",
    )(rel_bias)


def _attn_a_kernel(qidx_ref, w_ref, qa_ref, kidx_ref, ka_ref, va_ref, bt_ref, gn_ref, o_ref,
                   keys_scr, keyt_scr, m_scr, acc_scr, s_scr, mb_scr, *, topk):
    qb = pl.program_id(1)
    t0 = qb * QB_A
    kbl = (t0 + QB_A - 1) // KB_A
    row = t0 + lax.broadcasted_iota(I32, (QB_A, KB_A), 0)
    col0 = lax.broadcasted_iota(I32, (QB_A, KB_A), 1)
    nt = (((1,), (1,)), ((), ()))

    reps = KB_A // LANES

    def sort_key(v):
        bits = pltpu.bitcast(v, I32)
        return jnp.where(bits < 0, bits ^ jnp.int32(0x7FFFFFFF), bits)

    def key_value(k):
        return pltpu.bitcast(jnp.where(k < 0, k ^ jnp.int32(0x7FFFFFFF), k), F32)

    key_pos = lax.broadcasted_iota(I32, (KB_A, QB_A), 0)
    qry_pos = t0 + lax.broadcasted_iota(I32, (KB_A, QB_A), 1)

    def score_tile(kb, carry):
        smin, smax = carry
        kt = kidx_ref[pl.ds(pl.multiple_of(kb * KB_A, KB_A), KB_A), :]
        score = jnp.zeros((KB_A, QB_A), F32)
        for hp in range(IDX_HEADS // 2):
            q2 = qidx_ref[2 * hp:2 * hp + 2].reshape(2 * QB_A, IDX_DIM)
            sc = lax.dot_general(kt, q2, nt, preferred_element_type=F32)
            score = score + w_ref[2 * hp:2 * hp + 1, :] * jnp.maximum(sc[:, :QB_A], 0.0)
            score = score + w_ref[2 * hp + 1:2 * hp + 2, :] * jnp.maximum(sc[:, QB_A:], 0.0)
        causal = (kb * KB_A + key_pos) <= qry_pos
        key_t = jnp.where(causal, sort_key(score), jnp.int32(INT_MIN))
        keyt_scr[kb] = key_t
        keys_scr[kb] = key_t.T
        smin = jnp.minimum(smin, jnp.min(jnp.where(causal, score, jnp.inf), axis=0, keepdims=True))
        smax = jnp.maximum(smax, jnp.max(jnp.where(causal, score, -jnp.inf), axis=0, keepdims=True))
        return smin, smax

    smin, smax = lax.fori_loop(0, kbl + 1, score_tile,
                               (jnp.full((1, QB_A), jnp.inf, F32), jnp.full((1, QB_A), -jnp.inf, F32)))

    qry1 = t0 + lax.broadcasted_iota(I32, (1, QB_A), 1)
    kf = float(topk)
    acc_rows = 32

    def search_pass(p, state):
        lo, hi, clo, chi, open_q = state
        lo_v = key_value(lo)
        hi_v = key_value(hi)
        gap = clo - chi
        frac = (clo - (kf - 0.5)) / gap
        frac = jnp.where(p % 2 == 1, 0.7 * frac + 0.15, frac)
        frac = jnp.where(gap > 16.0, frac, 0.5)
        cand = sort_key(lo_v + (hi_v - lo_v) * frac)
        cand = jnp.where(p % 8 == 7, lo + lax.shift_right_logical(hi - lo, 1), cand)
        cand = jnp.minimum(jnp.maximum(cand, lo + 1), hi - 1)

        def count(kb, cnt):
            ge = jnp.where(keyt_scr[kb] >= cand, 1.0, 0.0)
            return cnt + jnp.sum(ge.reshape(KB_A // acc_rows, acc_rows, QB_A), axis=0)

        cnt = lax.fori_loop(0, kbl + 1, count, jnp.zeros((acc_rows, QB_A), F32))
        tot = jnp.sum(cnt, axis=0, keepdims=True)
        ge = tot >= kf
        lo = jnp.where(ge, cand, lo)
        clo = jnp.where(ge, tot, clo)
        hi = jnp.where(ge, hi, cand)
        chi = jnp.where(ge, chi, tot)
        width = hi - lo
        settled = jnp.where(clo == kf, 1.0, jnp.where(width == 1, 1.0, 0.0))
        open_q = jnp.where(settled > 0.5, 0.0, open_q)
        return lo, hi, clo, chi, open_q

    def search_step(carry):
        p, state, _ = carry
        state = search_pass(p + 1, search_pass(p, state))
        return p + 2, state, jnp.max(state[4]) > 0.0

    n_causal = (qry1 + 1).astype(F32)
    open0 = jnp.where(qry1 >= topk, 1.0, 0.0)
    lo0 = sort_key(smin)
    hi0 = sort_key(smax) + 1
    open0 = jnp.where(hi0 - lo0 == 1, 0.0, open0)
    _, (lo, _, _, _, _), _ = lax.while_loop(
        lambda c: c[2], search_step,
        (jnp.int32(0), (lo0, hi0, n_causal, jnp.zeros((1, QB_A), F32), open0), jnp.max(open0) > 0.0))
    thr = jnp.where(qry1 >= topk, lo, jnp.int32(INT_MIN))
    thr = jnp.broadcast_to(thr, (QB_A, QB_A)).T
    thrb = jnp.tile(thr, (1, reps))

    m_scr[...] = jnp.full(m_scr.shape, NEG, F32)
    acc_scr[...] = jnp.zeros(acc_scr.shape, F32)

    def attend(kb, near):
        mb = jnp.where(keys_scr[kb] >= thrb, 0.0, NEG)
        if near:
            mb = jnp.where((kb * KB_A + col0) <= row, mb, NEG)
            di = (t0 - kb * KB_A) // LANES
        mb_scr[...] = mb
        start = pl.multiple_of(kb * KB_A, KB_A)
        kt = ka_ref[pl.ds(start, KB_A), :]
        vt = jnp.concatenate([va_ref[pl.ds(start, KB_A), :], jnp.ones((KB_A, LANES), BF16)], axis=1)
        for h in range(N_HEADS_A):
            q = qa_ref[:, h * HEAD_DIM:(h + 1) * HEAD_DIM]
            s_scr[h] = lax.dot_general(q, kt, nt, preferred_element_type=F32)
        for h in range(N_HEADS_A):
            s = s_scr[h] + mb_scr[...]
            if near:
                s = s + bt_ref[di, h]
            m_prev = m_scr[h]
            m_new = jnp.maximum(m_prev, jnp.max(s, axis=1, keepdims=True))
            alpha = jnp.exp(m_prev - m_new)
            p = jnp.exp(s - jnp.tile(m_new, (1, reps)))
            acc_scr[h] = (jnp.tile(alpha, (1, 2)) * acc_scr[h]
                          + jnp.dot(p.astype(BF16), vt, preferred_element_type=F32))
            m_scr[h] = m_new

    def far_tile(kb, carry):
        attend(kb, False)
        return carry

    lax.fori_loop(0, jnp.maximum(kbl - 1, 0), far_tile, 0)

    @pl.when(kbl >= 1)
    def _():
        attend(kbl - 1, True)

    attend(kbl, True)

    ssq = jnp.zeros((QB_A, LANES), F32)
    for h in range(N_HEADS_A):
        oh = acc_scr[h, :, :HEAD_DIM] / acc_scr[h, :, HEAD_DIM:]
        acc_scr[h, :, :HEAD_DIM] = oh
        ssq = ssq + jnp.sum(oh * oh, axis=1, keepdims=True)
    inv = lax.rsqrt(ssq * (1.0 / D_A) + EPS)
    for h in range(N_HEADS_A):
        sl = slice(h * HEAD_DIM, (h + 1) * HEAD_DIM)
        o_ref[:, sl] = (acc_scr[h, :, :HEAD_DIM] * inv * gn_ref[:, sl]).astype(o_ref.dtype)


def _attn_a(qidx, widx, qa, kidx, ka, va, bt, gn_a, batch, seq):
    n = qa.shape[0]
    nq = seq // QB_A
    nkt = seq // KB_A
    topk = min(TOPK_MAX, seq // 4)
    return pl.pallas_call(
        functools.partial(_attn_a_kernel, topk=topk),
        grid=(batch, nq),
        in_specs=[pl.BlockSpec((IDX_HEADS, QB_A, IDX_DIM), lambda b, q: (0, b * nq + q, 0)),
                  pl.BlockSpec((IDX_HEADS, QB_A), lambda b, q: (0, b * nq + q)),
                  pl.BlockSpec((QB_A, D_A), lambda b, q: (b * nq + q, 0)),
                  pl.BlockSpec((seq, IDX_DIM), lambda b, q: (b, 0)),
                  pl.BlockSpec((seq, HEAD_DIM), lambda b, q: (b, 0)),
                  pl.BlockSpec((seq, HEAD_DIM), lambda b, q: (b, 0)),
                  pl.BlockSpec(bt.shape, lambda b, q: (0, 0, 0, 0)),
                  pl.BlockSpec((1, D_A), lambda b, q: (0, 0))],
        out_specs=pl.BlockSpec((QB_A, D_A), lambda b, q: (b * nq + q, 0)),
        out_shape=jax.ShapeDtypeStruct((n, D_A), BF16),
        scratch_shapes=[pltpu.VMEM((nkt, QB_A, KB_A), I32),
                        pltpu.VMEM((nkt, KB_A, QB_A), I32),
                        pltpu.VMEM((N_HEADS_A, QB_A, LANES), F32),
                        pltpu.VMEM((N_HEADS_A, QB_A, 2 * HEAD_DIM), F32),
                        pltpu.VMEM((N_HEADS_A, QB_A, KB_A), F32),
                        pltpu.VMEM((QB_A, KB_A), F32)],
        compiler_params=_cparams(("parallel", "arbitrary")),
        name="dsa_attention",
    )(qidx, widx, qa, kidx, ka, va, bt, gn_a)


def _attn_b_kernel(q_ref, k_ref, v_ref, tri_ref, o_ref, rest_scr, z_scr, cs_scr):
    qb = pl.program_id(1)
    row = lax.broadcasted_iota(I32, (QB_B, QB_B), 0)
    col = lax.broadcasted_iota(I32, (QB_B, QB_B), 1)
    strict = col < row
    nt = (((1,), (1,)), ((), ()))
    scale = HEAD_DIM ** -0.5

    def step(kb, diag):
        start = pl.multiple_of(kb * QB_B, QB_B)
        heads = [slice(h * HEAD_DIM, (h + 1) * HEAD_DIM) for h in range(N_HEADS_B)]
        for h, sl in enumerate(heads):
            kt = k_ref[pl.ds(start, QB_B), sl]
            z_scr[h] = lax.dot_general(q_ref[:, sl], kt, nt, preferred_element_type=F32) * scale
        for h, sl in enumerate(heads):
            z = z_scr[h]
            sp = jnp.maximum(z, 0.0) + jnp.log(1.0 + jnp.exp(-jnp.abs(z)))
            if diag:
                sp = jnp.where(strict, sp, 0.0)
            hi = sp.astype(BF16)
            lo = (sp - hi.astype(F32)).astype(BF16)
            cs_scr[h] = jnp.dot(jnp.concatenate([hi, lo], axis=1), tri_ref[...],
                                preferred_element_type=F32)
        worst = None
        for h, sl in enumerate(heads):
            vt = v_ref[pl.ds(start, QB_B), sl]
            z = z_scr[h]
            cs = cs_scr[h, :, :QB_B]
            tot = cs_scr[h, :, QB_B:]
            if diag:
                a = jnp.where(strict, jnp.exp(z - cs), 0.0)
                o_ref[:, sl] = jnp.dot(a.astype(BF16), vt, preferred_element_type=F32)
                rest = tot
            else:
                rest = rest_scr[h]
                a = jnp.exp(z - cs - rest)
                o_ref[:, sl] += jnp.dot(a.astype(BF16), vt, preferred_element_type=F32)
                rest = rest + tot
            rest_scr[h] = rest
            worst = rest if worst is None else jnp.minimum(worst, rest)
        return jnp.min(worst)

    def more(kb, smallest):
        return jnp.logical_and(kb >= 0, smallest < EXP_UNDERFLOW)

    def body(carry):
        kb, _ = carry
        return kb - 1, more(kb - 1, step(kb, False))

    lax.while_loop(lambda c: c[1], body, (qb - 1, more(qb - 1, step(qb, True))))


def _attn_b(proj, tri, batch, seq):
    n = proj.shape[0]
    nq = seq // QB_B
    return pl.pallas_call(
        _attn_b_kernel,
        grid=(batch, nq),
        in_specs=[pl.BlockSpec((QB_B, D_B), lambda b, q: (b * nq + q, COL_QB // D_B)),
                  pl.BlockSpec((seq, D_B), lambda b, q: (b, COL_KB // D_B)),
                  pl.BlockSpec((seq, D_B), lambda b, q: (b, COL_VB // D_B)),
                  pl.BlockSpec(tri.shape, lambda b, q: (0, 0))],
        out_specs=pl.BlockSpec((QB_B, D_B), lambda b, q: (b * nq + q, 0)),
        out_shape=jax.ShapeDtypeStruct((n, D_B), F32),
        scratch_shapes=[pltpu.VMEM((N_HEADS_B, QB_B, QB_B), F32),
                        pltpu.VMEM((N_HEADS_B, QB_B, QB_B), F32),
                        pltpu.VMEM((N_HEADS_B, QB_B, 2 * QB_B), F32)],
        compiler_params=_cparams(("parallel", "arbitrary")),
        name="stick_breaking_attention",
    )(proj, proj, proj, tri)


def _out_proj_kernel(x_ref, oa_ref, ob_ref, gnb_ref, w_ref, g1_ref, ln_ref, sc_ref, sh_ref, rw_ref,
                     x1_ref, h2_ref, lg_ref):
    obn = (_rms(ob_ref[...]) * gnb_ref[...]).astype(BF16)
    y = jnp.dot(oa_ref[...], w_ref[0:D_A, :], preferred_element_type=F32)
    y = y + jnp.dot(obn, w_ref[D_A:D_A + D_B, :], preferred_element_type=F32)
    x1 = x_ref[...] + g1_ref[0] * y
    x1_ref[...] = x1
    h2 = _rms(x1) * ln_ref[...]
    h2 = h2 * (1.0 + sc_ref[0]) + sh_ref[0]
    h2_ref[...] = h2
    hh = h2.astype(BF16)
    hl = (h2 - hh.astype(F32)).astype(BF16)
    rw = rw_ref[...]
    rh = rw.astype(BF16)
    rl = (rw - rh.astype(F32)).astype(BF16)
    lg = jnp.dot(hh, rh, preferred_element_type=F32)
    lg = lg + jnp.dot(hh, rl, preferred_element_type=F32)
    lg = lg + jnp.dot(hl, rh, preferred_element_type=F32)
    lg_ref[...] = lg


def _out_proj(x2, oa, ob, gn_b, w_out, g1, ln_g, sc, sh, rw, seq):
    n, d = x2.shape
    tm = 256
    tpb = seq // tm
    row = lambda i: (i, 0)
    fixed = lambda i: (0, 0)
    perb = lambda i: (i // tpb, 0, 0)
    return pl.pallas_call(
        _out_proj_kernel,
        grid=(n // tm,),
        in_specs=[pl.BlockSpec((tm, d), row),
                  pl.BlockSpec((tm, D_A), row),
                  pl.BlockSpec((tm, D_B), row),
                  pl.BlockSpec((1, D_B), fixed),
                  pl.BlockSpec(w_out.shape, fixed),
                  pl.BlockSpec((1, 1, d), perb),
                  pl.BlockSpec((1, d), fixed),
                  pl.BlockSpec((1, 1, d), perb),
                  pl.BlockSpec((1, 1, d), perb),
                  pl.BlockSpec(rw.shape, fixed)],
        out_specs=[pl.BlockSpec((tm, d), row),
                   pl.BlockSpec((tm, d), row),
                   pl.BlockSpec((tm, LANES), row)],
        out_shape=[jax.ShapeDtypeStruct((n, d), F32),
                   jax.ShapeDtypeStruct((n, d), F32),
                   jax.ShapeDtypeStruct((n, LANES), F32)],
        compiler_params=_cparams(("parallel",)),
        name="out_proj_ln2_router",
    )(x2, oa, ob, gn_b, w_out, g1, ln_g, sc, sh, rw)


def _route_kernel(lg_ref, info_ref, cnt_ref):
    @pl.when(pl.program_id(0) == 0)
    def _():
        cnt_ref[...] = jnp.zeros(cnt_ref.shape, F32)

    lg = lg_ref[...]
    lane = lax.broadcasted_iota(I32, lg.shape, 1)
    lanef = lane.astype(F32)
    big = float(4 * LANES)
    gm = jnp.where(lane >= N_EXPERTS, jnp.where(lane < N_EXPERTS + N_GROUPS, 1.0, 0.0), 0.0) > 0.5
    lgm = jnp.where(gm, lg, NEG)
    mg = jnp.max(lgm, axis=1, keepdims=True)
    eg = jnp.where(gm, jnp.exp(lgm - mg), 0.0)
    pg = eg / jnp.sum(eg, axis=1, keepdims=True)
    gw = jnp.max(pg, axis=1, keepdims=True)
    gidx = jnp.min(jnp.where(gm, jnp.where(pg == gw, lanef - N_EXPERTS, big), big), axis=1, keepdims=True)
    lane_group = (lane // EXPERTS_PER_GROUP).astype(F32)
    em = jnp.where(lane < N_EXPERTS, jnp.where(lane_group == gidx, 1.0, 0.0), 0.0) > 0.5
    lem = jnp.where(em, lg, NEG)
    me = jnp.max(lem, axis=1, keepdims=True)
    ee = jnp.where(em, jnp.exp(lem - me), 0.0)
    pe = jnp.where(em, ee / jnp.sum(ee, axis=1, keepdims=True), -1.0)
    p1 = jnp.max(pe, axis=1, keepdims=True)
    i1 = jnp.min(jnp.where(pe == p1, lanef, big), axis=1, keepdims=True)
    pe2 = jnp.where(lanef == i1, -1.0, pe)
    p2 = jnp.max(pe2, axis=1, keepdims=True)
    i2 = jnp.min(jnp.where(pe2 == p2, lanef, big), axis=1, keepdims=True)
    den = p1 + p2
    g0 = gw * p1 / den
    g1 = gw * p2 / den
    info = jnp.where(lane == 0, i1, jnp.where(lane == 1, i2,
                     jnp.where(lane == 2, g0, jnp.where(lane == 3, g1, 0.0))))
    info_ref[...] = info
    oh = jnp.where(lanef == i1, 1.0, 0.0) + jnp.where(lanef == i2, 1.0, 0.0)
    cnt_ref[...] += jnp.sum(oh, axis=0, keepdims=True)


def _route(lg):
    n = lg.shape[0]
    tm = min(1024, n)
    return pl.pallas_call(
        _route_kernel,
        grid=(n // tm,),
        in_specs=[pl.BlockSpec((tm, LANES), lambda i: (i, 0))],
        out_specs=[pl.BlockSpec((tm, LANES), lambda i: (i, 0)),
                   pl.BlockSpec((1, LANES), lambda i: (0, 0))],
        out_shape=[jax.ShapeDtypeStruct((n, LANES), F32),
                   jax.ShapeDtypeStruct((1, LANES), F32)],
        compiler_params=_cparams(("arbitrary",)),
        name="moe_route",
    )(lg)


def _plan_kernel(cnt_ref, ps_ref, be_ref, nu_ref, *, nblk):
    def fill(i, c):
        be_ref[i] = N_EXPERTS - 1
        return c

    lax.fori_loop(0, nblk, fill, 0)

    def per_expert(e, pos):
        ps_ref[e] = pos * MOE_BLK
        nb = (cnt_ref[e] + MOE_BLK - 1) // MOE_BLK

        def mark(k, c):
            be_ref[pos + k] = e
            return c

        lax.fori_loop(0, nb, mark, 0)
        return pos + nb

    nu_ref[0] = lax.fori_loop(0, N_EXPERTS, per_expert, jnp.int32(0))


def _plan(counts, nblk):
    smem = pl.BlockSpec(memory_space=pltpu.SMEM)
    return pl.pallas_call(
        functools.partial(_plan_kernel, nblk=nblk),
        in_specs=[smem],
        out_specs=[smem, smem, smem],
        out_shape=[jax.ShapeDtypeStruct((N_EXPERTS,), I32),
                   jax.ShapeDtypeStruct((nblk,), I32),
                   jax.ShapeDtypeStruct((1,), I32)],
        name="moe_block_plan",
    )(counts)


def _dest_kernel(info_ref, ps_ref, tri_ref, o_ref, carry_scr):
    @pl.when(pl.program_id(0) == 0)
    def _():
        carry_scr[...] = jnp.zeros(carry_scr.shape, F32)

    info = info_ref[...]
    lane = lax.broadcasted_iota(I32, info.shape, 1)
    lanef = lane.astype(F32)
    o1 = jnp.where(lanef == info[:, 0:1], 1.0, 0.0)
    o2 = jnp.where(lanef == info[:, 1:2], 1.0, 0.0)
    oh = o1 + o2
    before = jnp.dot(tri_ref[...], oh.astype(BF16), preferred_element_type=F32)
    base = before + carry_scr[...] + ps_ref[...]
    d1 = jnp.sum(o1 * base, axis=1, keepdims=True)
    d2 = jnp.sum(o2 * base, axis=1, keepdims=True)
    o_ref[...] = jnp.where(lane == 0, d1, jnp.where(lane == 1, d2, 0.0))
    carry_scr[...] += jnp.sum(oh, axis=0, keepdims=True)


def _dest(info, ps_lanes, tri):
    n = info.shape[0]
    tm = tri.shape[0]
    return pl.pallas_call(
        _dest_kernel,
        grid=(n // tm,),
        in_specs=[pl.BlockSpec((tm, LANES), lambda i: (i, 0)),
                  pl.BlockSpec((1, LANES), lambda i: (0, 0)),
                  pl.BlockSpec((tm, tm), lambda i: (0, 0))],
        out_specs=pl.BlockSpec((tm, LANES), lambda i: (i, 0)),
        out_shape=jax.ShapeDtypeStruct((n, LANES), F32),
        scratch_shapes=[pltpu.VMEM((1, LANES), F32)],
        compiler_params=_cparams(("arbitrary",)),
        name="moe_dest_rows",
    )(info, ps_lanes, tri)


def _scatter_kernel(d0_ref, d1_ref, h_ref, xz_ref, xb_ref, sem):
    del xz_ref
    base = pl.program_id(0) * TOK_TILE

    def row_copy(r, dst_row):
        return pltpu.make_async_copy(h_ref.at[pl.ds(r, 1)], xb_ref.at[pl.ds(dst_row, 1)], sem)

    def issue(r, c):
        row_copy(r, d0_ref[base + r]).start()
        row_copy(r, d1_ref[base + r]).start()
        return c

    lax.fori_loop(0, TOK_TILE, issue, 0)

    def drain(r, c):
        row_copy(0, 0).wait()
        row_copy(0, 0).wait()
        return c

    lax.fori_loop(0, TOK_TILE, drain, 0)


def _scatter_rows(d0, d1, h2, xzero):
    n, d = h2.shape
    return pl.pallas_call(
        _scatter_kernel,
        grid_spec=pltpu.PrefetchScalarGridSpec(
            num_scalar_prefetch=2,
            grid=(n // TOK_TILE,),
            in_specs=[pl.BlockSpec((TOK_TILE, d), lambda i, a, b: (i, 0)),
                      pl.BlockSpec(memory_space=pl.ANY)],
            out_specs=pl.BlockSpec(memory_space=pl.ANY),
            scratch_shapes=[pltpu.SemaphoreType.DMA(())]),
        out_shape=jax.ShapeDtypeStruct(xzero.shape, xzero.dtype),
        input_output_aliases={3: 0},
        compiler_params=_cparams(("arbitrary",)),
        name="moe_scatter_rows",
    )(d0, d1, h2, xzero)


def _expert_kernel(be_ref, nu_ref, x_ref, wg_ref, wu_ref, wd_ref, o_ref):
    del be_ref
    i = pl.program_id(0)

    @pl.when(i < nu_ref[0])
    def _():
        x = x_ref[...].astype(BF16)
        g = jnp.dot(x, wg_ref[0], preferred_element_type=F32)
        u = jnp.dot(x, wu_ref[0], preferred_element_type=F32)
        act = (g * (1.0 / (1.0 + jnp.exp(-g))) * u).astype(BF16)
        o_ref[...] = jnp.dot(act, wd_ref[0], preferred_element_type=F32)

    @pl.when(i >= nu_ref[0])
    def _():
        o_ref[...] = jnp.zeros(o_ref.shape, o_ref.dtype)


def _experts(be, nu, xb, wg, wu, wd):
    p, d = xb.shape
    de = wg.shape[2]
    return pl.pallas_call(
        _expert_kernel,
        grid_spec=pltpu.PrefetchScalarGridSpec(
            num_scalar_prefetch=2,
            grid=(p // MOE_BLK,),
            in_specs=[pl.BlockSpec((MOE_BLK, d), lambda i, be, nu: (i, 0)),
                      pl.BlockSpec((1, d, de), lambda i, be, nu: (be[i], 0, 0)),
                      pl.BlockSpec((1, d, de), lambda i, be, nu: (be[i], 0, 0)),
                      pl.BlockSpec((1, de, d), lambda i, be, nu: (be[i], 0, 0))],
            out_specs=pl.BlockSpec((MOE_BLK, d), lambda i, be, nu: (i, 0))),
        out_shape=jax.ShapeDtypeStruct((p, d), F32),
        compiler_params=_cparams(("arbitrary",)),
        name="moe_expert_ffn",
    )(be, nu, xb, wg, wu, wd)


def _combine_kernel(d0_ref, d1_ref, x_ref, info_ref, g2_ref, yb_ref, o_ref, rows_scr, sem):
    base = pl.program_id(0) * TOK_TILE

    def row_copy(slot, r, src_row):
        return pltpu.make_async_copy(yb_ref.at[pl.ds(src_row, 1)], rows_scr.at[slot, pl.ds(r, 1)], sem)

    def issue(r, c):
        row_copy(0, r, d0_ref[base + r]).start()
        row_copy(1, r, d1_ref[base + r]).start()
        return c

    lax.fori_loop(0, TOK_TILE, issue, 0)

    def drain(r, c):
        row_copy(0, 0, 0).wait()
        row_copy(1, 0, 0).wait()
        return c

    lax.fori_loop(0, TOK_TILE, drain, 0)

    info = info_ref[...]
    y = info[:, 2:3] * rows_scr[0] + info[:, 3:4] * rows_scr[1]
    o_ref[...] = x_ref[...] + g2_ref[0] * y


def _combine(d0, d1, x1, info, g2, yb, seq):
    n, d = x1.shape
    tpb = seq // TOK_TILE
    return pl.pallas_call(
        _combine_kernel,
        grid_spec=pltpu.PrefetchScalarGridSpec(
            num_scalar_prefetch=2,
            grid=(n // TOK_TILE,),
            in_specs=[pl.BlockSpec((TOK_TILE, d), lambda i, a, b: (i, 0)),
                      pl.BlockSpec((TOK_TILE, LANES), lambda i, a, b: (i, 0)),
                      pl.BlockSpec((1, 1, d), lambda i, a, b: (i // tpb, 0, 0)),
                      pl.BlockSpec(memory_space=pl.ANY)],
            out_specs=pl.BlockSpec((TOK_TILE, d), lambda i, a, b: (i, 0)),
            scratch_shapes=[pltpu.VMEM((2, TOK_TILE, d), F32),
                            pltpu.SemaphoreType.DMA(())]),
        out_shape=jax.ShapeDtypeStruct((n, d), F32),
        compiler_params=_cparams(("arbitrary",)),
        name="moe_combine",
    )(d0, d1, x1, info, g2, yb)


def _tri_inclusive_rev(k):
    l = (np.arange(k)[:, None] >= np.arange(k)[None, :]).astype(np.float32)
    half = np.concatenate([l, np.ones((k, k), np.float32)], axis=1)
    return jnp.asarray(np.concatenate([half, half], axis=0), dtype=BF16)


def _tri_strict_lower(k):
    return jnp.asarray((np.arange(k)[None, :] < np.arange(k)[:, None]).astype(np.float32), dtype=BF16)


def kernel(x, c, w_mod, b_mod, ln1_g, w_in, q_norm_g, w_q_up, q_gain, k_gain, rel_bias, gn_a, gn_b,
           w_out, ln2_g, router_g, router_e, w_gate, w_up, w_down):
    batch, seq, d = x.shape
    n = batch * seq
    assert w_mod.shape[0] == 1 and d == D_A + D_B
    assert seq % KB_A == 0 and seq % TOK_TILE == 0 and n % 512 == 0
    x2 = x.reshape(n, d)

    c8 = jnp.pad(c, ((0, 8 - batch), (0, 0)))
    mod = _modulation(c8, w_mod.reshape(d, -1), b_mod.reshape(1, -1))[:batch]
    sh1, sc1, g1, sh2, sc2, g2 = [m.reshape(batch, 1, d) for m in jnp.split(mod, 6, axis=-1)]

    wi = w_in.reshape(d, -1)
    n_a = Q_RANK + 2 * HEAD_DIM + IDX_DIM + IDX_HEADS
    wi = jnp.concatenate([wi[:, n_a:], wi[:, :n_a],
                          jnp.zeros((d, D_IN_PAD - wi.shape[1]), wi.dtype)], axis=1).astype(BF16)
    proj = _ln_proj(x2, ln1_g.reshape(1, d), sc1, sh1, wi, seq)

    qa, qidx, ka, va, kidx, widx = _aprep(
        proj, q_norm_g.reshape(1, -1), w_q_up.reshape(Q_RANK, -1).astype(BF16),
        q_gain.reshape(1, -1), k_gain.reshape(1, -1))

    bt = _bias_tiles(rel_bias)
    oa = _attn_a(qidx, widx.T, qa, kidx, ka, va, bt, gn_a.reshape(1, -1), batch, seq)
    ob = _attn_b(proj, _tri_inclusive_rev(QB_B), batch, seq)

    rw = jnp.concatenate([router_e.reshape(d, -1), router_g.reshape(d, -1),
                          jnp.zeros((d, LANES - N_EXPERTS - N_GROUPS), F32)], axis=1)
    x1, h2, lg = _out_proj(x2, oa, ob, gn_b.reshape(1, -1), w_out.reshape(d, d).astype(BF16),
                           g1, ln2_g.reshape(1, d), sc2, sh2, rw, seq)

    info, cnt = _route(lg)
    counts = cnt[0, :N_EXPERTS].astype(I32)
    p_rows = 2 * n + N_EXPERTS * MOE_BLK
    ps, be, nu = _plan(counts, p_rows // MOE_BLK)
    ps_lanes = jnp.pad(ps.astype(F32), (0, LANES - N_EXPERTS)).reshape(1, LANES)
    dinfo = _dest(info, ps_lanes, _tri_strict_lower(512))
    d0 = dinfo[:, 0].astype(I32)
    d1 = dinfo[:, 1].astype(I32)
    xb = _scatter_rows(d0, d1, h2, jnp.zeros((p_rows, d), F32))
    yb = _experts(be, nu, xb,
                  w_gate.reshape(N_EXPERTS, d, D_EXPERT).astype(BF16),
                  w_up.reshape(N_EXPERTS, d, D_EXPERT).astype(BF16),
                  w_down.reshape(N_EXPERTS, D_EXPERT, d).astype(BF16))
    out = _combine(d0, d1, x1, info, g2, yb, seq)
    return out.reshape(batch, seq, d)
```

```python
import functools
import math

import numpy as np
import jax
import jax.numpy as jnp
from jax import lax
from jax.experimental import pallas as pl
from jax.experimental.pallas import tpu as pltpu

F32 = jnp.float32
BF16 = jnp.bfloat16
I32 = jnp.int32

HEAD_DIM = 128
N_HEADS_A = 8
N_HEADS_B = 8
D_A = N_HEADS_A * HEAD_DIM
D_B = N_HEADS_B * HEAD_DIM
Q_RANK = 512
IDX_HEADS = 16
IDX_DIM = 64
TOPK_MAX = 256
N_BUCKETS = 32
MAX_DISTANCE = 128
N_GROUPS = 4
EXPERTS_PER_GROUP = 8
N_EXPERTS = N_GROUPS * EXPERTS_PER_GROUP
D_EXPERT = 512
EPS = 1e-6

LANES = 128
SUBLANES = 8
VMEM_LIMIT = 56 * 1024 * 1024
NEG = -1e30
INT_MIN = -(2 ** 31)
EXP_UNDERFLOW = 104.0
LOG2E = math.log2(math.e)
PLAIN_SOFTMAX_LIMIT = 64.0

QB_A = 128
KB_A = 256
QB_B = 128
MOE_BLK = 256
TOK_TILE = 256

COL_QB, COL_KB, COL_VB, COL_A = 0, D_B, 2 * D_B, 3 * D_B
A_CQ, A_KA, A_VA, A_KIDX, A_WIDX = 0, 512, 640, 768, 832
A_WIDTH = 1024
D_IN_PAD = COL_A + A_WIDTH


def _cparams(sem=None):
    return pltpu.CompilerParams(dimension_semantics=sem, vmem_limit_bytes=VMEM_LIMIT)


def _rms(x):
    return x * lax.rsqrt(jnp.mean(x * x, axis=-1, keepdims=True) + EPS)


def _mod_kernel(c_ref, w_ref, b_ref, o_ref):
    c = c_ref[...]
    s = c * (1.0 / (1.0 + jnp.exp(-c)))
    o_ref[...] = jnp.dot(s, w_ref[...], preferred_element_type=F32,
                         precision=lax.Precision.HIGHEST) + b_ref[...]


def _modulation(c8, w_mod, b_mod):
    d, n6 = w_mod.shape
    tn = 1024
    return pl.pallas_call(
        _mod_kernel,
        grid=(n6 // tn,),
        in_specs=[pl.BlockSpec((8, d), lambda j: (0, 0)),
                  pl.BlockSpec((d, tn), lambda j: (0, j)),
                  pl.BlockSpec((1, tn), lambda j: (0, j))],
        out_specs=pl.BlockSpec((8, tn), lambda j: (0, j)),
        out_shape=jax.ShapeDtypeStruct((8, n6), F32),
        compiler_params=_cparams(("arbitrary",)),
        name="modulation",
    )(c8, w_mod, b_mod)


def _ln_proj_kernel(x_ref, g_ref, sc_ref, sh_ref, w_ref, o_ref, h_scr):
    @pl.when(pl.program_id(1) == 0)
    def _():
        h = _rms(x_ref[...]) * g_ref[...]
        h = h * (1.0 + sc_ref[0]) + sh_ref[0]
        h_scr[...] = h.astype(BF16)

    o_ref[...] = jnp.dot(h_scr[...], w_ref[...], preferred_element_type=F32).astype(o_ref.dtype)


def _ln_proj(x2, ln_g, sc, sh, w, seq):
    n, d = x2.shape
    ncol = w.shape[1]
    tm = min(1024, seq)
    tn = 512
    tpb = seq // tm
    return pl.pallas_call(
        _ln_proj_kernel,
        grid=(n // tm, ncol // tn),
        in_specs=[pl.BlockSpec((tm, d), lambda i, j: (i, 0)),
                  pl.BlockSpec((1, d), lambda i, j: (0, 0)),
                  pl.BlockSpec((1, 1, d), lambda i, j: (i // tpb, 0, 0)),
                  pl.BlockSpec((1, 1, d), lambda i, j: (i // tpb, 0, 0)),
                  pl.BlockSpec((d, tn), lambda i, j: (0, j))],
        out_specs=pl.BlockSpec((tm, tn), lambda i, j: (i, j)),
        out_shape=jax.ShapeDtypeStruct((n, ncol), BF16),
        scratch_shapes=[pltpu.VMEM((tm, d), BF16)],
        compiler_params=_cparams(("parallel", "arbitrary")),
        name="ln_in_proj",
    )(x2, ln_g, sc, sh, w)


def _aprep_kernel(a_ref, qng_ref, wq_ref, qg_ref, kg_ref,
                  qa_ref, qidx_ref, ka_ref, va_ref, kidx_ref, widx_ref):
    cq = a_ref[:, A_CQ:A_CQ + Q_RANK].astype(F32)
    cqn = (_rms(cq) * qng_ref[...]).astype(BF16)
    qup = jnp.dot(cqn, wq_ref[...], preferred_element_type=F32)
    for h in range(N_HEADS_A):
        qh = qup[:, h * HEAD_DIM:(h + 1) * HEAD_DIM]
        qn = _rms(qh) * qg_ref[...] * (HEAD_DIM ** -0.5 * LOG2E)
        qa_ref[:, h * HEAD_DIM:(h + 1) * HEAD_DIM] = qn.astype(BF16)
    for h in range(IDX_HEADS):
        qi = qup[:, D_A + h * IDX_DIM:D_A + (h + 1) * IDX_DIM] * (IDX_DIM ** -0.5)
        qidx_ref[h] = qi.astype(BF16)
    ka = a_ref[:, A_KA:A_KA + HEAD_DIM].astype(F32)
    ka_ref[...] = (_rms(ka) * kg_ref[...]).astype(BF16)
    va_ref[...] = a_ref[:, A_VA:A_VA + HEAD_DIM]
    kidx_ref[...] = a_ref[:, A_KIDX:A_KIDX + IDX_DIM]
    widx_ref[...] = a_ref[:, A_WIDX:A_WIDX + IDX_HEADS].astype(F32) * (IDX_HEADS ** -0.5)


def _aprep(proj, q_norm_g, w_q_up, q_gain, k_gain):
    n = proj.shape[0]
    tm = 512
    cblk = COL_A // A_WIDTH
    nup = w_q_up.shape[1]
    return pl.pallas_call(
        _aprep_kernel,
        grid=(n // tm,),
        in_specs=[pl.BlockSpec((tm, A_WIDTH), lambda i: (i, cblk)),
                  pl.BlockSpec((1, Q_RANK), lambda i: (0, 0)),
                  pl.BlockSpec((Q_RANK, nup), lambda i: (0, 0)),
                  pl.BlockSpec((1, HEAD_DIM), lambda i: (0, 0)),
                  pl.BlockSpec((1, HEAD_DIM), lambda i: (0, 0))],
        out_specs=[pl.BlockSpec((tm, D_A), lambda i: (i, 0)),
                   pl.BlockSpec((IDX_HEADS, tm, IDX_DIM), lambda i: (0, i, 0)),
                   pl.BlockSpec((tm, HEAD_DIM), lambda i: (i, 0)),
                   pl.BlockSpec((tm, HEAD_DIM), lambda i: (i, 0)),
                   pl.BlockSpec((tm, IDX_DIM), lambda i: (i, 0)),
                   pl.BlockSpec((tm, IDX_HEADS), lambda i: (i, 0))],
        out_shape=[jax.ShapeDtypeStruct((n, D_A), BF16),
                   jax.ShapeDtypeStruct((IDX_HEADS, n, IDX_DIM), BF16),
                   jax.ShapeDtypeStruct((n, HEAD_DIM), BF16),
                   jax.ShapeDtypeStruct((n, HEAD_DIM), BF16),
                   jax.ShapeDtypeStruct((n, IDX_DIM), BF16),
                   jax.ShapeDtypeStruct((n, IDX_HEADS), F32)],
        compiler_params=_cparams(("parallel",)),
        name="group_a_prep",
    )(proj, q_norm_g, w_q_up, q_gain, k_gain)


def _t5_bucket_starts():
    max_exact = N_BUCKETS // 2
    d = np.arange(0, 4 * MAX_DISTANCE, dtype=np.int64)
    df = np.maximum(d, 1).astype(np.float32)
    large = max_exact + (np.log(df / np.float32(max_exact)) / np.float32(math.log(MAX_DISTANCE / max_exact))
                         * np.float32(N_BUCKETS - max_exact)).astype(np.int32)
    large = np.minimum(large, N_BUCKETS - 1)
    bucket = np.where(d < max_exact, d, large)
    assert np.all(np.diff(bucket) >= 0) and bucket[-1] == N_BUCKETS - 1
    return [int(np.argmax(bucket >= b)) for b in range(N_BUCKETS)]


_BUCKET_START = _t5_bucket_starts()
N_BIAS_TILES = 2 * KB_A // LANES


def _bias_kernel(rb_ref, o_ref):
    di = pl.program_id(0)
    h = pl.program_id(1)
    i = lax.broadcasted_iota(I32, (QB_A, KB_A), 0)
    j = lax.broadcasted_iota(I32, (QB_A, KB_A), 1)
    d = di * LANES + i - j
    val = jnp.full((QB_A, KB_A), rb_ref[0, h], F32)
    for b in range(1, N_BUCKETS):
        val = jnp.where(d >= _BUCKET_START[b], rb_ref[b, h], val)
    o_ref[0, 0] = (val - rb_ref[N_BUCKETS - 1, h]) * LOG2E


def _bias_tiles(rel_bias):
    return pl.pallas_call(
        _bias_kernel,
        grid=(N_BIAS_TILES, N_HEADS_A),
        in_specs=[pl.BlockSpec(memory_space=pltpu.SMEM)],
        out_specs=pl.BlockSpec((1, 1, QB_A, KB_A), lambda a, h: (a, h, 0, 0)),
        out_shape=jax.ShapeDtypeStruct((N_BIAS_TILES, N_HEADS_A, QB_A, KB_A), F32),
        compiler_params=_cparams(("arbitrary", "arbitrary")),
        name="t5_bias_tiles",
    )(rel_bias)


def _bound_kernel(qg_ref, kg_ref, rb_ref, o_ref):
    qmax = jnp.max(jnp.abs(qg_ref[...]), axis=1, keepdims=True)
    kmax = jnp.max(jnp.abs(kg_ref[...]), axis=1, keepdims=True)
    rb = rb_ref[...]
    shifted = jnp.abs(rb - rb[N_BUCKETS - 1:N_BUCKETS, :])
    bmax = jnp.max(jnp.max(shifted, axis=1, keepdims=True), axis=0, keepdims=True)
    bound = qmax * kmax * (math.sqrt(HEAD_DIM) * 1.02) + bmax
    o_ref[...] = jnp.broadcast_to(bound, o_ref.shape)


def _logit_bound(q_gain, k_gain, rel_bias):
    return pl.pallas_call(
        _bound_kernel,
        out_shape=jax.ShapeDtypeStruct((1, LANES), F32),
        name="dsa_logit_bound",
    )(q_gain, k_gain, rel_bias)


def _attn_a_kernel(qidx_ref, w_ref, qa_ref, kidx_ref, ka_ref, va_ref, bt_ref, gn_ref, bound_ref, o_ref,
                   keys_scr, keyt_scr, m_scr, acc_scr, s_scr, mb_scr, *, topk):
    qb = pl.program_id(1)
    t0 = qb * QB_A
    kbl = (t0 + QB_A - 1) // KB_A
    row = t0 + lax.broadcasted_iota(I32, (QB_A, KB_A), 0)
    col0 = lax.broadcasted_iota(I32, (QB_A, KB_A), 1)
    nt = (((1,), (1,)), ((), ()))

    reps = KB_A // LANES

    def sort_key(v):
        bits = pltpu.bitcast(v, I32)
        return jnp.where(bits < 0, bits ^ jnp.int32(0x7FFFFFFF), bits)

    def key_value(k):
        return pltpu.bitcast(jnp.where(k < 0, k ^ jnp.int32(0x7FFFFFFF), k), F32)

    key_pos = lax.broadcasted_iota(I32, (KB_A, QB_A), 0)
    qry_pos = t0 + lax.broadcasted_iota(I32, (KB_A, QB_A), 1)

    def score_tile(kb, carry):
        smin, smax = carry
        kt = kidx_ref[pl.ds(pl.multiple_of(kb * KB_A, KB_A), KB_A), :]
        score = jnp.zeros((KB_A, QB_A), F32)
        for hp in range(IDX_HEADS // 2):
            q2 = qidx_ref[2 * hp:2 * hp + 2].reshape(2 * QB_A, IDX_DIM)
            sc = lax.dot_general(kt, q2, nt, preferred_element_type=F32)
            score = score + w_ref[2 * hp:2 * hp + 1, :] * jnp.maximum(sc[:, :QB_A], 0.0)
            score = score + w_ref[2 * hp + 1:2 * hp + 2, :] * jnp.maximum(sc[:, QB_A:], 0.0)
        causal = (kb * KB_A + key_pos) <= qry_pos
        key_t = jnp.where(causal, sort_key(score), jnp.int32(INT_MIN))
        keyt_scr[kb] = key_t
        keys_scr[kb] = key_t.T
        smin = jnp.minimum(smin, jnp.min(jnp.where(causal, score, jnp.inf), axis=0, keepdims=True))
        smax = jnp.maximum(smax, jnp.max(jnp.where(causal, score, -jnp.inf), axis=0, keepdims=True))
        return smin, smax

    def score_pair(i, carry):
        return score_tile(2 * i + 1, score_tile(2 * i, carry))

    n_tiles = kbl + 1
    extremes = lax.fori_loop(0, n_tiles // 2, score_pair,
                             (jnp.full((1, QB_A), jnp.inf, F32), jnp.full((1, QB_A), -jnp.inf, F32)))
    smin, smax = lax.cond(n_tiles % 2 == 1, lambda c: score_tile(n_tiles - 1, c), lambda c: c, extremes)

    qry1 = t0 + lax.broadcasted_iota(I32, (1, QB_A), 1)
    kf = float(topk)
    acc_rows = 32

    def search_pass(p, state):
        lo, hi, clo, chi, open_q = state
        lo_v = key_value(lo)
        hi_v = key_value(hi)
        gap = clo - chi
        frac = (clo - (kf - 0.5)) / gap
        frac = jnp.where(p % 2 == 1, 0.7 * frac + 0.15, frac)
        frac = jnp.where(gap > 16.0, frac, 0.5)
        cand = sort_key(lo_v + (hi_v - lo_v) * frac)
        cand = jnp.where(p % 8 == 7, lo + lax.shift_right_logical(hi - lo, 1), cand)
        cand = jnp.minimum(jnp.maximum(cand, lo + 1), hi - 1)

        def count(kb, cnt):
            ge = jnp.where(keyt_scr[kb] >= cand, 1.0, 0.0)
            return cnt + jnp.sum(ge.reshape(KB_A // acc_rows, acc_rows, QB_A), axis=0)

        cnt = lax.fori_loop(0, kbl + 1, count, jnp.zeros((acc_rows, QB_A), F32))
        tot = jnp.sum(cnt, axis=0, keepdims=True)
        ge = tot >= kf
        lo = jnp.where(ge, cand, lo)
        clo = jnp.where(ge, tot, clo)
        hi = jnp.where(ge, hi, cand)
        chi = jnp.where(ge, chi, tot)
        width = hi - lo
        settled = jnp.where(clo == kf, 1.0, jnp.where(width == 1, 1.0, 0.0))
        open_q = jnp.where(settled > 0.5, 0.0, open_q)
        return lo, hi, clo, chi, open_q

    def search_step(carry):
        p, state, _ = carry
        state = search_pass(p + 1, search_pass(p, state))
        return p + 2, state, jnp.max(state[4]) > 0.0

    n_causal = (qry1 + 1).astype(F32)
    open0 = jnp.where(qry1 >= topk, 1.0, 0.0)
    lo0 = sort_key(smin)
    hi0 = sort_key(smax) + 1
    open0 = jnp.where(hi0 - lo0 == 1, 0.0, open0)
    _, (lo, _, _, _, _), _ = lax.while_loop(
        lambda c: c[2], search_step,
        (jnp.int32(0), (lo0, hi0, n_causal, jnp.zeros((1, QB_A), F32), open0), jnp.max(open0) > 0.0))
    thr = jnp.where(qry1 >= topk, lo, jnp.int32(INT_MIN))
    thr = jnp.broadcast_to(thr, (QB_A, QB_A)).T
    thrb = jnp.tile(thr, (1, reps))

    m_scr[...] = jnp.full(m_scr.shape, NEG, F32)
    acc_scr[...] = jnp.zeros(acc_scr.shape, F32)

    plain = bound_ref[0] <= PLAIN_SOFTMAX_LIMIT

    def attend(kb, near, online):
        mb = jnp.where(keys_scr[kb] >= thrb, 0.0, NEG)
        if near:
            mb = jnp.where((kb * KB_A + col0) <= row, mb, NEG)
            di = (t0 - kb * KB_A) // LANES
        mb_scr[...] = mb
        start = pl.multiple_of(kb * KB_A, KB_A)
        kt = ka_ref[pl.ds(start, KB_A), :]
        vt = jnp.concatenate([va_ref[pl.ds(start, KB_A), :], jnp.ones((KB_A, LANES), BF16)], axis=1)
        for h in range(N_HEADS_A):
            q = qa_ref[:, h * HEAD_DIM:(h + 1) * HEAD_DIM]
            s_scr[h] = lax.dot_general(q, kt, nt, preferred_element_type=F32)
        for h in range(N_HEADS_A):
            s = s_scr[h] + mb_scr[...]
            if near:
                s = s + bt_ref[di, h]
            if online:
                m_prev = m_scr[h]
                m_new = jnp.maximum(m_prev, jnp.max(s, axis=1, keepdims=True))
                alpha = jnp.exp2(m_prev - m_new)
                p = jnp.exp2(s - jnp.tile(m_new, (1, reps)))
                acc_scr[h] = (jnp.tile(alpha, (1, 2)) * acc_scr[h]
                              + jnp.dot(p.astype(BF16), vt, preferred_element_type=F32))
                m_scr[h] = m_new
            else:
                acc_scr[h] += jnp.dot(jnp.exp2(s).astype(BF16), vt, preferred_element_type=F32)

    def attend_all(online):
        n_far = jnp.maximum(kbl - 1, 0)

        def far_pair(i, carry):
            attend(2 * i, False, online)
            attend(2 * i + 1, False, online)
            return carry

        lax.fori_loop(0, n_far // 2, far_pair, 0)

        @pl.when(n_far % 2 == 1)
        def _():
            attend(n_far - 1, False, online)

        @pl.when(kbl >= 1)
        def _():
            attend(kbl - 1, True, online)

        attend(kbl, True, online)

    @pl.when(plain)
    def _():
        attend_all(False)

    @pl.when(jnp.logical_not(plain))
    def _():
        attend_all(True)

    ssq = jnp.zeros((QB_A, LANES), F32)
    for h in range(N_HEADS_A):
        oh = acc_scr[h, :, :HEAD_DIM] / acc_scr[h, :, HEAD_DIM:]
        acc_scr[h, :, :HEAD_DIM] = oh
        ssq = ssq + jnp.sum(oh * oh, axis=1, keepdims=True)
    inv = lax.rsqrt(ssq * (1.0 / D_A) + EPS)
    for h in range(N_HEADS_A):
        sl = slice(h * HEAD_DIM, (h + 1) * HEAD_DIM)
        o_ref[:, sl] = (acc_scr[h, :, :HEAD_DIM] * inv * gn_ref[:, sl]).astype(o_ref.dtype)


def _attn_a(qidx, widx, qa, kidx, ka, va, bt, gn_a, bound, batch, seq):
    n = qa.shape[0]
    nq = seq // QB_A
    nkt = seq // KB_A
    topk = min(TOPK_MAX, seq // 4)
    return pl.pallas_call(
        functools.partial(_attn_a_kernel, topk=topk),
        grid=(batch, nq),
        in_specs=[pl.BlockSpec((IDX_HEADS, QB_A, IDX_DIM), lambda b, q: (0, b * nq + q, 0)),
                  pl.BlockSpec((IDX_HEADS, QB_A), lambda b, q: (0, b * nq + q)),
                  pl.BlockSpec((QB_A, D_A), lambda b, q: (b * nq + q, 0)),
                  pl.BlockSpec((seq, IDX_DIM), lambda b, q: (b, 0)),
                  pl.BlockSpec((seq, HEAD_DIM), lambda b, q: (b, 0)),
                  pl.BlockSpec((seq, HEAD_DIM), lambda b, q: (b, 0)),
                  pl.BlockSpec(bt.shape, lambda b, q: (0, 0, 0, 0)),
                  pl.BlockSpec((1, D_A), lambda b, q: (0, 0)),
                  pl.BlockSpec(memory_space=pltpu.SMEM)],
        out_specs=pl.BlockSpec((QB_A, D_A), lambda b, q: (b * nq + q, 0)),
        out_shape=jax.ShapeDtypeStruct((n, D_A), BF16),
        scratch_shapes=[pltpu.VMEM((nkt, QB_A, KB_A), I32),
                        pltpu.VMEM((nkt, KB_A, QB_A), I32),
                        pltpu.VMEM((N_HEADS_A, QB_A, LANES), F32),
                        pltpu.VMEM((N_HEADS_A, QB_A, 2 * HEAD_DIM), F32),
                        pltpu.VMEM((N_HEADS_A, QB_A, KB_A), F32),
                        pltpu.VMEM((QB_A, KB_A), F32)],
        compiler_params=_cparams(("parallel", "arbitrary")),
        name="dsa_attention",
    )(qidx, widx, qa, kidx, ka, va, bt, gn_a, bound)


def _attn_b_kernel(q_ref, k_ref, v_ref, tri_ref, o_ref, rest_scr, z_scr, cs_scr):
    qb = pl.program_id(1)
    row = lax.broadcasted_iota(I32, (QB_B, QB_B), 0)
    col = lax.broadcasted_iota(I32, (QB_B, QB_B), 1)
    strict = col < row
    nt = (((1,), (1,)), ((), ()))
    scale = HEAD_DIM ** -0.5

    def step(kb, diag):
        start = pl.multiple_of(kb * QB_B, QB_B)
        heads = [slice(h * HEAD_DIM, (h + 1) * HEAD_DIM) for h in range(N_HEADS_B)]
        for h, sl in enumerate(heads):
            kt = k_ref[pl.ds(start, QB_B), sl]
            z_scr[h] = lax.dot_general(q_ref[:, sl], kt, nt, preferred_element_type=F32) * scale
        for h, sl in enumerate(heads):
            z = z_scr[h]
            sp = jnp.maximum(z, 0.0) + jnp.log(1.0 + jnp.exp(-jnp.abs(z)))
            if diag:
                sp = jnp.where(strict, sp, 0.0)
            hi = sp.astype(BF16)
            lo = (sp - hi.astype(F32)).astype(BF16)
            cs_scr[h] = jnp.dot(jnp.concatenate([hi, lo], axis=1), tri_ref[...],
                                preferred_element_type=F32)
        worst = None
        for h, sl in enumerate(heads):
            vt = v_ref[pl.ds(start, QB_B), sl]
            z = z_scr[h]
            cs = cs_scr[h, :, :QB_B]
            tot = cs_scr[h, :, QB_B:]
            if diag:
                a = jnp.where(strict, jnp.exp(z - cs), 0.0)
                o_ref[:, sl] = jnp.dot(a.astype(BF16), vt, preferred_element_type=F32)
                rest = tot
            else:
                rest = rest_scr[h]
                a = jnp.exp(z - cs - rest)
                o_ref[:, sl] += jnp.dot(a.astype(BF16), vt, preferred_element_type=F32)
                rest = rest + tot
            rest_scr[h] = rest
            worst = rest if worst is None else jnp.minimum(worst, rest)
        return jnp.min(worst)

    def more(kb, smallest):
        return jnp.logical_and(kb >= 0, smallest < EXP_UNDERFLOW)

    def body(carry):
        kb, _ = carry
        return kb - 1, more(kb - 1, step(kb, False))

    lax.while_loop(lambda c: c[1], body, (qb - 1, more(qb - 1, step(qb, True))))


def _attn_b(proj, tri, batch, seq):
    n = proj.shape[0]
    nq = seq // QB_B
    return pl.pallas_call(
        _attn_b_kernel,
        grid=(batch, nq),
        in_specs=[pl.BlockSpec((QB_B, D_B), lambda b, q: (b * nq + q, COL_QB // D_B)),
                  pl.BlockSpec((seq, D_B), lambda b, q: (b, COL_KB // D_B)),
                  pl.BlockSpec((seq, D_B), lambda b, q: (b, COL_VB // D_B)),
                  pl.BlockSpec(tri.shape, lambda b, q: (0, 0))],
        out_specs=pl.BlockSpec((QB_B, D_B), lambda b, q: (b * nq + q, 0)),
        out_shape=jax.ShapeDtypeStruct((n, D_B), F32),
        scratch_shapes=[pltpu.VMEM((N_HEADS_B, QB_B, QB_B), F32),
                        pltpu.VMEM((N_HEADS_B, QB_B, QB_B), F32),
                        pltpu.VMEM((N_HEADS_B, QB_B, 2 * QB_B), F32)],
        compiler_params=_cparams(("parallel", "arbitrary")),
        name="stick_breaking_attention",
    )(proj, proj, proj, tri)


def _out_proj_kernel(x_ref, oa_ref, ob_ref, gnb_ref, w_ref, g1_ref, ln_ref, sc_ref, sh_ref, rw_ref,
                     x1_ref, h2_ref, lg_ref):
    obn = (_rms(ob_ref[...]) * gnb_ref[...]).astype(BF16)
    y = jnp.dot(oa_ref[...], w_ref[0:D_A, :], preferred_element_type=F32)
    y = y + jnp.dot(obn, w_ref[D_A:D_A + D_B, :], preferred_element_type=F32)
    x1 = x_ref[...] + g1_ref[0] * y
    x1_ref[...] = x1
    h2 = _rms(x1) * ln_ref[...]
    h2 = h2 * (1.0 + sc_ref[0]) + sh_ref[0]
    h2_ref[...] = h2
    hh = h2.astype(BF16)
    hl = (h2 - hh.astype(F32)).astype(BF16)
    rw = rw_ref[...]
    rh = rw.astype(BF16)
    rl = (rw - rh.astype(F32)).astype(BF16)
    lg = jnp.dot(hh, rh, preferred_element_type=F32)
    lg = lg + jnp.dot(hh, rl, preferred_element_type=F32)
    lg = lg + jnp.dot(hl, rh, preferred_element_type=F32)
    lg_ref[...] = lg


def _out_proj(x2, oa, ob, gn_b, w_out, g1, ln_g, sc, sh, rw, seq):
    n, d = x2.shape
    tm = 256
    tpb = seq // tm
    row = lambda i: (i, 0)
    fixed = lambda i: (0, 0)
    perb = lambda i: (i // tpb, 0, 0)
    return pl.pallas_call(
        _out_proj_kernel,
        grid=(n // tm,),
        in_specs=[pl.BlockSpec((tm, d), row),
                  pl.BlockSpec((tm, D_A), row),
                  pl.BlockSpec((tm, D_B), row),
                  pl.BlockSpec((1, D_B), fixed),
                  pl.BlockSpec(w_out.shape, fixed),
                  pl.BlockSpec((1, 1, d), perb),
                  pl.BlockSpec((1, d), fixed),
                  pl.BlockSpec((1, 1, d), perb),
                  pl.BlockSpec((1, 1, d), perb),
                  pl.BlockSpec(rw.shape, fixed)],
        out_specs=[pl.BlockSpec((tm, d), row),
                   pl.BlockSpec((tm, d), row),
                   pl.BlockSpec((tm, LANES), row)],
        out_shape=[jax.ShapeDtypeStruct((n, d), F32),
                   jax.ShapeDtypeStruct((n, d), F32),
                   jax.ShapeDtypeStruct((n, LANES), F32)],
        compiler_params=_cparams(("parallel",)),
        name="out_proj_ln2_router",
    )(x2, oa, ob, gn_b, w_out, g1, ln_g, sc, sh, rw)


def _route_kernel(lg_ref, info_ref, cnt_ref):
    @pl.when(pl.program_id(0) == 0)
    def _():
        cnt_ref[...] = jnp.zeros(cnt_ref.shape, F32)

    lg = lg_ref[...]
    lane = lax.broadcasted_iota(I32, lg.shape, 1)
    lanef = lane.astype(F32)
    big = float(4 * LANES)
    gm = jnp.where(lane >= N_EXPERTS, jnp.where(lane < N_EXPERTS + N_GROUPS, 1.0, 0.0), 0.0) > 0.5
    lgm = jnp.where(gm, lg, NEG)
    mg = jnp.max(lgm, axis=1, keepdims=True)
    eg = jnp.where(gm, jnp.exp(lgm - mg), 0.0)
    pg = eg / jnp.sum(eg, axis=1, keepdims=True)
    gw = jnp.max(pg, axis=1, keepdims=True)
    gidx = jnp.min(jnp.where(gm, jnp.where(pg == gw, lanef - N_EXPERTS, big), big), axis=1, keepdims=True)
    lane_group = (lane // EXPERTS_PER_GROUP).astype(F32)
    em = jnp.where(lane < N_EXPERTS, jnp.where(lane_group == gidx, 1.0, 0.0), 0.0) > 0.5
    lem = jnp.where(em, lg, NEG)
    me = jnp.max(lem, axis=1, keepdims=True)
    ee = jnp.where(em, jnp.exp(lem - me), 0.0)
    pe = jnp.where(em, ee / jnp.sum(ee, axis=1, keepdims=True), -1.0)
    p1 = jnp.max(pe, axis=1, keepdims=True)
    i1 = jnp.min(jnp.where(pe == p1, lanef, big), axis=1, keepdims=True)
    pe2 = jnp.where(lanef == i1, -1.0, pe)
    p2 = jnp.max(pe2, axis=1, keepdims=True)
    i2 = jnp.min(jnp.where(pe2 == p2, lanef, big), axis=1, keepdims=True)
    den = p1 + p2
    g0 = gw * p1 / den
    g1 = gw * p2 / den
    info = jnp.where(lane == 0, i1, jnp.where(lane == 1, i2,
                     jnp.where(lane == 2, g0, jnp.where(lane == 3, g1, 0.0))))
    info_ref[...] = info
    oh = jnp.where(lanef == i1, 1.0, 0.0) + jnp.where(lanef == i2, 1.0, 0.0)
    cnt_ref[...] += jnp.sum(oh, axis=0, keepdims=True)


def _route(lg):
    n = lg.shape[0]
    tm = min(1024, n)
    return pl.pallas_call(
        _route_kernel,
        grid=(n // tm,),
        in_specs=[pl.BlockSpec((tm, LANES), lambda i: (i, 0))],
        out_specs=[pl.BlockSpec((tm, LANES), lambda i: (i, 0)),
                   pl.BlockSpec((1, LANES), lambda i: (0, 0))],
        out_shape=[jax.ShapeDtypeStruct((n, LANES), F32),
                   jax.ShapeDtypeStruct((1, LANES), F32)],
        compiler_params=_cparams(("arbitrary",)),
        name="moe_route",
    )(lg)


def _plan_kernel(cnt_ref, ps_ref, be_ref, nu_ref, *, nblk):
    def fill(i, c):
        be_ref[i] = N_EXPERTS - 1
        return c

    lax.fori_loop(0, nblk, fill, 0)

    def per_expert(e, pos):
        ps_ref[e] = pos * MOE_BLK
        nb = (cnt_ref[e] + MOE_BLK - 1) // MOE_BLK

        def mark(k, c):
            be_ref[pos + k] = e
            return c

        lax.fori_loop(0, nb, mark, 0)
        return pos + nb

    nu_ref[0] = lax.fori_loop(0, N_EXPERTS, per_expert, jnp.int32(0))


def _plan(counts, nblk):
    smem = pl.BlockSpec(memory_space=pltpu.SMEM)
    return pl.pallas_call(
        functools.partial(_plan_kernel, nblk=nblk),
        in_specs=[smem],
        out_specs=[smem, smem, smem],
        out_shape=[jax.ShapeDtypeStruct((N_EXPERTS,), I32),
                   jax.ShapeDtypeStruct((nblk,), I32),
                   jax.ShapeDtypeStruct((1,), I32)],
        name="moe_block_plan",
    )(counts)


def _dest_kernel(info_ref, ps_ref, tri_ref, o_ref, carry_scr):
    @pl.when(pl.program_id(0) == 0)
    def _():
        carry_scr[...] = jnp.zeros(carry_scr.shape, F32)

    info = info_ref[...]
    lane = lax.broadcasted_iota(I32, info.shape, 1)
    lanef = lane.astype(F32)
    o1 = jnp.where(lanef == info[:, 0:1], 1.0, 0.0)
    o2 = jnp.where(lanef == info[:, 1:2], 1.0, 0.0)
    oh = o1 + o2
    before = jnp.dot(tri_ref[...], oh.astype(BF16), preferred_element_type=F32)
    base = before + carry_scr[...] + ps_ref[...]
    d1 = jnp.sum(o1 * base, axis=1, keepdims=True)
    d2 = jnp.sum(o2 * base, axis=1, keepdims=True)
    o_ref[...] = jnp.where(lane == 0, d1, jnp.where(lane == 1, d2, 0.0))
    carry_scr[...] += jnp.sum(oh, axis=0, keepdims=True)


def _dest(info, ps_lanes, tri):
    n = info.shape[0]
    tm = tri.shape[0]
    return pl.pallas_call(
        _dest_kernel,
        grid=(n // tm,),
        in_specs=[pl.BlockSpec((tm, LANES), lambda i: (i, 0)),
                  pl.BlockSpec((1, LANES), lambda i: (0, 0)),
                  pl.BlockSpec((tm, tm), lambda i: (0, 0))],
        out_specs=pl.BlockSpec((tm, LANES), lambda i: (i, 0)),
        out_shape=jax.ShapeDtypeStruct((n, LANES), F32),
        scratch_shapes=[pltpu.VMEM((1, LANES), F32)],
        compiler_params=_cparams(("arbitrary",)),
        name="moe_dest_rows",
    )(info, ps_lanes, tri)


SCATTER_WINDOW = 256


def _scatter_kernel(d0_ref, d1_ref, cnt_ref, ps_ref, nu_ref, h_ref, xb_ref, zero_scr, sem, zsem,
                    *, nblk):
    n = h_ref.shape[0]
    zero_scr[...] = jnp.zeros(zero_scr.shape, zero_scr.dtype)
    pad_sizes = [s for s in (1 << k for k in range(MOE_BLK.bit_length() - 2, -1, -1))
                 if s >= SUBLANES]

    def zero_copy(rows, dst_row):
        return pltpu.make_async_copy(zero_scr.at[pl.ds(0, rows)], xb_ref.at[pl.ds(dst_row, rows)], zsem)

    def row_copy(t, dst_row, rows=1):
        return pltpu.make_async_copy(h_ref.at[pl.ds(t, rows)], xb_ref.at[pl.ds(dst_row, rows)], sem)

    def for_each_zero_copy(act):
        def per_expert(e, c):
            pos = ps_ref[e] + cnt_ref[e]
            pad = (MOE_BLK - (cnt_ref[e] & (MOE_BLK - 1))) & (MOE_BLK - 1)
            head = pad & (SUBLANES - 1)
            for k in range(SUBLANES - 1):
                @pl.when(k < head)
                def _():
                    act(zero_copy(1, pos + k))
            pos = pl.multiple_of(pos + head, SUBLANES)
            for size in pad_sizes:
                @pl.when((pad & size) != 0)
                def _():
                    act(zero_copy(size, pos))
                pos = pl.multiple_of(pos + (pad & size), SUBLANES)
            return c

        def per_tail_block(i, c):
            act(zero_copy(MOE_BLK, i * MOE_BLK))
            return c

        lax.fori_loop(0, N_EXPERTS, per_expert, 0)
        lax.fori_loop(nu_ref[0], nblk, per_tail_block, 0)

    for_each_zero_copy(lambda cp: cp.start())

    def issue(g, c):
        for k in range(SUBLANES):
            t = g * SUBLANES + k
            row_copy(t, d0_ref[t]).start()
            row_copy(t, d1_ref[t]).start()
        return c

    def retire(g, c):
        row_copy(0, 0, 2 * SUBLANES).wait()
        return c

    def issue_and_retire(g, c):
        return retire(g, issue(g, c))

    window = SCATTER_WINDOW // SUBLANES
    lax.fori_loop(0, window, issue, 0)
    lax.fori_loop(window, n // SUBLANES, issue_and_retire, 0)
    lax.fori_loop(0, window, retire, 0)

    for_each_zero_copy(lambda cp: cp.wait())


def _scatter_rows(d0, d1, counts, ps, nu, h2, p_rows):
    n, d = h2.shape
    assert n > SCATTER_WINDOW
    smem = pl.BlockSpec(memory_space=pltpu.SMEM)
    return pl.pallas_call(
        functools.partial(_scatter_kernel, nblk=p_rows // MOE_BLK),
        in_specs=[smem, smem, smem, smem, smem, pl.BlockSpec(memory_space=pl.ANY)],
        out_specs=pl.BlockSpec(memory_space=pl.ANY),
        out_shape=jax.ShapeDtypeStruct((p_rows, d), h2.dtype),
        scratch_shapes=[pltpu.VMEM((MOE_BLK, d), h2.dtype),
                        pltpu.SemaphoreType.DMA(()), pltpu.SemaphoreType.DMA(())],
        compiler_params=_cparams(),
        name="moe_scatter_rows",
    )(d0, d1, counts, ps, nu, h2)


def _expert_kernel(be_ref, nu_ref, x_ref, wg_ref, wu_ref, wd_ref, o_ref, wg_scr, wu_scr, wd_scr):
    i = pl.program_id(0)
    new_expert = jnp.logical_or(i == 0, be_ref[i] != be_ref[jnp.maximum(i - 1, 0)])

    @pl.when(jnp.logical_and(new_expert, i < nu_ref[0]))
    def _():
        wg_scr[...] = wg_ref[0].astype(BF16)
        wu_scr[...] = wu_ref[0].astype(BF16)
        wd_scr[...] = wd_ref[0].astype(BF16)

    @pl.when(i < nu_ref[0])
    def _():
        x = x_ref[...].astype(BF16)
        g = jnp.dot(x, wg_scr[...], preferred_element_type=F32)
        u = jnp.dot(x, wu_scr[...], preferred_element_type=F32)
        act = (g * (1.0 / (1.0 + jnp.exp(-g))) * u).astype(BF16)
        o_ref[...] = jnp.dot(act, wd_scr[...], preferred_element_type=F32)

    @pl.when(i >= nu_ref[0])
    def _():
        o_ref[...] = jnp.zeros(o_ref.shape, o_ref.dtype)


def _experts(be, nu, xb, wg, wu, wd):
    p, d = xb.shape
    de = wg.shape[2]
    return pl.pallas_call(
        _expert_kernel,
        grid_spec=pltpu.PrefetchScalarGridSpec(
            num_scalar_prefetch=2,
            grid=(p // MOE_BLK,),
            in_specs=[pl.BlockSpec((MOE_BLK, d), lambda i, be, nu: (i, 0)),
                      pl.BlockSpec((1, d, de), lambda i, be, nu: (be[i], 0, 0)),
                      pl.BlockSpec((1, d, de), lambda i, be, nu: (be[i], 0, 0)),
                      pl.BlockSpec((1, de, d), lambda i, be, nu: (be[i], 0, 0))],
            out_specs=pl.BlockSpec((MOE_BLK, d), lambda i, be, nu: (i, 0)),
            scratch_shapes=[pltpu.VMEM((d, de), BF16), pltpu.VMEM((d, de), BF16),
                            pltpu.VMEM((de, d), BF16)]),
        out_shape=jax.ShapeDtypeStruct((p, d), F32),
        compiler_params=_cparams(("arbitrary",)),
        name="moe_expert_ffn",
    )(be, nu, xb, wg, wu, wd)


def _combine_kernel(d0_ref, d1_ref, x_ref, info_ref, g2_ref, yb_ref, o_ref, rows_scr, sem):
    i = pl.program_id(0)
    cur = i % 2

    def copy(buf, which, r, src_row, rows=1):
        return pltpu.make_async_copy(yb_ref.at[pl.ds(src_row, rows)],
                                     rows_scr.at[buf, which, pl.ds(r, rows)], sem.at[buf])

    def start_gather(step, buf):
        base = step * TOK_TILE

        def issue(g, c):
            for k in range(SUBLANES):
                r = g * SUBLANES + k
                copy(buf, 0, r, d0_ref[base + r]).start()
                copy(buf, 1, r, d1_ref[base + r]).start()
            return c

        lax.fori_loop(0, TOK_TILE // SUBLANES, issue, 0)

    @pl.when(i == 0)
    def _():
        start_gather(0, 0)

    @pl.when(i + 1 < pl.num_programs(0))
    def _():
        start_gather(i + 1, 1 - cur)

    copy(cur, 0, 0, 0, TOK_TILE).wait()
    copy(cur, 1, 0, 0, TOK_TILE).wait()

    info = info_ref[...]
    y = info[:, 2:3] * rows_scr[cur, 0] + info[:, 3:4] * rows_scr[cur, 1]
    o_ref[...] = x_ref[...] + g2_ref[0] * y


def _combine(d0, d1, x1, info, g2, yb, seq):
    n, d = x1.shape
    tpb = seq // TOK_TILE
    return pl.pallas_call(
        _combine_kernel,
        grid_spec=pltpu.PrefetchScalarGridSpec(
            num_scalar_prefetch=2,
            grid=(n // TOK_TILE,),
            in_specs=[pl.BlockSpec((TOK_TILE, d), lambda i, a, b: (i, 0)),
                      pl.BlockSpec((TOK_TILE, LANES), lambda i, a, b: (i, 0)),
                      pl.BlockSpec((1, 1, d), lambda i, a, b: (i // tpb, 0, 0)),
                      pl.BlockSpec(memory_space=pl.ANY)],
            out_specs=pl.BlockSpec((TOK_TILE, d), lambda i, a, b: (i, 0)),
            scratch_shapes=[pltpu.VMEM((2, 2, TOK_TILE, d), F32),
                            pltpu.SemaphoreType.DMA((2,))]),
        out_shape=jax.ShapeDtypeStruct((n, d), F32),
        compiler_params=_cparams(("arbitrary",)),
        name="moe_combine",
    )(d0, d1, x1, info, g2, yb)


def _tri_inclusive_rev(k):
    l = (np.arange(k)[:, None] >= np.arange(k)[None, :]).astype(np.float32)
    half = np.concatenate([l, np.ones((k, k), np.float32)], axis=1)
    return jnp.asarray(np.concatenate([half, half], axis=0), dtype=BF16)


def _tri_strict_lower(k):
    return jnp.asarray((np.arange(k)[None, :] < np.arange(k)[:, None]).astype(np.float32), dtype=BF16)


def kernel(x, c, w_mod, b_mod, ln1_g, w_in, q_norm_g, w_q_up, q_gain, k_gain, rel_bias, gn_a, gn_b,
           w_out, ln2_g, router_g, router_e, w_gate, w_up, w_down):
    batch, seq, d = x.shape
    n = batch * seq
    assert w_mod.shape[0] == 1 and d == D_A + D_B
    assert seq % KB_A == 0 and seq % TOK_TILE == 0 and n % 512 == 0
    x2 = x.reshape(n, d)

    c8 = jnp.pad(c, ((0, 8 - batch), (0, 0)))
    mod = _modulation(c8, w_mod.reshape(d, -1), b_mod.reshape(1, -1))[:batch]
    sh1, sc1, g1, sh2, sc2, g2 = [m.reshape(batch, 1, d) for m in jnp.split(mod, 6, axis=-1)]

    wi = w_in.reshape(d, -1)
    n_a = Q_RANK + 2 * HEAD_DIM + IDX_DIM + IDX_HEADS
    wi = jnp.concatenate([wi[:, n_a:], wi[:, :n_a],
                          jnp.zeros((d, D_IN_PAD - wi.shape[1]), wi.dtype)], axis=1).astype(BF16)
    proj = _ln_proj(x2, ln1_g.reshape(1, d), sc1, sh1, wi, seq)

    qa, qidx, ka, va, kidx, widx = _aprep(
        proj, q_norm_g.reshape(1, -1), w_q_up.reshape(Q_RANK, -1).astype(BF16),
        q_gain.reshape(1, -1), k_gain.reshape(1, -1))

    bt = _bias_tiles(rel_bias)
    bound = _logit_bound(q_gain.reshape(1, -1), k_gain.reshape(1, -1), rel_bias)[0, :1]
    oa = _attn_a(qidx, widx.T, qa, kidx, ka, va, bt, gn_a.reshape(1, -1), bound, batch, seq)
    ob = _attn_b(proj, _tri_inclusive_rev(QB_B), batch, seq)

    rw = jnp.concatenate([router_e.reshape(d, -1), router_g.reshape(d, -1),
                          jnp.zeros((d, LANES - N_EXPERTS - N_GROUPS), F32)], axis=1)
    x1, h2, lg = _out_proj(x2, oa, ob, gn_b.reshape(1, -1), w_out.reshape(d, d).astype(BF16),
                           g1, ln2_g.reshape(1, d), sc2, sh2, rw, seq)

    info, cnt = _route(lg)
    counts = cnt[0, :N_EXPERTS].astype(I32)
    p_rows = 2 * n + N_EXPERTS * MOE_BLK
    ps, be, nu = _plan(counts, p_rows // MOE_BLK)
    ps_lanes = jnp.pad(ps.astype(F32), (0, LANES - N_EXPERTS)).reshape(1, LANES)
    dinfo = _dest(info, ps_lanes, _tri_strict_lower(512))
    d0 = dinfo[:, 0].astype(I32)
    d1 = dinfo[:, 1].astype(I32)
    xb = _scatter_rows(d0, d1, counts, ps, nu, h2, p_rows)
    yb = _experts(be, nu, xb,
                  w_gate.reshape(N_EXPERTS, d, D_EXPERT),
                  w_up.reshape(N_EXPERTS, d, D_EXPERT),
                  w_down.reshape(N_EXPERTS, D_EXPERT, d))
    out = _combine(d0, d1, x1, info, g2, yb, seq)
    return out.reshape(batch, seq, d)
```

```python
import functools
import math

import numpy as np
import jax
import jax.numpy as jnp
from jax import lax
from jax.experimental import pallas as pl
from jax.experimental.pallas import tpu as pltpu

F32 = jnp.float32
BF16 = jnp.bfloat16
I32 = jnp.int32

HEAD_DIM = 128
N_HEADS_A = 8
N_HEADS_B = 8
D_A = N_HEADS_A * HEAD_DIM
D_B = N_HEADS_B * HEAD_DIM
Q_RANK = 512
IDX_HEADS = 16
IDX_DIM = 64
TOPK_MAX = 256
N_BUCKETS = 32
MAX_DISTANCE = 128
N_GROUPS = 4
EXPERTS_PER_GROUP = 8
N_EXPERTS = N_GROUPS * EXPERTS_PER_GROUP
D_EXPERT = 512
EPS = 1e-6

LANES = 128
SUBLANES = 8
VMEM_LIMIT = 56 * 1024 * 1024
NEG = -1e30
INT_MIN = -(2 ** 31)
EXP_UNDERFLOW = 104.0
LOG2E = math.log2(math.e)
PLAIN_SOFTMAX_LIMIT = 64.0

QB_A = 128
KB_A = 256
QB_B = 128
MOE_BLK = 256
TOK_TILE = 256

COL_QB, COL_KB, COL_VB, COL_A = 0, D_B, 2 * D_B, 3 * D_B
A_CQ, A_KA, A_VA, A_KIDX, A_WIDX = 0, 512, 640, 768, 832
A_WIDTH = 1024
D_IN_PAD = COL_A + A_WIDTH


def _cparams(sem=None):
    return pltpu.CompilerParams(dimension_semantics=sem, vmem_limit_bytes=VMEM_LIMIT)


def _rms(x):
    return x * lax.rsqrt(jnp.mean(x * x, axis=-1, keepdims=True) + EPS)


def _mod_kernel(c_ref, w_ref, b_ref, o_ref):
    c = c_ref[...]
    s = c * (1.0 / (1.0 + jnp.exp(-c)))
    o_ref[...] = jnp.dot(s, w_ref[...], preferred_element_type=F32,
                         precision=lax.Precision.HIGHEST) + b_ref[...]


def _modulation(c8, w_mod, b_mod):
    d, n6 = w_mod.shape
    tn = 1024
    return pl.pallas_call(
        _mod_kernel,
        grid=(n6 // tn,),
        in_specs=[pl.BlockSpec((8, d), lambda j: (0, 0)),
                  pl.BlockSpec((d, tn), lambda j: (0, j)),
                  pl.BlockSpec((1, tn), lambda j: (0, j))],
        out_specs=pl.BlockSpec((8, tn), lambda j: (0, j)),
        out_shape=jax.ShapeDtypeStruct((8, n6), F32),
        compiler_params=_cparams(("arbitrary",)),
        name="modulation",
    )(c8, w_mod, b_mod)


def _ln_proj_kernel(x_ref, g_ref, sc_ref, sh_ref, w_ref, o_ref, h_scr):
    @pl.when(pl.program_id(1) == 0)
    def _():
        h = _rms(x_ref[...]) * g_ref[...]
        h = h * (1.0 + sc_ref[0]) + sh_ref[0]
        h_scr[...] = h.astype(BF16)

    o_ref[...] = jnp.dot(h_scr[...], w_ref[...], preferred_element_type=F32).astype(o_ref.dtype)


def _ln_proj(x2, ln_g, sc, sh, w, seq):
    n, d = x2.shape
    ncol = w.shape[1]
    tm = min(1024, seq)
    tn = 512
    tpb = seq // tm
    return pl.pallas_call(
        _ln_proj_kernel,
        grid=(n // tm, ncol // tn),
        in_specs=[pl.BlockSpec((tm, d), lambda i, j: (i, 0)),
                  pl.BlockSpec((1, d), lambda i, j: (0, 0)),
                  pl.BlockSpec((1, 1, d), lambda i, j: (i // tpb, 0, 0)),
                  pl.BlockSpec((1, 1, d), lambda i, j: (i // tpb, 0, 0)),
                  pl.BlockSpec((d, tn), lambda i, j: (0, j))],
        out_specs=pl.BlockSpec((tm, tn), lambda i, j: (i, j)),
        out_shape=jax.ShapeDtypeStruct((n, ncol), BF16),
        scratch_shapes=[pltpu.VMEM((tm, d), BF16)],
        compiler_params=_cparams(("parallel", "arbitrary")),
        name="ln_in_proj",
    )(x2, ln_g, sc, sh, w)


def _aprep_kernel(a_ref, qng_ref, wq_ref, qg_ref, kg_ref,
                  qa_ref, qidx_ref, ka_ref, va_ref, kidx_ref, widx_ref):
    cq = a_ref[:, A_CQ:A_CQ + Q_RANK].astype(F32)
    cqn = (_rms(cq) * qng_ref[...]).astype(BF16)
    qup = jnp.dot(cqn, wq_ref[...], preferred_element_type=F32)
    for h in range(N_HEADS_A):
        qh = qup[:, h * HEAD_DIM:(h + 1) * HEAD_DIM]
        qn = _rms(qh) * qg_ref[...] * (HEAD_DIM ** -0.5 * LOG2E)
        qa_ref[:, h * HEAD_DIM:(h + 1) * HEAD_DIM] = qn.astype(BF16)
    for h in range(IDX_HEADS):
        qi = qup[:, D_A + h * IDX_DIM:D_A + (h + 1) * IDX_DIM] * (IDX_DIM ** -0.5)
        qidx_ref[h] = qi.astype(BF16)
    ka = a_ref[:, A_KA:A_KA + HEAD_DIM].astype(F32)
    ka_ref[...] = (_rms(ka) * kg_ref[...]).astype(BF16)
    va_ref[...] = a_ref[:, A_VA:A_VA + HEAD_DIM]
    kidx_ref[...] = a_ref[:, A_KIDX:A_KIDX + IDX_DIM]
    widx_ref[...] = a_ref[:, A_WIDX:A_WIDX + IDX_HEADS].astype(F32) * (IDX_HEADS ** -0.5)


def _aprep(proj, q_norm_g, w_q_up, q_gain, k_gain):
    n = proj.shape[0]
    tm = 512
    cblk = COL_A // A_WIDTH
    nup = w_q_up.shape[1]
    return pl.pallas_call(
        _aprep_kernel,
        grid=(n // tm,),
        in_specs=[pl.BlockSpec((tm, A_WIDTH), lambda i: (i, cblk)),
                  pl.BlockSpec((1, Q_RANK), lambda i: (0, 0)),
                  pl.BlockSpec((Q_RANK, nup), lambda i: (0, 0)),
                  pl.BlockSpec((1, HEAD_DIM), lambda i: (0, 0)),
                  pl.BlockSpec((1, HEAD_DIM), lambda i: (0, 0))],
        out_specs=[pl.BlockSpec((tm, D_A), lambda i: (i, 0)),
                   pl.BlockSpec((IDX_HEADS, tm, IDX_DIM), lambda i: (0, i, 0)),
                   pl.BlockSpec((tm, HEAD_DIM), lambda i: (i, 0)),
                   pl.BlockSpec((tm, HEAD_DIM), lambda i: (i, 0)),
                   pl.BlockSpec((tm, IDX_DIM), lambda i: (i, 0)),
                   pl.BlockSpec((tm, IDX_HEADS), lambda i: (i, 0))],
        out_shape=[jax.ShapeDtypeStruct((n, D_A), BF16),
                   jax.ShapeDtypeStruct((IDX_HEADS, n, IDX_DIM), BF16),
                   jax.ShapeDtypeStruct((n, HEAD_DIM), BF16),
                   jax.ShapeDtypeStruct((n, HEAD_DIM), BF16),
                   jax.ShapeDtypeStruct((n, IDX_DIM), BF16),
                   jax.ShapeDtypeStruct((n, IDX_HEADS), F32)],
        compiler_params=_cparams(("parallel",)),
        name="group_a_prep",
    )(proj, q_norm_g, w_q_up, q_gain, k_gain)


def _t5_bucket_starts():
    max_exact = N_BUCKETS // 2
    d = np.arange(0, 4 * MAX_DISTANCE, dtype=np.int64)
    df = np.maximum(d, 1).astype(np.float32)
    large = max_exact + (np.log(df / np.float32(max_exact)) / np.float32(math.log(MAX_DISTANCE / max_exact))
                         * np.float32(N_BUCKETS - max_exact)).astype(np.int32)
    large = np.minimum(large, N_BUCKETS - 1)
    bucket = np.where(d < max_exact, d, large)
    assert np.all(np.diff(bucket) >= 0) and bucket[-1] == N_BUCKETS - 1
    return [int(np.argmax(bucket >= b)) for b in range(N_BUCKETS)]


_BUCKET_START = _t5_bucket_starts()
N_BIAS_TILES = 2 * KB_A // LANES


def _bias_kernel(rb_ref, o_ref):
    di = pl.program_id(0)
    h = pl.program_id(1)
    i = lax.broadcasted_iota(I32, (QB_A, KB_A), 0)
    j = lax.broadcasted_iota(I32, (QB_A, KB_A), 1)
    d = di * LANES + i - j
    val = jnp.full((QB_A, KB_A), rb_ref[0, h], F32)
    for b in range(1, N_BUCKETS):
        val = jnp.where(d >= _BUCKET_START[b], rb_ref[b, h], val)
    o_ref[0, 0] = (val - rb_ref[N_BUCKETS - 1, h]) * LOG2E


def _bias_tiles(rel_bias):
    return pl.pallas_call(
        _bias_kernel,
        grid=(N_BIAS_TILES, N_HEADS_A),
        in_specs=[pl.BlockSpec(memory_space=pltpu.SMEM)],
        out_specs=pl.BlockSpec((1, 1, QB_A, KB_A), lambda a, h: (a, h, 0, 0)),
        out_shape=jax.ShapeDtypeStruct((N_BIAS_TILES, N_HEADS_A, QB_A, KB_A), F32),
        compiler_params=_cparams(("arbitrary", "arbitrary")),
        name="t5_bias_tiles",
    )(rel_bias)


def _bound_kernel(qg_ref, kg_ref, rb_ref, o_ref):
    qmax = jnp.max(jnp.abs(qg_ref[...]), axis=1, keepdims=True)
    kmax = jnp.max(jnp.abs(kg_ref[...]), axis=1, keepdims=True)
    rb = rb_ref[...]
    shifted = jnp.abs(rb - rb[N_BUCKETS - 1:N_BUCKETS, :])
    bmax = jnp.max(jnp.max(shifted, axis=1, keepdims=True), axis=0, keepdims=True)
    bound = qmax * kmax * (math.sqrt(HEAD_DIM) * 1.02) + bmax
    o_ref[...] = jnp.broadcast_to(bound, o_ref.shape)


def _logit_bound(q_gain, k_gain, rel_bias):
    return pl.pallas_call(
        _bound_kernel,
        out_shape=jax.ShapeDtypeStruct((1, LANES), F32),
        name="dsa_logit_bound",
    )(q_gain, k_gain, rel_bias)


def _attn_a_kernel(qidx_ref, w_ref, qa_ref, kidx_ref, ka_ref, va_ref, bt_ref, gn_ref, bound_ref, o_ref,
                   keys_scr, keyt_scr, m_scr, acc_scr, s_scr, mb_scr, *, topk):
    qb = pl.program_id(1)
    t0 = qb * QB_A
    kbl = (t0 + QB_A - 1) // KB_A
    row = t0 + lax.broadcasted_iota(I32, (QB_A, KB_A), 0)
    col0 = lax.broadcasted_iota(I32, (QB_A, KB_A), 1)
    nt = (((1,), (1,)), ((), ()))

    reps = KB_A // LANES

    def sort_key(v):
        bits = pltpu.bitcast(v, I32)
        return jnp.where(bits < 0, bits ^ jnp.int32(0x7FFFFFFF), bits)

    def key_value(k):
        return pltpu.bitcast(jnp.where(k < 0, k ^ jnp.int32(0x7FFFFFFF), k), F32)

    key_pos = lax.broadcasted_iota(I32, (KB_A, QB_A), 0)
    qry_pos = t0 + lax.broadcasted_iota(I32, (KB_A, QB_A), 1)

    def score_tile(kb, carry):
        smin, smax = carry
        kt = kidx_ref[pl.ds(pl.multiple_of(kb * KB_A, KB_A), KB_A), :]
        score = jnp.zeros((KB_A, QB_A), F32)
        for hp in range(IDX_HEADS // 2):
            q2 = qidx_ref[2 * hp:2 * hp + 2].reshape(2 * QB_A, IDX_DIM)
            sc = lax.dot_general(kt, q2, nt, preferred_element_type=F32)
            score = score + w_ref[2 * hp:2 * hp + 1, :] * jnp.maximum(sc[:, :QB_A], 0.0)
            score = score + w_ref[2 * hp + 1:2 * hp + 2, :] * jnp.maximum(sc[:, QB_A:], 0.0)
        causal = (kb * KB_A + key_pos) <= qry_pos
        key_t = jnp.where(causal, sort_key(score), jnp.int32(INT_MIN))
        keyt_scr[kb] = key_t
        keys_scr[kb] = key_t.T
        smin = jnp.minimum(smin, jnp.min(jnp.where(causal, score, jnp.inf), axis=0, keepdims=True))
        smax = jnp.maximum(smax, jnp.max(jnp.where(causal, score, -jnp.inf), axis=0, keepdims=True))
        return smin, smax

    def score_pair(i, carry):
        return score_tile(2 * i + 1, score_tile(2 * i, carry))

    n_tiles = kbl + 1
    extremes = lax.fori_loop(0, n_tiles // 2, score_pair,
                             (jnp.full((1, QB_A), jnp.inf, F32), jnp.full((1, QB_A), -jnp.inf, F32)))
    smin, smax = lax.cond(n_tiles % 2 == 1, lambda c: score_tile(n_tiles - 1, c), lambda c: c, extremes)

    qry1 = t0 + lax.broadcasted_iota(I32, (1, QB_A), 1)
    kf = float(topk)
    acc_rows = 32

    def search_pass(p, state):
        lo, hi, clo, chi, open_q = state
        lo_v = key_value(lo)
        hi_v = key_value(hi)
        gap = clo - chi
        frac = (clo - (kf - 0.5)) / gap
        frac = jnp.where(p % 2 == 1, 0.7 * frac + 0.15, frac)
        frac = jnp.where(gap > 16.0, frac, 0.5)
        cand = sort_key(lo_v + (hi_v - lo_v) * frac)
        cand = jnp.where(p % 8 == 7, lo + lax.shift_right_logical(hi - lo, 1), cand)
        cand = jnp.minimum(jnp.maximum(cand, lo + 1), hi - 1)

        def count(kb, cnt):
            ge = jnp.where(keyt_scr[kb] >= cand, 1.0, 0.0)
            return cnt + jnp.sum(ge.reshape(KB_A // acc_rows, acc_rows, QB_A), axis=0)

        cnt = lax.fori_loop(0, kbl + 1, count, jnp.zeros((acc_rows, QB_A), F32))
        tot = jnp.sum(cnt, axis=0, keepdims=True)
        ge = tot >= kf
        lo = jnp.where(ge, cand, lo)
        clo = jnp.where(ge, tot, clo)
        hi = jnp.where(ge, hi, cand)
        chi = jnp.where(ge, chi, tot)
        width = hi - lo
        settled = jnp.where(clo == kf, 1.0, jnp.where(width == 1, 1.0, 0.0))
        open_q = jnp.where(settled > 0.5, 0.0, open_q)
        return lo, hi, clo, chi, open_q

    def search_step(carry):
        p, state, _ = carry
        state = search_pass(p + 1, search_pass(p, state))
        return p + 2, state, jnp.max(state[4]) > 0.0

    n_causal = (qry1 + 1).astype(F32)
    open0 = jnp.where(qry1 >= topk, 1.0, 0.0)
    lo0 = sort_key(smin)
    hi0 = sort_key(smax) + 1
    open0 = jnp.where(hi0 - lo0 == 1, 0.0, open0)
    _, (lo, _, _, _, _), _ = lax.while_loop(
        lambda c: c[2], search_step,
        (jnp.int32(0), (lo0, hi0, n_causal, jnp.zeros((1, QB_A), F32), open0), jnp.max(open0) > 0.0))
    thr = jnp.where(qry1 >= topk, lo, jnp.int32(INT_MIN))
    thr = jnp.broadcast_to(thr, (QB_A, QB_A)).T
    thrb = jnp.tile(thr, (1, reps))

    m_scr[...] = jnp.full(m_scr.shape, NEG, F32)
    acc_scr[...] = jnp.zeros(acc_scr.shape, F32)

    plain = bound_ref[0] <= PLAIN_SOFTMAX_LIMIT

    def attend(kb, near, online):
        mb = jnp.where(keys_scr[kb] >= thrb, 0.0, NEG)
        if near:
            mb = jnp.where((kb * KB_A + col0) <= row, mb, NEG)
            di = (t0 - kb * KB_A) // LANES
        mb_scr[...] = mb
        start = pl.multiple_of(kb * KB_A, KB_A)
        kt = ka_ref[pl.ds(start, KB_A), :]
        vt = jnp.concatenate([va_ref[pl.ds(start, KB_A), :], jnp.ones((KB_A, LANES), BF16)], axis=1)
        for h in range(N_HEADS_A):
            q = qa_ref[:, h * HEAD_DIM:(h + 1) * HEAD_DIM]
            s_scr[h] = lax.dot_general(q, kt, nt, preferred_element_type=F32)
        for h in range(N_HEADS_A):
            s = s_scr[h] + mb_scr[...]
            if near:
                s = s + bt_ref[di, h]
            if online:
                m_prev = m_scr[h]
                m_new = jnp.maximum(m_prev, jnp.max(s, axis=1, keepdims=True))
                alpha = jnp.exp2(m_prev - m_new)
                p = jnp.exp2(s - jnp.tile(m_new, (1, reps)))
                acc_scr[h] = (jnp.tile(alpha, (1, 2)) * acc_scr[h]
                              + jnp.dot(p.astype(BF16), vt, preferred_element_type=F32))
                m_scr[h] = m_new
            else:
                acc_scr[h] += jnp.dot(jnp.exp2(s).astype(BF16), vt, preferred_element_type=F32)

    def attend_all(online):
        n_far = jnp.maximum(kbl - 1, 0)

        def far_pair(i, carry):
            attend(2 * i, False, online)
            attend(2 * i + 1, False, online)
            return carry

        lax.fori_loop(0, n_far // 2, far_pair, 0)

        @pl.when(n_far % 2 == 1)
        def _():
            attend(n_far - 1, False, online)

        @pl.when(kbl >= 1)
        def _():
            attend(kbl - 1, True, online)

        attend(kbl, True, online)

    @pl.when(plain)
    def _():
        attend_all(False)

    @pl.when(jnp.logical_not(plain))
    def _():
        attend_all(True)

    ssq = jnp.zeros((QB_A, LANES), F32)
    for h in range(N_HEADS_A):
        oh = acc_scr[h, :, :HEAD_DIM] / acc_scr[h, :, HEAD_DIM:]
        acc_scr[h, :, :HEAD_DIM] = oh
        ssq = ssq + jnp.sum(oh * oh, axis=1, keepdims=True)
    inv = lax.rsqrt(ssq * (1.0 / D_A) + EPS)
    for h in range(N_HEADS_A):
        sl = slice(h * HEAD_DIM, (h + 1) * HEAD_DIM)
        o_ref[:, sl] = (acc_scr[h, :, :HEAD_DIM] * inv * gn_ref[:, sl]).astype(o_ref.dtype)


def _attn_a(qidx, widx, qa, kidx, ka, va, bt, gn_a, bound, batch, seq):
    n = qa.shape[0]
    nq = seq // QB_A
    nkt = seq // KB_A
    topk = min(TOPK_MAX, seq // 4)
    return pl.pallas_call(
        functools.partial(_attn_a_kernel, topk=topk),
        grid=(batch, nq),
        in_specs=[pl.BlockSpec((IDX_HEADS, QB_A, IDX_DIM), lambda b, q: (0, b * nq + q, 0)),
                  pl.BlockSpec((IDX_HEADS, QB_A), lambda b, q: (0, b * nq + q)),
                  pl.BlockSpec((QB_A, D_A), lambda b, q: (b * nq + q, 0)),
                  pl.BlockSpec((seq, IDX_DIM), lambda b, q: (b, 0)),
                  pl.BlockSpec((seq, HEAD_DIM), lambda b, q: (b, 0)),
                  pl.BlockSpec((seq, HEAD_DIM), lambda b, q: (b, 0)),
                  pl.BlockSpec(bt.shape, lambda b, q: (0, 0, 0, 0)),
                  pl.BlockSpec((1, D_A), lambda b, q: (0, 0)),
                  pl.BlockSpec(memory_space=pltpu.SMEM)],
        out_specs=pl.BlockSpec((QB_A, D_A), lambda b, q: (b * nq + q, 0)),
        out_shape=jax.ShapeDtypeStruct((n, D_A), BF16),
        scratch_shapes=[pltpu.VMEM((nkt, QB_A, KB_A), I32),
                        pltpu.VMEM((nkt, KB_A, QB_A), I32),
                        pltpu.VMEM((N_HEADS_A, QB_A, LANES), F32),
                        pltpu.VMEM((N_HEADS_A, QB_A, 2 * HEAD_DIM), F32),
                        pltpu.VMEM((N_HEADS_A, QB_A, KB_A), F32),
                        pltpu.VMEM((QB_A, KB_A), F32)],
        compiler_params=_cparams(("parallel", "arbitrary")),
        name="dsa_attention",
    )(qidx, widx, qa, kidx, ka, va, bt, gn_a, bound)


def _attn_b_kernel(q_ref, k_ref, v_ref, tri_ref, o_ref, rest_scr, z_scr, cs_scr):
    qb = pl.program_id(1)
    row = lax.broadcasted_iota(I32, (QB_B, QB_B), 0)
    col = lax.broadcasted_iota(I32, (QB_B, QB_B), 1)
    strict = col < row
    nt = (((1,), (1,)), ((), ()))
    scale = HEAD_DIM ** -0.5

    def step(kb, diag):
        start = pl.multiple_of(kb * QB_B, QB_B)
        heads = [slice(h * HEAD_DIM, (h + 1) * HEAD_DIM) for h in range(N_HEADS_B)]
        for h, sl in enumerate(heads):
            kt = k_ref[pl.ds(start, QB_B), sl]
            z_scr[h] = lax.dot_general(q_ref[:, sl], kt, nt, preferred_element_type=F32) * scale
        for h, sl in enumerate(heads):
            z = z_scr[h]
            sp = jnp.maximum(z, 0.0) + jnp.log(1.0 + jnp.exp(-jnp.abs(z)))
            if diag:
                sp = jnp.where(strict, sp, 0.0)
            hi = sp.astype(BF16)
            lo = (sp - hi.astype(F32)).astype(BF16)
            cs_scr[h] = jnp.dot(jnp.concatenate([hi, lo], axis=1), tri_ref[...],
                                preferred_element_type=F32)
        worst = None
        for h, sl in enumerate(heads):
            vt = v_ref[pl.ds(start, QB_B), sl]
            z = z_scr[h]
            cs = cs_scr[h, :, :QB_B]
            tot = cs_scr[h, :, QB_B:]
            if diag:
                a = jnp.where(strict, jnp.exp(z - cs), 0.0)
                o_ref[:, sl] = jnp.dot(a.astype(BF16), vt, preferred_element_type=F32)
                rest = tot
            else:
                rest = rest_scr[h]
                a = jnp.exp(z - cs - rest)
                o_ref[:, sl] += jnp.dot(a.astype(BF16), vt, preferred_element_type=F32)
                rest = rest + tot
            rest_scr[h] = rest
            worst = rest if worst is None else jnp.minimum(worst, rest)
        return jnp.min(worst)

    def more(kb, smallest):
        return jnp.logical_and(kb >= 0, smallest < EXP_UNDERFLOW)

    def body(carry):
        kb, _ = carry
        return kb - 1, more(kb - 1, step(kb, False))

    lax.while_loop(lambda c: c[1], body, (qb - 1, more(qb - 1, step(qb, True))))


def _attn_b(proj, tri, batch, seq):
    n = proj.shape[0]
    nq = seq // QB_B
    return pl.pallas_call(
        _attn_b_kernel,
        grid=(batch, nq),
        in_specs=[pl.BlockSpec((QB_B, D_B), lambda b, q: (b * nq + q, COL_QB // D_B)),
                  pl.BlockSpec((seq, D_B), lambda b, q: (b, COL_KB // D_B)),
                  pl.BlockSpec((seq, D_B), lambda b, q: (b, COL_VB // D_B)),
                  pl.BlockSpec(tri.shape, lambda b, q: (0, 0))],
        out_specs=pl.BlockSpec((QB_B, D_B), lambda b, q: (b * nq + q, 0)),
        out_shape=jax.ShapeDtypeStruct((n, D_B), F32),
        scratch_shapes=[pltpu.VMEM((N_HEADS_B, QB_B, QB_B), F32),
                        pltpu.VMEM((N_HEADS_B, QB_B, QB_B), F32),
                        pltpu.VMEM((N_HEADS_B, QB_B, 2 * QB_B), F32)],
        compiler_params=_cparams(("parallel", "arbitrary")),
        name="stick_breaking_attention",
    )(proj, proj, proj, tri)


def _out_proj_kernel(x_ref, oa_ref, ob_ref, gnb_ref, w_ref, g1_ref, ln_ref, sc_ref, sh_ref, rw_ref,
                     x1_ref, h2_ref, lg_ref):
    obn = (_rms(ob_ref[...]) * gnb_ref[...]).astype(BF16)
    y = jnp.dot(oa_ref[...], w_ref[0:D_A, :], preferred_element_type=F32)
    y = y + jnp.dot(obn, w_ref[D_A:D_A + D_B, :], preferred_element_type=F32)
    x1 = x_ref[...] + g1_ref[0] * y
    x1_ref[...] = x1
    h2 = _rms(x1) * ln_ref[...]
    h2 = h2 * (1.0 + sc_ref[0]) + sh_ref[0]
    h2_ref[...] = h2
    hh = h2.astype(BF16)
    hl = (h2 - hh.astype(F32)).astype(BF16)
    rw = rw_ref[...]
    rh = rw.astype(BF16)
    rl = (rw - rh.astype(F32)).astype(BF16)
    lg = jnp.dot(hh, rh, preferred_element_type=F32)
    lg = lg + jnp.dot(hh, rl, preferred_element_type=F32)
    lg = lg + jnp.dot(hl, rh, preferred_element_type=F32)
    lg_ref[...] = lg


def _out_proj(x2, oa, ob, gn_b, w_out, g1, ln_g, sc, sh, rw, seq):
    n, d = x2.shape
    tm = 256
    tpb = seq // tm
    row = lambda i: (i, 0)
    fixed = lambda i: (0, 0)
    perb = lambda i: (i // tpb, 0, 0)
    return pl.pallas_call(
        _out_proj_kernel,
        grid=(n // tm,),
        in_specs=[pl.BlockSpec((tm, d), row),
                  pl.BlockSpec((tm, D_A), row),
                  pl.BlockSpec((tm, D_B), row),
                  pl.BlockSpec((1, D_B), fixed),
                  pl.BlockSpec(w_out.shape, fixed),
                  pl.BlockSpec((1, 1, d), perb),
                  pl.BlockSpec((1, d), fixed),
                  pl.BlockSpec((1, 1, d), perb),
                  pl.BlockSpec((1, 1, d), perb),
                  pl.BlockSpec(rw.shape, fixed)],
        out_specs=[pl.BlockSpec((tm, d), row),
                   pl.BlockSpec((tm, d), row),
                   pl.BlockSpec((tm, LANES), row)],
        out_shape=[jax.ShapeDtypeStruct((n, d), F32),
                   jax.ShapeDtypeStruct((n, d), F32),
                   jax.ShapeDtypeStruct((n, LANES), F32)],
        compiler_params=_cparams(("parallel",)),
        name="out_proj_ln2_router",
    )(x2, oa, ob, gn_b, w_out, g1, ln_g, sc, sh, rw)


def _route_kernel(lg_ref, info_ref, cnt_ref):
    @pl.when(pl.program_id(0) == 0)
    def _():
        cnt_ref[...] = jnp.zeros(cnt_ref.shape, F32)

    lg = lg_ref[...]
    lane = lax.broadcasted_iota(I32, lg.shape, 1)
    lanef = lane.astype(F32)
    big = float(4 * LANES)
    gm = jnp.where(lane >= N_EXPERTS, jnp.where(lane < N_EXPERTS + N_GROUPS, 1.0, 0.0), 0.0) > 0.5
    lgm = jnp.where(gm, lg, NEG)
    mg = jnp.max(lgm, axis=1, keepdims=True)
    eg = jnp.where(gm, jnp.exp(lgm - mg), 0.0)
    pg = eg / jnp.sum(eg, axis=1, keepdims=True)
    gw = jnp.max(pg, axis=1, keepdims=True)
    gidx = jnp.min(jnp.where(gm, jnp.where(pg == gw, lanef - N_EXPERTS, big), big), axis=1, keepdims=True)
    lane_group = (lane // EXPERTS_PER_GROUP).astype(F32)
    em = jnp.where(lane < N_EXPERTS, jnp.where(lane_group == gidx, 1.0, 0.0), 0.0) > 0.5
    lem = jnp.where(em, lg, NEG)
    me = jnp.max(lem, axis=1, keepdims=True)
    ee = jnp.where(em, jnp.exp(lem - me), 0.0)
    pe = jnp.where(em, ee / jnp.sum(ee, axis=1, keepdims=True), -1.0)
    p1 = jnp.max(pe, axis=1, keepdims=True)
    i1 = jnp.min(jnp.where(pe == p1, lanef, big), axis=1, keepdims=True)
    pe2 = jnp.where(lanef == i1, -1.0, pe)
    p2 = jnp.max(pe2, axis=1, keepdims=True)
    i2 = jnp.min(jnp.where(pe2 == p2, lanef, big), axis=1, keepdims=True)
    den = p1 + p2
    g0 = gw * p1 / den
    g1 = gw * p2 / den
    info = jnp.where(lane == 0, i1, jnp.where(lane == 1, i2,
                     jnp.where(lane == 2, g0, jnp.where(lane == 3, g1, 0.0))))
    info_ref[...] = info
    oh = jnp.where(lanef == i1, 1.0, 0.0) + jnp.where(lanef == i2, 1.0, 0.0)
    cnt_ref[...] += jnp.sum(oh, axis=0, keepdims=True)


def _route(lg):
    n = lg.shape[0]
    tm = min(1024, n)
    return pl.pallas_call(
        _route_kernel,
        grid=(n // tm,),
        in_specs=[pl.BlockSpec((tm, LANES), lambda i: (i, 0))],
        out_specs=[pl.BlockSpec((tm, LANES), lambda i: (i, 0)),
                   pl.BlockSpec((1, LANES), lambda i: (0, 0))],
        out_shape=[jax.ShapeDtypeStruct((n, LANES), F32),
                   jax.ShapeDtypeStruct((1, LANES), F32)],
        compiler_params=_cparams(("arbitrary",)),
        name="moe_route",
    )(lg)


def _plan_kernel(cnt_ref, ps_ref, be_ref, nu_ref, *, nblk):
    def fill(i, c):
        be_ref[i] = N_EXPERTS - 1
        return c

    lax.fori_loop(0, nblk, fill, 0)

    def per_expert(e, pos):
        ps_ref[e] = pos * MOE_BLK
        nb = (cnt_ref[e] + MOE_BLK - 1) // MOE_BLK

        def mark(k, c):
            be_ref[pos + k] = e
            return c

        lax.fori_loop(0, nb, mark, 0)
        return pos + nb

    nu_ref[0] = lax.fori_loop(0, N_EXPERTS, per_expert, jnp.int32(0))


def _plan(counts, nblk):
    smem = pl.BlockSpec(memory_space=pltpu.SMEM)
    return pl.pallas_call(
        functools.partial(_plan_kernel, nblk=nblk),
        in_specs=[smem],
        out_specs=[smem, smem, smem],
        out_shape=[jax.ShapeDtypeStruct((N_EXPERTS,), I32),
                   jax.ShapeDtypeStruct((nblk,), I32),
                   jax.ShapeDtypeStruct((1,), I32)],
        name="moe_block_plan",
    )(counts)


def _dest_kernel(info_ref, ps_ref, tri_ref, o_ref, carry_scr):
    @pl.when(pl.program_id(0) == 0)
    def _():
        carry_scr[...] = jnp.zeros(carry_scr.shape, F32)

    info = info_ref[...]
    lane = lax.broadcasted_iota(I32, info.shape, 1)
    lanef = lane.astype(F32)
    o1 = jnp.where(lanef == info[:, 0:1], 1.0, 0.0)
    o2 = jnp.where(lanef == info[:, 1:2], 1.0, 0.0)
    oh = o1 + o2
    before = jnp.dot(tri_ref[...], oh.astype(BF16), preferred_element_type=F32)
    base = before + carry_scr[...] + ps_ref[...]
    d1 = jnp.sum(o1 * base, axis=1, keepdims=True)
    d2 = jnp.sum(o2 * base, axis=1, keepdims=True)
    o_ref[...] = jnp.where(lane == 0, d1, jnp.where(lane == 1, d2, 0.0))
    carry_scr[...] += jnp.sum(oh, axis=0, keepdims=True)


def _dest(info, ps_lanes, tri):
    n = info.shape[0]
    tm = tri.shape[0]
    return pl.pallas_call(
        _dest_kernel,
        grid=(n // tm,),
        in_specs=[pl.BlockSpec((tm, LANES), lambda i: (i, 0)),
                  pl.BlockSpec((1, LANES), lambda i: (0, 0)),
                  pl.BlockSpec((tm, tm), lambda i: (0, 0))],
        out_specs=pl.BlockSpec((tm, LANES), lambda i: (i, 0)),
        out_shape=jax.ShapeDtypeStruct((n, LANES), F32),
        scratch_shapes=[pltpu.VMEM((1, LANES), F32)],
        compiler_params=_cparams(("arbitrary",)),
        name="moe_dest_rows",
    )(info, ps_lanes, tri)


SCATTER_SLOTS = 3


def _scatter_kernel(d0_ref, d1_ref, cnt_ref, ps_ref, nu_ref, h_ref, xb_ref,
                    zero_scr, stage_scr, load_sem, out_sem, zero_sem, *, nblk):
    n_tiles = h_ref.shape[0] // TOK_TILE
    zero_scr[...] = jnp.zeros(zero_scr.shape, zero_scr.dtype)
    pad_sizes = [s for s in (1 << k for k in range(MOE_BLK.bit_length() - 2, -1, -1))
                 if s >= SUBLANES]

    def zero_copy(rows, dst_row):
        return pltpu.make_async_copy(zero_scr.at[pl.ds(0, rows)], xb_ref.at[pl.ds(dst_row, rows)],
                                     zero_sem)

    def load(j, slot):
        return pltpu.make_async_copy(h_ref.at[pl.ds(j * TOK_TILE, TOK_TILE)], stage_scr.at[slot],
                                     load_sem.at[slot])

    def row_copy(slot, r, dst_row, parity, rows=1):
        return pltpu.make_async_copy(stage_scr.at[slot, pl.ds(r, rows)],
                                     xb_ref.at[pl.ds(dst_row, rows)], out_sem.at[parity])

    def for_each_zero_copy(act):
        def per_expert(e, c):
            pos = ps_ref[e] + cnt_ref[e]
            pad = (MOE_BLK - (cnt_ref[e] & (MOE_BLK - 1))) & (MOE_BLK - 1)
            head = pad & (SUBLANES - 1)
            for k in range(SUBLANES - 1):
                @pl.when(k < head)
                def _():
                    act(zero_copy(1, pos + k))
            pos = pl.multiple_of(pos + head, SUBLANES)
            for size in pad_sizes:
                @pl.when((pad & size) != 0)
                def _():
                    act(zero_copy(size, pos))
                pos = pl.multiple_of(pos + (pad & size), SUBLANES)
            return c

        def per_tail_block(i, c):
            act(zero_copy(MOE_BLK, i * MOE_BLK))
            return c

        lax.fori_loop(0, N_EXPERTS, per_expert, 0)
        lax.fori_loop(nu_ref[0], nblk, per_tail_block, 0)

    for_each_zero_copy(lambda cp: cp.start())

    def retire(parity):
        row_copy(0, 0, 0, parity, TOK_TILE).wait()
        row_copy(0, 0, 0, parity, TOK_TILE).wait()

    def tile(j, c):
        slot = j % SCATTER_SLOTS
        parity = j % 2
        load(j, slot).wait()

        @pl.when(j + 1 < n_tiles)
        def _():
            load(j + 1, (j + 1) % SCATTER_SLOTS).start()

        def issue(g, cc):
            for k in range(SUBLANES):
                r = g * SUBLANES + k
                t = j * TOK_TILE + r
                row_copy(slot, r, d0_ref[t], parity).start()
                row_copy(slot, r, d1_ref[t], parity).start()
            return cc

        lax.fori_loop(0, TOK_TILE // SUBLANES, issue, 0)

        @pl.when(j >= 1)
        def _():
            retire(1 - parity)
        return c

    load(0, 0).start()
    lax.fori_loop(0, n_tiles, tile, 0)
    retire((n_tiles - 1) % 2)
    for_each_zero_copy(lambda cp: cp.wait())


def _scatter_rows(d0, d1, counts, ps, nu, h2, p_rows):
    n, d = h2.shape
    smem = pl.BlockSpec(memory_space=pltpu.SMEM)
    return pl.pallas_call(
        functools.partial(_scatter_kernel, nblk=p_rows // MOE_BLK),
        in_specs=[smem, smem, smem, smem, smem, pl.BlockSpec(memory_space=pl.ANY)],
        out_specs=pl.BlockSpec(memory_space=pl.ANY),
        out_shape=jax.ShapeDtypeStruct((p_rows, d), h2.dtype),
        scratch_shapes=[pltpu.VMEM((MOE_BLK, d), h2.dtype),
                        pltpu.VMEM((SCATTER_SLOTS, TOK_TILE, d), h2.dtype),
                        pltpu.SemaphoreType.DMA((SCATTER_SLOTS,)),
                        pltpu.SemaphoreType.DMA((2,)),
                        pltpu.SemaphoreType.DMA(())],
        compiler_params=_cparams(),
        name="moe_scatter_rows",
    )(d0, d1, counts, ps, nu, h2)


def _expert_kernel(be_ref, nu_ref, x_ref, wg_ref, wu_ref, wd_ref, o_ref, wg_scr, wu_scr, wd_scr):
    i = pl.program_id(0)
    new_expert = jnp.logical_or(i == 0, be_ref[i] != be_ref[jnp.maximum(i - 1, 0)])

    @pl.when(jnp.logical_and(new_expert, i < nu_ref[0]))
    def _():
        wg_scr[...] = wg_ref[0].astype(BF16)
        wu_scr[...] = wu_ref[0].astype(BF16)
        wd_scr[...] = wd_ref[0].astype(BF16)

    @pl.when(i < nu_ref[0])
    def _():
        x = x_ref[...].astype(BF16)
        g = jnp.dot(x, wg_scr[...], preferred_element_type=F32)
        u = jnp.dot(x, wu_scr[...], preferred_element_type=F32)
        act = (g * (1.0 / (1.0 + jnp.exp(-g))) * u).astype(BF16)
        o_ref[...] = jnp.dot(act, wd_scr[...], preferred_element_type=F32)

    @pl.when(i >= nu_ref[0])
    def _():
        o_ref[...] = jnp.zeros(o_ref.shape, o_ref.dtype)


def _experts(be, nu, xb, wg, wu, wd):
    p, d = xb.shape
    de = wg.shape[2]
    return pl.pallas_call(
        _expert_kernel,
        grid_spec=pltpu.PrefetchScalarGridSpec(
            num_scalar_prefetch=2,
            grid=(p // MOE_BLK,),
            in_specs=[pl.BlockSpec((MOE_BLK, d), lambda i, be, nu: (i, 0)),
                      pl.BlockSpec((1, d, de), lambda i, be, nu: (be[i], 0, 0)),
                      pl.BlockSpec((1, d, de), lambda i, be, nu: (be[i], 0, 0)),
                      pl.BlockSpec((1, de, d), lambda i, be, nu: (be[i], 0, 0))],
            out_specs=pl.BlockSpec((MOE_BLK, d), lambda i, be, nu: (i, 0)),
            scratch_shapes=[pltpu.VMEM((d, de), BF16), pltpu.VMEM((d, de), BF16),
                            pltpu.VMEM((de, d), BF16)]),
        out_shape=jax.ShapeDtypeStruct((p, d), F32),
        compiler_params=_cparams(("arbitrary",)),
        name="moe_expert_ffn",
    )(be, nu, xb, wg, wu, wd)


def _combine_kernel(d0_ref, d1_ref, x_ref, info_ref, g2_ref, yb_ref, o_ref, rows_scr, sem):
    i = pl.program_id(0)
    cur = i % 2

    def copy(buf, which, r, src_row, rows=1):
        return pltpu.make_async_copy(yb_ref.at[pl.ds(src_row, rows)],
                                     rows_scr.at[buf, which, pl.ds(r, rows)], sem.at[buf])

    def start_gather(step, buf):
        base = step * TOK_TILE

        def issue(g, c):
            for k in range(SUBLANES):
                r = g * SUBLANES + k
                copy(buf, 0, r, d0_ref[base + r]).start()
                copy(buf, 1, r, d1_ref[base + r]).start()
            return c

        lax.fori_loop(0, TOK_TILE // SUBLANES, issue, 0)

    @pl.when(i == 0)
    def _():
        start_gather(0, 0)

    @pl.when(i + 1 < pl.num_programs(0))
    def _():
        start_gather(i + 1, 1 - cur)

    copy(cur, 0, 0, 0, TOK_TILE).wait()
    copy(cur, 1, 0, 0, TOK_TILE).wait()

    info = info_ref[...]
    y = info[:, 2:3] * rows_scr[cur, 0] + info[:, 3:4] * rows_scr[cur, 1]
    o_ref[...] = x_ref[...] + g2_ref[0] * y


def _combine(d0, d1, x1, info, g2, yb, seq):
    n, d = x1.shape
    tpb = seq // TOK_TILE
    return pl.pallas_call(
        _combine_kernel,
        grid_spec=pltpu.PrefetchScalarGridSpec(
            num_scalar_prefetch=2,
            grid=(n // TOK_TILE,),
            in_specs=[pl.BlockSpec((TOK_TILE, d), lambda i, a, b: (i, 0)),
                      pl.BlockSpec((TOK_TILE, LANES), lambda i, a, b: (i, 0)),
                      pl.BlockSpec((1, 1, d), lambda i, a, b: (i // tpb, 0, 0)),
                      pl.BlockSpec(memory_space=pl.ANY)],
            out_specs=pl.BlockSpec((TOK_TILE, d), lambda i, a, b: (i, 0)),
            scratch_shapes=[pltpu.VMEM((2, 2, TOK_TILE, d), F32),
                            pltpu.SemaphoreType.DMA((2,))]),
        out_shape=jax.ShapeDtypeStruct((n, d), F32),
        compiler_params=_cparams(("arbitrary",)),
        name="moe_combine",
    )(d0, d1, x1, info, g2, yb)


def _tri_inclusive_rev(k):
    l = (np.arange(k)[:, None] >= np.arange(k)[None, :]).astype(np.float32)
    half = np.concatenate([l, np.ones((k, k), np.float32)], axis=1)
    return jnp.asarray(np.concatenate([half, half], axis=0), dtype=BF16)


def _tri_strict_lower(k):
    return jnp.asarray((np.arange(k)[None, :] < np.arange(k)[:, None]).astype(np.float32), dtype=BF16)


def kernel(x, c, w_mod, b_mod, ln1_g, w_in, q_norm_g, w_q_up, q_gain, k_gain, rel_bias, gn_a, gn_b,
           w_out, ln2_g, router_g, router_e, w_gate, w_up, w_down):
    batch, seq, d = x.shape
    n = batch * seq
    assert w_mod.shape[0] == 1 and d == D_A + D_B
    assert seq % KB_A == 0 and seq % TOK_TILE == 0 and n % 512 == 0
    x2 = x.reshape(n, d)

    c8 = jnp.pad(c, ((0, 8 - batch), (0, 0)))
    mod = _modulation(c8, w_mod.reshape(d, -1), b_mod.reshape(1, -1))[:batch]
    sh1, sc1, g1, sh2, sc2, g2 = [m.reshape(batch, 1, d) for m in jnp.split(mod, 6, axis=-1)]

    wi = w_in.reshape(d, -1)
    n_a = Q_RANK + 2 * HEAD_DIM + IDX_DIM + IDX_HEADS
    wi = jnp.concatenate([wi[:, n_a:], wi[:, :n_a],
                          jnp.zeros((d, D_IN_PAD - wi.shape[1]), wi.dtype)], axis=1).astype(BF16)
    proj = _ln_proj(x2, ln1_g.reshape(1, d), sc1, sh1, wi, seq)

    qa, qidx, ka, va, kidx, widx = _aprep(
        proj, q_norm_g.reshape(1, -1), w_q_up.reshape(Q_RANK, -1).astype(BF16),
        q_gain.reshape(1, -1), k_gain.reshape(1, -1))

    bt = _bias_tiles(rel_bias)
    bound = _logit_bound(q_gain.reshape(1, -1), k_gain.reshape(1, -1), rel_bias)[0, :1]
    oa = _attn_a(qidx, widx.T, qa, kidx, ka, va, bt, gn_a.reshape(1, -1), bound, batch, seq)
    ob = _attn_b(proj, _tri_inclusive_rev(QB_B), batch, seq)

    rw = jnp.concatenate([router_e.reshape(d, -1), router_g.reshape(d, -1),
                          jnp.zeros((d, LANES - N_EXPERTS - N_GROUPS), F32)], axis=1)
    x1, h2, lg = _out_proj(x2, oa, ob, gn_b.reshape(1, -1), w_out.reshape(d, d).astype(BF16),
                           g1, ln2_g.reshape(1, d), sc2, sh2, rw, seq)

    info, cnt = _route(lg)
    counts = cnt[0, :N_EXPERTS].astype(I32)
    p_rows = 2 * n + N_EXPERTS * MOE_BLK
    ps, be, nu = _plan(counts, p_rows // MOE_BLK)
    ps_lanes = jnp.pad(ps.astype(F32), (0, LANES - N_EXPERTS)).reshape(1, LANES)
    dinfo = _dest(info, ps_lanes, _tri_strict_lower(512))
    d0 = dinfo[:, 0].astype(I32)
    d1 = dinfo[:, 1].astype(I32)
    xb = _scatter_rows(d0, d1, counts, ps, nu, h2, p_rows)
    yb = _experts(be, nu, xb,
                  w_gate.reshape(N_EXPERTS, d, D_EXPERT),
                  w_up.reshape(N_EXPERTS, d, D_EXPERT),
                  w_down.reshape(N_EXPERTS, D_EXPERT, d))
    out = _combine(d0, d1, x1, info, g2, yb, seq)
    return out.reshape(batch, seq, d)
```

```python
import functools
import math

import numpy as np
import jax
import jax.numpy as jnp
from jax import lax
from jax.experimental import pallas as pl
from jax.experimental.pallas import tpu as pltpu

F32 = jnp.float32
BF16 = jnp.bfloat16
I32 = jnp.int32

HEAD_DIM = 128
N_HEADS_A = 8
N_HEADS_B = 8
D_A = N_HEADS_A * HEAD_DIM
D_B = N_HEADS_B * HEAD_DIM
Q_RANK = 512
IDX_HEADS = 16
IDX_DIM = 64
TOPK_MAX = 256
N_BUCKETS = 32
MAX_DISTANCE = 128
N_GROUPS = 4
EXPERTS_PER_GROUP = 8
N_EXPERTS = N_GROUPS * EXPERTS_PER_GROUP
D_EXPERT = 512
EPS = 1e-6

LANES = 128
SUBLANES = 8
VMEM_LIMIT = 56 * 1024 * 1024
NEG = -1e30
INT_MIN = -(2 ** 31)
EXP2_UNDERFLOW = 150.0
LOG2E = math.log2(math.e)
PLAIN_SOFTMAX_LIMIT = 64.0

QB_A = 128
KB_A = 256
QB_B = 128
MOE_BLK = 256
TOK_TILE = 256

COL_QB, COL_KB, COL_VB, COL_A = 0, D_B, 2 * D_B, 3 * D_B
A_CQ, A_KA, A_VA, A_KIDX, A_WIDX = 0, 512, 640, 768, 832
A_WIDTH = 1024
D_IN_PAD = COL_A + A_WIDTH


def _cparams(sem=None):
    return pltpu.CompilerParams(dimension_semantics=sem, vmem_limit_bytes=VMEM_LIMIT)


def _rms(x):
    return x * lax.rsqrt(jnp.mean(x * x, axis=-1, keepdims=True) + EPS)


def _pack_bf16_pairs(x):
    c = x.shape[1] // 2
    hi = pltpu.bitcast(x[:, :c].astype(BF16).astype(F32), I32)
    lo = pltpu.bitcast(x[:, c:].astype(BF16).astype(F32), I32)
    return hi | lax.shift_right_logical(lo, 16)


def _unpack_bf16_pairs(u):
    hi = pltpu.bitcast(u & jnp.int32(-65536), F32)
    lo = pltpu.bitcast(lax.shift_left(u, 16), F32)
    return hi, lo


def _mod_kernel(c_ref, w_ref, b_ref, o_ref):
    c = c_ref[...]
    s = c * (1.0 / (1.0 + jnp.exp(-c)))
    o_ref[...] = jnp.dot(s, w_ref[...], preferred_element_type=F32,
                         precision=lax.Precision.HIGHEST) + b_ref[...]


def _modulation(c8, w_mod, b_mod):
    d, n6 = w_mod.shape
    tn = 1024
    return pl.pallas_call(
        _mod_kernel,
        grid=(n6 // tn,),
        in_specs=[pl.BlockSpec((8, d), lambda j: (0, 0)),
                  pl.BlockSpec((d, tn), lambda j: (0, j)),
                  pl.BlockSpec((1, tn), lambda j: (0, j))],
        out_specs=pl.BlockSpec((8, tn), lambda j: (0, j)),
        out_shape=jax.ShapeDtypeStruct((8, n6), F32),
        compiler_params=_cparams(("arbitrary",)),
        name="modulation",
    )(c8, w_mod, b_mod)


def _ln_proj_kernel(x_ref, g_ref, sc_ref, sh_ref, w_ref, o_ref, h_scr, *, q_tiles):
    @pl.when(pl.program_id(1) == 0)
    def _():
        h = _rms(x_ref[...]) * g_ref[...]
        h = h * (1.0 + sc_ref[0]) + sh_ref[0]
        h_scr[...] = h.astype(BF16)

    col_scale = jnp.where(pl.program_id(1) < q_tiles, HEAD_DIM ** -0.5 * LOG2E, 1.0)
    acc = jnp.dot(h_scr[...], w_ref[...], preferred_element_type=F32)
    o_ref[...] = (acc * col_scale).astype(o_ref.dtype)


def _ln_proj(x2, ln_g, sc, sh, w, seq):
    n, d = x2.shape
    ncol = w.shape[1]
    tm = min(1024, seq)
    tn = 512
    tpb = seq // tm
    assert COL_QB == 0 and D_B % tn == 0
    return pl.pallas_call(
        functools.partial(_ln_proj_kernel, q_tiles=D_B // tn),
        grid=(n // tm, ncol // tn),
        in_specs=[pl.BlockSpec((tm, d), lambda i, j: (i, 0)),
                  pl.BlockSpec((1, d), lambda i, j: (0, 0)),
                  pl.BlockSpec((1, 1, d), lambda i, j: (i // tpb, 0, 0)),
                  pl.BlockSpec((1, 1, d), lambda i, j: (i // tpb, 0, 0)),
                  pl.BlockSpec((d, tn), lambda i, j: (0, j))],
        out_specs=pl.BlockSpec((tm, tn), lambda i, j: (i, j)),
        out_shape=jax.ShapeDtypeStruct((n, ncol), BF16),
        scratch_shapes=[pltpu.VMEM((tm, d), BF16)],
        compiler_params=_cparams(("parallel", "arbitrary")),
        name="ln_in_proj",
    )(x2, ln_g, sc, sh, w)


def _aprep_kernel(a_ref, qng_ref, wq_ref, qg_ref, kg_ref,
                  qa_ref, qidx_ref, ka_ref, va_ref, kidx_ref, widx_ref):
    cq = a_ref[:, A_CQ:A_CQ + Q_RANK].astype(F32)
    cqn = (_rms(cq) * qng_ref[...]).astype(BF16)
    qup = jnp.dot(cqn, wq_ref[...], preferred_element_type=F32)
    for h in range(N_HEADS_A):
        qh = qup[:, h * HEAD_DIM:(h + 1) * HEAD_DIM]
        qn = _rms(qh) * qg_ref[...] * (HEAD_DIM ** -0.5 * LOG2E)
        qa_ref[:, h * HEAD_DIM:(h + 1) * HEAD_DIM] = qn.astype(BF16)
    for h in range(IDX_HEADS):
        qi = qup[:, D_A + h * IDX_DIM:D_A + (h + 1) * IDX_DIM] * (IDX_DIM ** -0.5)
        qidx_ref[h] = qi.astype(BF16)
    ka = a_ref[:, A_KA:A_KA + HEAD_DIM].astype(F32)
    ka_ref[...] = (_rms(ka) * kg_ref[...]).astype(BF16)
    va_ref[...] = a_ref[:, A_VA:A_VA + HEAD_DIM]
    kidx_ref[...] = a_ref[:, A_KIDX:A_KIDX + IDX_DIM]
    widx_ref[...] = a_ref[:, A_WIDX:A_WIDX + IDX_HEADS].astype(F32) * (IDX_HEADS ** -0.5)


def _aprep(proj, q_norm_g, w_q_up, q_gain, k_gain):
    n = proj.shape[0]
    tm = 512
    cblk = COL_A // A_WIDTH
    nup = w_q_up.shape[1]
    return pl.pallas_call(
        _aprep_kernel,
        grid=(n // tm,),
        in_specs=[pl.BlockSpec((tm, A_WIDTH), lambda i: (i, cblk)),
                  pl.BlockSpec((1, Q_RANK), lambda i: (0, 0)),
                  pl.BlockSpec((Q_RANK, nup), lambda i: (0, 0)),
                  pl.BlockSpec((1, HEAD_DIM), lambda i: (0, 0)),
                  pl.BlockSpec((1, HEAD_DIM), lambda i: (0, 0))],
        out_specs=[pl.BlockSpec((tm, D_A), lambda i: (i, 0)),
                   pl.BlockSpec((IDX_HEADS, tm, IDX_DIM), lambda i: (0, i, 0)),
                   pl.BlockSpec((tm, HEAD_DIM), lambda i: (i, 0)),
                   pl.BlockSpec((tm, HEAD_DIM), lambda i: (i, 0)),
                   pl.BlockSpec((tm, IDX_DIM), lambda i: (i, 0)),
                   pl.BlockSpec((tm, IDX_HEADS), lambda i: (i, 0))],
        out_shape=[jax.ShapeDtypeStruct((n, D_A), BF16),
                   jax.ShapeDtypeStruct((IDX_HEADS, n, IDX_DIM), BF16),
                   jax.ShapeDtypeStruct((n, HEAD_DIM), BF16),
                   jax.ShapeDtypeStruct((n, HEAD_DIM), BF16),
                   jax.ShapeDtypeStruct((n, IDX_DIM), BF16),
                   jax.ShapeDtypeStruct((n, IDX_HEADS), F32)],
        compiler_params=_cparams(("parallel",)),
        name="group_a_prep",
    )(proj, q_norm_g, w_q_up, q_gain, k_gain)


def _t5_bucket_starts():
    max_exact = N_BUCKETS // 2
    d = np.arange(0, 4 * MAX_DISTANCE, dtype=np.int64)
    df = np.maximum(d, 1).astype(np.float32)
    large = max_exact + (np.log(df / np.float32(max_exact)) / np.float32(math.log(MAX_DISTANCE / max_exact))
                         * np.float32(N_BUCKETS - max_exact)).astype(np.int32)
    large = np.minimum(large, N_BUCKETS - 1)
    bucket = np.where(d < max_exact, d, large)
    assert np.all(np.diff(bucket) >= 0) and bucket[-1] == N_BUCKETS - 1
    return [int(np.argmax(bucket >= b)) for b in range(N_BUCKETS)]


_BUCKET_START = _t5_bucket_starts()
N_BIAS_TILES = 2 * KB_A // LANES


def _bias_kernel(rb_ref, o_ref):
    di = pl.program_id(0)
    h = pl.program_id(1)
    i = lax.broadcasted_iota(I32, (QB_A, KB_A), 0)
    j = lax.broadcasted_iota(I32, (QB_A, KB_A), 1)
    d = di * LANES + i - j
    val = jnp.full((QB_A, KB_A), rb_ref[0, h], F32)
    for b in range(1, N_BUCKETS):
        val = jnp.where(d >= _BUCKET_START[b], rb_ref[b, h], val)
    o_ref[0, 0] = (val - rb_ref[N_BUCKETS - 1, h]) * LOG2E


def _bias_tiles(rel_bias):
    return pl.pallas_call(
        _bias_kernel,
        grid=(N_BIAS_TILES, N_HEADS_A),
        in_specs=[pl.BlockSpec(memory_space=pltpu.SMEM)],
        out_specs=pl.BlockSpec((1, 1, QB_A, KB_A), lambda a, h: (a, h, 0, 0)),
        out_shape=jax.ShapeDtypeStruct((N_BIAS_TILES, N_HEADS_A, QB_A, KB_A), F32),
        compiler_params=_cparams(("arbitrary", "arbitrary")),
        name="t5_bias_tiles",
    )(rel_bias)


def _bound_kernel(qg_ref, kg_ref, rb_ref, o_ref):
    qmax = jnp.max(jnp.abs(qg_ref[...]), axis=1, keepdims=True)
    kmax = jnp.max(jnp.abs(kg_ref[...]), axis=1, keepdims=True)
    rb = rb_ref[...]
    shifted = jnp.abs(rb - rb[N_BUCKETS - 1:N_BUCKETS, :])
    bmax = jnp.max(jnp.max(shifted, axis=1, keepdims=True), axis=0, keepdims=True)
    bound = qmax * kmax * (math.sqrt(HEAD_DIM) * 1.02) + bmax
    o_ref[...] = jnp.broadcast_to(bound, o_ref.shape)


def _logit_bound(q_gain, k_gain, rel_bias):
    return pl.pallas_call(
        _bound_kernel,
        out_shape=jax.ShapeDtypeStruct((1, LANES), F32),
        name="dsa_logit_bound",
    )(q_gain, k_gain, rel_bias)


def _attn_a_kernel(qidx_ref, w_ref, qa_ref, kidx_ref, ka_ref, va_ref, bt_ref, gn_ref, bound_ref, o_ref,
                   keys_scr, keyt_scr, m_scr, acc_scr, s_scr, mb_scr, *, topk):
    qb = pl.program_id(1)
    t0 = qb * QB_A
    kbl = (t0 + QB_A - 1) // KB_A
    row = t0 + lax.broadcasted_iota(I32, (QB_A, KB_A), 0)
    col0 = lax.broadcasted_iota(I32, (QB_A, KB_A), 1)
    nt = (((1,), (1,)), ((), ()))

    reps = KB_A // LANES

    def sort_key(v):
        bits = pltpu.bitcast(v, I32)
        return jnp.where(bits < 0, bits ^ jnp.int32(0x7FFFFFFF), bits)

    def key_value(k):
        return pltpu.bitcast(jnp.where(k < 0, k ^ jnp.int32(0x7FFFFFFF), k), F32)

    key_pos = lax.broadcasted_iota(I32, (KB_A, QB_A), 0)
    qry_pos = t0 + lax.broadcasted_iota(I32, (KB_A, QB_A), 1)

    def score_tile(kb, carry):
        smin, smax = carry
        kt = kidx_ref[pl.ds(pl.multiple_of(kb * KB_A, KB_A), KB_A), :]
        score = jnp.zeros((KB_A, QB_A), F32)
        for hp in range(IDX_HEADS // 2):
            q2 = qidx_ref[2 * hp:2 * hp + 2].reshape(2 * QB_A, IDX_DIM)
            sc = lax.dot_general(kt, q2, nt, preferred_element_type=F32)
            score = score + w_ref[2 * hp:2 * hp + 1, :] * jnp.maximum(sc[:, :QB_A], 0.0)
            score = score + w_ref[2 * hp + 1:2 * hp + 2, :] * jnp.maximum(sc[:, QB_A:], 0.0)
        causal = (kb * KB_A + key_pos) <= qry_pos
        key_t = jnp.where(causal, sort_key(score), jnp.int32(INT_MIN))
        keyt_scr[kb] = key_t
        keys_scr[kb] = key_t.T
        smin = jnp.minimum(smin, jnp.min(jnp.where(causal, score, jnp.inf), axis=0, keepdims=True))
        smax = jnp.maximum(smax, jnp.max(jnp.where(causal, score, -jnp.inf), axis=0, keepdims=True))
        return smin, smax

    def score_pair(i, carry):
        return score_tile(2 * i + 1, score_tile(2 * i, carry))

    n_tiles = kbl + 1
    extremes = lax.fori_loop(0, n_tiles // 2, score_pair,
                             (jnp.full((1, QB_A), jnp.inf, F32), jnp.full((1, QB_A), -jnp.inf, F32)))
    smin, smax = lax.cond(n_tiles % 2 == 1, lambda c: score_tile(n_tiles - 1, c), lambda c: c, extremes)

    qry1 = t0 + lax.broadcasted_iota(I32, (1, QB_A), 1)
    kf = float(topk)
    acc_rows = 32

    def search_pass(p, state):
        lo, hi, clo, chi, open_q = state
        lo_v = key_value(lo)
        hi_v = key_value(hi)
        gap = clo - chi
        frac = (clo - (kf - 0.5)) / gap
        frac = jnp.where(p % 2 == 1, 0.7 * frac + 0.15, frac)
        frac = jnp.where(gap > 16.0, frac, 0.5)
        cand = sort_key(lo_v + (hi_v - lo_v) * frac)
        cand = jnp.where(p % 8 == 7, lo + lax.shift_right_logical(hi - lo, 1), cand)
        cand = jnp.minimum(jnp.maximum(cand, lo + 1), hi - 1)

        def count(kb, cnt):
            ge = jnp.where(keyt_scr[kb] >= cand, 1.0, 0.0)
            return cnt + jnp.sum(ge.reshape(KB_A // acc_rows, acc_rows, QB_A), axis=0)

        cnt = lax.fori_loop(0, kbl + 1, count, jnp.zeros((acc_rows, QB_A), F32))
        tot = jnp.sum(cnt, axis=0, keepdims=True)
        ge = tot >= kf
        lo = jnp.where(ge, cand, lo)
        clo = jnp.where(ge, tot, clo)
        hi = jnp.where(ge, hi, cand)
        chi = jnp.where(ge, chi, tot)
        width = hi - lo
        settled = jnp.where(clo == kf, 1.0, jnp.where(width == 1, 1.0, 0.0))
        open_q = jnp.where(settled > 0.5, 0.0, open_q)
        return lo, hi, clo, chi, open_q

    def search_step(carry):
        p, state, _ = carry
        state = search_pass(p + 1, search_pass(p, state))
        return p + 2, state, jnp.max(state[4]) > 0.0

    n_causal = (qry1 + 1).astype(F32)
    open0 = jnp.where(qry1 >= topk, 1.0, 0.0)
    lo0 = sort_key(smin)
    hi0 = sort_key(smax) + 1
    open0 = jnp.where(hi0 - lo0 == 1, 0.0, open0)
    _, (lo, _, _, _, _), _ = lax.while_loop(
        lambda c: c[2], search_step,
        (jnp.int32(0), (lo0, hi0, n_causal, jnp.zeros((1, QB_A), F32), open0), jnp.max(open0) > 0.0))
    thr = jnp.where(qry1 >= topk, lo, jnp.int32(INT_MIN))
    thr = jnp.broadcast_to(thr, (QB_A, QB_A)).T
    thrb = jnp.tile(thr, (1, reps))

    m_scr[...] = jnp.full(m_scr.shape, NEG, F32)
    acc_scr[...] = jnp.zeros(acc_scr.shape, F32)

    plain = bound_ref[0] <= PLAIN_SOFTMAX_LIMIT

    def attend(kb, near, online):
        mb = jnp.where(keys_scr[kb] >= thrb, 0.0, NEG)
        if near:
            mb = jnp.where((kb * KB_A + col0) <= row, mb, NEG)
            di = (t0 - kb * KB_A) // LANES
        mb_scr[...] = mb
        start = pl.multiple_of(kb * KB_A, KB_A)
        kt = ka_ref[pl.ds(start, KB_A), :]
        vt = jnp.concatenate([va_ref[pl.ds(start, KB_A), :], jnp.ones((KB_A, LANES), BF16)], axis=1)
        for h in range(N_HEADS_A):
            q = qa_ref[:, h * HEAD_DIM:(h + 1) * HEAD_DIM]
            s_scr[h] = lax.dot_general(q, kt, nt, preferred_element_type=F32)
        for h in range(N_HEADS_A):
            s = s_scr[h] + mb_scr[...]
            if near:
                s = s + bt_ref[di, h]
            if online:
                m_prev = m_scr[h]
                m_new = jnp.maximum(m_prev, jnp.max(s, axis=1, keepdims=True))
                alpha = jnp.exp2(m_prev - m_new)
                p = jnp.exp2(s - jnp.tile(m_new, (1, reps)))
                acc_scr[h] = (jnp.tile(alpha, (1, 2)) * acc_scr[h]
                              + jnp.dot(p.astype(BF16), vt, preferred_element_type=F32))
                m_scr[h] = m_new
            else:
                acc_scr[h] += jnp.dot(jnp.exp2(s).astype(BF16), vt, preferred_element_type=F32)

    def attend_all(online):
        n_far = jnp.maximum(kbl - 1, 0)

        def far_pair(i, carry):
            attend(2 * i, False, online)
            attend(2 * i + 1, False, online)
            return carry

        lax.fori_loop(0, n_far // 2, far_pair, 0)

        @pl.when(n_far % 2 == 1)
        def _():
            attend(n_far - 1, False, online)

        @pl.when(kbl >= 1)
        def _():
            attend(kbl - 1, True, online)

        attend(kbl, True, online)

    @pl.when(plain)
    def _():
        attend_all(False)

    @pl.when(jnp.logical_not(plain))
    def _():
        attend_all(True)

    ssq = jnp.zeros((QB_A, LANES), F32)
    for h in range(N_HEADS_A):
        oh = acc_scr[h, :, :HEAD_DIM] / acc_scr[h, :, HEAD_DIM:]
        acc_scr[h, :, :HEAD_DIM] = oh
        ssq = ssq + jnp.sum(oh * oh, axis=1, keepdims=True)
    inv = lax.rsqrt(ssq * (1.0 / D_A) + EPS)
    for h in range(N_HEADS_A):
        sl = slice(h * HEAD_DIM, (h + 1) * HEAD_DIM)
        o_ref[:, sl] = (acc_scr[h, :, :HEAD_DIM] * inv * gn_ref[:, sl]).astype(o_ref.dtype)


def _attn_a(qidx, widx, qa, kidx, ka, va, bt, gn_a, bound, batch, seq):
    n = qa.shape[0]
    nq = seq // QB_A
    nkt = seq // KB_A
    topk = min(TOPK_MAX, seq // 4)
    return pl.pallas_call(
        functools.partial(_attn_a_kernel, topk=topk),
        grid=(batch, nq),
        in_specs=[pl.BlockSpec((IDX_HEADS, QB_A, IDX_DIM), lambda b, q: (0, b * nq + q, 0)),
                  pl.BlockSpec((IDX_HEADS, QB_A), lambda b, q: (0, b * nq + q)),
                  pl.BlockSpec((QB_A, D_A), lambda b, q: (b * nq + q, 0)),
                  pl.BlockSpec((seq, IDX_DIM), lambda b, q: (b, 0)),
                  pl.BlockSpec((seq, HEAD_DIM), lambda b, q: (b, 0)),
                  pl.BlockSpec((seq, HEAD_DIM), lambda b, q: (b, 0)),
                  pl.BlockSpec(bt.shape, lambda b, q: (0, 0, 0, 0)),
                  pl.BlockSpec((1, D_A), lambda b, q: (0, 0)),
                  pl.BlockSpec(memory_space=pltpu.SMEM)],
        out_specs=pl.BlockSpec((QB_A, D_A), lambda b, q: (b * nq + q, 0)),
        out_shape=jax.ShapeDtypeStruct((n, D_A), BF16),
        scratch_shapes=[pltpu.VMEM((nkt, QB_A, KB_A), I32),
                        pltpu.VMEM((nkt, KB_A, QB_A), I32),
                        pltpu.VMEM((N_HEADS_A, QB_A, LANES), F32),
                        pltpu.VMEM((N_HEADS_A, QB_A, 2 * HEAD_DIM), F32),
                        pltpu.VMEM((N_HEADS_A, QB_A, KB_A), F32),
                        pltpu.VMEM((QB_A, KB_A), F32)],
        compiler_params=_cparams(("parallel", "arbitrary")),
        name="dsa_attention",
    )(qidx, widx, qa, kidx, ka, va, bt, gn_a, bound)


def _attn_b_kernel(q_ref, k_ref, v_ref, tri_ref, o_ref, rest_scr, z_scr, cs_scr):
    qb = pl.program_id(1)
    row = lax.broadcasted_iota(I32, (QB_B, QB_B), 0)
    col = lax.broadcasted_iota(I32, (QB_B, QB_B), 1)
    strict = col < row
    nt = (((1,), (1,)), ((), ()))
    def step(kb, diag):
        start = pl.multiple_of(kb * QB_B, QB_B)
        heads = [slice(h * HEAD_DIM, (h + 1) * HEAD_DIM) for h in range(N_HEADS_B)]
        for h, sl in enumerate(heads):
            kt = k_ref[pl.ds(start, QB_B), sl]
            z_scr[h] = lax.dot_general(q_ref[:, sl], kt, nt, preferred_element_type=F32)
        for h, sl in enumerate(heads):
            z = z_scr[h]
            sp = jnp.maximum(z, 0.0) + jnp.log(1.0 + jnp.exp2(-jnp.abs(z))) * LOG2E
            if diag:
                sp = jnp.where(strict, sp, 0.0)
            hi = sp.astype(BF16)
            lo = (sp - hi.astype(F32)).astype(BF16)
            cs_scr[h] = jnp.dot(jnp.concatenate([hi, lo], axis=1), tri_ref[...],
                                preferred_element_type=F32)
        worst = None
        for h, sl in enumerate(heads):
            vt = v_ref[pl.ds(start, QB_B), sl]
            z = z_scr[h]
            cs = cs_scr[h, :, :QB_B]
            tot = cs_scr[h, :, QB_B:]
            if diag:
                a = jnp.where(strict, jnp.exp2(z - cs), 0.0)
                o_ref[:, sl] = jnp.dot(a.astype(BF16), vt, preferred_element_type=F32)
                rest = tot
            else:
                rest = rest_scr[h]
                a = jnp.exp2(z - cs - rest)
                o_ref[:, sl] += jnp.dot(a.astype(BF16), vt, preferred_element_type=F32)
                rest = rest + tot
            rest_scr[h] = rest
            worst = rest if worst is None else jnp.minimum(worst, rest)
        return jnp.min(worst)

    def more(kb, smallest):
        return jnp.logical_and(kb >= 0, smallest < EXP2_UNDERFLOW)

    def body(carry):
        kb, _ = carry
        return kb - 1, more(kb - 1, step(kb, False))

    lax.while_loop(lambda c: c[1], body, (qb - 1, more(qb - 1, step(qb, True))))


def _attn_b(proj, tri, batch, seq):
    n = proj.shape[0]
    nq = seq // QB_B
    return pl.pallas_call(
        _attn_b_kernel,
        grid=(batch, nq),
        in_specs=[pl.BlockSpec((QB_B, D_B), lambda b, q: (b * nq + q, COL_QB // D_B)),
                  pl.BlockSpec((seq, D_B), lambda b, q: (b, COL_KB // D_B)),
                  pl.BlockSpec((seq, D_B), lambda b, q: (b, COL_VB // D_B)),
                  pl.BlockSpec(tri.shape, lambda b, q: (0, 0))],
        out_specs=pl.BlockSpec((QB_B, D_B), lambda b, q: (b * nq + q, 0)),
        out_shape=jax.ShapeDtypeStruct((n, D_B), F32),
        scratch_shapes=[pltpu.VMEM((N_HEADS_B, QB_B, QB_B), F32),
                        pltpu.VMEM((N_HEADS_B, QB_B, QB_B), F32),
                        pltpu.VMEM((N_HEADS_B, QB_B, 2 * QB_B), F32)],
        compiler_params=_cparams(("parallel", "arbitrary")),
        name="stick_breaking_attention",
    )(proj, proj, proj, tri)


def _out_proj_kernel(x_ref, oa_ref, ob_ref, gnb_ref, w_ref, g1_ref, ln_ref, sc_ref, sh_ref, rw_ref,
                     x1_ref, h2_ref, lg_ref):
    obn = (_rms(ob_ref[...]) * gnb_ref[...]).astype(BF16)
    y = jnp.dot(oa_ref[...], w_ref[0:D_A, :], preferred_element_type=F32)
    y = y + jnp.dot(obn, w_ref[D_A:D_A + D_B, :], preferred_element_type=F32)
    x1 = x_ref[...] + g1_ref[0] * y
    x1_ref[...] = x1
    h2 = _rms(x1) * ln_ref[...]
    h2 = h2 * (1.0 + sc_ref[0]) + sh_ref[0]
    h2_ref[...] = _pack_bf16_pairs(h2)
    hh = h2.astype(BF16)
    hl = (h2 - hh.astype(F32)).astype(BF16)
    rw = rw_ref[...]
    rh = rw.astype(BF16)
    rl = (rw - rh.astype(F32)).astype(BF16)
    lg = jnp.dot(hh, rh, preferred_element_type=F32)
    lg = lg + jnp.dot(hh, rl, preferred_element_type=F32)
    lg = lg + jnp.dot(hl, rh, preferred_element_type=F32)
    lg_ref[...] = lg


def _out_proj(x2, oa, ob, gn_b, w_out, g1, ln_g, sc, sh, rw, seq):
    n, d = x2.shape
    tm = 256
    tpb = seq // tm
    row = lambda i: (i, 0)
    fixed = lambda i: (0, 0)
    perb = lambda i: (i // tpb, 0, 0)
    return pl.pallas_call(
        _out_proj_kernel,
        grid=(n // tm,),
        in_specs=[pl.BlockSpec((tm, d), row),
                  pl.BlockSpec((tm, D_A), row),
                  pl.BlockSpec((tm, D_B), row),
                  pl.BlockSpec((1, D_B), fixed),
                  pl.BlockSpec(w_out.shape, fixed),
                  pl.BlockSpec((1, 1, d), perb),
                  pl.BlockSpec((1, d), fixed),
                  pl.BlockSpec((1, 1, d), perb),
                  pl.BlockSpec((1, 1, d), perb),
                  pl.BlockSpec(rw.shape, fixed)],
        out_specs=[pl.BlockSpec((tm, d), row),
                   pl.BlockSpec((tm, d // 2), row),
                   pl.BlockSpec((tm, LANES), row)],
        out_shape=[jax.ShapeDtypeStruct((n, d), F32),
                   jax.ShapeDtypeStruct((n, d // 2), I32),
                   jax.ShapeDtypeStruct((n, LANES), F32)],
        compiler_params=_cparams(("parallel",)),
        name="out_proj_ln2_router",
    )(x2, oa, ob, gn_b, w_out, g1, ln_g, sc, sh, rw)


def _route_kernel(lg_ref, info_ref, cnt_ref):
    @pl.when(pl.program_id(0) == 0)
    def _():
        cnt_ref[...] = jnp.zeros(cnt_ref.shape, F32)

    lg = lg_ref[...]
    lane = lax.broadcasted_iota(I32, lg.shape, 1)
    lanef = lane.astype(F32)
    big = float(4 * LANES)
    gm = jnp.where(lane >= N_EXPERTS, jnp.where(lane < N_EXPERTS + N_GROUPS, 1.0, 0.0), 0.0) > 0.5
    lgm = jnp.where(gm, lg, NEG)
    mg = jnp.max(lgm, axis=1, keepdims=True)
    eg = jnp.where(gm, jnp.exp(lgm - mg), 0.0)
    pg = eg / jnp.sum(eg, axis=1, keepdims=True)
    gw = jnp.max(pg, axis=1, keepdims=True)
    gidx = jnp.min(jnp.where(gm, jnp.where(pg == gw, lanef - N_EXPERTS, big), big), axis=1, keepdims=True)
    lane_group = (lane // EXPERTS_PER_GROUP).astype(F32)
    em = jnp.where(lane < N_EXPERTS, jnp.where(lane_group == gidx, 1.0, 0.0), 0.0) > 0.5
    lem = jnp.where(em, lg, NEG)
    me = jnp.max(lem, axis=1, keepdims=True)
    ee = jnp.where(em, jnp.exp(lem - me), 0.0)
    pe = jnp.where(em, ee / jnp.sum(ee, axis=1, keepdims=True), -1.0)
    p1 = jnp.max(pe, axis=1, keepdims=True)
    i1 = jnp.min(jnp.where(pe == p1, lanef, big), axis=1, keepdims=True)
    pe2 = jnp.where(lanef == i1, -1.0, pe)
    p2 = jnp.max(pe2, axis=1, keepdims=True)
    i2 = jnp.min(jnp.where(pe2 == p2, lanef, big), axis=1, keepdims=True)
    den = p1 + p2
    g0 = gw * p1 / den
    g1 = gw * p2 / den
    info = jnp.where(lane == 0, i1, jnp.where(lane == 1, i2,
                     jnp.where(lane == 2, g0, jnp.where(lane == 3, g1, 0.0))))
    info_ref[...] = info
    oh = jnp.where(lanef == i1, 1.0, 0.0) + jnp.where(lanef == i2, 1.0, 0.0)
    cnt_ref[...] += jnp.sum(oh, axis=0, keepdims=True)


def _route(lg):
    n = lg.shape[0]
    tm = min(1024, n)
    return pl.pallas_call(
        _route_kernel,
        grid=(n // tm,),
        in_specs=[pl.BlockSpec((tm, LANES), lambda i: (i, 0))],
        out_specs=[pl.BlockSpec((tm, LANES), lambda i: (i, 0)),
                   pl.BlockSpec((1, LANES), lambda i: (0, 0))],
        out_shape=[jax.ShapeDtypeStruct((n, LANES), F32),
                   jax.ShapeDtypeStruct((1, LANES), F32)],
        compiler_params=_cparams(("arbitrary",)),
        name="moe_route",
    )(lg)


def _plan_kernel(cnt_ref, ps_ref, be_ref, nu_ref, *, nblk):
    def fill(i, c):
        be_ref[i] = N_EXPERTS - 1
        return c

    lax.fori_loop(0, nblk, fill, 0)

    def per_expert(e, pos):
        ps_ref[e] = pos * MOE_BLK
        nb = (cnt_ref[e] + MOE_BLK - 1) // MOE_BLK

        def mark(k, c):
            be_ref[pos + k] = e
            return c

        lax.fori_loop(0, nb, mark, 0)
        return pos + nb

    nu_ref[0] = lax.fori_loop(0, N_EXPERTS, per_expert, jnp.int32(0))


def _plan(counts, nblk):
    smem = pl.BlockSpec(memory_space=pltpu.SMEM)
    return pl.pallas_call(
        functools.partial(_plan_kernel, nblk=nblk),
        in_specs=[smem],
        out_specs=[smem, smem, smem],
        out_shape=[jax.ShapeDtypeStruct((N_EXPERTS,), I32),
                   jax.ShapeDtypeStruct((nblk,), I32),
                   jax.ShapeDtypeStruct((1,), I32)],
        name="moe_block_plan",
    )(counts)


def _dest_kernel(info_ref, ps_ref, tri_ref, o_ref, carry_scr):
    @pl.when(pl.program_id(0) == 0)
    def _():
        carry_scr[...] = jnp.zeros(carry_scr.shape, F32)

    info = info_ref[...]
    lane = lax.broadcasted_iota(I32, info.shape, 1)
    lanef = lane.astype(F32)
    o1 = jnp.where(lanef == info[:, 0:1], 1.0, 0.0)
    o2 = jnp.where(lanef == info[:, 1:2], 1.0, 0.0)
    oh = o1 + o2
    before = jnp.dot(tri_ref[...], oh.astype(BF16), preferred_element_type=F32)
    base = before + carry_scr[...] + ps_ref[...]
    d1 = jnp.sum(o1 * base, axis=1, keepdims=True)
    d2 = jnp.sum(o2 * base, axis=1, keepdims=True)
    o_ref[...] = jnp.where(lane == 0, d1, jnp.where(lane == 1, d2, 0.0))
    carry_scr[...] += jnp.sum(oh, axis=0, keepdims=True)


def _dest(info, ps_lanes, tri):
    n = info.shape[0]
    tm = tri.shape[0]
    return pl.pallas_call(
        _dest_kernel,
        grid=(n // tm,),
        in_specs=[pl.BlockSpec((tm, LANES), lambda i: (i, 0)),
                  pl.BlockSpec((1, LANES), lambda i: (0, 0)),
                  pl.BlockSpec((tm, tm), lambda i: (0, 0))],
        out_specs=pl.BlockSpec((tm, LANES), lambda i: (i, 0)),
        out_shape=jax.ShapeDtypeStruct((n, LANES), F32),
        scratch_shapes=[pltpu.VMEM((1, LANES), F32)],
        compiler_params=_cparams(("arbitrary",)),
        name="moe_dest_rows",
    )(info, ps_lanes, tri)


SCATTER_SLOTS = 3


def _scatter_kernel(d0_ref, d1_ref, cnt_ref, ps_ref, nu_ref, h_ref, xb_ref,
                    zero_scr, stage_scr, load_sem, out_sem, zero_sem, *, nblk):
    n_tiles = h_ref.shape[0] // TOK_TILE
    zero_scr[...] = jnp.zeros(zero_scr.shape, zero_scr.dtype)
    pad_sizes = [s for s in (1 << k for k in range(MOE_BLK.bit_length() - 2, -1, -1))
                 if s >= SUBLANES]

    def zero_copy(rows, dst_row):
        return pltpu.make_async_copy(zero_scr.at[pl.ds(0, rows)], xb_ref.at[pl.ds(dst_row, rows)],
                                     zero_sem)

    def load(j, slot):
        return pltpu.make_async_copy(h_ref.at[pl.ds(j * TOK_TILE, TOK_TILE)], stage_scr.at[slot],
                                     load_sem.at[slot])

    def row_copy(slot, r, dst_row, parity, rows=1):
        return pltpu.make_async_copy(stage_scr.at[slot, pl.ds(r, rows)],
                                     xb_ref.at[pl.ds(dst_row, rows)], out_sem.at[parity])

    def for_each_zero_copy(act):
        def per_expert(e, c):
            pos = ps_ref[e] + cnt_ref[e]
            pad = (MOE_BLK - (cnt_ref[e] & (MOE_BLK - 1))) & (MOE_BLK - 1)
            head = pad & (SUBLANES - 1)
            for k in range(SUBLANES - 1):
                @pl.when(k < head)
                def _():
                    act(zero_copy(1, pos + k))
            pos = pl.multiple_of(pos + head, SUBLANES)
            for size in pad_sizes:
                @pl.when((pad & size) != 0)
                def _():
                    act(zero_copy(size, pos))
                pos = pl.multiple_of(pos + (pad & size), SUBLANES)
            return c

        def per_tail_block(i, c):
            act(zero_copy(MOE_BLK, i * MOE_BLK))
            return c

        lax.fori_loop(0, N_EXPERTS, per_expert, 0)
        lax.fori_loop(nu_ref[0], nblk, per_tail_block, 0)

    for_each_zero_copy(lambda cp: cp.start())

    def retire(parity):
        row_copy(0, 0, 0, parity, TOK_TILE).wait()
        row_copy(0, 0, 0, parity, TOK_TILE).wait()

    def tile(j, c):
        slot = j % SCATTER_SLOTS
        parity = j % 2
        load(j, slot).wait()

        @pl.when(j + 1 < n_tiles)
        def _():
            load(j + 1, (j + 1) % SCATTER_SLOTS).start()

        def issue(g, cc):
            for k in range(SUBLANES):
                r = g * SUBLANES + k
                t = j * TOK_TILE + r
                row_copy(slot, r, d0_ref[t], parity).start()
                row_copy(slot, r, d1_ref[t], parity).start()
            return cc

        lax.fori_loop(0, TOK_TILE // SUBLANES, issue, 0)

        @pl.when(j >= 1)
        def _():
            retire(1 - parity)
        return c

    load(0, 0).start()
    lax.fori_loop(0, n_tiles, tile, 0)
    retire((n_tiles - 1) % 2)
    for_each_zero_copy(lambda cp: cp.wait())


def _scatter_rows(d0, d1, counts, ps, nu, h2, p_rows):
    n, d = h2.shape
    smem = pl.BlockSpec(memory_space=pltpu.SMEM)
    return pl.pallas_call(
        functools.partial(_scatter_kernel, nblk=p_rows // MOE_BLK),
        in_specs=[smem, smem, smem, smem, smem, pl.BlockSpec(memory_space=pl.ANY)],
        out_specs=pl.BlockSpec(memory_space=pl.ANY),
        out_shape=jax.ShapeDtypeStruct((p_rows, d), h2.dtype),
        scratch_shapes=[pltpu.VMEM((MOE_BLK, d), h2.dtype),
                        pltpu.VMEM((SCATTER_SLOTS, TOK_TILE, d), h2.dtype),
                        pltpu.SemaphoreType.DMA((SCATTER_SLOTS,)),
                        pltpu.SemaphoreType.DMA((2,)),
                        pltpu.SemaphoreType.DMA(())],
        compiler_params=_cparams(),
        name="moe_scatter_rows",
    )(d0, d1, counts, ps, nu, h2)


def _expert_kernel(be_ref, nu_ref, x_ref, wg_ref, wu_ref, wd_ref, o_ref, wg_scr, wu_scr, wd_scr):
    i = pl.program_id(0)
    new_expert = jnp.logical_or(i == 0, be_ref[i] != be_ref[jnp.maximum(i - 1, 0)])

    @pl.when(jnp.logical_and(new_expert, i < nu_ref[0]))
    def _():
        wg_scr[...] = wg_ref[0].astype(BF16)
        wu_scr[...] = wu_ref[0].astype(BF16)
        wd_scr[...] = wd_ref[0].astype(BF16)

    @pl.when(i < nu_ref[0])
    def _():
        x_hi, x_lo = _unpack_bf16_pairs(x_ref[...])
        x = jnp.concatenate([x_hi.astype(BF16), x_lo.astype(BF16)], axis=1)
        g = jnp.dot(x, wg_scr[...], preferred_element_type=F32)
        u = jnp.dot(x, wu_scr[...], preferred_element_type=F32)
        act = (g * (1.0 / (1.0 + jnp.exp(-g))) * u).astype(BF16)
        o_ref[...] = _pack_bf16_pairs(jnp.dot(act, wd_scr[...], preferred_element_type=F32))

    @pl.when(i >= nu_ref[0])
    def _():
        o_ref[...] = jnp.zeros(o_ref.shape, o_ref.dtype)


def _experts(be, nu, xb, wg, wu, wd):
    p = xb.shape[0]
    _, d, de = wg.shape
    return pl.pallas_call(
        _expert_kernel,
        grid_spec=pltpu.PrefetchScalarGridSpec(
            num_scalar_prefetch=2,
            grid=(p // MOE_BLK,),
            in_specs=[pl.BlockSpec((MOE_BLK, d // 2), lambda i, be, nu: (i, 0)),
                      pl.BlockSpec((1, d, de), lambda i, be, nu: (be[i], 0, 0)),
                      pl.BlockSpec((1, d, de), lambda i, be, nu: (be[i], 0, 0)),
                      pl.BlockSpec((1, de, d), lambda i, be, nu: (be[i], 0, 0))],
            out_specs=pl.BlockSpec((MOE_BLK, d // 2), lambda i, be, nu: (i, 0)),
            scratch_shapes=[pltpu.VMEM((d, de), BF16), pltpu.VMEM((d, de), BF16),
                            pltpu.VMEM((de, d), BF16)]),
        out_shape=jax.ShapeDtypeStruct((p, d // 2), I32),
        compiler_params=_cparams(("arbitrary",)),
        name="moe_expert_ffn",
    )(be, nu, xb, wg, wu, wd)


def _combine_kernel(d0_ref, d1_ref, x_ref, info_ref, g2_ref, yb_ref, o_ref, rows_scr, sem):
    i = pl.program_id(0)
    cur = i % 2

    def copy(buf, which, r, src_row, rows=1):
        return pltpu.make_async_copy(yb_ref.at[pl.ds(src_row, rows)],
                                     rows_scr.at[buf, which, pl.ds(r, rows)], sem.at[buf])

    def start_gather(step, buf):
        base = step * TOK_TILE

        def issue(g, c):
            for k in range(SUBLANES):
                r = g * SUBLANES + k
                copy(buf, 0, r, d0_ref[base + r]).start()
                copy(buf, 1, r, d1_ref[base + r]).start()
            return c

        lax.fori_loop(0, TOK_TILE // SUBLANES, issue, 0)

    @pl.when(i == 0)
    def _():
        start_gather(0, 0)

    @pl.when(i + 1 < pl.num_programs(0))
    def _():
        start_gather(i + 1, 1 - cur)

    copy(cur, 0, 0, 0, TOK_TILE).wait()
    copy(cur, 1, 0, 0, TOK_TILE).wait()

    info = info_ref[...]
    half = o_ref.shape[1] // 2
    y0_hi, y0_lo = _unpack_bf16_pairs(rows_scr[cur, 0])
    y1_hi, y1_lo = _unpack_bf16_pairs(rows_scr[cur, 1])
    g2 = g2_ref[0]
    o_ref[:, :half] = x_ref[:, :half] + g2[:, :half] * (info[:, 2:3] * y0_hi + info[:, 3:4] * y1_hi)
    o_ref[:, half:] = x_ref[:, half:] + g2[:, half:] * (info[:, 2:3] * y0_lo + info[:, 3:4] * y1_lo)


def _combine(d0, d1, x1, info, g2, yb, seq):
    n, d = x1.shape
    tpb = seq // TOK_TILE
    return pl.pallas_call(
        _combine_kernel,
        grid_spec=pltpu.PrefetchScalarGridSpec(
            num_scalar_prefetch=2,
            grid=(n // TOK_TILE,),
            in_specs=[pl.BlockSpec((TOK_TILE, d), lambda i, a, b: (i, 0)),
                      pl.BlockSpec((TOK_TILE, LANES), lambda i, a, b: (i, 0)),
                      pl.BlockSpec((1, 1, d), lambda i, a, b: (i // tpb, 0, 0)),
                      pl.BlockSpec(memory_space=pl.ANY)],
            out_specs=pl.BlockSpec((TOK_TILE, d), lambda i, a, b: (i, 0)),
            scratch_shapes=[pltpu.VMEM((2, 2, TOK_TILE, d // 2), I32),
                            pltpu.SemaphoreType.DMA((2,))]),
        out_shape=jax.ShapeDtypeStruct((n, d), F32),
        compiler_params=_cparams(("arbitrary",)),
        name="moe_combine",
    )(d0, d1, x1, info, g2, yb)


def _tri_inclusive_rev(k):
    l = (np.arange(k)[:, None] >= np.arange(k)[None, :]).astype(np.float32)
    half = np.concatenate([l, np.ones((k, k), np.float32)], axis=1)
    return jnp.asarray(np.concatenate([half, half], axis=0), dtype=BF16)


def _tri_strict_lower(k):
    return jnp.asarray((np.arange(k)[None, :] < np.arange(k)[:, None]).astype(np.float32), dtype=BF16)


def kernel(x, c, w_mod, b_mod, ln1_g, w_in, q_norm_g, w_q_up, q_gain, k_gain, rel_bias, gn_a, gn_b,
           w_out, ln2_g, router_g, router_e, w_gate, w_up, w_down):
    batch, seq, d = x.shape
    n = batch * seq
    assert w_mod.shape[0] == 1 and d == D_A + D_B
    assert seq % KB_A == 0 and seq % TOK_TILE == 0 and n % 512 == 0
    x2 = x.reshape(n, d)

    c8 = jnp.pad(c, ((0, 8 - batch), (0, 0)))
    mod = _modulation(c8, w_mod.reshape(d, -1), b_mod.reshape(1, -1))[:batch]
    sh1, sc1, g1, sh2, sc2, g2 = [m.reshape(batch, 1, d) for m in jnp.split(mod, 6, axis=-1)]

    wi = w_in.reshape(d, -1)
    n_a = Q_RANK + 2 * HEAD_DIM + IDX_DIM + IDX_HEADS
    wi = jnp.concatenate([wi[:, n_a:], wi[:, :n_a],
                          jnp.zeros((d, D_IN_PAD - wi.shape[1]), wi.dtype)], axis=1).astype(BF16)
    proj = _ln_proj(x2, ln1_g.reshape(1, d), sc1, sh1, wi, seq)

    qa, qidx, ka, va, kidx, widx = _aprep(
        proj, q_norm_g.reshape(1, -1), w_q_up.reshape(Q_RANK, -1).astype(BF16),
        q_gain.reshape(1, -1), k_gain.reshape(1, -1))

    bt = _bias_tiles(rel_bias)
    bound = _logit_bound(q_gain.reshape(1, -1), k_gain.reshape(1, -1), rel_bias)[0, :1]
    oa = _attn_a(qidx, widx.T, qa, kidx, ka, va, bt, gn_a.reshape(1, -1), bound, batch, seq)
    ob = _attn_b(proj, _tri_inclusive_rev(QB_B), batch, seq)

    rw = jnp.concatenate([router_e.reshape(d, -1), router_g.reshape(d, -1),
                          jnp.zeros((d, LANES - N_EXPERTS - N_GROUPS), F32)], axis=1)
    x1, h2, lg = _out_proj(x2, oa, ob, gn_b.reshape(1, -1), w_out.reshape(d, d).astype(BF16),
                           g1, ln2_g.reshape(1, d), sc2, sh2, rw, seq)

    info, cnt = _route(lg)
    counts = cnt[0, :N_EXPERTS].astype(I32)
    p_rows = 2 * n + N_EXPERTS * MOE_BLK
    ps, be, nu = _plan(counts, p_rows // MOE_BLK)
    ps_lanes = jnp.pad(ps.astype(F32), (0, LANES - N_EXPERTS)).reshape(1, LANES)
    dinfo = _dest(info, ps_lanes, _tri_strict_lower(512))
    d0 = dinfo[:, 0].astype(I32)
    d1 = dinfo[:, 1].astype(I32)
    xb = _scatter_rows(d0, d1, counts, ps, nu, h2, p_rows)
    yb = _experts(be, nu, xb,
                  w_gate.reshape(N_EXPERTS, d, D_EXPERT),
                  w_up.reshape(N_EXPERTS, d, D_EXPERT),
                  w_down.reshape(N_EXPERTS, D_EXPERT, d))
    out = _combine(d0, d1, x1, info, g2, yb, seq)
    return out.reshape(batch, seq, d)
```

```python
import functools
import math

import numpy as np
import jax
import jax.numpy as jnp
from jax import lax
from jax.experimental import pallas as pl
from jax.experimental.pallas import tpu as pltpu

F32 = jnp.float32
BF16 = jnp.bfloat16
I32 = jnp.int32

HEAD_DIM = 128
N_HEADS_A = 8
N_HEADS_B = 8
D_A = N_HEADS_A * HEAD_DIM
D_B = N_HEADS_B * HEAD_DIM
Q_RANK = 512
IDX_HEADS = 16
IDX_DIM = 64
TOPK_MAX = 256
N_BUCKETS = 32
MAX_DISTANCE = 128
N_GROUPS = 4
EXPERTS_PER_GROUP = 8
N_EXPERTS = N_GROUPS * EXPERTS_PER_GROUP
D_EXPERT = 512
EPS = 1e-6

LANES = 128
SUBLANES = 8
VMEM_LIMIT = 56 * 1024 * 1024
NEG = -1e30
INT_MIN = -(2 ** 31)
EXP2_UNDERFLOW = 150.0
LOG2E = math.log2(math.e)
PLAIN_SOFTMAX_LIMIT = 64.0

QB_A = 128
KB_A = 256
QB_B = 128
MOE_BLK = 256
TOK_TILE = 256

COL_QB, COL_KB, COL_VB, COL_A = 0, D_B, 2 * D_B, 3 * D_B
A_CQ, A_KA, A_VA, A_KIDX, A_WIDX = 0, 512, 640, 768, 832
A_WIDTH = 1024
D_IN_PAD = COL_A + A_WIDTH


def _cparams(sem=None):
    return pltpu.CompilerParams(dimension_semantics=sem, vmem_limit_bytes=VMEM_LIMIT)


def _rms(x):
    return x * lax.rsqrt(jnp.mean(x * x, axis=-1, keepdims=True) + EPS)


def _pack_bf16_pairs(x):
    c = x.shape[1] // 2
    hi = pltpu.bitcast(x[:, :c].astype(BF16).astype(F32), I32)
    lo = pltpu.bitcast(x[:, c:].astype(BF16).astype(F32), I32)
    return hi | lax.shift_right_logical(lo, 16)


def _unpack_bf16_pairs(u):
    hi = pltpu.bitcast(u & jnp.int32(-65536), F32)
    lo = pltpu.bitcast(lax.shift_left(u, 16), F32)
    return hi, lo


def _mod_kernel(c_ref, w_ref, b_ref, o_ref):
    c = c_ref[...]
    s = c * (1.0 / (1.0 + jnp.exp(-c)))
    o_ref[...] = jnp.dot(s, w_ref[...], preferred_element_type=F32,
                         precision=lax.Precision.HIGHEST) + b_ref[...]


def _modulation(c8, w_mod, b_mod):
    d, n6 = w_mod.shape
    tn = 1024
    return pl.pallas_call(
        _mod_kernel,
        grid=(n6 // tn,),
        in_specs=[pl.BlockSpec((8, d), lambda j: (0, 0)),
                  pl.BlockSpec((d, tn), lambda j: (0, j)),
                  pl.BlockSpec((1, tn), lambda j: (0, j))],
        out_specs=pl.BlockSpec((8, tn), lambda j: (0, j)),
        out_shape=jax.ShapeDtypeStruct((8, n6), F32),
        compiler_params=_cparams(("arbitrary",)),
        name="modulation",
    )(c8, w_mod, b_mod)


def _ln_proj_kernel(x_ref, g_ref, sc_ref, sh_ref, w_ref, o_ref, h_scr, *, q_tiles):
    @pl.when(pl.program_id(1) == 0)
    def _():
        h = _rms(x_ref[...]) * g_ref[...]
        h = h * (1.0 + sc_ref[0]) + sh_ref[0]
        h_scr[...] = h.astype(BF16)

    col_scale = jnp.where(pl.program_id(1) < q_tiles, HEAD_DIM ** -0.5 * LOG2E, 1.0)
    acc = jnp.dot(h_scr[...], w_ref[...], preferred_element_type=F32)
    o_ref[...] = (acc * col_scale).astype(o_ref.dtype)


def _ln_proj(x2, ln_g, sc, sh, w, seq):
    n, d = x2.shape
    ncol = w.shape[1]
    tm = min(1024, seq)
    tn = 512
    tpb = seq // tm
    assert COL_QB == 0 and D_B % tn == 0
    return pl.pallas_call(
        functools.partial(_ln_proj_kernel, q_tiles=D_B // tn),
        grid=(n // tm, ncol // tn),
        in_specs=[pl.BlockSpec((tm, d), lambda i, j: (i, 0)),
                  pl.BlockSpec((1, d), lambda i, j: (0, 0)),
                  pl.BlockSpec((1, 1, d), lambda i, j: (i // tpb, 0, 0)),
                  pl.BlockSpec((1, 1, d), lambda i, j: (i // tpb, 0, 0)),
                  pl.BlockSpec((d, tn), lambda i, j: (0, j))],
        out_specs=pl.BlockSpec((tm, tn), lambda i, j: (i, j)),
        out_shape=jax.ShapeDtypeStruct((n, ncol), BF16),
        scratch_shapes=[pltpu.VMEM((tm, d), BF16)],
        compiler_params=_cparams(("parallel", "arbitrary")),
        name="ln_in_proj",
    )(x2, ln_g, sc, sh, w)


def _aprep_kernel(a_ref, qng_ref, wq_ref, qg_ref, kg_ref,
                  qa_ref, qidx_ref, ka_ref, va_ref, kidx_ref, widx_ref):
    cq = a_ref[:, A_CQ:A_CQ + Q_RANK].astype(F32)
    cqn = (_rms(cq) * qng_ref[...]).astype(BF16)
    qup = jnp.dot(cqn, wq_ref[...], preferred_element_type=F32)
    for h in range(N_HEADS_A):
        qh = qup[:, h * HEAD_DIM:(h + 1) * HEAD_DIM]
        qn = _rms(qh) * qg_ref[...] * (HEAD_DIM ** -0.5 * LOG2E)
        qa_ref[:, h * HEAD_DIM:(h + 1) * HEAD_DIM] = qn.astype(BF16)
    for h in range(IDX_HEADS):
        qi = qup[:, D_A + h * IDX_DIM:D_A + (h + 1) * IDX_DIM] * (IDX_DIM ** -0.5)
        qidx_ref[h] = qi.astype(BF16)
    ka = a_ref[:, A_KA:A_KA + HEAD_DIM].astype(F32)
    ka_ref[...] = (_rms(ka) * kg_ref[...]).astype(BF16)
    va_ref[...] = a_ref[:, A_VA:A_VA + HEAD_DIM]
    kidx_ref[...] = a_ref[:, A_KIDX:A_KIDX + IDX_DIM]
    widx_ref[...] = a_ref[:, A_WIDX:A_WIDX + IDX_HEADS].astype(F32) * (IDX_HEADS ** -0.5)


def _aprep(proj, q_norm_g, w_q_up, q_gain, k_gain):
    n = proj.shape[0]
    tm = 512
    cblk = COL_A // A_WIDTH
    nup = w_q_up.shape[1]
    return pl.pallas_call(
        _aprep_kernel,
        grid=(n // tm,),
        in_specs=[pl.BlockSpec((tm, A_WIDTH), lambda i: (i, cblk)),
                  pl.BlockSpec((1, Q_RANK), lambda i: (0, 0)),
                  pl.BlockSpec((Q_RANK, nup), lambda i: (0, 0)),
                  pl.BlockSpec((1, HEAD_DIM), lambda i: (0, 0)),
                  pl.BlockSpec((1, HEAD_DIM), lambda i: (0, 0))],
        out_specs=[pl.BlockSpec((tm, D_A), lambda i: (i, 0)),
                   pl.BlockSpec((IDX_HEADS, tm, IDX_DIM), lambda i: (0, i, 0)),
                   pl.BlockSpec((tm, HEAD_DIM), lambda i: (i, 0)),
                   pl.BlockSpec((tm, HEAD_DIM), lambda i: (i, 0)),
                   pl.BlockSpec((tm, IDX_DIM), lambda i: (i, 0)),
                   pl.BlockSpec((tm, IDX_HEADS), lambda i: (i, 0))],
        out_shape=[jax.ShapeDtypeStruct((n, D_A), BF16),
                   jax.ShapeDtypeStruct((IDX_HEADS, n, IDX_DIM), BF16),
                   jax.ShapeDtypeStruct((n, HEAD_DIM), BF16),
                   jax.ShapeDtypeStruct((n, HEAD_DIM), BF16),
                   jax.ShapeDtypeStruct((n, IDX_DIM), BF16),
                   jax.ShapeDtypeStruct((n, IDX_HEADS), F32)],
        compiler_params=_cparams(("parallel",)),
        name="group_a_prep",
    )(proj, q_norm_g, w_q_up, q_gain, k_gain)


def _t5_bucket_starts():
    max_exact = N_BUCKETS // 2
    d = np.arange(0, 4 * MAX_DISTANCE, dtype=np.int64)
    df = np.maximum(d, 1).astype(np.float32)
    large = max_exact + (np.log(df / np.float32(max_exact)) / np.float32(math.log(MAX_DISTANCE / max_exact))
                         * np.float32(N_BUCKETS - max_exact)).astype(np.int32)
    large = np.minimum(large, N_BUCKETS - 1)
    bucket = np.where(d < max_exact, d, large)
    assert np.all(np.diff(bucket) >= 0) and bucket[-1] == N_BUCKETS - 1
    return [int(np.argmax(bucket >= b)) for b in range(N_BUCKETS)]


_BUCKET_START = _t5_bucket_starts()
N_BIAS_TILES = 2 * KB_A // LANES


def _bias_kernel(rb_ref, o_ref):
    di = pl.program_id(0)
    h = pl.program_id(1)
    i = lax.broadcasted_iota(I32, (QB_A, KB_A), 0)
    j = lax.broadcasted_iota(I32, (QB_A, KB_A), 1)
    d = di * LANES + i - j
    val = jnp.full((QB_A, KB_A), rb_ref[0, h], F32)
    for b in range(1, N_BUCKETS):
        val = jnp.where(d >= _BUCKET_START[b], rb_ref[b, h], val)
    o_ref[0, 0] = (val - rb_ref[N_BUCKETS - 1, h]) * LOG2E


def _bias_tiles(rel_bias):
    return pl.pallas_call(
        _bias_kernel,
        grid=(N_BIAS_TILES, N_HEADS_A),
        in_specs=[pl.BlockSpec(memory_space=pltpu.SMEM)],
        out_specs=pl.BlockSpec((1, 1, QB_A, KB_A), lambda a, h: (a, h, 0, 0)),
        out_shape=jax.ShapeDtypeStruct((N_BIAS_TILES, N_HEADS_A, QB_A, KB_A), F32),
        compiler_params=_cparams(("arbitrary", "arbitrary")),
        name="t5_bias_tiles",
    )(rel_bias)


def _bound_kernel(qg_ref, kg_ref, rb_ref, o_ref):
    qmax = jnp.max(jnp.abs(qg_ref[...]), axis=1, keepdims=True)
    kmax = jnp.max(jnp.abs(kg_ref[...]), axis=1, keepdims=True)
    rb = rb_ref[...]
    shifted = jnp.abs(rb - rb[N_BUCKETS - 1:N_BUCKETS, :])
    bmax = jnp.max(jnp.max(shifted, axis=1, keepdims=True), axis=0, keepdims=True)
    bound = qmax * kmax * (math.sqrt(HEAD_DIM) * 1.02) + bmax
    o_ref[...] = jnp.broadcast_to(bound, o_ref.shape)


def _logit_bound(q_gain, k_gain, rel_bias):
    return pl.pallas_call(
        _bound_kernel,
        out_shape=jax.ShapeDtypeStruct((1, LANES), F32),
        name="dsa_logit_bound",
    )(q_gain, k_gain, rel_bias)


def _attn_a_kernel(qidx_ref, w_ref, qa_ref, kidx_ref, ka_ref, va_ref, bt_ref, gn_ref, bound_ref, o_ref,
                   keys_scr, keyt_scr, m_scr, acc_scr, s_scr, mb_scr, *, topk):
    qb = pl.program_id(1)
    t0 = qb * QB_A
    kbl = (t0 + QB_A - 1) // KB_A
    row = t0 + lax.broadcasted_iota(I32, (QB_A, KB_A), 0)
    col0 = lax.broadcasted_iota(I32, (QB_A, KB_A), 1)
    nt = (((1,), (1,)), ((), ()))

    reps = KB_A // LANES

    def sort_key(v):
        bits = pltpu.bitcast(v, I32)
        return jnp.where(bits < 0, bits ^ jnp.int32(0x7FFFFFFF), bits)

    def key_value(k):
        return pltpu.bitcast(jnp.where(k < 0, k ^ jnp.int32(0x7FFFFFFF), k), F32)

    key_pos = lax.broadcasted_iota(I32, (KB_A, QB_A), 0)
    qry_pos = t0 + lax.broadcasted_iota(I32, (KB_A, QB_A), 1)

    def score_tile(kb, carry):
        smin, smax = carry
        kt = kidx_ref[pl.ds(pl.multiple_of(kb * KB_A, KB_A), KB_A), :]
        score = jnp.zeros((KB_A, QB_A), F32)
        for hp in range(IDX_HEADS // 2):
            q2 = qidx_ref[2 * hp:2 * hp + 2].reshape(2 * QB_A, IDX_DIM)
            sc = lax.dot_general(kt, q2, nt, preferred_element_type=F32)
            score = score + w_ref[2 * hp:2 * hp + 1, :] * jnp.maximum(sc[:, :QB_A], 0.0)
            score = score + w_ref[2 * hp + 1:2 * hp + 2, :] * jnp.maximum(sc[:, QB_A:], 0.0)
        causal = (kb * KB_A + key_pos) <= qry_pos
        key_t = jnp.where(causal, sort_key(score), jnp.int32(INT_MIN))
        keyt_scr[kb] = key_t
        keys_scr[kb] = key_t.T
        smin = jnp.minimum(smin, jnp.min(jnp.where(causal, score, jnp.inf), axis=0, keepdims=True))
        smax = jnp.maximum(smax, jnp.max(jnp.where(causal, score, -jnp.inf), axis=0, keepdims=True))
        return smin, smax

    def score_pair(i, carry):
        return score_tile(2 * i + 1, score_tile(2 * i, carry))

    n_tiles = kbl + 1
    extremes = lax.fori_loop(0, n_tiles // 2, score_pair,
                             (jnp.full((1, QB_A), jnp.inf, F32), jnp.full((1, QB_A), -jnp.inf, F32)))
    smin, smax = lax.cond(n_tiles % 2 == 1, lambda c: score_tile(n_tiles - 1, c), lambda c: c, extremes)

    qry1 = t0 + lax.broadcasted_iota(I32, (1, QB_A), 1)
    kf = float(topk)
    acc_rows = 32

    def search_pass(p, state):
        lo, hi, clo, chi, open_q = state
        lo_v = key_value(lo)
        hi_v = key_value(hi)
        gap = clo - chi
        frac = (clo - (kf - 0.5)) / gap
        frac = jnp.where(p % 2 == 1, 0.7 * frac + 0.15, frac)
        frac = jnp.where(gap > 16.0, frac, 0.5)
        cand = sort_key(lo_v + (hi_v - lo_v) * frac)
        cand = jnp.where(p % 8 == 7, lo + lax.shift_right_logical(hi - lo, 1), cand)
        cand = jnp.minimum(jnp.maximum(cand, lo + 1), hi - 1)

        def count(kb, cnt):
            ge = jnp.where(keyt_scr[kb] >= cand, 1.0, 0.0)
            return cnt + jnp.sum(ge.reshape(KB_A // acc_rows, acc_rows, QB_A), axis=0)

        cnt = lax.fori_loop(0, kbl + 1, count, jnp.zeros((acc_rows, QB_A), F32))
        tot = jnp.sum(cnt, axis=0, keepdims=True)
        ge = tot >= kf
        lo = jnp.where(ge, cand, lo)
        clo = jnp.where(ge, tot, clo)
        hi = jnp.where(ge, hi, cand)
        chi = jnp.where(ge, chi, tot)
        width = hi - lo
        settled = jnp.where(clo == kf, 1.0, jnp.where(width == 1, 1.0, 0.0))
        open_q = jnp.where(settled > 0.5, 0.0, open_q)
        return lo, hi, clo, chi, open_q

    def search_step(carry):
        p, state, _ = carry
        state = search_pass(p + 1, search_pass(p, state))
        return p + 2, state, jnp.max(state[4]) > 0.0

    n_causal = (qry1 + 1).astype(F32)
    open0 = jnp.where(qry1 >= topk, 1.0, 0.0)
    lo0 = sort_key(smin)
    hi0 = sort_key(smax) + 1
    open0 = jnp.where(hi0 - lo0 == 1, 0.0, open0)
    _, (lo, _, _, _, _), _ = lax.while_loop(
        lambda c: c[2], search_step,
        (jnp.int32(0), (lo0, hi0, n_causal, jnp.zeros((1, QB_A), F32), open0), jnp.max(open0) > 0.0))
    thr = jnp.where(qry1 >= topk, lo, jnp.int32(INT_MIN))
    thr = jnp.broadcast_to(thr, (QB_A, QB_A)).T
    thrb = jnp.tile(thr, (1, reps))

    m_scr[...] = jnp.full(m_scr.shape, NEG, F32)
    acc_scr[...] = jnp.zeros(acc_scr.shape, F32)

    plain = bound_ref[0] <= PLAIN_SOFTMAX_LIMIT

    def attend(kb, near, online):
        mb = jnp.where(keys_scr[kb] >= thrb, 0.0, NEG)
        if near:
            mb = jnp.where((kb * KB_A + col0) <= row, mb, NEG)
            di = (t0 - kb * KB_A) // LANES
        mb_scr[...] = mb
        start = pl.multiple_of(kb * KB_A, KB_A)
        kt = ka_ref[pl.ds(start, KB_A), :]
        vt = jnp.concatenate([va_ref[pl.ds(start, KB_A), :], jnp.ones((KB_A, LANES), BF16)], axis=1)
        for h in range(N_HEADS_A):
            q = qa_ref[:, h * HEAD_DIM:(h + 1) * HEAD_DIM]
            s_scr[h] = lax.dot_general(q, kt, nt, preferred_element_type=F32)
        for h in range(N_HEADS_A):
            s = s_scr[h] + mb_scr[...]
            if near:
                s = s + bt_ref[di, h]
            if online:
                m_prev = m_scr[h]
                m_new = jnp.maximum(m_prev, jnp.max(s, axis=1, keepdims=True))
                alpha = jnp.exp2(m_prev - m_new)
                p = jnp.exp2(s - jnp.tile(m_new, (1, reps)))
                acc_scr[h] = (jnp.tile(alpha, (1, 2)) * acc_scr[h]
                              + jnp.dot(p.astype(BF16), vt, preferred_element_type=F32))
                m_scr[h] = m_new
            else:
                acc_scr[h] += jnp.dot(jnp.exp2(s).astype(BF16), vt, preferred_element_type=F32)

    def attend_all(online):
        n_far = jnp.maximum(kbl - 1, 0)

        def far_pair(i, carry):
            attend(2 * i, False, online)
            attend(2 * i + 1, False, online)
            return carry

        lax.fori_loop(0, n_far // 2, far_pair, 0)

        @pl.when(n_far % 2 == 1)
        def _():
            attend(n_far - 1, False, online)

        @pl.when(kbl >= 1)
        def _():
            attend(kbl - 1, True, online)

        attend(kbl, True, online)

    @pl.when(plain)
    def _():
        attend_all(False)

    @pl.when(jnp.logical_not(plain))
    def _():
        attend_all(True)

    ssq = jnp.zeros((QB_A, LANES), F32)
    for h in range(N_HEADS_A):
        oh = acc_scr[h, :, :HEAD_DIM] / acc_scr[h, :, HEAD_DIM:]
        acc_scr[h, :, :HEAD_DIM] = oh
        ssq = ssq + jnp.sum(oh * oh, axis=1, keepdims=True)
    inv = lax.rsqrt(ssq * (1.0 / D_A) + EPS)
    for h in range(N_HEADS_A):
        sl = slice(h * HEAD_DIM, (h + 1) * HEAD_DIM)
        o_ref[:, sl] = (acc_scr[h, :, :HEAD_DIM] * inv * gn_ref[:, sl]).astype(o_ref.dtype)


def _attn_a(qidx, widx, qa, kidx, ka, va, bt, gn_a, bound, batch, seq):
    n = qa.shape[0]
    nq = seq // QB_A
    nkt = seq // KB_A
    topk = min(TOPK_MAX, seq // 4)
    return pl.pallas_call(
        functools.partial(_attn_a_kernel, topk=topk),
        grid=(batch, nq),
        in_specs=[pl.BlockSpec((IDX_HEADS, QB_A, IDX_DIM), lambda b, q: (0, b * nq + q, 0)),
                  pl.BlockSpec((IDX_HEADS, QB_A), lambda b, q: (0, b * nq + q)),
                  pl.BlockSpec((QB_A, D_A), lambda b, q: (b * nq + q, 0)),
                  pl.BlockSpec((seq, IDX_DIM), lambda b, q: (b, 0)),
                  pl.BlockSpec((seq, HEAD_DIM), lambda b, q: (b, 0)),
                  pl.BlockSpec((seq, HEAD_DIM), lambda b, q: (b, 0)),
                  pl.BlockSpec(bt.shape, lambda b, q: (0, 0, 0, 0)),
                  pl.BlockSpec((1, D_A), lambda b, q: (0, 0)),
                  pl.BlockSpec(memory_space=pltpu.SMEM)],
        out_specs=pl.BlockSpec((QB_A, D_A), lambda b, q: (b * nq + q, 0)),
        out_shape=jax.ShapeDtypeStruct((n, D_A), BF16),
        scratch_shapes=[pltpu.VMEM((nkt, QB_A, KB_A), I32),
                        pltpu.VMEM((nkt, KB_A, QB_A), I32),
                        pltpu.VMEM((N_HEADS_A, QB_A, LANES), F32),
                        pltpu.VMEM((N_HEADS_A, QB_A, 2 * HEAD_DIM), F32),
                        pltpu.VMEM((N_HEADS_A, QB_A, KB_A), F32),
                        pltpu.VMEM((QB_A, KB_A), F32)],
        compiler_params=_cparams(("parallel", "arbitrary")),
        name="dsa_attention",
    )(qidx, widx, qa, kidx, ka, va, bt, gn_a, bound)


def _attn_b_kernel(q_ref, k_ref, v_ref, tri_ref, o_ref, rest_scr, z_scr, cs_scr):
    qb = pl.program_id(1)
    row = lax.broadcasted_iota(I32, (QB_B, QB_B), 0)
    col = lax.broadcasted_iota(I32, (QB_B, QB_B), 1)
    strict = col < row
    nt = (((1,), (1,)), ((), ()))
    def step(kb, diag):
        start = pl.multiple_of(kb * QB_B, QB_B)
        heads = [slice(h * HEAD_DIM, (h + 1) * HEAD_DIM) for h in range(N_HEADS_B)]
        for h, sl in enumerate(heads):
            kt = k_ref[pl.ds(start, QB_B), sl]
            z_scr[h] = lax.dot_general(q_ref[:, sl], kt, nt, preferred_element_type=F32)
        for h, sl in enumerate(heads):
            z = z_scr[h]
            sp = jnp.maximum(z, 0.0) + jnp.log(1.0 + jnp.exp2(-jnp.abs(z))) * LOG2E
            if diag:
                sp = jnp.where(strict, sp, 0.0)
            hi = sp.astype(BF16)
            lo = (sp - hi.astype(F32)).astype(BF16)
            cs_scr[h] = jnp.dot(jnp.concatenate([hi, lo], axis=1), tri_ref[...],
                                preferred_element_type=F32)
        worst = None
        for h, sl in enumerate(heads):
            vt = v_ref[pl.ds(start, QB_B), sl]
            z = z_scr[h]
            cs = cs_scr[h, :, :QB_B]
            tot = cs_scr[h, :, QB_B:]
            if diag:
                a = jnp.where(strict, jnp.exp2(z - cs), 0.0)
                o_ref[:, sl] = jnp.dot(a.astype(BF16), vt, preferred_element_type=F32)
                rest = tot
            else:
                rest = rest_scr[h]
                a = jnp.exp2(z - cs - rest)
                o_ref[:, sl] += jnp.dot(a.astype(BF16), vt, preferred_element_type=F32)
                rest = rest + tot
            rest_scr[h] = rest
            worst = rest if worst is None else jnp.minimum(worst, rest)
        return jnp.min(worst)

    def more(kb, smallest):
        return jnp.logical_and(kb >= 0, smallest < EXP2_UNDERFLOW)

    def body(carry):
        kb, _ = carry
        return kb - 1, more(kb - 1, step(kb, False))

    lax.while_loop(lambda c: c[1], body, (qb - 1, more(qb - 1, step(qb, True))))


def _attn_b(proj, tri, batch, seq):
    n = proj.shape[0]
    nq = seq // QB_B
    return pl.pallas_call(
        _attn_b_kernel,
        grid=(batch, nq),
        in_specs=[pl.BlockSpec((QB_B, D_B), lambda b, q: (b * nq + q, COL_QB // D_B)),
                  pl.BlockSpec((seq, D_B), lambda b, q: (b, COL_KB // D_B)),
                  pl.BlockSpec((seq, D_B), lambda b, q: (b, COL_VB // D_B)),
                  pl.BlockSpec(tri.shape, lambda b, q: (0, 0))],
        out_specs=pl.BlockSpec((QB_B, D_B), lambda b, q: (b * nq + q, 0)),
        out_shape=jax.ShapeDtypeStruct((n, D_B), F32),
        scratch_shapes=[pltpu.VMEM((N_HEADS_B, QB_B, QB_B), F32),
                        pltpu.VMEM((N_HEADS_B, QB_B, QB_B), F32),
                        pltpu.VMEM((N_HEADS_B, QB_B, 2 * QB_B), F32)],
        compiler_params=_cparams(("parallel", "arbitrary")),
        name="stick_breaking_attention",
    )(proj, proj, proj, tri)


def _out_proj_kernel(x_ref, oa_ref, ob_ref, gnb_ref, w_ref, g1_ref, ln_ref, sc_ref, sh_ref, rw_ref,
                     x1_ref, h2_ref, lg_ref):
    obn = (_rms(ob_ref[...]) * gnb_ref[...]).astype(BF16)
    y = jnp.dot(oa_ref[...], w_ref[0:D_A, :], preferred_element_type=F32)
    y = y + jnp.dot(obn, w_ref[D_A:D_A + D_B, :], preferred_element_type=F32)
    x1 = x_ref[...] + g1_ref[0] * y
    x1_ref[...] = x1
    h2 = _rms(x1) * ln_ref[...]
    h2 = h2 * (1.0 + sc_ref[0]) + sh_ref[0]
    h2_ref[...] = _pack_bf16_pairs(h2)
    hh = h2.astype(BF16)
    hl = (h2 - hh.astype(F32)).astype(BF16)
    rw = rw_ref[...]
    rh = rw.astype(BF16)
    rl = (rw - rh.astype(F32)).astype(BF16)
    both = jnp.dot(hh, jnp.concatenate([rh, rl], axis=1), preferred_element_type=F32)
    lg_ref[...] = both[:, :LANES] + both[:, LANES:] + jnp.dot(hl, rh, preferred_element_type=F32)


def _out_proj(x2, oa, ob, gn_b, w_out, g1, ln_g, sc, sh, rw, seq):
    n, d = x2.shape
    tm = 256
    tpb = seq // tm
    row = lambda i: (i, 0)
    fixed = lambda i: (0, 0)
    perb = lambda i: (i // tpb, 0, 0)
    return pl.pallas_call(
        _out_proj_kernel,
        grid=(n // tm,),
        in_specs=[pl.BlockSpec((tm, d), row),
                  pl.BlockSpec((tm, D_A), row),
                  pl.BlockSpec((tm, D_B), row),
                  pl.BlockSpec((1, D_B), fixed),
                  pl.BlockSpec(w_out.shape, fixed),
                  pl.BlockSpec((1, 1, d), perb),
                  pl.BlockSpec((1, d), fixed),
                  pl.BlockSpec((1, 1, d), perb),
                  pl.BlockSpec((1, 1, d), perb),
                  pl.BlockSpec(rw.shape, fixed)],
        out_specs=[pl.BlockSpec((tm, d), row),
                   pl.BlockSpec((tm, d // 2), row),
                   pl.BlockSpec((tm, LANES), row)],
        out_shape=[jax.ShapeDtypeStruct((n, d), F32),
                   jax.ShapeDtypeStruct((n, d // 2), I32),
                   jax.ShapeDtypeStruct((n, LANES), F32)],
        compiler_params=_cparams(("parallel",)),
        name="out_proj_ln2_router",
    )(x2, oa, ob, gn_b, w_out, g1, ln_g, sc, sh, rw)


def _route_kernel(lg_ref, info_ref, cnt_ref):
    @pl.when(pl.program_id(0) == 0)
    def _():
        cnt_ref[...] = jnp.zeros(cnt_ref.shape, F32)

    lg = lg_ref[...]
    lane = lax.broadcasted_iota(I32, lg.shape, 1)
    lanef = lane.astype(F32)
    big = float(4 * LANES)
    gm = jnp.where(lane >= N_EXPERTS, jnp.where(lane < N_EXPERTS + N_GROUPS, 1.0, 0.0), 0.0) > 0.5
    lgm = jnp.where(gm, lg, NEG)
    mg = jnp.max(lgm, axis=1, keepdims=True)
    eg = jnp.where(gm, jnp.exp(lgm - mg), 0.0)
    pg = eg / jnp.sum(eg, axis=1, keepdims=True)
    gw = jnp.max(pg, axis=1, keepdims=True)
    gidx = jnp.min(jnp.where(gm, jnp.where(pg == gw, lanef - N_EXPERTS, big), big), axis=1, keepdims=True)
    lane_group = (lane // EXPERTS_PER_GROUP).astype(F32)
    em = jnp.where(lane < N_EXPERTS, jnp.where(lane_group == gidx, 1.0, 0.0), 0.0) > 0.5
    lem = jnp.where(em, lg, NEG)
    me = jnp.max(lem, axis=1, keepdims=True)
    ee = jnp.where(em, jnp.exp(lem - me), 0.0)
    pe = jnp.where(em, ee / jnp.sum(ee, axis=1, keepdims=True), -1.0)
    p1 = jnp.max(pe, axis=1, keepdims=True)
    i1 = jnp.min(jnp.where(pe == p1, lanef, big), axis=1, keepdims=True)
    pe2 = jnp.where(lanef == i1, -1.0, pe)
    p2 = jnp.max(pe2, axis=1, keepdims=True)
    i2 = jnp.min(jnp.where(pe2 == p2, lanef, big), axis=1, keepdims=True)
    den = p1 + p2
    g0 = gw * p1 / den
    g1 = gw * p2 / den
    info = jnp.where(lane == 0, i1, jnp.where(lane == 1, i2,
                     jnp.where(lane == 2, g0, jnp.where(lane == 3, g1, 0.0))))
    info_ref[...] = info
    oh = jnp.where(lanef == i1, 1.0, 0.0) + jnp.where(lanef == i2, 1.0, 0.0)
    cnt_ref[...] += jnp.sum(oh, axis=0, keepdims=True)


def _route(lg):
    n = lg.shape[0]
    tm = min(1024, n)
    return pl.pallas_call(
        _route_kernel,
        grid=(n // tm,),
        in_specs=[pl.BlockSpec((tm, LANES), lambda i: (i, 0))],
        out_specs=[pl.BlockSpec((tm, LANES), lambda i: (i, 0)),
                   pl.BlockSpec((1, LANES), lambda i: (0, 0))],
        out_shape=[jax.ShapeDtypeStruct((n, LANES), F32),
                   jax.ShapeDtypeStruct((1, LANES), F32)],
        compiler_params=_cparams(("arbitrary",)),
        name="moe_route",
    )(lg)


def _plan_kernel(cnt_ref, ps_ref, be_ref, nu_ref, *, nblk):
    def fill(i, c):
        be_ref[i] = N_EXPERTS - 1
        return c

    lax.fori_loop(0, nblk, fill, 0)

    def per_expert(e, pos):
        ps_ref[e] = pos * MOE_BLK
        nb = (cnt_ref[e] + MOE_BLK - 1) // MOE_BLK

        def mark(k, c):
            be_ref[pos + k] = e
            return c

        lax.fori_loop(0, nb, mark, 0)
        return pos + nb

    nu_ref[0] = lax.fori_loop(0, N_EXPERTS, per_expert, jnp.int32(0))


def _plan(counts, nblk):
    smem = pl.BlockSpec(memory_space=pltpu.SMEM)
    return pl.pallas_call(
        functools.partial(_plan_kernel, nblk=nblk),
        in_specs=[smem],
        out_specs=[smem, smem, smem],
        out_shape=[jax.ShapeDtypeStruct((N_EXPERTS,), I32),
                   jax.ShapeDtypeStruct((nblk,), I32),
                   jax.ShapeDtypeStruct((1,), I32)],
        name="moe_block_plan",
    )(counts)


def _dest_kernel(info_ref, ps_ref, tri_ref, o_ref, carry_scr):
    @pl.when(pl.program_id(0) == 0)
    def _():
        carry_scr[...] = jnp.zeros(carry_scr.shape, F32)

    info = info_ref[...]
    lane = lax.broadcasted_iota(I32, info.shape, 1)
    lanef = lane.astype(F32)
    o1 = jnp.where(lanef == info[:, 0:1], 1.0, 0.0)
    o2 = jnp.where(lanef == info[:, 1:2], 1.0, 0.0)
    oh = o1 + o2
    before = jnp.dot(tri_ref[...], oh.astype(BF16), preferred_element_type=F32)
    base = before + carry_scr[...] + ps_ref[...]
    d1 = jnp.sum(o1 * base, axis=1, keepdims=True)
    d2 = jnp.sum(o2 * base, axis=1, keepdims=True)
    o_ref[...] = jnp.where(lane == 0, d1, jnp.where(lane == 1, d2, 0.0))
    carry_scr[...] += jnp.sum(oh, axis=0, keepdims=True)


def _dest(info, ps_lanes, tri):
    n = info.shape[0]
    tm = tri.shape[0]
    return pl.pallas_call(
        _dest_kernel,
        grid=(n // tm,),
        in_specs=[pl.BlockSpec((tm, LANES), lambda i: (i, 0)),
                  pl.BlockSpec((1, LANES), lambda i: (0, 0)),
                  pl.BlockSpec((tm, tm), lambda i: (0, 0))],
        out_specs=pl.BlockSpec((tm, LANES), lambda i: (i, 0)),
        out_shape=jax.ShapeDtypeStruct((n, LANES), F32),
        scratch_shapes=[pltpu.VMEM((1, LANES), F32)],
        compiler_params=_cparams(("arbitrary",)),
        name="moe_dest_rows",
    )(info, ps_lanes, tri)


SCATTER_SLOTS = 3


def _scatter_kernel(d0_ref, d1_ref, cnt_ref, ps_ref, nu_ref, h_ref, xb_ref,
                    zero_scr, stage_scr, load_sem, out_sem, zero_sem, *, nblk):
    n_tiles = h_ref.shape[0] // TOK_TILE
    zero_scr[...] = jnp.zeros(zero_scr.shape, zero_scr.dtype)
    pad_sizes = [s for s in (1 << k for k in range(MOE_BLK.bit_length() - 2, -1, -1))
                 if s >= SUBLANES]

    def zero_copy(rows, dst_row):
        return pltpu.make_async_copy(zero_scr.at[pl.ds(0, rows)], xb_ref.at[pl.ds(dst_row, rows)],
                                     zero_sem)

    def load(j, slot):
        return pltpu.make_async_copy(h_ref.at[pl.ds(j * TOK_TILE, TOK_TILE)], stage_scr.at[slot],
                                     load_sem.at[slot])

    def row_copy(slot, r, dst_row, parity, rows=1):
        return pltpu.make_async_copy(stage_scr.at[slot, pl.ds(r, rows)],
                                     xb_ref.at[pl.ds(dst_row, rows)], out_sem.at[parity])

    def for_each_zero_copy(act):
        def per_expert(e, c):
            pos = ps_ref[e] + cnt_ref[e]
            pad = (MOE_BLK - (cnt_ref[e] & (MOE_BLK - 1))) & (MOE_BLK - 1)
            head = pad & (SUBLANES - 1)
            for k in range(SUBLANES - 1):
                @pl.when(k < head)
                def _():
                    act(zero_copy(1, pos + k))
            pos = pl.multiple_of(pos + head, SUBLANES)
            for size in pad_sizes:
                @pl.when((pad & size) != 0)
                def _():
                    act(zero_copy(size, pos))
                pos = pl.multiple_of(pos + (pad & size), SUBLANES)
            return c

        def per_tail_block(i, c):
            act(zero_copy(MOE_BLK, i * MOE_BLK))
            return c

        lax.fori_loop(0, N_EXPERTS, per_expert, 0)
        lax.fori_loop(nu_ref[0], nblk, per_tail_block, 0)

    for_each_zero_copy(lambda cp: cp.start())

    def retire(parity):
        row_copy(0, 0, 0, parity, TOK_TILE).wait()
        row_copy(0, 0, 0, parity, TOK_TILE).wait()

    def tile(j, c):
        slot = j % SCATTER_SLOTS
        parity = j % 2
        load(j, slot).wait()

        @pl.when(j + 1 < n_tiles)
        def _():
            load(j + 1, (j + 1) % SCATTER_SLOTS).start()

        def issue(g, cc):
            for k in range(SUBLANES):
                r = g * SUBLANES + k
                t = j * TOK_TILE + r
                row_copy(slot, r, d0_ref[t], parity).start()
                row_copy(slot, r, d1_ref[t], parity).start()
            return cc

        lax.fori_loop(0, TOK_TILE // SUBLANES, issue, 0)

        @pl.when(j >= 1)
        def _():
            retire(1 - parity)
        return c

    load(0, 0).start()
    lax.fori_loop(0, n_tiles, tile, 0)
    retire((n_tiles - 1) % 2)
    for_each_zero_copy(lambda cp: cp.wait())


def _scatter_rows(d0, d1, counts, ps, nu, h2, p_rows):
    n, d = h2.shape
    smem = pl.BlockSpec(memory_space=pltpu.SMEM)
    return pl.pallas_call(
        functools.partial(_scatter_kernel, nblk=p_rows // MOE_BLK),
        in_specs=[smem, smem, smem, smem, smem, pl.BlockSpec(memory_space=pl.ANY)],
        out_specs=pl.BlockSpec(memory_space=pl.ANY),
        out_shape=jax.ShapeDtypeStruct((p_rows, d), h2.dtype),
        scratch_shapes=[pltpu.VMEM((MOE_BLK, d), h2.dtype),
                        pltpu.VMEM((SCATTER_SLOTS, TOK_TILE, d), h2.dtype),
                        pltpu.SemaphoreType.DMA((SCATTER_SLOTS,)),
                        pltpu.SemaphoreType.DMA((2,)),
                        pltpu.SemaphoreType.DMA(())],
        compiler_params=_cparams(),
        name="moe_scatter_rows",
    )(d0, d1, counts, ps, nu, h2)


def _expert_kernel(be_ref, nu_ref, x_ref, wg_hbm, wu_hbm, wd_hbm, o_ref,
                   wg_f32, wu_f32, wd_f32, wg_scr, wu_scr, wd_scr, slot_ref, sem):
    i = pl.program_id(0)
    n_used = nu_ref[0]
    expert = be_ref[i]
    new_expert = jnp.logical_or(i == 0, expert != be_ref[jnp.maximum(i - 1, 0)])

    def weight_copies(e, slot):
        return [pltpu.make_async_copy(src.at[e], dst.at[slot], sem.at[slot])
                for src, dst in ((wg_hbm, wg_f32), (wu_hbm, wu_f32), (wd_hbm, wd_f32))]

    @pl.when(i == 0)
    def _():
        slot_ref[0] = 0
        for cp in weight_copies(expert, 0):
            cp.start()

    @pl.when(jnp.logical_and(new_expert, i < n_used))
    def _():
        slot = slot_ref[0]
        for cp in weight_copies(expert, slot):
            cp.wait()
        nxt = lax.while_loop(lambda j: jnp.logical_and(j < n_used, be_ref[jnp.minimum(j, n_used - 1)] == expert),
                             lambda j: j + 1, i + 1)

        @pl.when(nxt < n_used)
        def _():
            for cp in weight_copies(be_ref[nxt], 1 - slot):
                cp.start()

        wg_scr[...] = wg_f32[slot].astype(BF16)
        wu_scr[...] = wu_f32[slot].astype(BF16)
        wd_scr[...] = wd_f32[slot].astype(BF16)
        slot_ref[0] = 1 - slot

    @pl.when(i < n_used)
    def _():
        x_hi, x_lo = _unpack_bf16_pairs(x_ref[...])
        x = jnp.concatenate([x_hi.astype(BF16), x_lo.astype(BF16)], axis=1)
        g = jnp.dot(x, wg_scr[...], preferred_element_type=F32)
        u = jnp.dot(x, wu_scr[...], preferred_element_type=F32)
        act = (g * (1.0 / (1.0 + jnp.exp(-g))) * u).astype(BF16)
        o_ref[...] = _pack_bf16_pairs(jnp.dot(act, wd_scr[...], preferred_element_type=F32))

    @pl.when(i >= nu_ref[0])
    def _():
        o_ref[...] = jnp.zeros(o_ref.shape, o_ref.dtype)


def _experts(be, nu, xb, wg, wu, wd):
    p = xb.shape[0]
    _, d, de = wg.shape
    return pl.pallas_call(
        _expert_kernel,
        grid_spec=pltpu.PrefetchScalarGridSpec(
            num_scalar_prefetch=2,
            grid=(p // MOE_BLK,),
            in_specs=[pl.BlockSpec((MOE_BLK, d // 2), lambda i, be, nu: (i, 0)),
                      pl.BlockSpec(memory_space=pl.ANY),
                      pl.BlockSpec(memory_space=pl.ANY),
                      pl.BlockSpec(memory_space=pl.ANY)],
            out_specs=pl.BlockSpec((MOE_BLK, d // 2), lambda i, be, nu: (i, 0)),
            scratch_shapes=[pltpu.VMEM((2, d, de), F32), pltpu.VMEM((2, d, de), F32),
                            pltpu.VMEM((2, de, d), F32),
                            pltpu.VMEM((d, de), BF16), pltpu.VMEM((d, de), BF16),
                            pltpu.VMEM((de, d), BF16),
                            pltpu.SMEM((1,), I32),
                            pltpu.SemaphoreType.DMA((2,))]),
        out_shape=jax.ShapeDtypeStruct((p, d // 2), I32),
        compiler_params=_cparams(("arbitrary",)),
        name="moe_expert_ffn",
    )(be, nu, xb, wg, wu, wd)


def _combine_kernel(d0_ref, d1_ref, x_ref, info_ref, g2_ref, yb_ref, o_ref, rows_scr, sem):
    i = pl.program_id(0)
    cur = i % 2

    def copy(buf, which, r, src_row, rows=1):
        return pltpu.make_async_copy(yb_ref.at[pl.ds(src_row, rows)],
                                     rows_scr.at[buf, which, pl.ds(r, rows)], sem.at[buf])

    def start_gather(step, buf):
        base = step * TOK_TILE

        def issue(g, c):
            for k in range(SUBLANES):
                r = g * SUBLANES + k
                copy(buf, 0, r, d0_ref[base + r]).start()
                copy(buf, 1, r, d1_ref[base + r]).start()
            return c

        lax.fori_loop(0, TOK_TILE // SUBLANES, issue, 0)

    @pl.when(i == 0)
    def _():
        start_gather(0, 0)

    @pl.when(i + 1 < pl.num_programs(0))
    def _():
        start_gather(i + 1, 1 - cur)

    copy(cur, 0, 0, 0, TOK_TILE).wait()
    copy(cur, 1, 0, 0, TOK_TILE).wait()

    info = info_ref[...]
    half = o_ref.shape[1] // 2
    y0_hi, y0_lo = _unpack_bf16_pairs(rows_scr[cur, 0])
    y1_hi, y1_lo = _unpack_bf16_pairs(rows_scr[cur, 1])
    g2 = g2_ref[0]
    o_ref[:, :half] = x_ref[:, :half] + g2[:, :half] * (info[:, 2:3] * y0_hi + info[:, 3:4] * y1_hi)
    o_ref[:, half:] = x_ref[:, half:] + g2[:, half:] * (info[:, 2:3] * y0_lo + info[:, 3:4] * y1_lo)


def _combine(d0, d1, x1, info, g2, yb, seq):
    n, d = x1.shape
    tpb = seq // TOK_TILE
    return pl.pallas_call(
        _combine_kernel,
        grid_spec=pltpu.PrefetchScalarGridSpec(
            num_scalar_prefetch=2,
            grid=(n // TOK_TILE,),
            in_specs=[pl.BlockSpec((TOK_TILE, d), lambda i, a, b: (i, 0)),
                      pl.BlockSpec((TOK_TILE, LANES), lambda i, a, b: (i, 0)),
                      pl.BlockSpec((1, 1, d), lambda i, a, b: (i // tpb, 0, 0)),
                      pl.BlockSpec(memory_space=pl.ANY)],
            out_specs=pl.BlockSpec((TOK_TILE, d), lambda i, a, b: (i, 0)),
            scratch_shapes=[pltpu.VMEM((2, 2, TOK_TILE, d // 2), I32),
                            pltpu.SemaphoreType.DMA((2,))]),
        out_shape=jax.ShapeDtypeStruct((n, d), F32),
        compiler_params=_cparams(("arbitrary",)),
        name="moe_combine",
    )(d0, d1, x1, info, g2, yb)


def _tri_inclusive_rev(k):
    l = (np.arange(k)[:, None] >= np.arange(k)[None, :]).astype(np.float32)
    half = np.concatenate([l, np.ones((k, k), np.float32)], axis=1)
    return jnp.asarray(np.concatenate([half, half], axis=0), dtype=BF16)


def _tri_strict_lower(k):
    return jnp.asarray((np.arange(k)[None, :] < np.arange(k)[:, None]).astype(np.float32), dtype=BF16)


def kernel(x, c, w_mod, b_mod, ln1_g, w_in, q_norm_g, w_q_up, q_gain, k_gain, rel_bias, gn_a, gn_b,
           w_out, ln2_g, router_g, router_e, w_gate, w_up, w_down):
    batch, seq, d = x.shape
    n = batch * seq
    assert w_mod.shape[0] == 1 and d == D_A + D_B
    assert seq % KB_A == 0 and seq % TOK_TILE == 0 and n % 512 == 0
    x2 = x.reshape(n, d)

    c8 = jnp.pad(c, ((0, 8 - batch), (0, 0)))
    mod = _modulation(c8, w_mod.reshape(d, -1), b_mod.reshape(1, -1))[:batch]
    sh1, sc1, g1, sh2, sc2, g2 = [m.reshape(batch, 1, d) for m in jnp.split(mod, 6, axis=-1)]

    wi = w_in.reshape(d, -1)
    n_a = Q_RANK + 2 * HEAD_DIM + IDX_DIM + IDX_HEADS
    wi = jnp.concatenate([wi[:, n_a:], wi[:, :n_a],
                          jnp.zeros((d, D_IN_PAD - wi.shape[1]), wi.dtype)], axis=1).astype(BF16)
    proj = _ln_proj(x2, ln1_g.reshape(1, d), sc1, sh1, wi, seq)

    qa, qidx, ka, va, kidx, widx = _aprep(
        proj, q_norm_g.reshape(1, -1), w_q_up.reshape(Q_RANK, -1).astype(BF16),
        q_gain.reshape(1, -1), k_gain.reshape(1, -1))

    bt = _bias_tiles(rel_bias)
    bound = _logit_bound(q_gain.reshape(1, -1), k_gain.reshape(1, -1), rel_bias)[0, :1]
    oa = _attn_a(qidx, widx.T, qa, kidx, ka, va, bt, gn_a.reshape(1, -1), bound, batch, seq)
    ob = _attn_b(proj, _tri_inclusive_rev(QB_B), batch, seq)

    rw = jnp.concatenate([router_e.reshape(d, -1), router_g.reshape(d, -1),
                          jnp.zeros((d, LANES - N_EXPERTS - N_GROUPS), F32)], axis=1)
    x1, h2, lg = _out_proj(x2, oa, ob, gn_b.reshape(1, -1), w_out.reshape(d, d).astype(BF16),
                           g1, ln2_g.reshape(1, d), sc2, sh2, rw, seq)

    info, cnt = _route(lg)
    counts = cnt[0, :N_EXPERTS].astype(I32)
    p_rows = 2 * n + N_EXPERTS * MOE_BLK
    ps, be, nu = _plan(counts, p_rows // MOE_BLK)
    ps_lanes = jnp.pad(ps.astype(F32), (0, LANES - N_EXPERTS)).reshape(1, LANES)
    dinfo = _dest(info, ps_lanes, _tri_strict_lower(512))
    d0 = dinfo[:, 0].astype(I32)
    d1 = dinfo[:, 1].astype(I32)
    xb = _scatter_rows(d0, d1, counts, ps, nu, h2, p_rows)
    yb = _experts(be, nu, xb,
                  w_gate.reshape(N_EXPERTS, d, D_EXPERT),
                  w_up.reshape(N_EXPERTS, d, D_EXPERT),
                  w_down.reshape(N_EXPERTS, D_EXPERT, d))
    out = _combine(d0, d1, x1, info, g2, yb, seq)
    return out.reshape(batch, seq, d)
```

```python
import functools
import math

import numpy as np
import jax
import jax.numpy as jnp
from jax import lax
from jax.experimental import pallas as pl
from jax.experimental.pallas import tpu as pltpu

F32 = jnp.float32
BF16 = jnp.bfloat16
I32 = jnp.int32

HEAD_DIM = 128
N_HEADS_A = 8
N_HEADS_B = 8
D_A = N_HEADS_A * HEAD_DIM
D_B = N_HEADS_B * HEAD_DIM
Q_RANK = 512
IDX_HEADS = 16
IDX_DIM = 64
TOPK_MAX = 256
N_BUCKETS = 32
MAX_DISTANCE = 128
N_GROUPS = 4
EXPERTS_PER_GROUP = 8
N_EXPERTS = N_GROUPS * EXPERTS_PER_GROUP
D_EXPERT = 512
EPS = 1e-6

LANES = 128
SUBLANES = 8
VMEM_LIMIT = 56 * 1024 * 1024
NEG = -1e30
INT_MIN = -(2 ** 31)
EXP2_UNDERFLOW = 150.0
LOG2E = math.log2(math.e)
PLAIN_SOFTMAX_LIMIT = 64.0
GUESS_HALF_WIDTH = 0.1
EXTRACT_BELOW = 3.0

QB_A = 128
KB_A = 256
QB_B = 128
MOE_BLK = 256
TOK_TILE = 256

COL_QB, COL_KB, COL_VB, COL_A = 0, D_B, 2 * D_B, 3 * D_B
A_CQ, A_KA, A_VA, A_KIDX, A_WIDX = 0, 512, 640, 768, 832
A_WIDTH = 1024
D_IN_PAD = COL_A + A_WIDTH


def _cparams(sem=None):
    return pltpu.CompilerParams(dimension_semantics=sem, vmem_limit_bytes=VMEM_LIMIT)


def _rms(x):
    return x * lax.rsqrt(jnp.mean(x * x, axis=-1, keepdims=True) + EPS)


def _pack_bf16_pairs(x):
    c = x.shape[1] // 2
    hi = pltpu.bitcast(x[:, :c].astype(BF16).astype(F32), I32)
    lo = pltpu.bitcast(x[:, c:].astype(BF16).astype(F32), I32)
    return hi | lax.shift_right_logical(lo, 16)


def _unpack_bf16_pairs(u):
    hi = pltpu.bitcast(u & jnp.int32(-65536), F32)
    lo = pltpu.bitcast(lax.shift_left(u, 16), F32)
    return hi, lo


def _mod_kernel(c_ref, w_ref, b_ref, o_ref):
    c = c_ref[...]
    s = c * (1.0 / (1.0 + jnp.exp(-c)))
    o_ref[...] = jnp.dot(s, w_ref[...], preferred_element_type=F32,
                         precision=lax.Precision.HIGHEST) + b_ref[...]


def _modulation(c8, w_mod, b_mod):
    d, n6 = w_mod.shape
    tn = 1024
    return pl.pallas_call(
        _mod_kernel,
        grid=(n6 // tn,),
        in_specs=[pl.BlockSpec((8, d), lambda j: (0, 0)),
                  pl.BlockSpec((d, tn), lambda j: (0, j)),
                  pl.BlockSpec((1, tn), lambda j: (0, j))],
        out_specs=pl.BlockSpec((8, tn), lambda j: (0, j)),
        out_shape=jax.ShapeDtypeStruct((8, n6), F32),
        compiler_params=_cparams(("arbitrary",)),
        name="modulation",
    )(c8, w_mod, b_mod)


def _ln_proj_kernel(x_ref, g_ref, sc_ref, sh_ref, w_ref, o_ref, h_scr, *, q_tiles):
    @pl.when(pl.program_id(1) == 0)
    def _():
        h = _rms(x_ref[...]) * g_ref[...]
        h = h * (1.0 + sc_ref[0]) + sh_ref[0]
        h_scr[...] = h.astype(BF16)

    col_scale = jnp.where(pl.program_id(1) < q_tiles, HEAD_DIM ** -0.5 * LOG2E, 1.0)
    acc = jnp.dot(h_scr[...], w_ref[...], preferred_element_type=F32)
    o_ref[...] = (acc * col_scale).astype(o_ref.dtype)


def _ln_proj(x2, ln_g, sc, sh, w, seq):
    n, d = x2.shape
    ncol = w.shape[1]
    tm = min(1024, seq)
    tn = 512
    tpb = seq // tm
    assert COL_QB == 0 and D_B % tn == 0
    return pl.pallas_call(
        functools.partial(_ln_proj_kernel, q_tiles=D_B // tn),
        grid=(n // tm, ncol // tn),
        in_specs=[pl.BlockSpec((tm, d), lambda i, j: (i, 0)),
                  pl.BlockSpec((1, d), lambda i, j: (0, 0)),
                  pl.BlockSpec((1, 1, d), lambda i, j: (i // tpb, 0, 0)),
                  pl.BlockSpec((1, 1, d), lambda i, j: (i // tpb, 0, 0)),
                  pl.BlockSpec((d, tn), lambda i, j: (0, j))],
        out_specs=pl.BlockSpec((tm, tn), lambda i, j: (i, j)),
        out_shape=jax.ShapeDtypeStruct((n, ncol), BF16),
        scratch_shapes=[pltpu.VMEM((tm, d), BF16)],
        compiler_params=_cparams(("parallel", "arbitrary")),
        name="ln_in_proj",
    )(x2, ln_g, sc, sh, w)


def _aprep_kernel(a_ref, qng_ref, wq_ref, qg_ref, kg_ref,
                  qa_ref, qidx_ref, ka_ref, va_ref, kidx_ref, widx_ref):
    cq = a_ref[:, A_CQ:A_CQ + Q_RANK].astype(F32)
    cqn = (_rms(cq) * qng_ref[...]).astype(BF16)
    qup = jnp.dot(cqn, wq_ref[...], preferred_element_type=F32)
    for h in range(N_HEADS_A):
        qh = qup[:, h * HEAD_DIM:(h + 1) * HEAD_DIM]
        qn = _rms(qh) * qg_ref[...] * (HEAD_DIM ** -0.5 * LOG2E)
        qa_ref[:, h * HEAD_DIM:(h + 1) * HEAD_DIM] = qn.astype(BF16)
    for h in range(IDX_HEADS):
        qi = qup[:, D_A + h * IDX_DIM:D_A + (h + 1) * IDX_DIM] * (IDX_DIM ** -0.5)
        qidx_ref[h] = qi.astype(BF16)
    ka = a_ref[:, A_KA:A_KA + HEAD_DIM].astype(F32)
    ka_ref[...] = (_rms(ka) * kg_ref[...]).astype(BF16)
    va_ref[...] = a_ref[:, A_VA:A_VA + HEAD_DIM]
    kidx_ref[...] = a_ref[:, A_KIDX:A_KIDX + IDX_DIM]
    widx_ref[...] = a_ref[:, A_WIDX:A_WIDX + IDX_HEADS].astype(F32) * (IDX_HEADS ** -0.5)


def _aprep(proj, q_norm_g, w_q_up, q_gain, k_gain):
    n = proj.shape[0]
    tm = 512
    cblk = COL_A // A_WIDTH
    nup = w_q_up.shape[1]
    return pl.pallas_call(
        _aprep_kernel,
        grid=(n // tm,),
        in_specs=[pl.BlockSpec((tm, A_WIDTH), lambda i: (i, cblk)),
                  pl.BlockSpec((1, Q_RANK), lambda i: (0, 0)),
                  pl.BlockSpec((Q_RANK, nup), lambda i: (0, 0)),
                  pl.BlockSpec((1, HEAD_DIM), lambda i: (0, 0)),
                  pl.BlockSpec((1, HEAD_DIM), lambda i: (0, 0))],
        out_specs=[pl.BlockSpec((tm, D_A), lambda i: (i, 0)),
                   pl.BlockSpec((IDX_HEADS, tm, IDX_DIM), lambda i: (0, i, 0)),
                   pl.BlockSpec((tm, HEAD_DIM), lambda i: (i, 0)),
                   pl.BlockSpec((tm, HEAD_DIM), lambda i: (i, 0)),
                   pl.BlockSpec((tm, IDX_DIM), lambda i: (i, 0)),
                   pl.BlockSpec((tm, IDX_HEADS), lambda i: (i, 0))],
        out_shape=[jax.ShapeDtypeStruct((n, D_A), BF16),
                   jax.ShapeDtypeStruct((IDX_HEADS, n, IDX_DIM), BF16),
                   jax.ShapeDtypeStruct((n, HEAD_DIM), BF16),
                   jax.ShapeDtypeStruct((n, HEAD_DIM), BF16),
                   jax.ShapeDtypeStruct((n, IDX_DIM), BF16),
                   jax.ShapeDtypeStruct((n, IDX_HEADS), F32)],
        compiler_params=_cparams(("parallel",)),
        name="group_a_prep",
    )(proj, q_norm_g, w_q_up, q_gain, k_gain)


def _t5_bucket_starts():
    max_exact = N_BUCKETS // 2
    d = np.arange(0, 4 * MAX_DISTANCE, dtype=np.int64)
    df = np.maximum(d, 1).astype(np.float32)
    large = max_exact + (np.log(df / np.float32(max_exact)) / np.float32(math.log(MAX_DISTANCE / max_exact))
                         * np.float32(N_BUCKETS - max_exact)).astype(np.int32)
    large = np.minimum(large, N_BUCKETS - 1)
    bucket = np.where(d < max_exact, d, large)
    assert np.all(np.diff(bucket) >= 0) and bucket[-1] == N_BUCKETS - 1
    return [int(np.argmax(bucket >= b)) for b in range(N_BUCKETS)]


_BUCKET_START = _t5_bucket_starts()
N_BIAS_TILES = 2 * KB_A // LANES


def _bias_kernel(rb_ref, o_ref):
    di = pl.program_id(0)
    h = pl.program_id(1)
    i = lax.broadcasted_iota(I32, (QB_A, KB_A), 0)
    j = lax.broadcasted_iota(I32, (QB_A, KB_A), 1)
    d = di * LANES + i - j
    val = jnp.full((QB_A, KB_A), rb_ref[0, h], F32)
    for b in range(1, N_BUCKETS):
        val = jnp.where(d >= _BUCKET_START[b], rb_ref[b, h], val)
    o_ref[0, 0] = (val - rb_ref[N_BUCKETS - 1, h]) * LOG2E


def _bias_tiles(rel_bias):
    return pl.pallas_call(
        _bias_kernel,
        grid=(N_BIAS_TILES, N_HEADS_A),
        in_specs=[pl.BlockSpec(memory_space=pltpu.SMEM)],
        out_specs=pl.BlockSpec((1, 1, QB_A, KB_A), lambda a, h: (a, h, 0, 0)),
        out_shape=jax.ShapeDtypeStruct((N_BIAS_TILES, N_HEADS_A, QB_A, KB_A), F32),
        compiler_params=_cparams(("arbitrary", "arbitrary")),
        name="t5_bias_tiles",
    )(rel_bias)


def _bound_kernel(qg_ref, kg_ref, rb_ref, o_ref):
    qmax = jnp.max(jnp.abs(qg_ref[...]), axis=1, keepdims=True)
    kmax = jnp.max(jnp.abs(kg_ref[...]), axis=1, keepdims=True)
    rb = rb_ref[...]
    shifted = jnp.abs(rb - rb[N_BUCKETS - 1:N_BUCKETS, :])
    bmax = jnp.max(jnp.max(shifted, axis=1, keepdims=True), axis=0, keepdims=True)
    bound = qmax * kmax * (math.sqrt(HEAD_DIM) * 1.02) + bmax
    o_ref[...] = jnp.broadcast_to(bound, o_ref.shape)


def _logit_bound(q_gain, k_gain, rel_bias):
    return pl.pallas_call(
        _bound_kernel,
        out_shape=jax.ShapeDtypeStruct((1, LANES), F32),
        name="dsa_logit_bound",
    )(q_gain, k_gain, rel_bias)


def _attn_a_kernel(qidx_ref, w_ref, qa_ref, kidx_ref, ka_ref, va_ref, bt_ref, gn_ref, bound_ref, o_ref,
                   keys_scr, keyt_scr, m_scr, acc_scr, s_scr, mb_scr, *, topk):
    qb = pl.program_id(1)
    t0 = qb * QB_A
    kbl = (t0 + QB_A - 1) // KB_A
    row = t0 + lax.broadcasted_iota(I32, (QB_A, KB_A), 0)
    col0 = lax.broadcasted_iota(I32, (QB_A, KB_A), 1)
    nt = (((1,), (1,)), ((), ()))

    reps = KB_A // LANES

    def sort_key(v):
        bits = pltpu.bitcast(v, I32)
        return jnp.where(bits < 0, bits ^ jnp.int32(0x7FFFFFFF), bits)

    def key_value(k):
        return pltpu.bitcast(jnp.where(k < 0, k ^ jnp.int32(0x7FFFFFFF), k), F32)

    key_pos = lax.broadcasted_iota(I32, (KB_A, QB_A), 0)
    qry_pos = t0 + lax.broadcasted_iota(I32, (KB_A, QB_A), 1)

    def score_tile(kb, carry):
        smin, smax, s1, s2 = carry
        kt = kidx_ref[pl.ds(pl.multiple_of(kb * KB_A, KB_A), KB_A), :]
        score = jnp.zeros((KB_A, QB_A), F32)
        for hp in range(IDX_HEADS // 2):
            q2 = qidx_ref[2 * hp:2 * hp + 2].reshape(2 * QB_A, IDX_DIM)
            sc = lax.dot_general(kt, q2, nt, preferred_element_type=F32)
            score = score + w_ref[2 * hp:2 * hp + 1, :] * jnp.maximum(sc[:, :QB_A], 0.0)
            score = score + w_ref[2 * hp + 1:2 * hp + 2, :] * jnp.maximum(sc[:, QB_A:], 0.0)
        causal = (kb * KB_A + key_pos) <= qry_pos
        key_t = jnp.where(causal, sort_key(score), jnp.int32(INT_MIN))
        keyt_scr[kb] = key_t
        keys_scr[kb] = key_t.T
        smin = jnp.minimum(smin, jnp.min(jnp.where(causal, score, jnp.inf), axis=0, keepdims=True))
        smax = jnp.maximum(smax, jnp.max(jnp.where(causal, score, -jnp.inf), axis=0, keepdims=True))
        live = jnp.where(causal, score, 0.0)
        s1 = s1 + jnp.sum(live, axis=0, keepdims=True)
        s2 = s2 + jnp.sum(live * live, axis=0, keepdims=True)
        return smin, smax, s1, s2

    def score_pair(i, carry):
        return score_tile(2 * i + 1, score_tile(2 * i, carry))

    n_tiles = kbl + 1
    zero_row = jnp.zeros((1, QB_A), F32)
    stats = lax.fori_loop(0, n_tiles // 2, score_pair,
                          (jnp.full((1, QB_A), jnp.inf, F32), jnp.full((1, QB_A), -jnp.inf, F32),
                           zero_row, zero_row))
    smin, smax, s1, s2 = lax.cond(n_tiles % 2 == 1, lambda c: score_tile(n_tiles - 1, c),
                                  lambda c: c, stats)

    qry1 = t0 + lax.broadcasted_iota(I32, (1, QB_A), 1)
    kf = float(topk)
    acc_rows = 32
    n_causal = (qry1 + 1).astype(F32)

    mean = s1 / n_causal
    dev = jnp.sqrt(jnp.maximum(s2 / n_causal - mean * mean, 0.0))
    frac_top = jnp.minimum(kf / n_causal, 1.0)
    tail = jnp.minimum(frac_top, 1.0 - frac_top)
    tq = jnp.sqrt(-2.0 * jnp.log(jnp.maximum(tail, 1e-6)))
    zq = tq - ((0.010328 * tq + 0.802853) * tq + 2.515517) / (((0.001308 * tq + 0.189269) * tq + 1.432788) * tq + 1.0)
    zq = jnp.where(frac_top > 0.5, -zq, zq)

    def for_each_key_tile(fold, init):
        return lax.fori_loop(0, kbl + 1, lambda kb, a: fold(a, keyt_scr[kb]), init)

    def search_pass(p, state):
        lo, hi, clo, chi, open_q = state
        lo_v = key_value(lo)
        hi_v = key_value(hi)
        gap = clo - chi
        frac = (clo - (kf - 0.5)) / gap
        frac = jnp.where(p % 2 == 1, 0.7 * frac + 0.15, frac)
        frac = jnp.where(gap > 16.0, frac, 0.5)
        value = lo_v + (hi_v - lo_v) * frac
        value = jnp.where(p == 0, mean + (zq - GUESS_HALF_WIDTH) * dev, value)
        value = jnp.where(p == 1, mean + (zq + GUESS_HALF_WIDTH) * dev, value)
        cand = sort_key(value)
        cand = jnp.where(p % 8 == 7, lo + lax.shift_right_logical(hi - lo, 1), cand)
        cand = jnp.minimum(jnp.maximum(cand, lo + 1), hi - 1)

        def count(cnt, keys):
            ge = jnp.where(keys >= cand, 1.0, 0.0)
            return cnt + jnp.sum(ge.reshape(KB_A // acc_rows, acc_rows, QB_A), axis=0)

        tot = jnp.sum(for_each_key_tile(count, jnp.zeros((acc_rows, QB_A), F32)), axis=0, keepdims=True)
        ge = tot >= kf
        lo = jnp.where(ge, cand, lo)
        clo = jnp.where(ge, tot, clo)
        hi = jnp.where(ge, hi, cand)
        chi = jnp.where(ge, chi, tot)
        width = hi - lo
        settled = jnp.where(clo == kf, 1.0, jnp.where(width == 1, 1.0, 0.0))
        open_q = jnp.where(settled > 0.5, 0.0, open_q)
        return lo, hi, clo, chi, open_q

    def extract_pass(state):
        lo, hi, clo, chi, open_q = state

        def top_below(best, keys):
            below = jnp.where(keys < hi, keys, jnp.int32(INT_MIN))
            return jnp.maximum(best, jnp.max(below.reshape(KB_A // acc_rows, acc_rows, QB_A), axis=0))

        best = for_each_key_tile(top_below, jnp.full((acc_rows, QB_A), INT_MIN, I32))
        best = jnp.max(best, axis=0, keepdims=True)
        is_open = open_q > 0.5
        last = jnp.logical_and(is_open, kf - chi <= 1.0)
        more = jnp.logical_and(is_open, kf - chi > 1.0)
        lo = jnp.where(last, best, lo)
        clo = jnp.where(last, chi + 1.0, clo)
        hi = jnp.where(more, best, hi)
        chi = jnp.where(more, chi + 1.0, chi)
        open_q = jnp.where(last, 0.0, open_q)
        return lo, hi, clo, chi, open_q

    def missing(state):
        return jnp.max(jnp.where(state[4] > 0.5, kf - state[3], 0.0))

    def search_step(carry):
        p, state, lacking = carry
        state = lax.cond(lacking <= EXTRACT_BELOW,
                         lambda st: extract_pass(st),
                         lambda st: search_pass(p + 1, search_pass(p, st)), state)
        return p + 2, state, missing(state)

    open0 = jnp.where(qry1 >= topk, 1.0, 0.0)
    lo0 = sort_key(smin)
    hi0 = sort_key(smax) + 1
    open0 = jnp.where(hi0 - lo0 == 1, 0.0, open0)
    state0 = (lo0, hi0, n_causal, zero_row, open0)
    _, (lo, _, _, _, _), _ = lax.while_loop(
        lambda c: c[2] > 0.0, search_step, (jnp.int32(0), state0, missing(state0)))
    thr = jnp.where(qry1 >= topk, lo, jnp.int32(INT_MIN))
    thr = jnp.broadcast_to(thr, (QB_A, QB_A)).T
    thrb = jnp.tile(thr, (1, reps))

    m_scr[...] = jnp.full(m_scr.shape, NEG, F32)
    acc_scr[...] = jnp.zeros(acc_scr.shape, F32)

    plain = bound_ref[0] <= PLAIN_SOFTMAX_LIMIT

    def attend(kb, near, online):
        mb = jnp.where(keys_scr[kb] >= thrb, 0.0, NEG)
        if near:
            mb = jnp.where((kb * KB_A + col0) <= row, mb, NEG)
            di = (t0 - kb * KB_A) // LANES
        mb_scr[...] = mb
        start = pl.multiple_of(kb * KB_A, KB_A)
        kt = ka_ref[pl.ds(start, KB_A), :]
        vt = jnp.concatenate([va_ref[pl.ds(start, KB_A), :], jnp.ones((KB_A, LANES), BF16)], axis=1)
        for h in range(N_HEADS_A):
            q = qa_ref[:, h * HEAD_DIM:(h + 1) * HEAD_DIM]
            s_scr[h] = lax.dot_general(q, kt, nt, preferred_element_type=F32)
        for h in range(N_HEADS_A):
            s = s_scr[h] + mb_scr[...]
            if near:
                s = s + bt_ref[di, h]
            if online:
                m_prev = m_scr[h]
                m_new = jnp.maximum(m_prev, jnp.max(s, axis=1, keepdims=True))
                alpha = jnp.exp2(m_prev - m_new)
                p = jnp.exp2(s - jnp.tile(m_new, (1, reps)))
                acc_scr[h] = (jnp.tile(alpha, (1, 2)) * acc_scr[h]
                              + jnp.dot(p.astype(BF16), vt, preferred_element_type=F32))
                m_scr[h] = m_new
            else:
                acc_scr[h] += jnp.dot(jnp.exp2(s).astype(BF16), vt, preferred_element_type=F32)

    def attend_all(online):
        n_far = jnp.maximum(kbl - 1, 0)

        def far_pair(i, carry):
            attend(2 * i, False, online)
            attend(2 * i + 1, False, online)
            return carry

        lax.fori_loop(0, n_far // 2, far_pair, 0)

        @pl.when(n_far % 2 == 1)
        def _():
            attend(n_far - 1, False, online)

        @pl.when(kbl >= 1)
        def _():
            attend(kbl - 1, True, online)

        attend(kbl, True, online)

    @pl.when(plain)
    def _():
        attend_all(False)

    @pl.when(jnp.logical_not(plain))
    def _():
        attend_all(True)

    ssq = jnp.zeros((QB_A, LANES), F32)
    for h in range(N_HEADS_A):
        oh = acc_scr[h, :, :HEAD_DIM] / acc_scr[h, :, HEAD_DIM:]
        acc_scr[h, :, :HEAD_DIM] = oh
        ssq = ssq + jnp.sum(oh * oh, axis=1, keepdims=True)
    inv = lax.rsqrt(ssq * (1.0 / D_A) + EPS)
    for h in range(N_HEADS_A):
        sl = slice(h * HEAD_DIM, (h + 1) * HEAD_DIM)
        o_ref[:, sl] = (acc_scr[h, :, :HEAD_DIM] * inv * gn_ref[:, sl]).astype(o_ref.dtype)


def _attn_a(qidx, widx, qa, kidx, ka, va, bt, gn_a, bound, batch, seq):
    n = qa.shape[0]
    nq = seq // QB_A
    nkt = seq // KB_A
    topk = min(TOPK_MAX, seq // 4)
    return pl.pallas_call(
        functools.partial(_attn_a_kernel, topk=topk),
        grid=(batch, nq),
        in_specs=[pl.BlockSpec((IDX_HEADS, QB_A, IDX_DIM), lambda b, q: (0, b * nq + q, 0)),
                  pl.BlockSpec((IDX_HEADS, QB_A), lambda b, q: (0, b * nq + q)),
                  pl.BlockSpec((QB_A, D_A), lambda b, q: (b * nq + q, 0)),
                  pl.BlockSpec((seq, IDX_DIM), lambda b, q: (b, 0)),
                  pl.BlockSpec((seq, HEAD_DIM), lambda b, q: (b, 0)),
                  pl.BlockSpec((seq, HEAD_DIM), lambda b, q: (b, 0)),
                  pl.BlockSpec(bt.shape, lambda b, q: (0, 0, 0, 0)),
                  pl.BlockSpec((1, D_A), lambda b, q: (0, 0)),
                  pl.BlockSpec(memory_space=pltpu.SMEM)],
        out_specs=pl.BlockSpec((QB_A, D_A), lambda b, q: (b * nq + q, 0)),
        out_shape=jax.ShapeDtypeStruct((n, D_A), BF16),
        scratch_shapes=[pltpu.VMEM((nkt, QB_A, KB_A), I32),
                        pltpu.VMEM((nkt, KB_A, QB_A), I32),
                        pltpu.VMEM((N_HEADS_A, QB_A, LANES), F32),
                        pltpu.VMEM((N_HEADS_A, QB_A, 2 * HEAD_DIM), F32),
                        pltpu.VMEM((N_HEADS_A, QB_A, KB_A), F32),
                        pltpu.VMEM((QB_A, KB_A), F32)],
        compiler_params=_cparams(("parallel", "arbitrary")),
        name="dsa_attention",
    )(qidx, widx, qa, kidx, ka, va, bt, gn_a, bound)


def _attn_b_kernel(q_ref, k_ref, v_ref, tri_ref, o_ref, rest_scr, z_scr, cs_scr):
    qb = pl.program_id(1)
    row = lax.broadcasted_iota(I32, (QB_B, QB_B), 0)
    col = lax.broadcasted_iota(I32, (QB_B, QB_B), 1)
    strict = col < row
    nt = (((1,), (1,)), ((), ()))
    def step(kb, diag):
        start = pl.multiple_of(kb * QB_B, QB_B)
        heads = [slice(h * HEAD_DIM, (h + 1) * HEAD_DIM) for h in range(N_HEADS_B)]
        for h, sl in enumerate(heads):
            kt = k_ref[pl.ds(start, QB_B), sl]
            z_scr[h] = lax.dot_general(q_ref[:, sl], kt, nt, preferred_element_type=F32)
        for h, sl in enumerate(heads):
            z = z_scr[h]
            sp = jnp.maximum(z, 0.0) + jnp.log(1.0 + jnp.exp2(-jnp.abs(z))) * LOG2E
            if diag:
                sp = jnp.where(strict, sp, 0.0)
            hi = sp.astype(BF16)
            lo = (sp - hi.astype(F32)).astype(BF16)
            cs_scr[h] = jnp.dot(jnp.concatenate([hi, lo], axis=1), tri_ref[...],
                                preferred_element_type=F32)
        worst = None
        for h, sl in enumerate(heads):
            vt = v_ref[pl.ds(start, QB_B), sl]
            z = z_scr[h]
            cs = cs_scr[h, :, :QB_B]
            tot = cs_scr[h, :, QB_B:]
            if diag:
                a = jnp.where(strict, jnp.exp2(z - cs), 0.0)
                o_ref[:, sl] = jnp.dot(a.astype(BF16), vt, preferred_element_type=F32)
                rest = tot
            else:
                rest = rest_scr[h]
                a = jnp.exp2(z - cs - rest)
                o_ref[:, sl] += jnp.dot(a.astype(BF16), vt, preferred_element_type=F32)
                rest = rest + tot
            rest_scr[h] = rest
            worst = rest if worst is None else jnp.minimum(worst, rest)
        return jnp.min(worst)

    def more(kb, smallest):
        return jnp.logical_and(kb >= 0, smallest < EXP2_UNDERFLOW)

    def body(carry):
        kb, _ = carry
        return kb - 1, more(kb - 1, step(kb, False))

    lax.while_loop(lambda c: c[1], body, (qb - 1, more(qb - 1, step(qb, True))))


def _attn_b(proj, tri, batch, seq):
    n = proj.shape[0]
    nq = seq // QB_B
    return pl.pallas_call(
        _attn_b_kernel,
        grid=(batch, nq),
        in_specs=[pl.BlockSpec((QB_B, D_B), lambda b, q: (b * nq + q, COL_QB // D_B)),
                  pl.BlockSpec((seq, D_B), lambda b, q: (b, COL_KB // D_B)),
                  pl.BlockSpec((seq, D_B), lambda b, q: (b, COL_VB // D_B)),
                  pl.BlockSpec(tri.shape, lambda b, q: (0, 0))],
        out_specs=pl.BlockSpec((QB_B, D_B), lambda b, q: (b * nq + q, 0)),
        out_shape=jax.ShapeDtypeStruct((n, D_B), F32),
        scratch_shapes=[pltpu.VMEM((N_HEADS_B, QB_B, QB_B), F32),
                        pltpu.VMEM((N_HEADS_B, QB_B, QB_B), F32),
                        pltpu.VMEM((N_HEADS_B, QB_B, 2 * QB_B), F32)],
        compiler_params=_cparams(("parallel", "arbitrary")),
        name="stick_breaking_attention",
    )(proj, proj, proj, tri)


def _out_proj_kernel(x_ref, oa_ref, ob_ref, gnb_ref, w_ref, g1_ref, ln_ref, sc_ref, sh_ref, rw_ref,
                     x1_ref, h2_ref, lg_ref):
    obn = (_rms(ob_ref[...]) * gnb_ref[...]).astype(BF16)
    y = jnp.dot(oa_ref[...], w_ref[0:D_A, :], preferred_element_type=F32)
    y = y + jnp.dot(obn, w_ref[D_A:D_A + D_B, :], preferred_element_type=F32)
    x1 = x_ref[...] + g1_ref[0] * y
    x1_ref[...] = x1
    h2 = _rms(x1) * ln_ref[...]
    h2 = h2 * (1.0 + sc_ref[0]) + sh_ref[0]
    h2_ref[...] = _pack_bf16_pairs(h2)
    hh = h2.astype(BF16)
    hl = (h2 - hh.astype(F32)).astype(BF16)
    rw = rw_ref[...]
    rh = rw.astype(BF16)
    rl = (rw - rh.astype(F32)).astype(BF16)
    both = jnp.dot(hh, jnp.concatenate([rh, rl], axis=1), preferred_element_type=F32)
    lg_ref[...] = both[:, :LANES] + both[:, LANES:] + jnp.dot(hl, rh, preferred_element_type=F32)


def _out_proj(x2, oa, ob, gn_b, w_out, g1, ln_g, sc, sh, rw, seq):
    n, d = x2.shape
    tm = 256
    tpb = seq // tm
    row = lambda i: (i, 0)
    fixed = lambda i: (0, 0)
    perb = lambda i: (i // tpb, 0, 0)
    return pl.pallas_call(
        _out_proj_kernel,
        grid=(n // tm,),
        in_specs=[pl.BlockSpec((tm, d), row),
                  pl.BlockSpec((tm, D_A), row),
                  pl.BlockSpec((tm, D_B), row),
                  pl.BlockSpec((1, D_B), fixed),
                  pl.BlockSpec(w_out.shape, fixed),
                  pl.BlockSpec((1, 1, d), perb),
                  pl.BlockSpec((1, d), fixed),
                  pl.BlockSpec((1, 1, d), perb),
                  pl.BlockSpec((1, 1, d), perb),
                  pl.BlockSpec(rw.shape, fixed)],
        out_specs=[pl.BlockSpec((tm, d), row),
                   pl.BlockSpec((tm, d // 2), row),
                   pl.BlockSpec((tm, LANES), row)],
        out_shape=[jax.ShapeDtypeStruct((n, d), F32),
                   jax.ShapeDtypeStruct((n, d // 2), I32),
                   jax.ShapeDtypeStruct((n, LANES), F32)],
        compiler_params=_cparams(("parallel",)),
        name="out_proj_ln2_router",
    )(x2, oa, ob, gn_b, w_out, g1, ln_g, sc, sh, rw)


def _route_kernel(lg_ref, info_ref, cnt_ref):
    @pl.when(pl.program_id(0) == 0)
    def _():
        cnt_ref[...] = jnp.zeros(cnt_ref.shape, F32)

    lg = lg_ref[...]
    lane = lax.broadcasted_iota(I32, lg.shape, 1)
    lanef = lane.astype(F32)
    big = float(4 * LANES)
    gm = jnp.where(lane >= N_EXPERTS, jnp.where(lane < N_EXPERTS + N_GROUPS, 1.0, 0.0), 0.0) > 0.5
    lgm = jnp.where(gm, lg, NEG)
    mg = jnp.max(lgm, axis=1, keepdims=True)
    eg = jnp.where(gm, jnp.exp(lgm - mg), 0.0)
    pg = eg / jnp.sum(eg, axis=1, keepdims=True)
    gw = jnp.max(pg, axis=1, keepdims=True)
    gidx = jnp.min(jnp.where(gm, jnp.where(pg == gw, lanef - N_EXPERTS, big), big), axis=1, keepdims=True)
    lane_group = (lane // EXPERTS_PER_GROUP).astype(F32)
    em = jnp.where(lane < N_EXPERTS, jnp.where(lane_group == gidx, 1.0, 0.0), 0.0) > 0.5
    lem = jnp.where(em, lg, NEG)
    me = jnp.max(lem, axis=1, keepdims=True)
    ee = jnp.where(em, jnp.exp(lem - me), 0.0)
    pe = jnp.where(em, ee / jnp.sum(ee, axis=1, keepdims=True), -1.0)
    p1 = jnp.max(pe, axis=1, keepdims=True)
    i1 = jnp.min(jnp.where(pe == p1, lanef, big), axis=1, keepdims=True)
    pe2 = jnp.where(lanef == i1, -1.0, pe)
    p2 = jnp.max(pe2, axis=1, keepdims=True)
    i2 = jnp.min(jnp.where(pe2 == p2, lanef, big), axis=1, keepdims=True)
    den = p1 + p2
    g0 = gw * p1 / den
    g1 = gw * p2 / den
    info = jnp.where(lane == 0, i1, jnp.where(lane == 1, i2,
                     jnp.where(lane == 2, g0, jnp.where(lane == 3, g1, 0.0))))
    info_ref[...] = info
    oh = jnp.where(lanef == i1, 1.0, 0.0) + jnp.where(lanef == i2, 1.0, 0.0)
    cnt_ref[...] += jnp.sum(oh, axis=0, keepdims=True)


def _route(lg):
    n = lg.shape[0]
    tm = min(1024, n)
    return pl.pallas_call(
        _route_kernel,
        grid=(n // tm,),
        in_specs=[pl.BlockSpec((tm, LANES), lambda i: (i, 0))],
        out_specs=[pl.BlockSpec((tm, LANES), lambda i: (i, 0)),
                   pl.BlockSpec((1, LANES), lambda i: (0, 0))],
        out_shape=[jax.ShapeDtypeStruct((n, LANES), F32),
                   jax.ShapeDtypeStruct((1, LANES), F32)],
        compiler_params=_cparams(("arbitrary",)),
        name="moe_route",
    )(lg)


def _plan_kernel(cnt_ref, ps_ref, be_ref, nu_ref, *, nblk):
    def fill(i, c):
        be_ref[i] = N_EXPERTS - 1
        return c

    lax.fori_loop(0, nblk, fill, 0)

    def per_expert(e, pos):
        ps_ref[e] = pos * MOE_BLK
        nb = (cnt_ref[e] + MOE_BLK - 1) // MOE_BLK

        def mark(k, c):
            be_ref[pos + k] = e
            return c

        lax.fori_loop(0, nb, mark, 0)
        return pos + nb

    nu_ref[0] = lax.fori_loop(0, N_EXPERTS, per_expert, jnp.int32(0))


def _plan(counts, nblk):
    smem = pl.BlockSpec(memory_space=pltpu.SMEM)
    return pl.pallas_call(
        functools.partial(_plan_kernel, nblk=nblk),
        in_specs=[smem],
        out_specs=[smem, smem, smem],
        out_shape=[jax.ShapeDtypeStruct((N_EXPERTS,), I32),
                   jax.ShapeDtypeStruct((nblk,), I32),
                   jax.ShapeDtypeStruct((1,), I32)],
        name="moe_block_plan",
    )(counts)


def _dest_kernel(info_ref, ps_ref, tri_ref, o_ref, carry_scr):
    @pl.when(pl.program_id(0) == 0)
    def _():
        carry_scr[...] = jnp.zeros(carry_scr.shape, F32)

    info = info_ref[...]
    lane = lax.broadcasted_iota(I32, info.shape, 1)
    lanef = lane.astype(F32)
    o1 = jnp.where(lanef == info[:, 0:1], 1.0, 0.0)
    o2 = jnp.where(lanef == info[:, 1:2], 1.0, 0.0)
    oh = o1 + o2
    before = jnp.dot(tri_ref[...], oh.astype(BF16), preferred_element_type=F32)
    base = before + carry_scr[...] + ps_ref[...]
    d1 = jnp.sum(o1 * base, axis=1, keepdims=True)
    d2 = jnp.sum(o2 * base, axis=1, keepdims=True)
    o_ref[...] = jnp.where(lane == 0, d1, jnp.where(lane == 1, d2, 0.0))
    carry_scr[...] += jnp.sum(oh, axis=0, keepdims=True)


def _dest(info, ps_lanes, tri):
    n = info.shape[0]
    tm = tri.shape[0]
    return pl.pallas_call(
        _dest_kernel,
        grid=(n // tm,),
        in_specs=[pl.BlockSpec((tm, LANES), lambda i: (i, 0)),
                  pl.BlockSpec((1, LANES), lambda i: (0, 0)),
                  pl.BlockSpec((tm, tm), lambda i: (0, 0))],
        out_specs=pl.BlockSpec((tm, LANES), lambda i: (i, 0)),
        out_shape=jax.ShapeDtypeStruct((n, LANES), F32),
        scratch_shapes=[pltpu.VMEM((1, LANES), F32)],
        compiler_params=_cparams(("arbitrary",)),
        name="moe_dest_rows",
    )(info, ps_lanes, tri)


SCATTER_SLOTS = 3


def _scatter_kernel(d0_ref, d1_ref, cnt_ref, ps_ref, nu_ref, h_ref, xb_ref,
                    zero_scr, stage_scr, load_sem, out_sem, zero_sem, *, nblk):
    n_tiles = h_ref.shape[0] // TOK_TILE
    zero_scr[...] = jnp.zeros(zero_scr.shape, zero_scr.dtype)
    pad_sizes = [s for s in (1 << k for k in range(MOE_BLK.bit_length() - 2, -1, -1))
                 if s >= SUBLANES]

    def zero_copy(rows, dst_row):
        return pltpu.make_async_copy(zero_scr.at[pl.ds(0, rows)], xb_ref.at[pl.ds(dst_row, rows)],
                                     zero_sem)

    def load(j, slot):
        return pltpu.make_async_copy(h_ref.at[pl.ds(j * TOK_TILE, TOK_TILE)], stage_scr.at[slot],
                                     load_sem.at[slot])

    def row_copy(slot, r, dst_row, parity, rows=1):
        return pltpu.make_async_copy(stage_scr.at[slot, pl.ds(r, rows)],
                                     xb_ref.at[pl.ds(dst_row, rows)], out_sem.at[parity])

    def for_each_zero_copy(act):
        def per_expert(e, c):
            pos = ps_ref[e] + cnt_ref[e]
            pad = (MOE_BLK - (cnt_ref[e] & (MOE_BLK - 1))) & (MOE_BLK - 1)
            head = pad & (SUBLANES - 1)
            for k in range(SUBLANES - 1):
                @pl.when(k < head)
                def _():
                    act(zero_copy(1, pos + k))
            pos = pl.multiple_of(pos + head, SUBLANES)
            for size in pad_sizes:
                @pl.when((pad & size) != 0)
                def _():
                    act(zero_copy(size, pos))
                pos = pl.multiple_of(pos + (pad & size), SUBLANES)
            return c

        def per_tail_block(i, c):
            act(zero_copy(MOE_BLK, i * MOE_BLK))
            return c

        lax.fori_loop(0, N_EXPERTS, per_expert, 0)
        lax.fori_loop(nu_ref[0], nblk, per_tail_block, 0)

    for_each_zero_copy(lambda cp: cp.start())

    def retire(parity):
        row_copy(0, 0, 0, parity, TOK_TILE).wait()
        row_copy(0, 0, 0, parity, TOK_TILE).wait()

    def tile(j, c):
        slot = j % SCATTER_SLOTS
        parity = j % 2
        load(j, slot).wait()

        @pl.when(j + 1 < n_tiles)
        def _():
            load(j + 1, (j + 1) % SCATTER_SLOTS).start()

        def issue(g, cc):
            for k in range(SUBLANES):
                r = g * SUBLANES + k
                t = j * TOK_TILE + r
                row_copy(slot, r, d0_ref[t], parity).start()
                row_copy(slot, r, d1_ref[t], parity).start()
            return cc

        lax.fori_loop(0, TOK_TILE // SUBLANES, issue, 0)

        @pl.when(j >= 1)
        def _():
            retire(1 - parity)
        return c

    load(0, 0).start()
    lax.fori_loop(0, n_tiles, tile, 0)
    retire((n_tiles - 1) % 2)
    for_each_zero_copy(lambda cp: cp.wait())


def _scatter_rows(d0, d1, counts, ps, nu, h2, p_rows):
    n, d = h2.shape
    smem = pl.BlockSpec(memory_space=pltpu.SMEM)
    return pl.pallas_call(
        functools.partial(_scatter_kernel, nblk=p_rows // MOE_BLK),
        in_specs=[smem, smem, smem, smem, smem, pl.BlockSpec(memory_space=pl.ANY)],
        out_specs=pl.BlockSpec(memory_space=pl.ANY),
        out_shape=jax.ShapeDtypeStruct((p_rows, d), h2.dtype),
        scratch_shapes=[pltpu.VMEM((MOE_BLK, d), h2.dtype),
                        pltpu.VMEM((SCATTER_SLOTS, TOK_TILE, d), h2.dtype),
                        pltpu.SemaphoreType.DMA((SCATTER_SLOTS,)),
                        pltpu.SemaphoreType.DMA((2,)),
                        pltpu.SemaphoreType.DMA(())],
        compiler_params=_cparams(),
        name="moe_scatter_rows",
    )(d0, d1, counts, ps, nu, h2)


def _expert_kernel(be_ref, nu_ref, x_ref, wg_hbm, wu_hbm, wd_hbm, o_ref,
                   wg_f32, wu_f32, wd_f32, wg_scr, wu_scr, wd_scr, slot_ref, sem):
    i = pl.program_id(0)
    n_used = nu_ref[0]
    expert = be_ref[i]
    new_expert = jnp.logical_or(i == 0, expert != be_ref[jnp.maximum(i - 1, 0)])

    def weight_copies(e, slot):
        return [pltpu.make_async_copy(src.at[e], dst.at[slot], sem.at[slot])
                for src, dst in ((wg_hbm, wg_f32), (wu_hbm, wu_f32), (wd_hbm, wd_f32))]

    @pl.when(i == 0)
    def _():
        slot_ref[0] = 0
        for cp in weight_copies(expert, 0):
            cp.start()

    @pl.when(jnp.logical_and(new_expert, i < n_used))
    def _():
        slot = slot_ref[0]
        for cp in weight_copies(expert, slot):
            cp.wait()
        nxt = lax.while_loop(lambda j: jnp.logical_and(j < n_used, be_ref[jnp.minimum(j, n_used - 1)] == expert),
                             lambda j: j + 1, i + 1)

        @pl.when(nxt < n_used)
        def _():
            for cp in weight_copies(be_ref[nxt], 1 - slot):
                cp.start()

        wg_scr[...] = wg_f32[slot].astype(BF16)
        wu_scr[...] = wu_f32[slot].astype(BF16)
        wd_scr[...] = wd_f32[slot].astype(BF16)
        slot_ref[0] = 1 - slot

    @pl.when(i < n_used)
    def _():
        x_hi, x_lo = _unpack_bf16_pairs(x_ref[...])
        x = jnp.concatenate([x_hi.astype(BF16), x_lo.astype(BF16)], axis=1)
        g = jnp.dot(x, wg_scr[...], preferred_element_type=F32)
        u = jnp.dot(x, wu_scr[...], preferred_element_type=F32)
        act = (g * (1.0 / (1.0 + jnp.exp(-g))) * u).astype(BF16)
        o_ref[...] = _pack_bf16_pairs(jnp.dot(act, wd_scr[...], preferred_element_type=F32))

    @pl.when(i >= nu_ref[0])
    def _():
        o_ref[...] = jnp.zeros(o_ref.shape, o_ref.dtype)


def _experts(be, nu, xb, wg, wu, wd):
    p = xb.shape[0]
    _, d, de = wg.shape
    return pl.pallas_call(
        _expert_kernel,
        grid_spec=pltpu.PrefetchScalarGridSpec(
            num_scalar_prefetch=2,
            grid=(p // MOE_BLK,),
            in_specs=[pl.BlockSpec((MOE_BLK, d // 2), lambda i, be, nu: (i, 0)),
                      pl.BlockSpec(memory_space=pl.ANY),
                      pl.BlockSpec(memory_space=pl.ANY),
                      pl.BlockSpec(memory_space=pl.ANY)],
            out_specs=pl.BlockSpec((MOE_BLK, d // 2), lambda i, be, nu: (i, 0)),
            scratch_shapes=[pltpu.VMEM((2, d, de), F32), pltpu.VMEM((2, d, de), F32),
                            pltpu.VMEM((2, de, d), F32),
                            pltpu.VMEM((d, de), BF16), pltpu.VMEM((d, de), BF16),
                            pltpu.VMEM((de, d), BF16),
                            pltpu.SMEM((1,), I32),
                            pltpu.SemaphoreType.DMA((2,))]),
        out_shape=jax.ShapeDtypeStruct((p, d // 2), I32),
        compiler_params=_cparams(("arbitrary",)),
        name="moe_expert_ffn",
    )(be, nu, xb, wg, wu, wd)


def _combine_kernel(d0_ref, d1_ref, x_ref, info_ref, g2_ref, yb_ref, o_ref, rows_scr, sem):
    i = pl.program_id(0)
    cur = i % 2

    def copy(buf, which, r, src_row, rows=1):
        return pltpu.make_async_copy(yb_ref.at[pl.ds(src_row, rows)],
                                     rows_scr.at[buf, which, pl.ds(r, rows)], sem.at[buf])

    def start_gather(step, buf):
        base = step * TOK_TILE

        def issue(g, c):
            for k in range(SUBLANES):
                r = g * SUBLANES + k
                copy(buf, 0, r, d0_ref[base + r]).start()
                copy(buf, 1, r, d1_ref[base + r]).start()
            return c

        lax.fori_loop(0, TOK_TILE // SUBLANES, issue, 0)

    @pl.when(i == 0)
    def _():
        start_gather(0, 0)

    @pl.when(i + 1 < pl.num_programs(0))
    def _():
        start_gather(i + 1, 1 - cur)

    copy(cur, 0, 0, 0, TOK_TILE).wait()
    copy(cur, 1, 0, 0, TOK_TILE).wait()

    info = info_ref[...]
    half = o_ref.shape[1] // 2
    y0_hi, y0_lo = _unpack_bf16_pairs(rows_scr[cur, 0])
    y1_hi, y1_lo = _unpack_bf16_pairs(rows_scr[cur, 1])
    g2 = g2_ref[0]
    o_ref[:, :half] = x_ref[:, :half] + g2[:, :half] * (info[:, 2:3] * y0_hi + info[:, 3:4] * y1_hi)
    o_ref[:, half:] = x_ref[:, half:] + g2[:, half:] * (info[:, 2:3] * y0_lo + info[:, 3:4] * y1_lo)


def _combine(d0, d1, x1, info, g2, yb, seq):
    n, d = x1.shape
    tpb = seq // TOK_TILE
    return pl.pallas_call(
        _combine_kernel,
        grid_spec=pltpu.PrefetchScalarGridSpec(
            num_scalar_prefetch=2,
            grid=(n // TOK_TILE,),
            in_specs=[pl.BlockSpec((TOK_TILE, d), lambda i, a, b: (i, 0)),
                      pl.BlockSpec((TOK_TILE, LANES), lambda i, a, b: (i, 0)),
                      pl.BlockSpec((1, 1, d), lambda i, a, b: (i // tpb, 0, 0)),
                      pl.BlockSpec(memory_space=pl.ANY)],
            out_specs=pl.BlockSpec((TOK_TILE, d), lambda i, a, b: (i, 0)),
            scratch_shapes=[pltpu.VMEM((2, 2, TOK_TILE, d // 2), I32),
                            pltpu.SemaphoreType.DMA((2,))]),
        out_shape=jax.ShapeDtypeStruct((n, d), F32),
        compiler_params=_cparams(("arbitrary",)),
        name="moe_combine",
    )(d0, d1, x1, info, g2, yb)


def _tri_inclusive_rev(k):
    l = (np.arange(k)[:, None] >= np.arange(k)[None, :]).astype(np.float32)
    half = np.concatenate([l, np.ones((k, k), np.float32)], axis=1)
    return jnp.asarray(np.concatenate([half, half], axis=0), dtype=BF16)


def _tri_strict_lower(k):
    return jnp.asarray((np.arange(k)[None, :] < np.arange(k)[:, None]).astype(np.float32), dtype=BF16)


def kernel(x, c, w_mod, b_mod, ln1_g, w_in, q_norm_g, w_q_up, q_gain, k_gain, rel_bias, gn_a, gn_b,
           w_out, ln2_g, router_g, router_e, w_gate, w_up, w_down):
    batch, seq, d = x.shape
    n = batch * seq
    assert w_mod.shape[0] == 1 and d == D_A + D_B
    assert seq % KB_A == 0 and seq % TOK_TILE == 0 and n % 512 == 0
    x2 = x.reshape(n, d)

    c8 = jnp.pad(c, ((0, 8 - batch), (0, 0)))
    mod = _modulation(c8, w_mod.reshape(d, -1), b_mod.reshape(1, -1))[:batch]
    sh1, sc1, g1, sh2, sc2, g2 = [m.reshape(batch, 1, d) for m in jnp.split(mod, 6, axis=-1)]

    wi = w_in.reshape(d, -1)
    n_a = Q_RANK + 2 * HEAD_DIM + IDX_DIM + IDX_HEADS
    wi = jnp.concatenate([wi[:, n_a:], wi[:, :n_a],
                          jnp.zeros((d, D_IN_PAD - wi.shape[1]), wi.dtype)], axis=1).astype(BF16)
    proj = _ln_proj(x2, ln1_g.reshape(1, d), sc1, sh1, wi, seq)

    qa, qidx, ka, va, kidx, widx = _aprep(
        proj, q_norm_g.reshape(1, -1), w_q_up.reshape(Q_RANK, -1).astype(BF16),
        q_gain.reshape(1, -1), k_gain.reshape(1, -1))

    bt = _bias_tiles(rel_bias)
    bound = _logit_bound(q_gain.reshape(1, -1), k_gain.reshape(1, -1), rel_bias)[0, :1]
    oa = _attn_a(qidx, widx.T, qa, kidx, ka, va, bt, gn_a.reshape(1, -1), bound, batch, seq)
    ob = _attn_b(proj, _tri_inclusive_rev(QB_B), batch, seq)

    rw = jnp.concatenate([router_e.reshape(d, -1), router_g.reshape(d, -1),
                          jnp.zeros((d, LANES - N_EXPERTS - N_GROUPS), F32)], axis=1)
    x1, h2, lg = _out_proj(x2, oa, ob, gn_b.reshape(1, -1), w_out.reshape(d, d).astype(BF16),
                           g1, ln2_g.reshape(1, d), sc2, sh2, rw, seq)

    info, cnt = _route(lg)
    counts = cnt[0, :N_EXPERTS].astype(I32)
    p_rows = 2 * n + N_EXPERTS * MOE_BLK
    ps, be, nu = _plan(counts, p_rows // MOE_BLK)
    ps_lanes = jnp.pad(ps.astype(F32), (0, LANES - N_EXPERTS)).reshape(1, LANES)
    dinfo = _dest(info, ps_lanes, _tri_strict_lower(512))
    d0 = dinfo[:, 0].astype(I32)
    d1 = dinfo[:, 1].astype(I32)
    xb = _scatter_rows(d0, d1, counts, ps, nu, h2, p_rows)
    yb = _experts(be, nu, xb,
                  w_gate.reshape(N_EXPERTS, d, D_EXPERT),
                  w_up.reshape(N_EXPERTS, d, D_EXPERT),
                  w_down.reshape(N_EXPERTS, D_EXPERT, d))
    out = _combine(d0, d1, x1, info, g2, yb, seq)
    return out.reshape(batch, seq, d)
```

```python
import functools
import math

import numpy as np
import jax
import jax.numpy as jnp
from jax import lax
from jax.experimental import pallas as pl
from jax.experimental.pallas import tpu as pltpu

F32 = jnp.float32
BF16 = jnp.bfloat16
I32 = jnp.int32

HEAD_DIM = 128
N_HEADS_A = 8
N_HEADS_B = 8
D_A = N_HEADS_A * HEAD_DIM
D_B = N_HEADS_B * HEAD_DIM
Q_RANK = 512
IDX_HEADS = 16
IDX_DIM = 64
TOPK_MAX = 256
N_BUCKETS = 32
MAX_DISTANCE = 128
N_GROUPS = 4
EXPERTS_PER_GROUP = 8
N_EXPERTS = N_GROUPS * EXPERTS_PER_GROUP
D_EXPERT = 512
EPS = 1e-6

LANES = 128
SUBLANES = 8
VMEM_LIMIT = 56 * 1024 * 1024
NEG = -1e30
INT_MIN = -(2 ** 31)
EXP2_UNDERFLOW = 150.0
LOG2E = math.log2(math.e)
PLAIN_SOFTMAX_LIMIT = 64.0
GUESS_HALF_WIDTH = 0.1
EXTRACT_BELOW = 3.0

QB_A = 128
KB_A = 256
QB_B = 128
MOE_BLK = 256
TOK_TILE = 256

COL_QB, COL_KB, COL_VB, COL_A = 0, D_B, 2 * D_B, 3 * D_B
A_CQ, A_KA, A_VA, A_KIDX, A_WIDX = 0, 512, 640, 768, 832
A_WIDTH = 1024
D_IN_PAD = COL_A + A_WIDTH


def _cparams(sem=None):
    return pltpu.CompilerParams(dimension_semantics=sem, vmem_limit_bytes=VMEM_LIMIT)


def _rms(x):
    return x * lax.rsqrt(jnp.mean(x * x, axis=-1, keepdims=True) + EPS)


def _pack_bf16_pairs(x):
    c = x.shape[1] // 2
    hi = pltpu.bitcast(x[:, :c].astype(BF16).astype(F32), I32)
    lo = pltpu.bitcast(x[:, c:].astype(BF16).astype(F32), I32)
    return hi | lax.shift_right_logical(lo, 16)


def _unpack_bf16_pairs(u):
    hi = pltpu.bitcast(u & jnp.int32(-65536), F32)
    lo = pltpu.bitcast(lax.shift_left(u, 16), F32)
    return hi, lo


def _mod_kernel(c_ref, w_ref, b_ref, o_ref):
    c = c_ref[...]
    s = c * (1.0 / (1.0 + jnp.exp(-c)))
    o_ref[...] = jnp.dot(s, w_ref[...], preferred_element_type=F32,
                         precision=lax.Precision.HIGHEST) + b_ref[...]


def _modulation(c8, w_mod, b_mod):
    d, n6 = w_mod.shape
    tn = 1024
    return pl.pallas_call(
        _mod_kernel,
        grid=(n6 // tn,),
        in_specs=[pl.BlockSpec((8, d), lambda j: (0, 0)),
                  pl.BlockSpec((d, tn), lambda j: (0, j)),
                  pl.BlockSpec((1, tn), lambda j: (0, j))],
        out_specs=pl.BlockSpec((8, tn), lambda j: (0, j)),
        out_shape=jax.ShapeDtypeStruct((8, n6), F32),
        compiler_params=_cparams(("arbitrary",)),
        name="modulation",
    )(c8, w_mod, b_mod)


def _ln_proj_kernel(x_ref, g_ref, sc_ref, sh_ref, w_ref, o_ref, h_scr, *, q_tiles, chunk):
    i = pl.program_id(0)
    j = pl.program_id(1)

    def normalise_chunk():
        rows = pl.ds(pl.multiple_of(j * chunk, chunk), chunk)
        h = _rms(x_ref[rows, :]) * g_ref[...]
        h = h * (1.0 + sc_ref[0]) + sh_ref[0]
        h_scr[i % 2, rows, :] = h.astype(BF16)

    @pl.when(i == 0)
    def _():
        normalise_chunk()
        o_ref[...] = jnp.zeros(o_ref.shape, o_ref.dtype)

    @pl.when(i > 0)
    def _():
        normalise_chunk()
        col_scale = jnp.where(j < q_tiles, HEAD_DIM ** -0.5 * LOG2E, 1.0)
        acc = jnp.dot(h_scr[(i - 1) % 2], w_ref[...], preferred_element_type=F32)
        o_ref[...] = (acc * col_scale).astype(o_ref.dtype)


def _ln_proj(x2, ln_g, sc, sh, w, seq):
    n, d = x2.shape
    ncol = w.shape[1]
    tm = min(1024, seq)
    tn = 1024
    tpb = seq // tm
    n_i, n_j = n // tm, ncol // tn
    assert COL_QB == 0 and D_B % tn == 0 and tm % (n_j * 16) == 0
    last = n_i - 1
    return pl.pallas_call(
        functools.partial(_ln_proj_kernel, q_tiles=D_B // tn, chunk=tm // n_j),
        grid=(n_i + 1, n_j),
        in_specs=[pl.BlockSpec((tm, d), lambda i, j: (jnp.minimum(i, last), 0)),
                  pl.BlockSpec((1, d), lambda i, j: (0, 0)),
                  pl.BlockSpec((1, 1, d), lambda i, j: (jnp.minimum(i, last) // tpb, 0, 0)),
                  pl.BlockSpec((1, 1, d), lambda i, j: (jnp.minimum(i, last) // tpb, 0, 0)),
                  pl.BlockSpec((d, tn), lambda i, j: (0, j))],
        out_specs=pl.BlockSpec((tm, tn), lambda i, j: (jnp.where(i == 0, n_i, i - 1), j)),
        out_shape=jax.ShapeDtypeStruct((n + tm, ncol), BF16),
        scratch_shapes=[pltpu.VMEM((2, tm, d), BF16)],
        compiler_params=_cparams(("arbitrary", "arbitrary")),
        name="ln_in_proj",
    )(x2, ln_g, sc, sh, w)


def _aprep_kernel(a_ref, qng_ref, wq_ref, qg_ref, kg_ref,
                  qa_ref, qidx_ref, ka_ref, va_ref, kidx_ref, widx_ref):
    cq = a_ref[:, A_CQ:A_CQ + Q_RANK].astype(F32)
    cqn = (_rms(cq) * qng_ref[...]).astype(BF16)
    qup = jnp.dot(cqn, wq_ref[...], preferred_element_type=F32)
    for h in range(N_HEADS_A):
        qh = qup[:, h * HEAD_DIM:(h + 1) * HEAD_DIM]
        qn = _rms(qh) * qg_ref[...] * (HEAD_DIM ** -0.5 * LOG2E)
        qa_ref[:, h * HEAD_DIM:(h + 1) * HEAD_DIM] = qn.astype(BF16)
    for h in range(IDX_HEADS):
        qi = qup[:, D_A + h * IDX_DIM:D_A + (h + 1) * IDX_DIM] * (IDX_DIM ** -0.5)
        qidx_ref[h] = qi.astype(BF16)
    ka = a_ref[:, A_KA:A_KA + HEAD_DIM].astype(F32)
    ka_ref[...] = (_rms(ka) * kg_ref[...]).astype(BF16)
    va_ref[...] = a_ref[:, A_VA:A_VA + HEAD_DIM]
    kidx_ref[...] = a_ref[:, A_KIDX:A_KIDX + IDX_DIM]
    widx_ref[...] = a_ref[:, A_WIDX:A_WIDX + IDX_HEADS].astype(F32) * (IDX_HEADS ** -0.5)


def _aprep(proj, n, q_norm_g, w_q_up, q_gain, k_gain):
    tm = 512
    cblk = COL_A // A_WIDTH
    nup = w_q_up.shape[1]
    return pl.pallas_call(
        _aprep_kernel,
        grid=(n // tm,),
        in_specs=[pl.BlockSpec((tm, A_WIDTH), lambda i: (i, cblk)),
                  pl.BlockSpec((1, Q_RANK), lambda i: (0, 0)),
                  pl.BlockSpec((Q_RANK, nup), lambda i: (0, 0)),
                  pl.BlockSpec((1, HEAD_DIM), lambda i: (0, 0)),
                  pl.BlockSpec((1, HEAD_DIM), lambda i: (0, 0))],
        out_specs=[pl.BlockSpec((tm, D_A), lambda i: (i, 0)),
                   pl.BlockSpec((IDX_HEADS, tm, IDX_DIM), lambda i: (0, i, 0)),
                   pl.BlockSpec((tm, HEAD_DIM), lambda i: (i, 0)),
                   pl.BlockSpec((tm, HEAD_DIM), lambda i: (i, 0)),
                   pl.BlockSpec((tm, IDX_DIM), lambda i: (i, 0)),
                   pl.BlockSpec((tm, IDX_HEADS), lambda i: (i, 0))],
        out_shape=[jax.ShapeDtypeStruct((n, D_A), BF16),
                   jax.ShapeDtypeStruct((IDX_HEADS, n, IDX_DIM), BF16),
                   jax.ShapeDtypeStruct((n, HEAD_DIM), BF16),
                   jax.ShapeDtypeStruct((n, HEAD_DIM), BF16),
                   jax.ShapeDtypeStruct((n, IDX_DIM), BF16),
                   jax.ShapeDtypeStruct((n, IDX_HEADS), F32)],
        compiler_params=_cparams(("parallel",)),
        name="group_a_prep",
    )(proj, q_norm_g, w_q_up, q_gain, k_gain)


def _t5_bucket_starts():
    max_exact = N_BUCKETS // 2
    d = np.arange(0, 4 * MAX_DISTANCE, dtype=np.int64)
    df = np.maximum(d, 1).astype(np.float32)
    large = max_exact + (np.log(df / np.float32(max_exact)) / np.float32(math.log(MAX_DISTANCE / max_exact))
                         * np.float32(N_BUCKETS - max_exact)).astype(np.int32)
    large = np.minimum(large, N_BUCKETS - 1)
    bucket = np.where(d < max_exact, d, large)
    assert np.all(np.diff(bucket) >= 0) and bucket[-1] == N_BUCKETS - 1
    return [int(np.argmax(bucket >= b)) for b in range(N_BUCKETS)]


_BUCKET_START = _t5_bucket_starts()
N_BIAS_TILES = 2 * KB_A // LANES


def _bias_kernel(rb_ref, o_ref):
    di = pl.program_id(0)
    h = pl.program_id(1)
    i = lax.broadcasted_iota(I32, (QB_A, KB_A), 0)
    j = lax.broadcasted_iota(I32, (QB_A, KB_A), 1)
    d = di * LANES + i - j
    val = jnp.full((QB_A, KB_A), rb_ref[0, h], F32)
    for b in range(1, N_BUCKETS):
        val = jnp.where(d >= _BUCKET_START[b], rb_ref[b, h], val)
    o_ref[0, 0] = (val - rb_ref[N_BUCKETS - 1, h]) * LOG2E


def _bias_tiles(rel_bias):
    return pl.pallas_call(
        _bias_kernel,
        grid=(N_BIAS_TILES, N_HEADS_A),
        in_specs=[pl.BlockSpec(memory_space=pltpu.SMEM)],
        out_specs=pl.BlockSpec((1, 1, QB_A, KB_A), lambda a, h: (a, h, 0, 0)),
        out_shape=jax.ShapeDtypeStruct((N_BIAS_TILES, N_HEADS_A, QB_A, KB_A), F32),
        compiler_params=_cparams(("arbitrary", "arbitrary")),
        name="t5_bias_tiles",
    )(rel_bias)


def _bound_kernel(qg_ref, kg_ref, rb_ref, o_ref):
    qmax = jnp.max(jnp.abs(qg_ref[...]), axis=1, keepdims=True)
    kmax = jnp.max(jnp.abs(kg_ref[...]), axis=1, keepdims=True)
    rb = rb_ref[...]
    shifted = jnp.abs(rb - rb[N_BUCKETS - 1:N_BUCKETS, :])
    bmax = jnp.max(jnp.max(shifted, axis=1, keepdims=True), axis=0, keepdims=True)
    bound = qmax * kmax * (math.sqrt(HEAD_DIM) * 1.02) + bmax
    o_ref[...] = jnp.broadcast_to(bound, o_ref.shape)


def _logit_bound(q_gain, k_gain, rel_bias):
    return pl.pallas_call(
        _bound_kernel,
        out_shape=jax.ShapeDtypeStruct((1, LANES), F32),
        name="dsa_logit_bound",
    )(q_gain, k_gain, rel_bias)


def _attn_a_kernel(qidx_ref, w_ref, qa_ref, kidx_ref, ka_ref, va_ref, bt_ref, gn_ref, bound_ref, o_ref,
                   keys_scr, keyt_scr, m_scr, acc_scr, s_scr, mb_scr, *, topk):
    qb = pl.program_id(1)
    t0 = qb * QB_A
    kbl = (t0 + QB_A - 1) // KB_A
    row = t0 + lax.broadcasted_iota(I32, (QB_A, KB_A), 0)
    col0 = lax.broadcasted_iota(I32, (QB_A, KB_A), 1)
    nt = (((1,), (1,)), ((), ()))

    reps = KB_A // LANES

    def sort_key(v):
        bits = pltpu.bitcast(v, I32)
        return jnp.where(bits < 0, bits ^ jnp.int32(0x7FFFFFFF), bits)

    def key_value(k):
        return pltpu.bitcast(jnp.where(k < 0, k ^ jnp.int32(0x7FFFFFFF), k), F32)

    key_pos = lax.broadcasted_iota(I32, (KB_A, QB_A), 0)
    qry_pos = t0 + lax.broadcasted_iota(I32, (KB_A, QB_A), 1)

    def score_tile(kb, carry):
        smin, smax, s1, s2 = carry
        kt = kidx_ref[pl.ds(pl.multiple_of(kb * KB_A, KB_A), KB_A), :]
        score = jnp.zeros((KB_A, QB_A), F32)
        for hp in range(IDX_HEADS // 2):
            q2 = qidx_ref[2 * hp:2 * hp + 2].reshape(2 * QB_A, IDX_DIM)
            sc = lax.dot_general(kt, q2, nt, preferred_element_type=F32)
            score = score + w_ref[2 * hp:2 * hp + 1, :] * jnp.maximum(sc[:, :QB_A], 0.0)
            score = score + w_ref[2 * hp + 1:2 * hp + 2, :] * jnp.maximum(sc[:, QB_A:], 0.0)
        causal = (kb * KB_A + key_pos) <= qry_pos
        key_t = jnp.where(causal, sort_key(score), jnp.int32(INT_MIN))
        keyt_scr[kb] = key_t
        keys_scr[kb] = key_t.T
        smin = jnp.minimum(smin, jnp.min(jnp.where(causal, score, jnp.inf), axis=0, keepdims=True))
        smax = jnp.maximum(smax, jnp.max(jnp.where(causal, score, -jnp.inf), axis=0, keepdims=True))
        live = jnp.where(causal, score, 0.0)
        s1 = s1 + jnp.sum(live, axis=0, keepdims=True)
        s2 = s2 + jnp.sum(live * live, axis=0, keepdims=True)
        return smin, smax, s1, s2

    def score_pair(i, carry):
        return score_tile(2 * i + 1, score_tile(2 * i, carry))

    n_tiles = kbl + 1
    zero_row = jnp.zeros((1, QB_A), F32)
    stats = lax.fori_loop(0, n_tiles // 2, score_pair,
                          (jnp.full((1, QB_A), jnp.inf, F32), jnp.full((1, QB_A), -jnp.inf, F32),
                           zero_row, zero_row))
    smin, smax, s1, s2 = lax.cond(n_tiles % 2 == 1, lambda c: score_tile(n_tiles - 1, c),
                                  lambda c: c, stats)

    qry1 = t0 + lax.broadcasted_iota(I32, (1, QB_A), 1)
    kf = float(topk)
    acc_rows = 32
    n_causal = (qry1 + 1).astype(F32)

    mean = s1 / n_causal
    dev = jnp.sqrt(jnp.maximum(s2 / n_causal - mean * mean, 0.0))
    frac_top = jnp.minimum(kf / n_causal, 1.0)
    tail = jnp.minimum(frac_top, 1.0 - frac_top)
    tq = jnp.sqrt(-2.0 * jnp.log(jnp.maximum(tail, 1e-6)))
    zq = tq - ((0.010328 * tq + 0.802853) * tq + 2.515517) / (((0.001308 * tq + 0.189269) * tq + 1.432788) * tq + 1.0)
    zq = jnp.where(frac_top > 0.5, -zq, zq)

    def for_each_key_tile(fold, init):
        return lax.fori_loop(0, kbl + 1, lambda kb, a: fold(a, keyt_scr[kb]), init)

    def search_pass(p, state):
        lo, hi, clo, chi, open_q = state
        lo_v = key_value(lo)
        hi_v = key_value(hi)
        gap = clo - chi
        frac = (clo - (kf - 0.5)) / gap
        frac = jnp.where(p % 2 == 1, 0.7 * frac + 0.15, frac)
        frac = jnp.where(gap > 16.0, frac, 0.5)
        value = lo_v + (hi_v - lo_v) * frac
        value = jnp.where(p == 0, mean + (zq - GUESS_HALF_WIDTH) * dev, value)
        value = jnp.where(p == 1, mean + (zq + GUESS_HALF_WIDTH) * dev, value)
        cand = sort_key(value)
        cand = jnp.where(p % 8 == 7, lo + lax.shift_right_logical(hi - lo, 1), cand)
        cand = jnp.minimum(jnp.maximum(cand, lo + 1), hi - 1)

        def count(cnt, keys):
            ge = jnp.where(keys >= cand, 1.0, 0.0)
            return cnt + jnp.sum(ge.reshape(KB_A // acc_rows, acc_rows, QB_A), axis=0)

        tot = jnp.sum(for_each_key_tile(count, jnp.zeros((acc_rows, QB_A), F32)), axis=0, keepdims=True)
        ge = tot >= kf
        lo = jnp.where(ge, cand, lo)
        clo = jnp.where(ge, tot, clo)
        hi = jnp.where(ge, hi, cand)
        chi = jnp.where(ge, chi, tot)
        width = hi - lo
        settled = jnp.where(clo == kf, 1.0, jnp.where(width == 1, 1.0, 0.0))
        open_q = jnp.where(settled > 0.5, 0.0, open_q)
        return lo, hi, clo, chi, open_q

    def extract_pass(state):
        lo, hi, clo, chi, open_q = state

        def top_below(best, keys):
            below = jnp.where(keys < hi, keys, jnp.int32(INT_MIN))
            return jnp.maximum(best, jnp.max(below.reshape(KB_A // acc_rows, acc_rows, QB_A), axis=0))

        best = for_each_key_tile(top_below, jnp.full((acc_rows, QB_A), INT_MIN, I32))
        best = jnp.max(best, axis=0, keepdims=True)
        is_open = open_q > 0.5
        last = jnp.logical_and(is_open, kf - chi <= 1.0)
        more = jnp.logical_and(is_open, kf - chi > 1.0)
        lo = jnp.where(last, best, lo)
        clo = jnp.where(last, chi + 1.0, clo)
        hi = jnp.where(more, best, hi)
        chi = jnp.where(more, chi + 1.0, chi)
        open_q = jnp.where(last, 0.0, open_q)
        return lo, hi, clo, chi, open_q

    def missing(state):
        return jnp.max(jnp.where(state[4] > 0.5, kf - state[3], 0.0))

    def search_step(carry):
        p, state, lacking = carry
        state = lax.cond(lacking <= EXTRACT_BELOW,
                         lambda st: extract_pass(st),
                         lambda st: search_pass(p + 1, search_pass(p, st)), state)
        return p + 2, state, missing(state)

    open0 = jnp.where(qry1 >= topk, 1.0, 0.0)
    lo0 = sort_key(smin)
    hi0 = sort_key(smax) + 1
    open0 = jnp.where(hi0 - lo0 == 1, 0.0, open0)
    state0 = (lo0, hi0, n_causal, zero_row, open0)
    _, (lo, _, _, _, _), _ = lax.while_loop(
        lambda c: c[2] > 0.0, search_step, (jnp.int32(0), state0, missing(state0)))
    thr = jnp.where(qry1 >= topk, lo, jnp.int32(INT_MIN))
    thr = jnp.broadcast_to(thr, (QB_A, QB_A)).T
    thrb = jnp.tile(thr, (1, reps))

    m_scr[...] = jnp.full(m_scr.shape, NEG, F32)
    acc_scr[...] = jnp.zeros(acc_scr.shape, F32)

    plain = bound_ref[0] <= PLAIN_SOFTMAX_LIMIT

    def attend(kb, near, online):
        mb = jnp.where(keys_scr[kb] >= thrb, 0.0, NEG)
        if near:
            mb = jnp.where((kb * KB_A + col0) <= row, mb, NEG)
            di = (t0 - kb * KB_A) // LANES
        mb_scr[...] = mb
        start = pl.multiple_of(kb * KB_A, KB_A)
        kt = ka_ref[pl.ds(start, KB_A), :]
        vt = jnp.concatenate([va_ref[pl.ds(start, KB_A), :], jnp.ones((KB_A, LANES), BF16)], axis=1)
        for h in range(N_HEADS_A):
            q = qa_ref[:, h * HEAD_DIM:(h + 1) * HEAD_DIM]
            s_scr[h] = lax.dot_general(q, kt, nt, preferred_element_type=F32)
        for h in range(N_HEADS_A):
            s = s_scr[h] + mb_scr[...]
            if near:
                s = s + bt_ref[di, h]
            if online:
                m_prev = m_scr[h]
                m_new = jnp.maximum(m_prev, jnp.max(s, axis=1, keepdims=True))
                alpha = jnp.exp2(m_prev - m_new)
                p = jnp.exp2(s - jnp.tile(m_new, (1, reps)))
                acc_scr[h] = (jnp.tile(alpha, (1, 2)) * acc_scr[h]
                              + jnp.dot(p.astype(BF16), vt, preferred_element_type=F32))
                m_scr[h] = m_new
            else:
                acc_scr[h] += jnp.dot(jnp.exp2(s).astype(BF16), vt, preferred_element_type=F32)

    def attend_all(online):
        n_far = jnp.maximum(kbl - 1, 0)

        def far_pair(i, carry):
            attend(2 * i, False, online)
            attend(2 * i + 1, False, online)
            return carry

        lax.fori_loop(0, n_far // 2, far_pair, 0)

        @pl.when(n_far % 2 == 1)
        def _():
            attend(n_far - 1, False, online)

        @pl.when(kbl >= 1)
        def _():
            attend(kbl - 1, True, online)

        attend(kbl, True, online)

    @pl.when(plain)
    def _():
        attend_all(False)

    @pl.when(jnp.logical_not(plain))
    def _():
        attend_all(True)

    ssq = jnp.zeros((QB_A, LANES), F32)
    for h in range(N_HEADS_A):
        oh = acc_scr[h, :, :HEAD_DIM] / acc_scr[h, :, HEAD_DIM:]
        acc_scr[h, :, :HEAD_DIM] = oh
        ssq = ssq + jnp.sum(oh * oh, axis=1, keepdims=True)
    inv = lax.rsqrt(ssq * (1.0 / D_A) + EPS)
    for h in range(N_HEADS_A):
        sl = slice(h * HEAD_DIM, (h + 1) * HEAD_DIM)
        o_ref[:, sl] = (acc_scr[h, :, :HEAD_DIM] * inv * gn_ref[:, sl]).astype(o_ref.dtype)


def _attn_a(qidx, widx, qa, kidx, ka, va, bt, gn_a, bound, batch, seq):
    n = qa.shape[0]
    nq = seq // QB_A
    nkt = seq // KB_A
    topk = min(TOPK_MAX, seq // 4)
    return pl.pallas_call(
        functools.partial(_attn_a_kernel, topk=topk),
        grid=(batch, nq),
        in_specs=[pl.BlockSpec((IDX_HEADS, QB_A, IDX_DIM), lambda b, q: (0, b * nq + q, 0)),
                  pl.BlockSpec((IDX_HEADS, QB_A), lambda b, q: (0, b * nq + q)),
                  pl.BlockSpec((QB_A, D_A), lambda b, q: (b * nq + q, 0)),
                  pl.BlockSpec((seq, IDX_DIM), lambda b, q: (b, 0)),
                  pl.BlockSpec((seq, HEAD_DIM), lambda b, q: (b, 0)),
                  pl.BlockSpec((seq, HEAD_DIM), lambda b, q: (b, 0)),
                  pl.BlockSpec(bt.shape, lambda b, q: (0, 0, 0, 0)),
                  pl.BlockSpec((1, D_A), lambda b, q: (0, 0)),
                  pl.BlockSpec(memory_space=pltpu.SMEM)],
        out_specs=pl.BlockSpec((QB_A, D_A), lambda b, q: (b * nq + q, 0)),
        out_shape=jax.ShapeDtypeStruct((n, D_A), BF16),
        scratch_shapes=[pltpu.VMEM((nkt, QB_A, KB_A), I32),
                        pltpu.VMEM((nkt, KB_A, QB_A), I32),
                        pltpu.VMEM((N_HEADS_A, QB_A, LANES), F32),
                        pltpu.VMEM((N_HEADS_A, QB_A, 2 * HEAD_DIM), F32),
                        pltpu.VMEM((N_HEADS_A, QB_A, KB_A), F32),
                        pltpu.VMEM((QB_A, KB_A), F32)],
        compiler_params=_cparams(("parallel", "arbitrary")),
        name="dsa_attention",
    )(qidx, widx, qa, kidx, ka, va, bt, gn_a, bound)


def _attn_b_kernel(q_ref, k_ref, v_ref, tri_ref, o_ref, rest_scr, z_scr, cs_scr):
    qb = pl.program_id(1)
    row = lax.broadcasted_iota(I32, (QB_B, QB_B), 0)
    col = lax.broadcasted_iota(I32, (QB_B, QB_B), 1)
    strict = col < row
    nt = (((1,), (1,)), ((), ()))
    def step(kb, diag):
        start = pl.multiple_of(kb * QB_B, QB_B)
        heads = [slice(h * HEAD_DIM, (h + 1) * HEAD_DIM) for h in range(N_HEADS_B)]
        for h, sl in enumerate(heads):
            kt = k_ref[pl.ds(start, QB_B), sl]
            z_scr[h] = lax.dot_general(q_ref[:, sl], kt, nt, preferred_element_type=F32)
        for h, sl in enumerate(heads):
            z = z_scr[h]
            sp = jnp.maximum(z, 0.0) + jnp.log(1.0 + jnp.exp2(-jnp.abs(z))) * LOG2E
            if diag:
                sp = jnp.where(strict, sp, 0.0)
            hi = sp.astype(BF16)
            lo = (sp - hi.astype(F32)).astype(BF16)
            cs_scr[h] = jnp.dot(jnp.concatenate([hi, lo], axis=1), tri_ref[...],
                                preferred_element_type=F32)
        worst = None
        for h, sl in enumerate(heads):
            vt = v_ref[pl.ds(start, QB_B), sl]
            z = z_scr[h]
            cs = cs_scr[h, :, :QB_B]
            tot = cs_scr[h, :, QB_B:]
            if diag:
                a = jnp.where(strict, jnp.exp2(z - cs), 0.0)
                o_ref[:, sl] = jnp.dot(a.astype(BF16), vt, preferred_element_type=F32)
                rest = tot
            else:
                rest = rest_scr[h]
                a = jnp.exp2(z - cs - rest)
                o_ref[:, sl] += jnp.dot(a.astype(BF16), vt, preferred_element_type=F32)
                rest = rest + tot
            rest_scr[h] = rest
            worst = rest if worst is None else jnp.minimum(worst, rest)
        return jnp.min(worst)

    def more(kb, smallest):
        return jnp.logical_and(kb >= 0, smallest < EXP2_UNDERFLOW)

    def body(carry):
        kb, _ = carry
        return kb - 1, more(kb - 1, step(kb, False))

    lax.while_loop(lambda c: c[1], body, (qb - 1, more(qb - 1, step(qb, True))))


def _attn_b(proj, tri, batch, seq):
    n = batch * seq
    nq = seq // QB_B
    return pl.pallas_call(
        _attn_b_kernel,
        grid=(batch, nq),
        in_specs=[pl.BlockSpec((QB_B, D_B), lambda b, q: (b * nq + q, COL_QB // D_B)),
                  pl.BlockSpec((seq, D_B), lambda b, q: (b, COL_KB // D_B)),
                  pl.BlockSpec((seq, D_B), lambda b, q: (b, COL_VB // D_B)),
                  pl.BlockSpec(tri.shape, lambda b, q: (0, 0))],
        out_specs=pl.BlockSpec((QB_B, D_B), lambda b, q: (b * nq + q, 0)),
        out_shape=jax.ShapeDtypeStruct((n, D_B), F32),
        scratch_shapes=[pltpu.VMEM((N_HEADS_B, QB_B, QB_B), F32),
                        pltpu.VMEM((N_HEADS_B, QB_B, QB_B), F32),
                        pltpu.VMEM((N_HEADS_B, QB_B, 2 * QB_B), F32)],
        compiler_params=_cparams(("parallel", "arbitrary")),
        name="stick_breaking_attention",
    )(proj, proj, proj, tri)


def _out_proj_kernel(x_ref, oa_ref, ob_ref, gnb_ref, w_ref, g1_ref, ln_ref, sc_ref, sh_ref, rw_ref,
                     x1_ref, h2_ref, lg_ref):
    obn = (_rms(ob_ref[...]) * gnb_ref[...]).astype(BF16)
    y = jnp.dot(oa_ref[...], w_ref[0:D_A, :], preferred_element_type=F32)
    y = y + jnp.dot(obn, w_ref[D_A:D_A + D_B, :], preferred_element_type=F32)
    x1 = x_ref[...] + g1_ref[0] * y
    x1_ref[...] = x1
    h2 = _rms(x1) * ln_ref[...]
    h2 = h2 * (1.0 + sc_ref[0]) + sh_ref[0]
    h2_ref[...] = _pack_bf16_pairs(h2)
    hh = h2.astype(BF16)
    hl = (h2 - hh.astype(F32)).astype(BF16)
    rw = rw_ref[...]
    rh = rw.astype(BF16)
    rl = (rw - rh.astype(F32)).astype(BF16)
    both = jnp.dot(hh, jnp.concatenate([rh, rl], axis=1), preferred_element_type=F32)
    lg_ref[...] = both[:, :LANES] + both[:, LANES:] + jnp.dot(hl, rh, preferred_element_type=F32)


def _out_proj(x2, oa, ob, gn_b, w_out, g1, ln_g, sc, sh, rw, seq):
    n, d = x2.shape
    tm = 256
    tpb = seq // tm
    row = lambda i: (i, 0)
    fixed = lambda i: (0, 0)
    perb = lambda i: (i // tpb, 0, 0)
    return pl.pallas_call(
        _out_proj_kernel,
        grid=(n // tm,),
        in_specs=[pl.BlockSpec((tm, d), row),
                  pl.BlockSpec((tm, D_A), row),
                  pl.BlockSpec((tm, D_B), row),
                  pl.BlockSpec((1, D_B), fixed),
                  pl.BlockSpec(w_out.shape, fixed),
                  pl.BlockSpec((1, 1, d), perb),
                  pl.BlockSpec((1, d), fixed),
                  pl.BlockSpec((1, 1, d), perb),
                  pl.BlockSpec((1, 1, d), perb),
                  pl.BlockSpec(rw.shape, fixed)],
        out_specs=[pl.BlockSpec((tm, d), row),
                   pl.BlockSpec((tm, d // 2), row),
                   pl.BlockSpec((tm, LANES), row)],
        out_shape=[jax.ShapeDtypeStruct((n, d), F32),
                   jax.ShapeDtypeStruct((n, d // 2), I32),
                   jax.ShapeDtypeStruct((n, LANES), F32)],
        compiler_params=_cparams(("parallel",)),
        name="out_proj_ln2_router",
    )(x2, oa, ob, gn_b, w_out, g1, ln_g, sc, sh, rw)


def _route_kernel(lg_ref, info_ref, cnt_ref):
    @pl.when(pl.program_id(0) == 0)
    def _():
        cnt_ref[...] = jnp.zeros(cnt_ref.shape, F32)

    lg = lg_ref[...]
    lane = lax.broadcasted_iota(I32, lg.shape, 1)
    lanef = lane.astype(F32)
    big = float(4 * LANES)
    gm = jnp.where(lane >= N_EXPERTS, jnp.where(lane < N_EXPERTS + N_GROUPS, 1.0, 0.0), 0.0) > 0.5
    lgm = jnp.where(gm, lg, NEG)
    mg = jnp.max(lgm, axis=1, keepdims=True)
    eg = jnp.where(gm, jnp.exp(lgm - mg), 0.0)
    pg = eg / jnp.sum(eg, axis=1, keepdims=True)
    gw = jnp.max(pg, axis=1, keepdims=True)
    gidx = jnp.min(jnp.where(gm, jnp.where(pg == gw, lanef - N_EXPERTS, big), big), axis=1, keepdims=True)
    lane_group = (lane // EXPERTS_PER_GROUP).astype(F32)
    em = jnp.where(lane < N_EXPERTS, jnp.where(lane_group == gidx, 1.0, 0.0), 0.0) > 0.5
    lem = jnp.where(em, lg, NEG)
    me = jnp.max(lem, axis=1, keepdims=True)
    ee = jnp.where(em, jnp.exp(lem - me), 0.0)
    pe = jnp.where(em, ee / jnp.sum(ee, axis=1, keepdims=True), -1.0)
    p1 = jnp.max(pe, axis=1, keepdims=True)
    i1 = jnp.min(jnp.where(pe == p1, lanef, big), axis=1, keepdims=True)
    pe2 = jnp.where(lanef == i1, -1.0, pe)
    p2 = jnp.max(pe2, axis=1, keepdims=True)
    i2 = jnp.min(jnp.where(pe2 == p2, lanef, big), axis=1, keepdims=True)
    den = p1 + p2
    g0 = gw * p1 / den
    g1 = gw * p2 / den
    info = jnp.where(lane == 0, i1, jnp.where(lane == 1, i2,
                     jnp.where(lane == 2, g0, jnp.where(lane == 3, g1, 0.0))))
    info_ref[...] = info
    oh = jnp.where(lanef == i1, 1.0, 0.0) + jnp.where(lanef == i2, 1.0, 0.0)
    cnt_ref[...] += jnp.sum(oh, axis=0, keepdims=True)


def _route(lg):
    n = lg.shape[0]
    tm = min(1024, n)
    return pl.pallas_call(
        _route_kernel,
        grid=(n // tm,),
        in_specs=[pl.BlockSpec((tm, LANES), lambda i: (i, 0))],
        out_specs=[pl.BlockSpec((tm, LANES), lambda i: (i, 0)),
                   pl.BlockSpec((1, LANES), lambda i: (0, 0))],
        out_shape=[jax.ShapeDtypeStruct((n, LANES), F32),
                   jax.ShapeDtypeStruct((1, LANES), F32)],
        compiler_params=_cparams(("arbitrary",)),
        name="moe_route",
    )(lg)


def _plan_kernel(cnt_ref, ps_ref, be_ref, nu_ref, *, nblk):
    def fill(i, c):
        be_ref[i] = N_EXPERTS - 1
        return c

    lax.fori_loop(0, nblk, fill, 0)

    def per_expert(e, pos):
        ps_ref[e] = pos * MOE_BLK
        nb = (cnt_ref[e] + MOE_BLK - 1) // MOE_BLK

        def mark(k, c):
            be_ref[pos + k] = e
            return c

        lax.fori_loop(0, nb, mark, 0)
        return pos + nb

    nu_ref[0] = lax.fori_loop(0, N_EXPERTS, per_expert, jnp.int32(0))


def _plan(counts, nblk):
    smem = pl.BlockSpec(memory_space=pltpu.SMEM)
    return pl.pallas_call(
        functools.partial(_plan_kernel, nblk=nblk),
        in_specs=[smem],
        out_specs=[smem, smem, smem],
        out_shape=[jax.ShapeDtypeStruct((N_EXPERTS,), I32),
                   jax.ShapeDtypeStruct((nblk,), I32),
                   jax.ShapeDtypeStruct((1,), I32)],
        name="moe_block_plan",
    )(counts)


def _dest_kernel(info_ref, ps_ref, tri_ref, o_ref, carry_scr):
    @pl.when(pl.program_id(0) == 0)
    def _():
        carry_scr[...] = jnp.zeros(carry_scr.shape, F32)

    info = info_ref[...]
    lane = lax.broadcasted_iota(I32, info.shape, 1)
    lanef = lane.astype(F32)
    o1 = jnp.where(lanef == info[:, 0:1], 1.0, 0.0)
    o2 = jnp.where(lanef == info[:, 1:2], 1.0, 0.0)
    oh = o1 + o2
    before = jnp.dot(tri_ref[...], oh.astype(BF16), preferred_element_type=F32)
    base = before + carry_scr[...] + ps_ref[...]
    d1 = jnp.sum(o1 * base, axis=1, keepdims=True)
    d2 = jnp.sum(o2 * base, axis=1, keepdims=True)
    o_ref[...] = jnp.where(lane == 0, d1, jnp.where(lane == 1, d2, 0.0))
    carry_scr[...] += jnp.sum(oh, axis=0, keepdims=True)


def _dest(info, ps_lanes, tri):
    n = info.shape[0]
    tm = tri.shape[0]
    return pl.pallas_call(
        _dest_kernel,
        grid=(n // tm,),
        in_specs=[pl.BlockSpec((tm, LANES), lambda i: (i, 0)),
                  pl.BlockSpec((1, LANES), lambda i: (0, 0)),
                  pl.BlockSpec((tm, tm), lambda i: (0, 0))],
        out_specs=pl.BlockSpec((tm, LANES), lambda i: (i, 0)),
        out_shape=jax.ShapeDtypeStruct((n, LANES), F32),
        scratch_shapes=[pltpu.VMEM((1, LANES), F32)],
        compiler_params=_cparams(("arbitrary",)),
        name="moe_dest_rows",
    )(info, ps_lanes, tri)


SCATTER_SLOTS = 3


def _scatter_kernel(d0_ref, d1_ref, cnt_ref, ps_ref, nu_ref, h_ref, xb_ref,
                    zero_scr, stage_scr, load_sem, out_sem, zero_sem, *, nblk):
    n_tiles = h_ref.shape[0] // TOK_TILE
    zero_scr[...] = jnp.zeros(zero_scr.shape, zero_scr.dtype)
    pad_sizes = [s for s in (1 << k for k in range(MOE_BLK.bit_length() - 2, -1, -1))
                 if s >= SUBLANES]

    def zero_copy(rows, dst_row):
        return pltpu.make_async_copy(zero_scr.at[pl.ds(0, rows)], xb_ref.at[pl.ds(dst_row, rows)],
                                     zero_sem)

    def load(j, slot):
        return pltpu.make_async_copy(h_ref.at[pl.ds(j * TOK_TILE, TOK_TILE)], stage_scr.at[slot],
                                     load_sem.at[slot])

    def row_copy(slot, r, dst_row, parity, rows=1):
        return pltpu.make_async_copy(stage_scr.at[slot, pl.ds(r, rows)],
                                     xb_ref.at[pl.ds(dst_row, rows)], out_sem.at[parity])

    def for_each_zero_copy(act):
        def per_expert(e, c):
            pos = ps_ref[e] + cnt_ref[e]
            pad = (MOE_BLK - (cnt_ref[e] & (MOE_BLK - 1))) & (MOE_BLK - 1)
            head = pad & (SUBLANES - 1)
            for k in range(SUBLANES - 1):
                @pl.when(k < head)
                def _():
                    act(zero_copy(1, pos + k))
            pos = pl.multiple_of(pos + head, SUBLANES)
            for size in pad_sizes:
                @pl.when((pad & size) != 0)
                def _():
                    act(zero_copy(size, pos))
                pos = pl.multiple_of(pos + (pad & size), SUBLANES)
            return c

        def per_tail_block(i, c):
            act(zero_copy(MOE_BLK, i * MOE_BLK))
            return c

        lax.fori_loop(0, N_EXPERTS, per_expert, 0)
        lax.fori_loop(nu_ref[0], nblk, per_tail_block, 0)

    for_each_zero_copy(lambda cp: cp.start())

    def retire(parity):
        row_copy(0, 0, 0, parity, TOK_TILE).wait()
        row_copy(0, 0, 0, parity, TOK_TILE).wait()

    def tile(j, c):
        slot = j % SCATTER_SLOTS
        parity = j % 2
        load(j, slot).wait()

        @pl.when(j + 1 < n_tiles)
        def _():
            load(j + 1, (j + 1) % SCATTER_SLOTS).start()

        def issue(g, cc):
            for k in range(SUBLANES):
                r = g * SUBLANES + k
                t = j * TOK_TILE + r
                row_copy(slot, r, d0_ref[t], parity).start()
                row_copy(slot, r, d1_ref[t], parity).start()
            return cc

        lax.fori_loop(0, TOK_TILE // SUBLANES, issue, 0)

        @pl.when(j >= 1)
        def _():
            retire(1 - parity)
        return c

    load(0, 0).start()
    lax.fori_loop(0, n_tiles, tile, 0)
    retire((n_tiles - 1) % 2)
    for_each_zero_copy(lambda cp: cp.wait())


def _scatter_rows(d0, d1, counts, ps, nu, h2, p_rows):
    n, d = h2.shape
    smem = pl.BlockSpec(memory_space=pltpu.SMEM)
    return pl.pallas_call(
        functools.partial(_scatter_kernel, nblk=p_rows // MOE_BLK),
        in_specs=[smem, smem, smem, smem, smem, pl.BlockSpec(memory_space=pl.ANY)],
        out_specs=pl.BlockSpec(memory_space=pl.ANY),
        out_shape=jax.ShapeDtypeStruct((p_rows, d), h2.dtype),
        scratch_shapes=[pltpu.VMEM((MOE_BLK, d), h2.dtype),
                        pltpu.VMEM((SCATTER_SLOTS, TOK_TILE, d), h2.dtype),
                        pltpu.SemaphoreType.DMA((SCATTER_SLOTS,)),
                        pltpu.SemaphoreType.DMA((2,)),
                        pltpu.SemaphoreType.DMA(())],
        compiler_params=_cparams(),
        name="moe_scatter_rows",
    )(d0, d1, counts, ps, nu, h2)


def _expert_kernel(be_ref, nu_ref, x_ref, wg_hbm, wu_hbm, wd_hbm, o_ref,
                   wg_f32, wu_f32, wd_f32, wg_scr, wu_scr, wd_scr, slot_ref, sem):
    i = pl.program_id(0)
    n_used = nu_ref[0]
    expert = be_ref[i]
    new_expert = jnp.logical_or(i == 0, expert != be_ref[jnp.maximum(i - 1, 0)])

    def weight_copies(e, slot):
        return [pltpu.make_async_copy(src.at[e], dst.at[slot], sem.at[slot])
                for src, dst in ((wg_hbm, wg_f32), (wu_hbm, wu_f32), (wd_hbm, wd_f32))]

    @pl.when(i == 0)
    def _():
        slot_ref[0] = 0
        for cp in weight_copies(expert, 0):
            cp.start()

    @pl.when(jnp.logical_and(new_expert, i < n_used))
    def _():
        slot = slot_ref[0]
        for cp in weight_copies(expert, slot):
            cp.wait()
        nxt = lax.while_loop(lambda j: jnp.logical_and(j < n_used, be_ref[jnp.minimum(j, n_used - 1)] == expert),
                             lambda j: j + 1, i + 1)

        @pl.when(nxt < n_used)
        def _():
            for cp in weight_copies(be_ref[nxt], 1 - slot):
                cp.start()

        wg_scr[...] = wg_f32[slot].astype(BF16)
        wu_scr[...] = wu_f32[slot].astype(BF16)
        wd_scr[...] = wd_f32[slot].astype(BF16)
        slot_ref[0] = 1 - slot

    @pl.when(i < n_used)
    def _():
        x_hi, x_lo = _unpack_bf16_pairs(x_ref[...])
        x = jnp.concatenate([x_hi.astype(BF16), x_lo.astype(BF16)], axis=1)
        g = jnp.dot(x, wg_scr[...], preferred_element_type=F32)
        u = jnp.dot(x, wu_scr[...], preferred_element_type=F32)
        act = (g * (1.0 / (1.0 + jnp.exp(-g))) * u).astype(BF16)
        o_ref[...] = _pack_bf16_pairs(jnp.dot(act, wd_scr[...], preferred_element_type=F32))

    @pl.when(i >= nu_ref[0])
    def _():
        o_ref[...] = jnp.zeros(o_ref.shape, o_ref.dtype)


def _experts(be, nu, xb, wg, wu, wd):
    p = xb.shape[0]
    _, d, de = wg.shape
    return pl.pallas_call(
        _expert_kernel,
        grid_spec=pltpu.PrefetchScalarGridSpec(
            num_scalar_prefetch=2,
            grid=(p // MOE_BLK,),
            in_specs=[pl.BlockSpec((MOE_BLK, d // 2), lambda i, be, nu: (i, 0)),
                      pl.BlockSpec(memory_space=pl.ANY),
                      pl.BlockSpec(memory_space=pl.ANY),
                      pl.BlockSpec(memory_space=pl.ANY)],
            out_specs=pl.BlockSpec((MOE_BLK, d // 2), lambda i, be, nu: (i, 0)),
            scratch_shapes=[pltpu.VMEM((2, d, de), F32), pltpu.VMEM((2, d, de), F32),
                            pltpu.VMEM((2, de, d), F32),
                            pltpu.VMEM((d, de), BF16), pltpu.VMEM((d, de), BF16),
                            pltpu.VMEM((de, d), BF16),
                            pltpu.SMEM((1,), I32),
                            pltpu.SemaphoreType.DMA((2,))]),
        out_shape=jax.ShapeDtypeStruct((p, d // 2), I32),
        compiler_params=_cparams(("arbitrary",)),
        name="moe_expert_ffn",
    )(be, nu, xb, wg, wu, wd)


def _combine_kernel(d0_ref, d1_ref, x_ref, info_ref, g2_ref, yb_ref, o_ref, rows_scr, sem):
    i = pl.program_id(0)
    cur = i % 2

    def copy(buf, which, r, src_row, rows=1):
        return pltpu.make_async_copy(yb_ref.at[pl.ds(src_row, rows)],
                                     rows_scr.at[buf, which, pl.ds(r, rows)], sem.at[buf])

    def start_gather(step, buf):
        base = step * TOK_TILE

        def issue(g, c):
            for k in range(SUBLANES):
                r = g * SUBLANES + k
                copy(buf, 0, r, d0_ref[base + r]).start()
                copy(buf, 1, r, d1_ref[base + r]).start()
            return c

        lax.fori_loop(0, TOK_TILE // SUBLANES, issue, 0)

    @pl.when(i == 0)
    def _():
        start_gather(0, 0)

    @pl.when(i + 1 < pl.num_programs(0))
    def _():
        start_gather(i + 1, 1 - cur)

    copy(cur, 0, 0, 0, TOK_TILE).wait()
    copy(cur, 1, 0, 0, TOK_TILE).wait()

    info = info_ref[...]
    half = o_ref.shape[1] // 2
    y0_hi, y0_lo = _unpack_bf16_pairs(rows_scr[cur, 0])
    y1_hi, y1_lo = _unpack_bf16_pairs(rows_scr[cur, 1])
    g2 = g2_ref[0]
    o_ref[:, :half] = x_ref[:, :half] + g2[:, :half] * (info[:, 2:3] * y0_hi + info[:, 3:4] * y1_hi)
    o_ref[:, half:] = x_ref[:, half:] + g2[:, half:] * (info[:, 2:3] * y0_lo + info[:, 3:4] * y1_lo)


def _combine(d0, d1, x1, info, g2, yb, seq):
    n, d = x1.shape
    tpb = seq // TOK_TILE
    return pl.pallas_call(
        _combine_kernel,
        grid_spec=pltpu.PrefetchScalarGridSpec(
            num_scalar_prefetch=2,
            grid=(n // TOK_TILE,),
            in_specs=[pl.BlockSpec((TOK_TILE, d), lambda i, a, b: (i, 0)),
                      pl.BlockSpec((TOK_TILE, LANES), lambda i, a, b: (i, 0)),
                      pl.BlockSpec((1, 1, d), lambda i, a, b: (i // tpb, 0, 0)),
                      pl.BlockSpec(memory_space=pl.ANY)],
            out_specs=pl.BlockSpec((TOK_TILE, d), lambda i, a, b: (i, 0)),
            scratch_shapes=[pltpu.VMEM((2, 2, TOK_TILE, d // 2), I32),
                            pltpu.SemaphoreType.DMA((2,))]),
        out_shape=jax.ShapeDtypeStruct((n, d), F32),
        compiler_params=_cparams(("arbitrary",)),
        name="moe_combine",
    )(d0, d1, x1, info, g2, yb)


def _tri_inclusive_rev(k):
    l = (np.arange(k)[:, None] >= np.arange(k)[None, :]).astype(np.float32)
    half = np.concatenate([l, np.ones((k, k), np.float32)], axis=1)
    return jnp.asarray(np.concatenate([half, half], axis=0), dtype=BF16)


def _tri_strict_lower(k):
    return jnp.asarray((np.arange(k)[None, :] < np.arange(k)[:, None]).astype(np.float32), dtype=BF16)


def kernel(x, c, w_mod, b_mod, ln1_g, w_in, q_norm_g, w_q_up, q_gain, k_gain, rel_bias, gn_a, gn_b,
           w_out, ln2_g, router_g, router_e, w_gate, w_up, w_down):
    batch, seq, d = x.shape
    n = batch * seq
    assert w_mod.shape[0] == 1 and d == D_A + D_B
    assert seq % KB_A == 0 and seq % TOK_TILE == 0 and n % 512 == 0
    x2 = x.reshape(n, d)

    c8 = jnp.pad(c, ((0, 8 - batch), (0, 0)))
    mod = _modulation(c8, w_mod.reshape(d, -1), b_mod.reshape(1, -1))[:batch]
    sh1, sc1, g1, sh2, sc2, g2 = [m.reshape(batch, 1, d) for m in jnp.split(mod, 6, axis=-1)]

    wi = w_in.reshape(d, -1).astype(BF16)
    n_a = Q_RANK + 2 * HEAD_DIM + IDX_DIM + IDX_HEADS
    wi = jnp.concatenate([wi[:, n_a:], wi[:, :n_a],
                          jnp.zeros((d, D_IN_PAD - wi.shape[1]), wi.dtype)], axis=1)
    proj = _ln_proj(x2, ln1_g.reshape(1, d), sc1, sh1, wi, seq)

    qa, qidx, ka, va, kidx, widx = _aprep(
        proj, n, q_norm_g.reshape(1, -1), w_q_up.reshape(Q_RANK, -1).astype(BF16),
        q_gain.reshape(1, -1), k_gain.reshape(1, -1))

    bt = _bias_tiles(rel_bias)
    bound = _logit_bound(q_gain.reshape(1, -1), k_gain.reshape(1, -1), rel_bias)[0, :1]
    oa = _attn_a(qidx, widx.T, qa, kidx, ka, va, bt, gn_a.reshape(1, -1), bound, batch, seq)
    ob = _attn_b(proj, _tri_inclusive_rev(QB_B), batch, seq)

    rw = jnp.concatenate([router_e.reshape(d, -1), router_g.reshape(d, -1),
                          jnp.zeros((d, LANES - N_EXPERTS - N_GROUPS), F32)], axis=1)
    x1, h2, lg = _out_proj(x2, oa, ob, gn_b.reshape(1, -1), w_out.reshape(d, d).astype(BF16),
                           g1, ln2_g.reshape(1, d), sc2, sh2, rw, seq)

    info, cnt = _route(lg)
    counts = cnt[0, :N_EXPERTS].astype(I32)
    p_rows = 2 * n + N_EXPERTS * MOE_BLK
    ps, be, nu = _plan(counts, p_rows // MOE_BLK)
    ps_lanes = jnp.pad(ps.astype(F32), (0, LANES - N_EXPERTS)).reshape(1, LANES)
    dinfo = _dest(info, ps_lanes, _tri_strict_lower(512))
    d0 = dinfo[:, 0].astype(I32)
    d1 = dinfo[:, 1].astype(I32)
    xb = _scatter_rows(d0, d1, counts, ps, nu, h2, p_rows)
    yb = _experts(be, nu, xb,
                  w_gate.reshape(N_EXPERTS, d, D_EXPERT),
                  w_up.reshape(N_EXPERTS, d, D_EXPERT),
                  w_down.reshape(N_EXPERTS, D_EXPERT, d))
    out = _combine(d0, d1, x1, info, g2, yb, seq)
    return out.reshape(batch, seq, d)
```

```python
import functools
import math

import numpy as np
import jax
import jax.numpy as jnp
from jax import lax
from jax.experimental import pallas as pl
from jax.experimental.pallas import tpu as pltpu

F32 = jnp.float32
BF16 = jnp.bfloat16
I32 = jnp.int32

HEAD_DIM = 128
N_HEADS_A = 8
N_HEADS_B = 8
D_A = N_HEADS_A * HEAD_DIM
D_B = N_HEADS_B * HEAD_DIM
Q_RANK = 512
IDX_HEADS = 16
IDX_DIM = 64
TOPK_MAX = 256
N_BUCKETS = 32
MAX_DISTANCE = 128
N_GROUPS = 4
EXPERTS_PER_GROUP = 8
N_EXPERTS = N_GROUPS * EXPERTS_PER_GROUP
D_EXPERT = 512
EPS = 1e-6

LANES = 128
SUBLANES = 8
VMEM_LIMIT = 56 * 1024 * 1024
NEG = -1e30
INT_MIN = -(2 ** 31)
EXP2_UNDERFLOW = 150.0
LOG2E = math.log2(math.e)
PLAIN_SOFTMAX_LIMIT = 64.0
GUESS_HALF_WIDTH = 0.1
EXTRACT_BELOW = 3.0

QB_A = 128
KB_A = 256
QB_B = 128
MOE_BLK = 256
TOK_TILE = 256

COL_QB, COL_KB, COL_VB, COL_A = 0, D_B, 2 * D_B, 3 * D_B
A_CQ, A_KA, A_VA, A_KIDX, A_WIDX = 0, 512, 640, 768, 832
A_WIDTH = 1024
D_IN_PAD = COL_A + A_WIDTH


def _cparams(sem=None):
    return pltpu.CompilerParams(dimension_semantics=sem, vmem_limit_bytes=VMEM_LIMIT)


def _rms(x):
    return x * lax.rsqrt(jnp.mean(x * x, axis=-1, keepdims=True) + EPS)


def _pack_bf16_pairs(x):
    c = x.shape[1] // 2
    hi = pltpu.bitcast(x[:, :c].astype(BF16).astype(F32), I32)
    lo = pltpu.bitcast(x[:, c:].astype(BF16).astype(F32), I32)
    return hi | lax.shift_right_logical(lo, 16)


def _unpack_bf16_pairs(u):
    hi = pltpu.bitcast(u & jnp.int32(-65536), F32)
    lo = pltpu.bitcast(lax.shift_left(u, 16), F32)
    return hi, lo


def _mod_kernel(c_ref, w_ref, b_ref, o_ref):
    c = c_ref[...]
    s = c * (1.0 / (1.0 + jnp.exp(-c)))
    o_ref[...] = jnp.dot(s, w_ref[...], preferred_element_type=F32,
                         precision=lax.Precision.HIGHEST) + b_ref[...]


def _modulation(c8, w_mod, b_mod):
    d, n6 = w_mod.shape
    tn = 2048
    return pl.pallas_call(
        _mod_kernel,
        grid=(n6 // tn,),
        in_specs=[pl.BlockSpec((8, d), lambda j: (0, 0)),
                  pl.BlockSpec((d, tn), lambda j: (0, j)),
                  pl.BlockSpec((1, tn), lambda j: (0, j))],
        out_specs=pl.BlockSpec((8, tn), lambda j: (0, j)),
        out_shape=jax.ShapeDtypeStruct((8, n6), F32),
        compiler_params=_cparams(("arbitrary",)),
        name="modulation",
    )(c8, w_mod, b_mod)


def _ln_proj_kernel(x_ref, g_ref, sc_ref, sh_ref, w_ref, o_ref, h_scr, *, q_tiles, chunk):
    i = pl.program_id(0)
    j = pl.program_id(1)

    def normalise_chunk():
        rows = pl.ds(pl.multiple_of(j * chunk, chunk), chunk)
        h = _rms(x_ref[rows, :]) * g_ref[...]
        h = h * (1.0 + sc_ref[0]) + sh_ref[0]
        h_scr[i % 2, rows, :] = h.astype(BF16)

    @pl.when(i == 0)
    def _():
        normalise_chunk()
        o_ref[...] = jnp.zeros(o_ref.shape, o_ref.dtype)

    @pl.when(i > 0)
    def _():
        normalise_chunk()
        col_scale = jnp.where(j < q_tiles, HEAD_DIM ** -0.5 * LOG2E, 1.0)
        acc = jnp.dot(h_scr[(i - 1) % 2], w_ref[...], preferred_element_type=F32)
        o_ref[...] = (acc * col_scale).astype(o_ref.dtype)


def _ln_proj(x2, ln_g, sc, sh, w, seq):
    n, d = x2.shape
    ncol = w.shape[1]
    tm = min(1024, seq)
    tn = 1024
    tpb = seq // tm
    n_i, n_j = n // tm, ncol // tn
    assert COL_QB == 0 and D_B % tn == 0 and tm % (n_j * 16) == 0
    last = n_i - 1
    return pl.pallas_call(
        functools.partial(_ln_proj_kernel, q_tiles=D_B // tn, chunk=tm // n_j),
        grid=(n_i + 1, n_j),
        in_specs=[pl.BlockSpec((tm, d), lambda i, j: (jnp.minimum(i, last), 0)),
                  pl.BlockSpec((1, d), lambda i, j: (0, 0)),
                  pl.BlockSpec((1, 1, d), lambda i, j: (jnp.minimum(i, last) // tpb, 0, 0)),
                  pl.BlockSpec((1, 1, d), lambda i, j: (jnp.minimum(i, last) // tpb, 0, 0)),
                  pl.BlockSpec((d, tn), lambda i, j: (0, j))],
        out_specs=pl.BlockSpec((tm, tn), lambda i, j: (jnp.where(i == 0, n_i, i - 1), j)),
        out_shape=jax.ShapeDtypeStruct((n + tm, ncol), BF16),
        scratch_shapes=[pltpu.VMEM((2, tm, d), BF16)],
        compiler_params=_cparams(("arbitrary", "arbitrary")),
        name="ln_in_proj",
    )(x2, ln_g, sc, sh, w)


def _aprep_kernel(a_ref, qng_ref, wq_ref, qg_ref, kg_ref,
                  qa_ref, qidx_ref, ka_ref, va_ref, kidx_ref, widx_ref):
    cq = a_ref[:, A_CQ:A_CQ + Q_RANK].astype(F32)
    cqn = (_rms(cq) * qng_ref[...]).astype(BF16)
    qup = jnp.dot(cqn, wq_ref[...], preferred_element_type=F32)
    for h in range(N_HEADS_A):
        qh = qup[:, h * HEAD_DIM:(h + 1) * HEAD_DIM]
        qn = _rms(qh) * qg_ref[...] * (HEAD_DIM ** -0.5 * LOG2E)
        qa_ref[:, h * HEAD_DIM:(h + 1) * HEAD_DIM] = qn.astype(BF16)
    for h in range(IDX_HEADS):
        qi = qup[:, D_A + h * IDX_DIM:D_A + (h + 1) * IDX_DIM] * (IDX_DIM ** -0.5)
        qidx_ref[h] = qi.astype(BF16)
    ka = a_ref[:, A_KA:A_KA + HEAD_DIM].astype(F32)
    ka_ref[...] = (_rms(ka) * kg_ref[...]).astype(BF16)
    va_ref[...] = a_ref[:, A_VA:A_VA + HEAD_DIM]
    kidx_ref[...] = a_ref[:, A_KIDX:A_KIDX + IDX_DIM]
    widx_ref[...] = a_ref[:, A_WIDX:A_WIDX + IDX_HEADS].astype(F32) * (IDX_HEADS ** -0.5)


def _aprep(proj, n, q_norm_g, w_q_up, q_gain, k_gain):
    tm = 512
    cblk = COL_A // A_WIDTH
    nup = w_q_up.shape[1]
    return pl.pallas_call(
        _aprep_kernel,
        grid=(n // tm,),
        in_specs=[pl.BlockSpec((tm, A_WIDTH), lambda i: (i, cblk)),
                  pl.BlockSpec((1, Q_RANK), lambda i: (0, 0)),
                  pl.BlockSpec((Q_RANK, nup), lambda i: (0, 0)),
                  pl.BlockSpec((1, HEAD_DIM), lambda i: (0, 0)),
                  pl.BlockSpec((1, HEAD_DIM), lambda i: (0, 0))],
        out_specs=[pl.BlockSpec((tm, D_A), lambda i: (i, 0)),
                   pl.BlockSpec((IDX_HEADS, tm, IDX_DIM), lambda i: (0, i, 0)),
                   pl.BlockSpec((tm, HEAD_DIM), lambda i: (i, 0)),
                   pl.BlockSpec((tm, HEAD_DIM), lambda i: (i, 0)),
                   pl.BlockSpec((tm, IDX_DIM), lambda i: (i, 0)),
                   pl.BlockSpec((tm, IDX_HEADS), lambda i: (i, 0))],
        out_shape=[jax.ShapeDtypeStruct((n, D_A), BF16),
                   jax.ShapeDtypeStruct((IDX_HEADS, n, IDX_DIM), BF16),
                   jax.ShapeDtypeStruct((n, HEAD_DIM), BF16),
                   jax.ShapeDtypeStruct((n, HEAD_DIM), BF16),
                   jax.ShapeDtypeStruct((n, IDX_DIM), BF16),
                   jax.ShapeDtypeStruct((n, IDX_HEADS), F32)],
        compiler_params=_cparams(("parallel",)),
        name="group_a_prep",
    )(proj, q_norm_g, w_q_up, q_gain, k_gain)


def _t5_bucket_starts():
    max_exact = N_BUCKETS // 2
    d = np.arange(0, 4 * MAX_DISTANCE, dtype=np.int64)
    df = np.maximum(d, 1).astype(np.float32)
    large = max_exact + (np.log(df / np.float32(max_exact)) / np.float32(math.log(MAX_DISTANCE / max_exact))
                         * np.float32(N_BUCKETS - max_exact)).astype(np.int32)
    large = np.minimum(large, N_BUCKETS - 1)
    bucket = np.where(d < max_exact, d, large)
    assert np.all(np.diff(bucket) >= 0) and bucket[-1] == N_BUCKETS - 1
    return [int(np.argmax(bucket >= b)) for b in range(N_BUCKETS)]


_BUCKET_START = _t5_bucket_starts()
N_BIAS_TILES = 2 * KB_A // LANES


def _bias_kernel(rb_ref, o_ref):
    di = pl.program_id(0)
    h = pl.program_id(1)
    i = lax.broadcasted_iota(I32, (QB_A, KB_A), 0)
    j = lax.broadcasted_iota(I32, (QB_A, KB_A), 1)
    d = di * LANES + i - j
    val = jnp.full((QB_A, KB_A), rb_ref[0, h], F32)
    for b in range(1, N_BUCKETS):
        val = jnp.where(d >= _BUCKET_START[b], rb_ref[b, h], val)
    o_ref[0, 0] = (val - rb_ref[N_BUCKETS - 1, h]) * LOG2E


def _bias_tiles(rel_bias):
    return pl.pallas_call(
        _bias_kernel,
        grid=(N_BIAS_TILES, N_HEADS_A),
        in_specs=[pl.BlockSpec(memory_space=pltpu.SMEM)],
        out_specs=pl.BlockSpec((1, 1, QB_A, KB_A), lambda a, h: (a, h, 0, 0)),
        out_shape=jax.ShapeDtypeStruct((N_BIAS_TILES, N_HEADS_A, QB_A, KB_A), F32),
        compiler_params=_cparams(("arbitrary", "arbitrary")),
        name="t5_bias_tiles",
    )(rel_bias)


def _bound_kernel(qg_ref, kg_ref, rb_ref, o_ref):
    qmax = jnp.max(jnp.abs(qg_ref[...]), axis=1, keepdims=True)
    kmax = jnp.max(jnp.abs(kg_ref[...]), axis=1, keepdims=True)
    rb = rb_ref[...]
    shifted = jnp.abs(rb - rb[N_BUCKETS - 1:N_BUCKETS, :])
    bmax = jnp.max(jnp.max(shifted, axis=1, keepdims=True), axis=0, keepdims=True)
    bound = qmax * kmax * (math.sqrt(HEAD_DIM) * 1.02) + bmax
    o_ref[...] = jnp.broadcast_to(bound, o_ref.shape)


def _logit_bound(q_gain, k_gain, rel_bias):
    return pl.pallas_call(
        _bound_kernel,
        out_shape=jax.ShapeDtypeStruct((1, LANES), F32),
        name="dsa_logit_bound",
    )(q_gain, k_gain, rel_bias)


def _attn_a_kernel(qidx_ref, w_ref, qa_ref, kidx_ref, ka_ref, va_ref, bt_ref, gn_ref, bound_ref, o_ref,
                   keys_scr, keyt_scr, m_scr, acc_scr, s_scr, mb_scr, *, topk):
    qb = pl.program_id(1)
    t0 = qb * QB_A
    kbl = (t0 + QB_A - 1) // KB_A
    row = t0 + lax.broadcasted_iota(I32, (QB_A, KB_A), 0)
    col0 = lax.broadcasted_iota(I32, (QB_A, KB_A), 1)
    nt = (((1,), (1,)), ((), ()))

    reps = KB_A // LANES

    def sort_key(v):
        bits = pltpu.bitcast(v, I32)
        return jnp.where(bits < 0, bits ^ jnp.int32(0x7FFFFFFF), bits)

    def key_value(k):
        return pltpu.bitcast(jnp.where(k < 0, k ^ jnp.int32(0x7FFFFFFF), k), F32)

    key_pos = lax.broadcasted_iota(I32, (KB_A, QB_A), 0)
    qry_pos = t0 + lax.broadcasted_iota(I32, (KB_A, QB_A), 1)

    def score_tile(kb, carry, diagonal):
        smin, smax, s1, s2 = carry
        kt = kidx_ref[pl.ds(pl.multiple_of(kb * KB_A, KB_A), KB_A), :]
        score = jnp.zeros((KB_A, QB_A), F32)
        for hp in range(IDX_HEADS // 2):
            q2 = qidx_ref[2 * hp:2 * hp + 2].reshape(2 * QB_A, IDX_DIM)
            sc = lax.dot_general(kt, q2, nt, preferred_element_type=F32)
            score = score + w_ref[2 * hp:2 * hp + 1, :] * jnp.maximum(sc[:, :QB_A], 0.0)
            score = score + w_ref[2 * hp + 1:2 * hp + 2, :] * jnp.maximum(sc[:, QB_A:], 0.0)
        key_t = sort_key(score)
        low, high, live = score, score, score
        if diagonal:
            causal = (kb * KB_A + key_pos) <= qry_pos
            key_t = jnp.where(causal, key_t, jnp.int32(INT_MIN))
            low = jnp.where(causal, score, jnp.inf)
            high = jnp.where(causal, score, -jnp.inf)
            live = jnp.where(causal, score, 0.0)
        keyt_scr[kb] = key_t
        keys_scr[kb] = key_t.T
        smin = jnp.minimum(smin, jnp.min(low, axis=0, keepdims=True))
        smax = jnp.maximum(smax, jnp.max(high, axis=0, keepdims=True))
        s1 = s1 + jnp.sum(live, axis=0, keepdims=True)
        s2 = s2 + jnp.sum(live * live, axis=0, keepdims=True)
        return smin, smax, s1, s2

    def score_pair(i, carry):
        return score_tile(2 * i + 1, score_tile(2 * i, carry, False), False)

    zero_row = jnp.zeros((1, QB_A), F32)
    stats = lax.fori_loop(0, kbl // 2, score_pair,
                          (jnp.full((1, QB_A), jnp.inf, F32), jnp.full((1, QB_A), -jnp.inf, F32),
                           zero_row, zero_row))
    stats = lax.cond(kbl % 2 == 1, lambda c: score_tile(kbl - 1, c, False), lambda c: c, stats)
    smin, smax, s1, s2 = score_tile(kbl, stats, True)

    qry1 = t0 + lax.broadcasted_iota(I32, (1, QB_A), 1)
    kf = float(topk)
    acc_rows = 32
    n_causal = (qry1 + 1).astype(F32)

    mean = s1 / n_causal
    dev = jnp.sqrt(jnp.maximum(s2 / n_causal - mean * mean, 0.0))
    frac_top = jnp.minimum(kf / n_causal, 1.0)
    tail = jnp.minimum(frac_top, 1.0 - frac_top)
    tq = jnp.sqrt(-2.0 * jnp.log(jnp.maximum(tail, 1e-6)))
    zq = tq - ((0.010328 * tq + 0.802853) * tq + 2.515517) / (((0.001308 * tq + 0.189269) * tq + 1.432788) * tq + 1.0)
    zq = jnp.where(frac_top > 0.5, -zq, zq)

    def for_each_key_tile(fold, init):
        return lax.fori_loop(0, kbl + 1, lambda kb, a: fold(a, keyt_scr[kb]), init)

    def search_pass(p, state):
        lo, hi, clo, chi, open_q = state
        lo_v = key_value(lo)
        hi_v = key_value(hi)
        gap = clo - chi
        frac = (clo - (kf - 0.5)) / gap
        frac = jnp.where(p % 2 == 1, 0.7 * frac + 0.15, frac)
        frac = jnp.where(gap > 16.0, frac, 0.5)
        value = lo_v + (hi_v - lo_v) * frac
        value = jnp.where(p == 0, mean + (zq - GUESS_HALF_WIDTH) * dev, value)
        value = jnp.where(p == 1, mean + (zq + GUESS_HALF_WIDTH) * dev, value)
        cand = sort_key(value)
        cand = jnp.where(p % 8 == 7, lo + lax.shift_right_logical(hi - lo, 1), cand)
        cand = jnp.minimum(jnp.maximum(cand, lo + 1), hi - 1)

        def count(cnt, keys):
            ge = jnp.where(keys >= cand, 1.0, 0.0)
            return cnt + jnp.sum(ge.reshape(KB_A // acc_rows, acc_rows, QB_A), axis=0)

        tot = jnp.sum(for_each_key_tile(count, jnp.zeros((acc_rows, QB_A), F32)), axis=0, keepdims=True)
        ge = tot >= kf
        lo = jnp.where(ge, cand, lo)
        clo = jnp.where(ge, tot, clo)
        hi = jnp.where(ge, hi, cand)
        chi = jnp.where(ge, chi, tot)
        width = hi - lo
        settled = jnp.where(clo == kf, 1.0, jnp.where(width == 1, 1.0, 0.0))
        open_q = jnp.where(settled > 0.5, 0.0, open_q)
        return lo, hi, clo, chi, open_q

    def extract_pass(state):
        lo, hi, clo, chi, open_q = state

        def top_below(best, keys):
            below = jnp.where(keys < hi, keys, jnp.int32(INT_MIN))
            return jnp.maximum(best, jnp.max(below.reshape(KB_A // acc_rows, acc_rows, QB_A), axis=0))

        best = for_each_key_tile(top_below, jnp.full((acc_rows, QB_A), INT_MIN, I32))
        best = jnp.max(best, axis=0, keepdims=True)
        is_open = open_q > 0.5
        last = jnp.logical_and(is_open, kf - chi <= 1.0)
        more = jnp.logical_and(is_open, kf - chi > 1.0)
        lo = jnp.where(last, best, lo)
        clo = jnp.where(last, chi + 1.0, clo)
        hi = jnp.where(more, best, hi)
        chi = jnp.where(more, chi + 1.0, chi)
        open_q = jnp.where(last, 0.0, open_q)
        return lo, hi, clo, chi, open_q

    def missing(state):
        return jnp.max(jnp.where(state[4] > 0.5, kf - state[3], 0.0))

    def search_step(carry):
        p, state, lacking = carry
        state = lax.cond(lacking <= EXTRACT_BELOW,
                         lambda st: extract_pass(st),
                         lambda st: search_pass(p + 1, search_pass(p, st)), state)
        return p + 2, state, missing(state)

    open0 = jnp.where(qry1 >= topk, 1.0, 0.0)
    lo0 = sort_key(smin)
    hi0 = sort_key(smax) + 1
    open0 = jnp.where(hi0 - lo0 == 1, 0.0, open0)
    state0 = (lo0, hi0, n_causal, zero_row, open0)
    _, (lo, _, _, _, _), _ = lax.while_loop(
        lambda c: c[2] > 0.0, search_step, (jnp.int32(0), state0, missing(state0)))
    thr = jnp.where(qry1 >= topk, lo, jnp.int32(INT_MIN))
    thr = jnp.broadcast_to(thr, (QB_A, QB_A)).T
    thrb = jnp.tile(thr, (1, reps))

    m_scr[...] = jnp.full(m_scr.shape, NEG, F32)
    acc_scr[...] = jnp.zeros(acc_scr.shape, F32)

    plain = bound_ref[0] <= PLAIN_SOFTMAX_LIMIT

    def attend(kb, near, online):
        mb = jnp.where(keys_scr[kb] >= thrb, 0.0, NEG)
        if near:
            mb = jnp.where((kb * KB_A + col0) <= row, mb, NEG)
            di = (t0 - kb * KB_A) // LANES
        mb_scr[...] = mb
        start = pl.multiple_of(kb * KB_A, KB_A)
        kt = ka_ref[pl.ds(start, KB_A), :]
        vt = jnp.concatenate([va_ref[pl.ds(start, KB_A), :], jnp.ones((KB_A, LANES), BF16)], axis=1)
        for h in range(N_HEADS_A):
            q = qa_ref[:, h * HEAD_DIM:(h + 1) * HEAD_DIM]
            s_scr[h] = lax.dot_general(q, kt, nt, preferred_element_type=F32)
        for h in range(N_HEADS_A):
            s = s_scr[h] + mb_scr[...]
            if near:
                s = s + bt_ref[di, h]
            if online:
                m_prev = m_scr[h]
                m_new = jnp.maximum(m_prev, jnp.max(s, axis=1, keepdims=True))
                alpha = jnp.exp2(m_prev - m_new)
                p = jnp.exp2(s - jnp.tile(m_new, (1, reps)))
                acc_scr[h] = (jnp.tile(alpha, (1, 2)) * acc_scr[h]
                              + jnp.dot(p.astype(BF16), vt, preferred_element_type=F32))
                m_scr[h] = m_new
            else:
                acc_scr[h] += jnp.dot(jnp.exp2(s).astype(BF16), vt, preferred_element_type=F32)

    def attend_all(online):
        prev_near = jnp.logical_and(kbl >= 1, t0 - kbl * KB_A < _BUCKET_START[-1] - 1)
        n_far = jnp.where(prev_near, kbl - 1, kbl)

        def far_pair(i, carry):
            attend(2 * i, False, online)
            attend(2 * i + 1, False, online)
            return carry

        lax.fori_loop(0, n_far // 2, far_pair, 0)

        @pl.when(n_far % 2 == 1)
        def _():
            attend(n_far - 1, False, online)

        @pl.when(prev_near)
        def _():
            attend(kbl - 1, True, online)

        attend(kbl, True, online)

    @pl.when(plain)
    def _():
        attend_all(False)

    @pl.when(jnp.logical_not(plain))
    def _():
        attend_all(True)

    ssq = jnp.zeros((QB_A, LANES), F32)
    for h in range(N_HEADS_A):
        oh = acc_scr[h, :, :HEAD_DIM] / acc_scr[h, :, HEAD_DIM:]
        acc_scr[h, :, :HEAD_DIM] = oh
        ssq = ssq + jnp.sum(oh * oh, axis=1, keepdims=True)
    inv = lax.rsqrt(ssq * (1.0 / D_A) + EPS)
    for h in range(N_HEADS_A):
        sl = slice(h * HEAD_DIM, (h + 1) * HEAD_DIM)
        o_ref[:, sl] = (acc_scr[h, :, :HEAD_DIM] * inv * gn_ref[:, sl]).astype(o_ref.dtype)


def _attn_a(qidx, widx, qa, kidx, ka, va, bt, gn_a, bound, batch, seq):
    n = qa.shape[0]
    nq = seq // QB_A
    nkt = seq // KB_A
    topk = min(TOPK_MAX, seq // 4)
    return pl.pallas_call(
        functools.partial(_attn_a_kernel, topk=topk),
        grid=(batch, nq),
        in_specs=[pl.BlockSpec((IDX_HEADS, QB_A, IDX_DIM), lambda b, q: (0, b * nq + q, 0)),
                  pl.BlockSpec((IDX_HEADS, QB_A), lambda b, q: (0, b * nq + q)),
                  pl.BlockSpec((QB_A, D_A), lambda b, q: (b * nq + q, 0)),
                  pl.BlockSpec((seq, IDX_DIM), lambda b, q: (b, 0)),
                  pl.BlockSpec((seq, HEAD_DIM), lambda b, q: (b, 0)),
                  pl.BlockSpec((seq, HEAD_DIM), lambda b, q: (b, 0)),
                  pl.BlockSpec(bt.shape, lambda b, q: (0, 0, 0, 0)),
                  pl.BlockSpec((1, D_A), lambda b, q: (0, 0)),
                  pl.BlockSpec(memory_space=pltpu.SMEM)],
        out_specs=pl.BlockSpec((QB_A, D_A), lambda b, q: (b * nq + q, 0)),
        out_shape=jax.ShapeDtypeStruct((n, D_A), BF16),
        scratch_shapes=[pltpu.VMEM((nkt, QB_A, KB_A), I32),
                        pltpu.VMEM((nkt, KB_A, QB_A), I32),
                        pltpu.VMEM((N_HEADS_A, QB_A, LANES), F32),
                        pltpu.VMEM((N_HEADS_A, QB_A, 2 * HEAD_DIM), F32),
                        pltpu.VMEM((N_HEADS_A, QB_A, KB_A), F32),
                        pltpu.VMEM((QB_A, KB_A), F32)],
        compiler_params=_cparams(("parallel", "arbitrary")),
        name="dsa_attention",
    )(qidx, widx, qa, kidx, ka, va, bt, gn_a, bound)


def _attn_b_kernel(q_ref, k_ref, v_ref, tri_ref, o_ref, rest_scr, z_scr, cs_scr):
    qb = pl.program_id(1)
    row = lax.broadcasted_iota(I32, (QB_B, QB_B), 0)
    col = lax.broadcasted_iota(I32, (QB_B, QB_B), 1)
    strict = col < row
    nt = (((1,), (1,)), ((), ()))
    def step(kb, diag):
        start = pl.multiple_of(kb * QB_B, QB_B)
        heads = [slice(h * HEAD_DIM, (h + 1) * HEAD_DIM) for h in range(N_HEADS_B)]
        for h, sl in enumerate(heads):
            kt = k_ref[pl.ds(start, QB_B), sl]
            z_scr[h] = lax.dot_general(q_ref[:, sl], kt, nt, preferred_element_type=F32)
        for h, sl in enumerate(heads):
            z = z_scr[h]
            sp = jnp.maximum(z, 0.0) + jnp.log(1.0 + jnp.exp2(-jnp.abs(z))) * LOG2E
            if diag:
                sp = jnp.where(strict, sp, 0.0)
            hi = sp.astype(BF16)
            lo = (sp - hi.astype(F32)).astype(BF16)
            cs_scr[h] = jnp.dot(jnp.concatenate([hi, lo], axis=1), tri_ref[...],
                                preferred_element_type=F32)
        worst = None
        for h, sl in enumerate(heads):
            vt = v_ref[pl.ds(start, QB_B), sl]
            z = z_scr[h]
            cs = cs_scr[h, :, :QB_B]
            tot = cs_scr[h, :, QB_B:]
            if diag:
                a = jnp.where(strict, jnp.exp2(z - cs), 0.0)
                o_ref[:, sl] = jnp.dot(a.astype(BF16), vt, preferred_element_type=F32)
                rest = tot
            else:
                rest = rest_scr[h]
                a = jnp.exp2(z - cs - rest)
                o_ref[:, sl] += jnp.dot(a.astype(BF16), vt, preferred_element_type=F32)
                rest = rest + tot
            rest_scr[h] = rest
            worst = rest if worst is None else jnp.minimum(worst, rest)
        return jnp.min(worst)

    def more(kb, smallest):
        return jnp.logical_and(kb >= 0, smallest < EXP2_UNDERFLOW)

    def body(carry):
        kb, _ = carry
        return kb - 1, more(kb - 1, step(kb, False))

    lax.while_loop(lambda c: c[1], body, (qb - 1, more(qb - 1, step(qb, True))))


def _attn_b(proj, tri, batch, seq):
    n = batch * seq
    nq = seq // QB_B
    return pl.pallas_call(
        _attn_b_kernel,
        grid=(batch, nq),
        in_specs=[pl.BlockSpec((QB_B, D_B), lambda b, q: (b * nq + q, COL_QB // D_B)),
                  pl.BlockSpec((seq, D_B), lambda b, q: (b, COL_KB // D_B)),
                  pl.BlockSpec((seq, D_B), lambda b, q: (b, COL_VB // D_B)),
                  pl.BlockSpec(tri.shape, lambda b, q: (0, 0))],
        out_specs=pl.BlockSpec((QB_B, D_B), lambda b, q: (b * nq + q, 0)),
        out_shape=jax.ShapeDtypeStruct((n, D_B), F32),
        scratch_shapes=[pltpu.VMEM((N_HEADS_B, QB_B, QB_B), F32),
                        pltpu.VMEM((N_HEADS_B, QB_B, QB_B), F32),
                        pltpu.VMEM((N_HEADS_B, QB_B, 2 * QB_B), F32)],
        compiler_params=_cparams(("parallel", "arbitrary")),
        name="stick_breaking_attention",
    )(proj, proj, proj, tri)


def _out_proj_kernel(x_ref, oa_ref, ob_ref, gnb_ref, w_ref, g1_ref, ln_ref, sc_ref, sh_ref, rw_ref,
                     x1_ref, h2_ref, lg_ref):
    obn = (_rms(ob_ref[...]) * gnb_ref[...]).astype(BF16)
    y = jnp.dot(oa_ref[...], w_ref[0:D_A, :], preferred_element_type=F32)
    y = y + jnp.dot(obn, w_ref[D_A:D_A + D_B, :], preferred_element_type=F32)
    x1 = x_ref[...] + g1_ref[0] * y
    x1_ref[...] = x1
    h2 = _rms(x1) * ln_ref[...]
    h2 = h2 * (1.0 + sc_ref[0]) + sh_ref[0]
    h2_ref[...] = _pack_bf16_pairs(h2)
    hh = h2.astype(BF16)
    hl = (h2 - hh.astype(F32)).astype(BF16)
    rw = rw_ref[...]
    rh = rw.astype(BF16)
    rl = (rw - rh.astype(F32)).astype(BF16)
    both = jnp.dot(hh, jnp.concatenate([rh, rl], axis=1), preferred_element_type=F32)
    lg_ref[...] = both[:, :LANES] + both[:, LANES:] + jnp.dot(hl, rh, preferred_element_type=F32)


def _out_proj(x2, oa, ob, gn_b, w_out, g1, ln_g, sc, sh, rw, seq):
    n, d = x2.shape
    tm = 256
    tpb = seq // tm
    row = lambda i: (i, 0)
    fixed = lambda i: (0, 0)
    perb = lambda i: (i // tpb, 0, 0)
    return pl.pallas_call(
        _out_proj_kernel,
        grid=(n // tm,),
        in_specs=[pl.BlockSpec((tm, d), row),
                  pl.BlockSpec((tm, D_A), row),
                  pl.BlockSpec((tm, D_B), row),
                  pl.BlockSpec((1, D_B), fixed),
                  pl.BlockSpec(w_out.shape, fixed),
                  pl.BlockSpec((1, 1, d), perb),
                  pl.BlockSpec((1, d), fixed),
                  pl.BlockSpec((1, 1, d), perb),
                  pl.BlockSpec((1, 1, d), perb),
                  pl.BlockSpec(rw.shape, fixed)],
        out_specs=[pl.BlockSpec((tm, d), row),
                   pl.BlockSpec((tm, d // 2), row),
                   pl.BlockSpec((tm, LANES), row)],
        out_shape=[jax.ShapeDtypeStruct((n, d), F32),
                   jax.ShapeDtypeStruct((n, d // 2), I32),
                   jax.ShapeDtypeStruct((n, LANES), F32)],
        compiler_params=_cparams(("parallel",)),
        name="out_proj_ln2_router",
    )(x2, oa, ob, gn_b, w_out, g1, ln_g, sc, sh, rw)


def _route_kernel(lg_ref, info_ref, cnt_ref):
    @pl.when(pl.program_id(0) == 0)
    def _():
        cnt_ref[...] = jnp.zeros(cnt_ref.shape, F32)

    lg = lg_ref[...]
    lane = lax.broadcasted_iota(I32, lg.shape, 1)
    lanef = lane.astype(F32)
    big = float(4 * LANES)
    gm = jnp.where(lane >= N_EXPERTS, jnp.where(lane < N_EXPERTS + N_GROUPS, 1.0, 0.0), 0.0) > 0.5
    lgm = jnp.where(gm, lg, NEG)
    mg = jnp.max(lgm, axis=1, keepdims=True)
    eg = jnp.where(gm, jnp.exp(lgm - mg), 0.0)
    pg = eg / jnp.sum(eg, axis=1, keepdims=True)
    gw = jnp.max(pg, axis=1, keepdims=True)
    gidx = jnp.min(jnp.where(gm, jnp.where(pg == gw, lanef - N_EXPERTS, big), big), axis=1, keepdims=True)
    lane_group = (lane // EXPERTS_PER_GROUP).astype(F32)
    em = jnp.where(lane < N_EXPERTS, jnp.where(lane_group == gidx, 1.0, 0.0), 0.0) > 0.5
    lem = jnp.where(em, lg, NEG)
    me = jnp.max(lem, axis=1, keepdims=True)
    ee = jnp.where(em, jnp.exp(lem - me), 0.0)
    pe = jnp.where(em, ee / jnp.sum(ee, axis=1, keepdims=True), -1.0)
    p1 = jnp.max(pe, axis=1, keepdims=True)
    i1 = jnp.min(jnp.where(pe == p1, lanef, big), axis=1, keepdims=True)
    pe2 = jnp.where(lanef == i1, -1.0, pe)
    p2 = jnp.max(pe2, axis=1, keepdims=True)
    i2 = jnp.min(jnp.where(pe2 == p2, lanef, big), axis=1, keepdims=True)
    den = p1 + p2
    g0 = gw * p1 / den
    g1 = gw * p2 / den
    info = jnp.where(lane == 0, i1, jnp.where(lane == 1, i2,
                     jnp.where(lane == 2, g0, jnp.where(lane == 3, g1, 0.0))))
    info_ref[...] = info
    oh = jnp.where(lanef == i1, 1.0, 0.0) + jnp.where(lanef == i2, 1.0, 0.0)
    cnt_ref[...] += jnp.sum(oh, axis=0, keepdims=True)


def _route(lg):
    n = lg.shape[0]
    tm = min(1024, n)
    return pl.pallas_call(
        _route_kernel,
        grid=(n // tm,),
        in_specs=[pl.BlockSpec((tm, LANES), lambda i: (i, 0))],
        out_specs=[pl.BlockSpec((tm, LANES), lambda i: (i, 0)),
                   pl.BlockSpec((1, LANES), lambda i: (0, 0))],
        out_shape=[jax.ShapeDtypeStruct((n, LANES), F32),
                   jax.ShapeDtypeStruct((1, LANES), F32)],
        compiler_params=_cparams(("arbitrary",)),
        name="moe_route",
    )(lg)


def _plan_kernel(cnt_ref, ps_ref, be_ref, nu_ref, *, nblk):
    def fill(i, c):
        be_ref[i] = N_EXPERTS - 1
        return c

    lax.fori_loop(0, nblk, fill, 0)

    def per_expert(e, pos):
        ps_ref[e] = pos * MOE_BLK
        nb = (cnt_ref[e] + MOE_BLK - 1) // MOE_BLK

        def mark(k, c):
            be_ref[pos + k] = e
            return c

        lax.fori_loop(0, nb, mark, 0)
        return pos + nb

    nu_ref[0] = lax.fori_loop(0, N_EXPERTS, per_expert, jnp.int32(0))


def _plan(counts, nblk):
    smem = pl.BlockSpec(memory_space=pltpu.SMEM)
    return pl.pallas_call(
        functools.partial(_plan_kernel, nblk=nblk),
        in_specs=[smem],
        out_specs=[smem, smem, smem],
        out_shape=[jax.ShapeDtypeStruct((N_EXPERTS,), I32),
                   jax.ShapeDtypeStruct((nblk,), I32),
                   jax.ShapeDtypeStruct((1,), I32)],
        name="moe_block_plan",
    )(counts)


def _dest_kernel(info_ref, ps_ref, tri_ref, o_ref, carry_scr):
    @pl.when(pl.program_id(0) == 0)
    def _():
        carry_scr[...] = jnp.zeros(carry_scr.shape, F32)

    info = info_ref[...]
    lane = lax.broadcasted_iota(I32, info.shape, 1)
    lanef = lane.astype(F32)
    o1 = jnp.where(lanef == info[:, 0:1], 1.0, 0.0)
    o2 = jnp.where(lanef == info[:, 1:2], 1.0, 0.0)
    oh = o1 + o2
    before = jnp.dot(tri_ref[...], oh.astype(BF16), preferred_element_type=F32)
    base = before + carry_scr[...] + ps_ref[...]
    d1 = jnp.sum(o1 * base, axis=1, keepdims=True)
    d2 = jnp.sum(o2 * base, axis=1, keepdims=True)
    o_ref[...] = jnp.where(lane == 0, d1, jnp.where(lane == 1, d2, 0.0))
    carry_scr[...] += jnp.sum(oh, axis=0, keepdims=True)


def _dest(info, ps_lanes, tri):
    n = info.shape[0]
    tm = tri.shape[0]
    return pl.pallas_call(
        _dest_kernel,
        grid=(n // tm,),
        in_specs=[pl.BlockSpec((tm, LANES), lambda i: (i, 0)),
                  pl.BlockSpec((1, LANES), lambda i: (0, 0)),
                  pl.BlockSpec((tm, tm), lambda i: (0, 0))],
        out_specs=pl.BlockSpec((tm, LANES), lambda i: (i, 0)),
        out_shape=jax.ShapeDtypeStruct((n, LANES), F32),
        scratch_shapes=[pltpu.VMEM((1, LANES), F32)],
        compiler_params=_cparams(("arbitrary",)),
        name="moe_dest_rows",
    )(info, ps_lanes, tri)


SCATTER_SLOTS = 3


def _scatter_kernel(d0_ref, d1_ref, cnt_ref, ps_ref, nu_ref, h_ref, xb_ref,
                    zero_scr, stage_scr, load_sem, out_sem, zero_sem, *, nblk):
    n_tiles = h_ref.shape[0] // TOK_TILE
    zero_scr[...] = jnp.zeros(zero_scr.shape, zero_scr.dtype)
    pad_sizes = [s for s in (1 << k for k in range(MOE_BLK.bit_length() - 2, -1, -1))
                 if s >= SUBLANES]

    def zero_copy(rows, dst_row):
        return pltpu.make_async_copy(zero_scr.at[pl.ds(0, rows)], xb_ref.at[pl.ds(dst_row, rows)],
                                     zero_sem)

    def load(j, slot):
        return pltpu.make_async_copy(h_ref.at[pl.ds(j * TOK_TILE, TOK_TILE)], stage_scr.at[slot],
                                     load_sem.at[slot])

    def row_copy(slot, r, dst_row, parity, rows=1):
        return pltpu.make_async_copy(stage_scr.at[slot, pl.ds(r, rows)],
                                     xb_ref.at[pl.ds(dst_row, rows)], out_sem.at[parity])

    def for_each_zero_copy(act):
        def per_expert(e, c):
            pos = ps_ref[e] + cnt_ref[e]
            pad = (MOE_BLK - (cnt_ref[e] & (MOE_BLK - 1))) & (MOE_BLK - 1)
            head = pad & (SUBLANES - 1)
            for k in range(SUBLANES - 1):
                @pl.when(k < head)
                def _():
                    act(zero_copy(1, pos + k))
            pos = pl.multiple_of(pos + head, SUBLANES)
            for size in pad_sizes:
                @pl.when((pad & size) != 0)
                def _():
                    act(zero_copy(size, pos))
                pos = pl.multiple_of(pos + (pad & size), SUBLANES)
            return c

        def per_tail_block(i, c):
            act(zero_copy(MOE_BLK, i * MOE_BLK))
            return c

        lax.fori_loop(0, N_EXPERTS, per_expert, 0)
        lax.fori_loop(nu_ref[0], nblk, per_tail_block, 0)

    for_each_zero_copy(lambda cp: cp.start())

    def retire(parity):
        row_copy(0, 0, 0, parity, TOK_TILE).wait()
        row_copy(0, 0, 0, parity, TOK_TILE).wait()

    def tile(j, c):
        slot = j % SCATTER_SLOTS
        parity = j % 2
        load(j, slot).wait()

        @pl.when(j + 1 < n_tiles)
        def _():
            load(j + 1, (j + 1) % SCATTER_SLOTS).start()

        def issue(g, cc):
            for k in range(SUBLANES):
                r = g * SUBLANES + k
                t = j * TOK_TILE + r
                row_copy(slot, r, d0_ref[t], parity).start()
                row_copy(slot, r, d1_ref[t], parity).start()
            return cc

        lax.fori_loop(0, TOK_TILE // SUBLANES, issue, 0)

        @pl.when(j >= 1)
        def _():
            retire(1 - parity)
        return c

    load(0, 0).start()
    lax.fori_loop(0, n_tiles, tile, 0)
    retire((n_tiles - 1) % 2)
    for_each_zero_copy(lambda cp: cp.wait())


def _scatter_rows(d0, d1, counts, ps, nu, h2, p_rows):
    n, d = h2.shape
    smem = pl.BlockSpec(memory_space=pltpu.SMEM)
    return pl.pallas_call(
        functools.partial(_scatter_kernel, nblk=p_rows // MOE_BLK),
        in_specs=[smem, smem, smem, smem, smem, pl.BlockSpec(memory_space=pl.ANY)],
        out_specs=pl.BlockSpec(memory_space=pl.ANY),
        out_shape=jax.ShapeDtypeStruct((p_rows, d), h2.dtype),
        scratch_shapes=[pltpu.VMEM((MOE_BLK, d), h2.dtype),
                        pltpu.VMEM((SCATTER_SLOTS, TOK_TILE, d), h2.dtype),
                        pltpu.SemaphoreType.DMA((SCATTER_SLOTS,)),
                        pltpu.SemaphoreType.DMA((2,)),
                        pltpu.SemaphoreType.DMA(())],
        compiler_params=_cparams(),
        name="moe_scatter_rows",
    )(d0, d1, counts, ps, nu, h2)


def _expert_kernel(be_ref, nu_ref, x_ref, wg_hbm, wu_hbm, wd_hbm, o_ref,
                   wg_f32, wu_f32, wd_f32, wg_scr, wu_scr, wd_scr, slot_ref, sem):
    i = pl.program_id(0)
    n_used = nu_ref[0]
    expert = be_ref[i]
    new_expert = jnp.logical_or(i == 0, expert != be_ref[jnp.maximum(i - 1, 0)])

    def weight_copies(e, slot):
        return [pltpu.make_async_copy(src.at[e], dst.at[slot], sem.at[slot])
                for src, dst in ((wg_hbm, wg_f32), (wu_hbm, wu_f32), (wd_hbm, wd_f32))]

    @pl.when(i == 0)
    def _():
        slot_ref[0] = 0
        for cp in weight_copies(expert, 0):
            cp.start()

    @pl.when(jnp.logical_and(new_expert, i < n_used))
    def _():
        slot = slot_ref[0]
        for cp in weight_copies(expert, slot):
            cp.wait()
        nxt = lax.while_loop(lambda j: jnp.logical_and(j < n_used, be_ref[jnp.minimum(j, n_used - 1)] == expert),
                             lambda j: j + 1, i + 1)

        @pl.when(nxt < n_used)
        def _():
            for cp in weight_copies(be_ref[nxt], 1 - slot):
                cp.start()

        wg_scr[...] = wg_f32[slot].astype(BF16)
        wu_scr[...] = wu_f32[slot].astype(BF16)
        wd_scr[...] = wd_f32[slot].astype(BF16)
        slot_ref[0] = 1 - slot

    @pl.when(i < n_used)
    def _():
        x_hi, x_lo = _unpack_bf16_pairs(x_ref[...])
        x = jnp.concatenate([x_hi.astype(BF16), x_lo.astype(BF16)], axis=1)
        g = jnp.dot(x, wg_scr[...], preferred_element_type=F32)
        u = jnp.dot(x, wu_scr[...], preferred_element_type=F32)
        act = (g * (1.0 / (1.0 + jnp.exp(-g))) * u).astype(BF16)
        o_ref[...] = _pack_bf16_pairs(jnp.dot(act, wd_scr[...], preferred_element_type=F32))

    @pl.when(i >= nu_ref[0])
    def _():
        o_ref[...] = jnp.zeros(o_ref.shape, o_ref.dtype)


def _experts(be, nu, xb, wg, wu, wd):
    p = xb.shape[0]
    _, d, de = wg.shape
    return pl.pallas_call(
        _expert_kernel,
        grid_spec=pltpu.PrefetchScalarGridSpec(
            num_scalar_prefetch=2,
            grid=(p // MOE_BLK,),
            in_specs=[pl.BlockSpec((MOE_BLK, d // 2), lambda i, be, nu: (i, 0)),
                      pl.BlockSpec(memory_space=pl.ANY),
                      pl.BlockSpec(memory_space=pl.ANY),
                      pl.BlockSpec(memory_space=pl.ANY)],
            out_specs=pl.BlockSpec((MOE_BLK, d // 2), lambda i, be, nu: (i, 0)),
            scratch_shapes=[pltpu.VMEM((2, d, de), F32), pltpu.VMEM((2, d, de), F32),
                            pltpu.VMEM((2, de, d), F32),
                            pltpu.VMEM((d, de), BF16), pltpu.VMEM((d, de), BF16),
                            pltpu.VMEM((de, d), BF16),
                            pltpu.SMEM((1,), I32),
                            pltpu.SemaphoreType.DMA((2,))]),
        out_shape=jax.ShapeDtypeStruct((p, d // 2), I32),
        compiler_params=_cparams(("arbitrary",)),
        name="moe_expert_ffn",
    )(be, nu, xb, wg, wu, wd)


def _combine_kernel(d0_ref, d1_ref, x_ref, info_ref, g2_ref, yb_ref, o_ref, rows_scr, sem):
    i = pl.program_id(0)
    cur = i % 2

    def copy(buf, which, r, src_row, rows=1):
        return pltpu.make_async_copy(yb_ref.at[pl.ds(src_row, rows)],
                                     rows_scr.at[buf, which, pl.ds(r, rows)], sem.at[buf])

    def start_gather(step, buf):
        base = step * TOK_TILE

        def issue(g, c):
            for k in range(SUBLANES):
                r = g * SUBLANES + k
                copy(buf, 0, r, d0_ref[base + r]).start()
                copy(buf, 1, r, d1_ref[base + r]).start()
            return c

        lax.fori_loop(0, TOK_TILE // SUBLANES, issue, 0)

    @pl.when(i == 0)
    def _():
        start_gather(0, 0)

    @pl.when(i + 1 < pl.num_programs(0))
    def _():
        start_gather(i + 1, 1 - cur)

    copy(cur, 0, 0, 0, TOK_TILE).wait()
    copy(cur, 1, 0, 0, TOK_TILE).wait()

    info = info_ref[...]
    half = o_ref.shape[1] // 2
    y0_hi, y0_lo = _unpack_bf16_pairs(rows_scr[cur, 0])
    y1_hi, y1_lo = _unpack_bf16_pairs(rows_scr[cur, 1])
    g2 = g2_ref[0]
    o_ref[:, :half] = x_ref[:, :half] + g2[:, :half] * (info[:, 2:3] * y0_hi + info[:, 3:4] * y1_hi)
    o_ref[:, half:] = x_ref[:, half:] + g2[:, half:] * (info[:, 2:3] * y0_lo + info[:, 3:4] * y1_lo)


def _combine(d0, d1, x1, info, g2, yb, seq):
    n, d = x1.shape
    tpb = seq // TOK_TILE
    return pl.pallas_call(
        _combine_kernel,
        grid_spec=pltpu.PrefetchScalarGridSpec(
            num_scalar_prefetch=2,
            grid=(n // TOK_TILE,),
            in_specs=[pl.BlockSpec((TOK_TILE, d), lambda i, a, b: (i, 0)),
                      pl.BlockSpec((TOK_TILE, LANES), lambda i, a, b: (i, 0)),
                      pl.BlockSpec((1, 1, d), lambda i, a, b: (i // tpb, 0, 0)),
                      pl.BlockSpec(memory_space=pl.ANY)],
            out_specs=pl.BlockSpec((TOK_TILE, d), lambda i, a, b: (i, 0)),
            scratch_shapes=[pltpu.VMEM((2, 2, TOK_TILE, d // 2), I32),
                            pltpu.SemaphoreType.DMA((2,))]),
        out_shape=jax.ShapeDtypeStruct((n, d), F32),
        compiler_params=_cparams(("arbitrary",)),
        name="moe_combine",
    )(d0, d1, x1, info, g2, yb)


def _tri_inclusive_rev(k):
    l = (np.arange(k)[:, None] >= np.arange(k)[None, :]).astype(np.float32)
    half = np.concatenate([l, np.ones((k, k), np.float32)], axis=1)
    return jnp.asarray(np.concatenate([half, half], axis=0), dtype=BF16)


def _tri_strict_lower(k):
    return jnp.asarray((np.arange(k)[None, :] < np.arange(k)[:, None]).astype(np.float32), dtype=BF16)


def kernel(x, c, w_mod, b_mod, ln1_g, w_in, q_norm_g, w_q_up, q_gain, k_gain, rel_bias, gn_a, gn_b,
           w_out, ln2_g, router_g, router_e, w_gate, w_up, w_down):
    batch, seq, d = x.shape
    n = batch * seq
    assert w_mod.shape[0] == 1 and d == D_A + D_B
    assert seq % KB_A == 0 and KB_A % QB_A == 0 and seq % TOK_TILE == 0 and n % 512 == 0
    x2 = x.reshape(n, d)

    c8 = jnp.pad(c, ((0, 8 - batch), (0, 0)))
    mod = _modulation(c8, w_mod.reshape(d, -1), b_mod.reshape(1, -1))[:batch]
    sh1, sc1, g1, sh2, sc2, g2 = [m.reshape(batch, 1, d) for m in jnp.split(mod, 6, axis=-1)]

    wi = w_in.reshape(d, -1).astype(BF16)
    n_a = Q_RANK + 2 * HEAD_DIM + IDX_DIM + IDX_HEADS
    wi = jnp.concatenate([wi[:, n_a:], wi[:, :n_a],
                          jnp.zeros((d, D_IN_PAD - wi.shape[1]), wi.dtype)], axis=1)
    proj = _ln_proj(x2, ln1_g.reshape(1, d), sc1, sh1, wi, seq)

    qa, qidx, ka, va, kidx, widx = _aprep(
        proj, n, q_norm_g.reshape(1, -1), w_q_up.reshape(Q_RANK, -1).astype(BF16),
        q_gain.reshape(1, -1), k_gain.reshape(1, -1))

    bt = _bias_tiles(rel_bias)
    bound = _logit_bound(q_gain.reshape(1, -1), k_gain.reshape(1, -1), rel_bias)[0, :1]
    oa = _attn_a(qidx, widx.T, qa, kidx, ka, va, bt, gn_a.reshape(1, -1), bound, batch, seq)
    ob = _attn_b(proj, _tri_inclusive_rev(QB_B), batch, seq)

    rw = jnp.concatenate([router_e.reshape(d, -1), router_g.reshape(d, -1),
                          jnp.zeros((d, LANES - N_EXPERTS - N_GROUPS), F32)], axis=1)
    x1, h2, lg = _out_proj(x2, oa, ob, gn_b.reshape(1, -1), w_out.reshape(d, d).astype(BF16),
                           g1, ln2_g.reshape(1, d), sc2, sh2, rw, seq)

    info, cnt = _route(lg)
    counts = cnt[0, :N_EXPERTS].astype(I32)
    p_rows = 2 * n + N_EXPERTS * MOE_BLK
    ps, be, nu = _plan(counts, p_rows // MOE_BLK)
    ps_lanes = jnp.pad(ps.astype(F32), (0, LANES - N_EXPERTS)).reshape(1, LANES)
    dinfo = _dest(info, ps_lanes, _tri_strict_lower(512))
    d0 = dinfo[:, 0].astype(I32)
    d1 = dinfo[:, 1].astype(I32)
    xb = _scatter_rows(d0, d1, counts, ps, nu, h2, p_rows)
    yb = _experts(be, nu, xb,
                  w_gate.reshape(N_EXPERTS, d, D_EXPERT),
                  w_up.reshape(N_EXPERTS, d, D_EXPERT),
                  w_down.reshape(N_EXPERTS, D_EXPERT, d))
    out = _combine(d0, d1, x1, info, g2, yb, seq)
    return out.reshape(batch, seq, d)
```

```python
import functools
import math

import numpy as np
import jax
import jax.numpy as jnp
from jax import lax
from jax.experimental import pallas as pl
from jax.experimental.pallas import tpu as pltpu

F32 = jnp.float32
BF16 = jnp.bfloat16
I32 = jnp.int32

HEAD_DIM = 128
N_HEADS_A = 8
N_HEADS_B = 8
D_A = N_HEADS_A * HEAD_DIM
D_B = N_HEADS_B * HEAD_DIM
Q_RANK = 512
IDX_HEADS = 16
IDX_DIM = 64
TOPK_MAX = 256
N_BUCKETS = 32
MAX_DISTANCE = 128
N_GROUPS = 4
EXPERTS_PER_GROUP = 8
N_EXPERTS = N_GROUPS * EXPERTS_PER_GROUP
D_EXPERT = 512
EPS = 1e-6

LANES = 128
SUBLANES = 8
VMEM_LIMIT = 56 * 1024 * 1024
NEG = -1e30
INT_MIN = -(2 ** 31)
EXP2_UNDERFLOW = 150.0
LOG2E = math.log2(math.e)
PLAIN_SOFTMAX_LIMIT = 64.0
GUESS_HALF_WIDTH = 0.1
EXTRACT_BELOW = 3.0

QB_A = 128
KB_A = 256
QB_B = 128
MOE_BLK = 256
TOK_TILE = 256

COL_QB, COL_KB, COL_VB, COL_A = 0, D_B, 2 * D_B, 3 * D_B
A_CQ, A_KA, A_VA, A_KIDX, A_WIDX = 0, 512, 640, 768, 832
A_WIDTH = 1024
D_IN_PAD = COL_A + A_WIDTH


def _cparams(sem=None):
    return pltpu.CompilerParams(dimension_semantics=sem, vmem_limit_bytes=VMEM_LIMIT)


def _rms(x):
    return x * lax.rsqrt(jnp.mean(x * x, axis=-1, keepdims=True) + EPS)


def _pack_bf16_pairs(x):
    c = x.shape[1] // 2
    hi = pltpu.bitcast(x[:, :c].astype(BF16).astype(F32), I32)
    lo = pltpu.bitcast(x[:, c:].astype(BF16).astype(F32), I32)
    return hi | lax.shift_right_logical(lo, 16)


def _unpack_bf16_pairs(u):
    hi = pltpu.bitcast(u & jnp.int32(-65536), F32)
    lo = pltpu.bitcast(lax.shift_left(u, 16), F32)
    return hi, lo


def _mod_kernel(c_ref, w_ref, b_ref, o_ref):
    c = c_ref[...]
    s = c * (1.0 / (1.0 + jnp.exp(-c)))
    o_ref[...] = jnp.dot(s, w_ref[...], preferred_element_type=F32,
                         precision=lax.Precision.HIGHEST) + b_ref[...]


def _modulation(c8, w_mod, b_mod):
    d, n6 = w_mod.shape
    tn = 1024
    return pl.pallas_call(
        _mod_kernel,
        grid=(n6 // tn,),
        in_specs=[pl.BlockSpec((8, d), lambda j: (0, 0)),
                  pl.BlockSpec((d, tn), lambda j: (0, j)),
                  pl.BlockSpec((1, tn), lambda j: (0, j))],
        out_specs=pl.BlockSpec((8, tn), lambda j: (0, j)),
        out_shape=jax.ShapeDtypeStruct((8, n6), F32),
        compiler_params=_cparams(("arbitrary",)),
        name="modulation",
    )(c8, w_mod, b_mod)


def _ln_proj_kernel(x_ref, g_ref, sc_ref, sh_ref, w_ref, o_ref, h_scr, *, q_tiles, chunk):
    i = pl.program_id(0)
    j = pl.program_id(1)

    def normalise_chunk():
        rows = pl.ds(pl.multiple_of(j * chunk, chunk), chunk)
        h = _rms(x_ref[rows, :]) * g_ref[...]
        h = h * (1.0 + sc_ref[0]) + sh_ref[0]
        h_scr[i % 2, rows, :] = h.astype(BF16)

    @pl.when(i == 0)
    def _():
        normalise_chunk()
        o_ref[...] = jnp.zeros(o_ref.shape, o_ref.dtype)

    @pl.when(i > 0)
    def _():
        normalise_chunk()
        col_scale = jnp.where(j < q_tiles, HEAD_DIM ** -0.5 * LOG2E, 1.0)
        acc = jnp.dot(h_scr[(i - 1) % 2], w_ref[...], preferred_element_type=F32)
        o_ref[...] = (acc * col_scale).astype(o_ref.dtype)


def _ln_proj(x2, ln_g, sc, sh, w, seq):
    n, d = x2.shape
    ncol = w.shape[1]
    tm = min(1024, seq)
    tn = 1024
    tpb = seq // tm
    n_i, n_j = n // tm, ncol // tn
    assert COL_QB == 0 and D_B % tn == 0 and tm % (n_j * 16) == 0
    last = n_i - 1
    return pl.pallas_call(
        functools.partial(_ln_proj_kernel, q_tiles=D_B // tn, chunk=tm // n_j),
        grid=(n_i + 1, n_j),
        in_specs=[pl.BlockSpec((tm, d), lambda i, j: (jnp.minimum(i, last), 0)),
                  pl.BlockSpec((1, d), lambda i, j: (0, 0)),
                  pl.BlockSpec((1, 1, d), lambda i, j: (jnp.minimum(i, last) // tpb, 0, 0)),
                  pl.BlockSpec((1, 1, d), lambda i, j: (jnp.minimum(i, last) // tpb, 0, 0)),
                  pl.BlockSpec((d, tn), lambda i, j: (0, j))],
        out_specs=pl.BlockSpec((tm, tn), lambda i, j: (jnp.where(i == 0, n_i, i - 1), j)),
        out_shape=jax.ShapeDtypeStruct((n + tm, ncol), BF16),
        scratch_shapes=[pltpu.VMEM((2, tm, d), BF16)],
        compiler_params=_cparams(("arbitrary", "arbitrary")),
        name="ln_in_proj",
    )(x2, ln_g, sc, sh, w)


def _aprep_kernel(a_ref, qng_ref, wq_ref, qg_ref, kg_ref,
                  qa_ref, qidx_ref, ka_ref, va_ref, kidx_ref, widx_ref):
    cq = a_ref[:, A_CQ:A_CQ + Q_RANK].astype(F32)
    cqn = (_rms(cq) * qng_ref[...]).astype(BF16)
    qup = jnp.dot(cqn, wq_ref[...], preferred_element_type=F32)
    for h in range(N_HEADS_A):
        qh = qup[:, h * HEAD_DIM:(h + 1) * HEAD_DIM]
        qn = _rms(qh) * qg_ref[...] * (HEAD_DIM ** -0.5 * LOG2E)
        qa_ref[:, h * HEAD_DIM:(h + 1) * HEAD_DIM] = qn.astype(BF16)
    for h in range(IDX_HEADS):
        qi = qup[:, D_A + h * IDX_DIM:D_A + (h + 1) * IDX_DIM] * (IDX_DIM ** -0.5)
        qidx_ref[h] = qi.astype(BF16)
    ka = a_ref[:, A_KA:A_KA + HEAD_DIM].astype(F32)
    ka_ref[...] = (_rms(ka) * kg_ref[...]).astype(BF16)
    va_ref[...] = a_ref[:, A_VA:A_VA + HEAD_DIM]
    kidx_ref[...] = a_ref[:, A_KIDX:A_KIDX + IDX_DIM]
    widx_ref[...] = a_ref[:, A_WIDX:A_WIDX + IDX_HEADS].astype(F32) * (IDX_HEADS ** -0.5)


def _aprep(proj, n, q_norm_g, w_q_up, q_gain, k_gain):
    tm = 512
    cblk = COL_A // A_WIDTH
    nup = w_q_up.shape[1]
    return pl.pallas_call(
        _aprep_kernel,
        grid=(n // tm,),
        in_specs=[pl.BlockSpec((tm, A_WIDTH), lambda i: (i, cblk)),
                  pl.BlockSpec((1, Q_RANK), lambda i: (0, 0)),
                  pl.BlockSpec((Q_RANK, nup), lambda i: (0, 0)),
                  pl.BlockSpec((1, HEAD_DIM), lambda i: (0, 0)),
                  pl.BlockSpec((1, HEAD_DIM), lambda i: (0, 0))],
        out_specs=[pl.BlockSpec((tm, D_A), lambda i: (i, 0)),
                   pl.BlockSpec((IDX_HEADS, tm, IDX_DIM), lambda i: (0, i, 0)),
                   pl.BlockSpec((tm, HEAD_DIM), lambda i: (i, 0)),
                   pl.BlockSpec((tm, HEAD_DIM), lambda i: (i, 0)),
                   pl.BlockSpec((tm, IDX_DIM), lambda i: (i, 0)),
                   pl.BlockSpec((tm, IDX_HEADS), lambda i: (i, 0))],
        out_shape=[jax.ShapeDtypeStruct((n, D_A), BF16),
                   jax.ShapeDtypeStruct((IDX_HEADS, n, IDX_DIM), BF16),
                   jax.ShapeDtypeStruct((n, HEAD_DIM), BF16),
                   jax.ShapeDtypeStruct((n, HEAD_DIM), BF16),
                   jax.ShapeDtypeStruct((n, IDX_DIM), BF16),
                   jax.ShapeDtypeStruct((n, IDX_HEADS), F32)],
        compiler_params=_cparams(("parallel",)),
        name="group_a_prep",
    )(proj, q_norm_g, w_q_up, q_gain, k_gain)


def _t5_bucket_starts():
    max_exact = N_BUCKETS // 2
    d = np.arange(0, 4 * MAX_DISTANCE, dtype=np.int64)
    df = np.maximum(d, 1).astype(np.float32)
    large = max_exact + (np.log(df / np.float32(max_exact)) / np.float32(math.log(MAX_DISTANCE / max_exact))
                         * np.float32(N_BUCKETS - max_exact)).astype(np.int32)
    large = np.minimum(large, N_BUCKETS - 1)
    bucket = np.where(d < max_exact, d, large)
    assert np.all(np.diff(bucket) >= 0) and bucket[-1] == N_BUCKETS - 1
    return [int(np.argmax(bucket >= b)) for b in range(N_BUCKETS)]


_BUCKET_START = _t5_bucket_starts()
N_BIAS_TILES = 2 * KB_A // LANES


def _bias_kernel(rb_ref, o_ref):
    di = pl.program_id(0)
    h = pl.program_id(1)
    i = lax.broadcasted_iota(I32, (QB_A, KB_A), 0)
    j = lax.broadcasted_iota(I32, (QB_A, KB_A), 1)
    d = di * LANES + i - j
    val = jnp.full((QB_A, KB_A), rb_ref[0, h], F32)
    for b in range(1, N_BUCKETS):
        val = jnp.where(d >= _BUCKET_START[b], rb_ref[b, h], val)
    o_ref[0, 0] = (val - rb_ref[N_BUCKETS - 1, h]) * LOG2E


def _bias_tiles(rel_bias):
    return pl.pallas_call(
        _bias_kernel,
        grid=(N_BIAS_TILES, N_HEADS_A),
        in_specs=[pl.BlockSpec(memory_space=pltpu.SMEM)],
        out_specs=pl.BlockSpec((1, 1, QB_A, KB_A), lambda a, h: (a, h, 0, 0)),
        out_shape=jax.ShapeDtypeStruct((N_BIAS_TILES, N_HEADS_A, QB_A, KB_A), F32),
        compiler_params=_cparams(("arbitrary", "arbitrary")),
        name="t5_bias_tiles",
    )(rel_bias)


def _bound_kernel(qg_ref, kg_ref, rb_ref, o_ref):
    qmax = jnp.max(jnp.abs(qg_ref[...]), axis=1, keepdims=True)
    kmax = jnp.max(jnp.abs(kg_ref[...]), axis=1, keepdims=True)
    rb = rb_ref[...]
    shifted = jnp.abs(rb - rb[N_BUCKETS - 1:N_BUCKETS, :])
    bmax = jnp.max(jnp.max(shifted, axis=1, keepdims=True), axis=0, keepdims=True)
    bound = qmax * kmax * (math.sqrt(HEAD_DIM) * 1.02) + bmax
    o_ref[...] = jnp.broadcast_to(bound, o_ref.shape)


def _logit_bound(q_gain, k_gain, rel_bias):
    return pl.pallas_call(
        _bound_kernel,
        out_shape=jax.ShapeDtypeStruct((1, LANES), F32),
        name="dsa_logit_bound",
    )(q_gain, k_gain, rel_bias)


def _attn_a_kernel(qidx_ref, w_ref, qa_ref, kidx_ref, ka_ref, va_ref, bt_ref, gn_ref, bound_ref, o_ref,
                   keys_scr, keyt_scr, m_scr, acc_scr, s_scr, mb_scr, *, topk):
    qb = pl.program_id(1)
    t0 = qb * QB_A
    kbl = (t0 + QB_A - 1) // KB_A
    row = t0 + lax.broadcasted_iota(I32, (QB_A, KB_A), 0)
    col0 = lax.broadcasted_iota(I32, (QB_A, KB_A), 1)
    nt = (((1,), (1,)), ((), ()))

    reps = KB_A // LANES

    def sort_key(v):
        bits = pltpu.bitcast(v, I32)
        return jnp.where(bits < 0, bits ^ jnp.int32(0x7FFFFFFF), bits)

    def key_value(k):
        return pltpu.bitcast(jnp.where(k < 0, k ^ jnp.int32(0x7FFFFFFF), k), F32)

    key_pos = lax.broadcasted_iota(I32, (KB_A, QB_A), 0)
    qry_pos = t0 + lax.broadcasted_iota(I32, (KB_A, QB_A), 1)

    def score_tile(kb, carry, diagonal):
        smin, smax, s1, s2 = carry
        kt = kidx_ref[pl.ds(pl.multiple_of(kb * KB_A, KB_A), KB_A), :]
        score = jnp.zeros((KB_A, QB_A), F32)
        for hp in range(IDX_HEADS // 2):
            q2 = qidx_ref[2 * hp:2 * hp + 2].reshape(2 * QB_A, IDX_DIM)
            sc = lax.dot_general(kt, q2, nt, preferred_element_type=F32)
            score = score + w_ref[2 * hp:2 * hp + 1, :] * jnp.maximum(sc[:, :QB_A], 0.0)
            score = score + w_ref[2 * hp + 1:2 * hp + 2, :] * jnp.maximum(sc[:, QB_A:], 0.0)
        key_t = sort_key(score)
        low, high, live = score, score, score
        if diagonal:
            causal = (kb * KB_A + key_pos) <= qry_pos
            key_t = jnp.where(causal, key_t, jnp.int32(INT_MIN))
            low = jnp.where(causal, score, jnp.inf)
            high = jnp.where(causal, score, -jnp.inf)
            live = jnp.where(causal, score, 0.0)
        keyt_scr[kb] = key_t
        keys_scr[kb] = key_t.T
        smin = jnp.minimum(smin, jnp.min(low, axis=0, keepdims=True))
        smax = jnp.maximum(smax, jnp.max(high, axis=0, keepdims=True))
        s1 = s1 + jnp.sum(live, axis=0, keepdims=True)
        s2 = s2 + jnp.sum(live * live, axis=0, keepdims=True)
        return smin, smax, s1, s2

    def score_group(first, count, carry):
        for u in range(count):
            carry = score_tile(first + u, carry, False)
        return carry

    zero_row = jnp.zeros((1, QB_A), F32)
    stats = lax.fori_loop(0, kbl // 4, lambda i, c: score_group(4 * i, 4, c),
                          (jnp.full((1, QB_A), jnp.inf, F32), jnp.full((1, QB_A), -jnp.inf, F32),
                           zero_row, zero_row))
    stats = lax.cond(kbl % 4 >= 2, lambda c: score_group((kbl // 4) * 4, 2, c), lambda c: c, stats)
    stats = lax.cond(kbl % 2 == 1, lambda c: score_tile(kbl - 1, c, False), lambda c: c, stats)
    smin, smax, s1, s2 = score_tile(kbl, stats, True)

    qry1 = t0 + lax.broadcasted_iota(I32, (1, QB_A), 1)
    kf = float(topk)
    acc_rows = 32
    n_causal = (qry1 + 1).astype(F32)

    mean = s1 / n_causal
    dev = jnp.sqrt(jnp.maximum(s2 / n_causal - mean * mean, 0.0))
    frac_top = jnp.minimum(kf / n_causal, 1.0)
    tail = jnp.minimum(frac_top, 1.0 - frac_top)
    tq = jnp.sqrt(-2.0 * jnp.log(jnp.maximum(tail, 1e-6)))
    zq = tq - ((0.010328 * tq + 0.802853) * tq + 2.515517) / (((0.001308 * tq + 0.189269) * tq + 1.432788) * tq + 1.0)
    zq = jnp.where(frac_top > 0.5, -zq, zq)

    def for_each_key_tile(fold, init):
        return lax.fori_loop(0, kbl + 1, lambda kb, a: fold(a, keyt_scr[kb]), init)

    def search_pass(p, state):
        lo, hi, clo, chi, open_q = state
        lo_v = key_value(lo)
        hi_v = key_value(hi)
        gap = clo - chi
        frac = (clo - (kf - 0.5)) / gap
        frac = jnp.where(p % 2 == 1, 0.7 * frac + 0.15, frac)
        frac = jnp.where(gap > 16.0, frac, 0.5)
        value = lo_v + (hi_v - lo_v) * frac
        value = jnp.where(p == 0, mean + (zq - GUESS_HALF_WIDTH) * dev, value)
        value = jnp.where(p == 1, mean + (zq + GUESS_HALF_WIDTH) * dev, value)
        cand = sort_key(value)
        cand = jnp.where(p % 8 == 7, lo + lax.shift_right_logical(hi - lo, 1), cand)
        cand = jnp.minimum(jnp.maximum(cand, lo + 1), hi - 1)

        def count(cnt, keys):
            ge = jnp.where(keys >= cand, 1.0, 0.0)
            return cnt + jnp.sum(ge.reshape(KB_A // acc_rows, acc_rows, QB_A), axis=0)

        tot = jnp.sum(for_each_key_tile(count, jnp.zeros((acc_rows, QB_A), F32)), axis=0, keepdims=True)
        ge = tot >= kf
        lo = jnp.where(ge, cand, lo)
        clo = jnp.where(ge, tot, clo)
        hi = jnp.where(ge, hi, cand)
        chi = jnp.where(ge, chi, tot)
        width = hi - lo
        settled = jnp.where(clo == kf, 1.0, jnp.where(width == 1, 1.0, 0.0))
        open_q = jnp.where(settled > 0.5, 0.0, open_q)
        return lo, hi, clo, chi, open_q

    def extract_pass(state):
        lo, hi, clo, chi, open_q = state

        def top_below(best, keys):
            below = jnp.where(keys < hi, keys, jnp.int32(INT_MIN))
            return jnp.maximum(best, jnp.max(below.reshape(KB_A // acc_rows, acc_rows, QB_A), axis=0))

        best = for_each_key_tile(top_below, jnp.full((acc_rows, QB_A), INT_MIN, I32))
        best = jnp.max(best, axis=0, keepdims=True)
        is_open = open_q > 0.5
        last = jnp.logical_and(is_open, kf - chi <= 1.0)
        more = jnp.logical_and(is_open, kf - chi > 1.0)
        lo = jnp.where(last, best, lo)
        clo = jnp.where(last, chi + 1.0, clo)
        hi = jnp.where(more, best, hi)
        chi = jnp.where(more, chi + 1.0, chi)
        open_q = jnp.where(last, 0.0, open_q)
        return lo, hi, clo, chi, open_q

    def missing(state):
        return jnp.max(jnp.where(state[4] > 0.5, kf - state[3], 0.0))

    def search_step(carry):
        p, state, lacking = carry
        state = lax.cond(lacking <= EXTRACT_BELOW,
                         lambda st: extract_pass(st),
                         lambda st: search_pass(p + 1, search_pass(p, st)), state)
        return p + 2, state, missing(state)

    open0 = jnp.where(qry1 >= topk, 1.0, 0.0)
    lo0 = sort_key(smin)
    hi0 = sort_key(smax) + 1
    open0 = jnp.where(hi0 - lo0 == 1, 0.0, open0)
    state0 = (lo0, hi0, n_causal, zero_row, open0)
    _, (lo, _, _, _, _), _ = lax.while_loop(
        lambda c: c[2] > 0.0, search_step, (jnp.int32(0), state0, missing(state0)))
    thr = jnp.where(qry1 >= topk, lo, jnp.int32(INT_MIN))
    thr = jnp.broadcast_to(thr, (QB_A, QB_A)).T
    thrb = jnp.tile(thr, (1, reps))

    m_scr[...] = jnp.full(m_scr.shape, NEG, F32)
    acc_scr[...] = jnp.zeros(acc_scr.shape, F32)

    plain = bound_ref[0] <= PLAIN_SOFTMAX_LIMIT

    def attend(kb, near, online):
        mb = jnp.where(keys_scr[kb] >= thrb, 0.0, NEG)
        if near:
            mb = jnp.where((kb * KB_A + col0) <= row, mb, NEG)
            di = (t0 - kb * KB_A) // LANES
        mb_scr[...] = mb
        start = pl.multiple_of(kb * KB_A, KB_A)
        kt = ka_ref[pl.ds(start, KB_A), :]
        vt = jnp.concatenate([va_ref[pl.ds(start, KB_A), :], jnp.ones((KB_A, LANES), BF16)], axis=1)
        for h in range(N_HEADS_A):
            q = qa_ref[:, h * HEAD_DIM:(h + 1) * HEAD_DIM]
            s_scr[h] = lax.dot_general(q, kt, nt, preferred_element_type=F32)
        for h in range(N_HEADS_A):
            s = s_scr[h] + mb_scr[...]
            if near:
                s = s + bt_ref[di, h]
            if online:
                m_prev = m_scr[h]
                m_new = jnp.maximum(m_prev, jnp.max(s, axis=1, keepdims=True))
                alpha = jnp.exp2(m_prev - m_new)
                p = jnp.exp2(s - jnp.tile(m_new, (1, reps)))
                acc_scr[h] = (jnp.tile(alpha, (1, 2)) * acc_scr[h]
                              + jnp.dot(p.astype(BF16), vt, preferred_element_type=F32))
                m_scr[h] = m_new
            else:
                acc_scr[h] += jnp.dot(jnp.exp2(s).astype(BF16), vt, preferred_element_type=F32)

    def attend_all(online):
        prev_near = jnp.logical_and(kbl >= 1, t0 - kbl * KB_A < _BUCKET_START[-1] - 1)
        n_far = jnp.where(prev_near, kbl - 1, kbl)
        group = 2 if online else 4

        def far_group(i, carry):
            for u in range(group):
                attend(group * i + u, False, online)
            return carry

        lax.fori_loop(0, n_far // group, far_group, 0)

        if group == 4:
            @pl.when(n_far % 4 >= 2)
            def _():
                attend((n_far // 4) * 4, False, online)
                attend((n_far // 4) * 4 + 1, False, online)

        @pl.when(n_far % 2 == 1)
        def _():
            attend(n_far - 1, False, online)

        @pl.when(prev_near)
        def _():
            attend(kbl - 1, True, online)

        attend(kbl, True, online)

    @pl.when(plain)
    def _():
        attend_all(False)

    @pl.when(jnp.logical_not(plain))
    def _():
        attend_all(True)

    ssq = jnp.zeros((QB_A, LANES), F32)
    for h in range(N_HEADS_A):
        oh = acc_scr[h, :, :HEAD_DIM] / acc_scr[h, :, HEAD_DIM:]
        acc_scr[h, :, :HEAD_DIM] = oh
        ssq = ssq + jnp.sum(oh * oh, axis=1, keepdims=True)
    inv = lax.rsqrt(ssq * (1.0 / D_A) + EPS)
    for h in range(N_HEADS_A):
        sl = slice(h * HEAD_DIM, (h + 1) * HEAD_DIM)
        o_ref[:, sl] = (acc_scr[h, :, :HEAD_DIM] * inv * gn_ref[:, sl]).astype(o_ref.dtype)


def _attn_a(qidx, widx, qa, kidx, ka, va, bt, gn_a, bound, batch, seq):
    n = qa.shape[0]
    nq = seq // QB_A
    nkt = seq // KB_A
    topk = min(TOPK_MAX, seq // 4)
    return pl.pallas_call(
        functools.partial(_attn_a_kernel, topk=topk),
        grid=(batch, nq),
        in_specs=[pl.BlockSpec((IDX_HEADS, QB_A, IDX_DIM), lambda b, q: (0, b * nq + q, 0)),
                  pl.BlockSpec((IDX_HEADS, QB_A), lambda b, q: (0, b * nq + q)),
                  pl.BlockSpec((QB_A, D_A), lambda b, q: (b * nq + q, 0)),
                  pl.BlockSpec((seq, IDX_DIM), lambda b, q: (b, 0)),
                  pl.BlockSpec((seq, HEAD_DIM), lambda b, q: (b, 0)),
                  pl.BlockSpec((seq, HEAD_DIM), lambda b, q: (b, 0)),
                  pl.BlockSpec(bt.shape, lambda b, q: (0, 0, 0, 0)),
                  pl.BlockSpec((1, D_A), lambda b, q: (0, 0)),
                  pl.BlockSpec(memory_space=pltpu.SMEM)],
        out_specs=pl.BlockSpec((QB_A, D_A), lambda b, q: (b * nq + q, 0)),
        out_shape=jax.ShapeDtypeStruct((n, D_A), BF16),
        scratch_shapes=[pltpu.VMEM((nkt, QB_A, KB_A), I32),
                        pltpu.VMEM((nkt, KB_A, QB_A), I32),
                        pltpu.VMEM((N_HEADS_A, QB_A, LANES), F32),
                        pltpu.VMEM((N_HEADS_A, QB_A, 2 * HEAD_DIM), F32),
                        pltpu.VMEM((N_HEADS_A, QB_A, KB_A), F32),
                        pltpu.VMEM((QB_A, KB_A), F32)],
        compiler_params=_cparams(("parallel", "arbitrary")),
        name="dsa_attention",
    )(qidx, widx, qa, kidx, ka, va, bt, gn_a, bound)


def _attn_b_kernel(q_ref, k_ref, v_ref, tri_ref, o_ref, rest_scr, z_scr, cs_scr):
    qb = pl.program_id(1)
    row = lax.broadcasted_iota(I32, (QB_B, QB_B), 0)
    col = lax.broadcasted_iota(I32, (QB_B, QB_B), 1)
    strict = col < row
    nt = (((1,), (1,)), ((), ()))
    def step(kb, diag):
        start = pl.multiple_of(kb * QB_B, QB_B)
        heads = [slice(h * HEAD_DIM, (h + 1) * HEAD_DIM) for h in range(N_HEADS_B)]
        for h, sl in enumerate(heads):
            kt = k_ref[pl.ds(start, QB_B), sl]
            z_scr[h] = lax.dot_general(q_ref[:, sl], kt, nt, preferred_element_type=F32)
        for h, sl in enumerate(heads):
            z = z_scr[h]
            sp = jnp.maximum(z, 0.0) + jnp.log(1.0 + jnp.exp2(-jnp.abs(z))) * LOG2E
            if diag:
                sp = jnp.where(strict, sp, 0.0)
            hi = sp.astype(BF16)
            lo = (sp - hi.astype(F32)).astype(BF16)
            cs_scr[h] = jnp.dot(jnp.concatenate([hi, lo], axis=1), tri_ref[...],
                                preferred_element_type=F32)
        worst = None
        for h, sl in enumerate(heads):
            vt = v_ref[pl.ds(start, QB_B), sl]
            z = z_scr[h]
            cs = cs_scr[h, :, :QB_B]
            tot = cs_scr[h, :, QB_B:]
            if diag:
                a = jnp.where(strict, jnp.exp2(z - cs), 0.0)
                o_ref[:, sl] = jnp.dot(a.astype(BF16), vt, preferred_element_type=F32)
                rest = tot
            else:
                rest = rest_scr[h]
                a = jnp.exp2(z - cs - rest)
                o_ref[:, sl] += jnp.dot(a.astype(BF16), vt, preferred_element_type=F32)
                rest = rest + tot
            rest_scr[h] = rest
            worst = rest if worst is None else jnp.minimum(worst, rest)
        return jnp.min(worst)

    def more(kb, smallest):
        return jnp.logical_and(kb >= 0, smallest < EXP2_UNDERFLOW)

    def body(carry):
        kb, _ = carry
        return kb - 1, more(kb - 1, step(kb, False))

    lax.while_loop(lambda c: c[1], body, (qb - 1, more(qb - 1, step(qb, True))))


def _attn_b(proj, tri, batch, seq):
    n = batch * seq
    nq = seq // QB_B
    return pl.pallas_call(
        _attn_b_kernel,
        grid=(batch, nq),
        in_specs=[pl.BlockSpec((QB_B, D_B), lambda b, q: (b * nq + q, COL_QB // D_B)),
                  pl.BlockSpec((seq, D_B), lambda b, q: (b, COL_KB // D_B)),
                  pl.BlockSpec((seq, D_B), lambda b, q: (b, COL_VB // D_B)),
                  pl.BlockSpec(tri.shape, lambda b, q: (0, 0))],
        out_specs=pl.BlockSpec((QB_B, D_B), lambda b, q: (b * nq + q, 0)),
        out_shape=jax.ShapeDtypeStruct((n, D_B), F32),
        scratch_shapes=[pltpu.VMEM((N_HEADS_B, QB_B, QB_B), F32),
                        pltpu.VMEM((N_HEADS_B, QB_B, QB_B), F32),
                        pltpu.VMEM((N_HEADS_B, QB_B, 2 * QB_B), F32)],
        compiler_params=_cparams(("parallel", "arbitrary")),
        name="stick_breaking_attention",
    )(proj, proj, proj, tri)


def _out_proj_kernel(x_ref, oa_ref, ob_ref, gnb_ref, w_ref, g1_ref, ln_ref, sc_ref, sh_ref, rw_ref,
                     x1_ref, h2_ref, lg_ref):
    obn = (_rms(ob_ref[...]) * gnb_ref[...]).astype(BF16)
    y = jnp.dot(oa_ref[...], w_ref[0:D_A, :], preferred_element_type=F32)
    y = y + jnp.dot(obn, w_ref[D_A:D_A + D_B, :], preferred_element_type=F32)
    x1 = x_ref[...] + g1_ref[0] * y
    x1_ref[...] = x1
    h2 = _rms(x1) * ln_ref[...]
    h2 = h2 * (1.0 + sc_ref[0]) + sh_ref[0]
    h2_ref[...] = _pack_bf16_pairs(h2)
    hh = h2.astype(BF16)
    hl = (h2 - hh.astype(F32)).astype(BF16)
    rw = rw_ref[...]
    rh = rw.astype(BF16)
    rl = (rw - rh.astype(F32)).astype(BF16)
    both = jnp.dot(hh, jnp.concatenate([rh, rl], axis=1), preferred_element_type=F32)
    lg_ref[...] = both[:, :LANES] + both[:, LANES:] + jnp.dot(hl, rh, preferred_element_type=F32)


def _out_proj(x2, oa, ob, gn_b, w_out, g1, ln_g, sc, sh, rw, seq):
    n, d = x2.shape
    tm = 256
    tpb = seq // tm
    row = lambda i: (i, 0)
    fixed = lambda i: (0, 0)
    perb = lambda i: (i // tpb, 0, 0)
    return pl.pallas_call(
        _out_proj_kernel,
        grid=(n // tm,),
        in_specs=[pl.BlockSpec((tm, d), row),
                  pl.BlockSpec((tm, D_A), row),
                  pl.BlockSpec((tm, D_B), row),
                  pl.BlockSpec((1, D_B), fixed),
                  pl.BlockSpec(w_out.shape, fixed),
                  pl.BlockSpec((1, 1, d), perb),
                  pl.BlockSpec((1, d), fixed),
                  pl.BlockSpec((1, 1, d), perb),
                  pl.BlockSpec((1, 1, d), perb),
                  pl.BlockSpec(rw.shape, fixed)],
        out_specs=[pl.BlockSpec((tm, d), row),
                   pl.BlockSpec((tm, d // 2), row),
                   pl.BlockSpec((tm, LANES), row)],
        out_shape=[jax.ShapeDtypeStruct((n, d), F32),
                   jax.ShapeDtypeStruct((n, d // 2), I32),
                   jax.ShapeDtypeStruct((n, LANES), F32)],
        compiler_params=_cparams(("parallel",)),
        name="out_proj_ln2_router",
    )(x2, oa, ob, gn_b, w_out, g1, ln_g, sc, sh, rw)


def _route_kernel(lg_ref, info_ref, cnt_ref):
    @pl.when(pl.program_id(0) == 0)
    def _():
        cnt_ref[...] = jnp.zeros(cnt_ref.shape, F32)

    lg = lg_ref[...]
    lane = lax.broadcasted_iota(I32, lg.shape, 1)
    lanef = lane.astype(F32)
    big = float(4 * LANES)
    gm = jnp.where(lane >= N_EXPERTS, jnp.where(lane < N_EXPERTS + N_GROUPS, 1.0, 0.0), 0.0) > 0.5
    lgm = jnp.where(gm, lg, NEG)
    mg = jnp.max(lgm, axis=1, keepdims=True)
    eg = jnp.where(gm, jnp.exp(lgm - mg), 0.0)
    pg = eg / jnp.sum(eg, axis=1, keepdims=True)
    gw = jnp.max(pg, axis=1, keepdims=True)
    gidx = jnp.min(jnp.where(gm, jnp.where(pg == gw, lanef - N_EXPERTS, big), big), axis=1, keepdims=True)
    lane_group = (lane // EXPERTS_PER_GROUP).astype(F32)
    em = jnp.where(lane < N_EXPERTS, jnp.where(lane_group == gidx, 1.0, 0.0), 0.0) > 0.5
    lem = jnp.where(em, lg, NEG)
    me = jnp.max(lem, axis=1, keepdims=True)
    ee = jnp.where(em, jnp.exp(lem - me), 0.0)
    pe = jnp.where(em, ee / jnp.sum(ee, axis=1, keepdims=True), -1.0)
    p1 = jnp.max(pe, axis=1, keepdims=True)
    i1 = jnp.min(jnp.where(pe == p1, lanef, big), axis=1, keepdims=True)
    pe2 = jnp.where(lanef == i1, -1.0, pe)
    p2 = jnp.max(pe2, axis=1, keepdims=True)
    i2 = jnp.min(jnp.where(pe2 == p2, lanef, big), axis=1, keepdims=True)
    den = p1 + p2
    g0 = gw * p1 / den
    g1 = gw * p2 / den
    info = jnp.where(lane == 0, i1, jnp.where(lane == 1, i2,
                     jnp.where(lane == 2, g0, jnp.where(lane == 3, g1, 0.0))))
    info_ref[...] = info
    oh = jnp.where(lanef == i1, 1.0, 0.0) + jnp.where(lanef == i2, 1.0, 0.0)
    cnt_ref[...] += jnp.sum(oh, axis=0, keepdims=True)


def _route(lg):
    n = lg.shape[0]
    tm = min(1024, n)
    return pl.pallas_call(
        _route_kernel,
        grid=(n // tm,),
        in_specs=[pl.BlockSpec((tm, LANES), lambda i: (i, 0))],
        out_specs=[pl.BlockSpec((tm, LANES), lambda i: (i, 0)),
                   pl.BlockSpec((1, LANES), lambda i: (0, 0))],
        out_shape=[jax.ShapeDtypeStruct((n, LANES), F32),
                   jax.ShapeDtypeStruct((1, LANES), F32)],
        compiler_params=_cparams(("arbitrary",)),
        name="moe_route",
    )(lg)


def _plan_kernel(cnt_ref, ps_ref, be_ref, nu_ref, *, nblk):
    def fill(i, c):
        be_ref[i] = N_EXPERTS - 1
        return c

    lax.fori_loop(0, nblk, fill, 0)

    def per_expert(e, pos):
        ps_ref[e] = pos * MOE_BLK
        nb = (cnt_ref[e] + MOE_BLK - 1) // MOE_BLK

        def mark(k, c):
            be_ref[pos + k] = e
            return c

        lax.fori_loop(0, nb, mark, 0)
        return pos + nb

    nu_ref[0] = lax.fori_loop(0, N_EXPERTS, per_expert, jnp.int32(0))


def _plan(counts, nblk):
    smem = pl.BlockSpec(memory_space=pltpu.SMEM)
    return pl.pallas_call(
        functools.partial(_plan_kernel, nblk=nblk),
        in_specs=[smem],
        out_specs=[smem, smem, smem],
        out_shape=[jax.ShapeDtypeStruct((N_EXPERTS,), I32),
                   jax.ShapeDtypeStruct((nblk,), I32),
                   jax.ShapeDtypeStruct((1,), I32)],
        name="moe_block_plan",
    )(counts)


def _dest_kernel(info_ref, ps_ref, tri_ref, o_ref, carry_scr):
    @pl.when(pl.program_id(0) == 0)
    def _():
        carry_scr[...] = jnp.zeros(carry_scr.shape, F32)

    info = info_ref[...]
    lane = lax.broadcasted_iota(I32, info.shape, 1)
    lanef = lane.astype(F32)
    o1 = jnp.where(lanef == info[:, 0:1], 1.0, 0.0)
    o2 = jnp.where(lanef == info[:, 1:2], 1.0, 0.0)
    oh = o1 + o2
    before = jnp.dot(tri_ref[...], oh.astype(BF16), preferred_element_type=F32)
    base = before + carry_scr[...] + ps_ref[...]
    d1 = jnp.sum(o1 * base, axis=1, keepdims=True)
    d2 = jnp.sum(o2 * base, axis=1, keepdims=True)
    o_ref[...] = jnp.where(lane == 0, d1, jnp.where(lane == 1, d2, 0.0))
    carry_scr[...] += jnp.sum(oh, axis=0, keepdims=True)


def _dest(info, ps_lanes, tri):
    n = info.shape[0]
    tm = tri.shape[0]
    return pl.pallas_call(
        _dest_kernel,
        grid=(n // tm,),
        in_specs=[pl.BlockSpec((tm, LANES), lambda i: (i, 0)),
                  pl.BlockSpec((1, LANES), lambda i: (0, 0)),
                  pl.BlockSpec((tm, tm), lambda i: (0, 0))],
        out_specs=pl.BlockSpec((tm, LANES), lambda i: (i, 0)),
        out_shape=jax.ShapeDtypeStruct((n, LANES), F32),
        scratch_shapes=[pltpu.VMEM((1, LANES), F32)],
        compiler_params=_cparams(("arbitrary",)),
        name="moe_dest_rows",
    )(info, ps_lanes, tri)


SCATTER_SLOTS = 3


def _scatter_kernel(d0_ref, d1_ref, cnt_ref, ps_ref, nu_ref, h_ref, xb_ref,
                    zero_scr, stage_scr, load_sem, out_sem, zero_sem, *, nblk):
    n_tiles = h_ref.shape[0] // TOK_TILE
    zero_scr[...] = jnp.zeros(zero_scr.shape, zero_scr.dtype)
    pad_sizes = [s for s in (1 << k for k in range(MOE_BLK.bit_length() - 2, -1, -1))
                 if s >= SUBLANES]

    def zero_copy(rows, dst_row):
        return pltpu.make_async_copy(zero_scr.at[pl.ds(0, rows)], xb_ref.at[pl.ds(dst_row, rows)],
                                     zero_sem)

    def load(j, slot):
        return pltpu.make_async_copy(h_ref.at[pl.ds(j * TOK_TILE, TOK_TILE)], stage_scr.at[slot],
                                     load_sem.at[slot])

    def row_copy(slot, r, dst_row, parity, rows=1):
        return pltpu.make_async_copy(stage_scr.at[slot, pl.ds(r, rows)],
                                     xb_ref.at[pl.ds(dst_row, rows)], out_sem.at[parity])

    def for_each_zero_copy(act):
        def per_expert(e, c):
            pos = ps_ref[e] + cnt_ref[e]
            pad = (MOE_BLK - (cnt_ref[e] & (MOE_BLK - 1))) & (MOE_BLK - 1)
            head = pad & (SUBLANES - 1)
            for k in range(SUBLANES - 1):
                @pl.when(k < head)
                def _():
                    act(zero_copy(1, pos + k))
            pos = pl.multiple_of(pos + head, SUBLANES)
            for size in pad_sizes:
                @pl.when((pad & size) != 0)
                def _():
                    act(zero_copy(size, pos))
                pos = pl.multiple_of(pos + (pad & size), SUBLANES)
            return c

        def per_tail_block(i, c):
            act(zero_copy(MOE_BLK, i * MOE_BLK))
            return c

        lax.fori_loop(0, N_EXPERTS, per_expert, 0)
        lax.fori_loop(nu_ref[0], nblk, per_tail_block, 0)

    for_each_zero_copy(lambda cp: cp.start())

    def retire(parity):
        row_copy(0, 0, 0, parity, TOK_TILE).wait()
        row_copy(0, 0, 0, parity, TOK_TILE).wait()

    def tile(j, c):
        slot = j % SCATTER_SLOTS
        parity = j % 2
        load(j, slot).wait()

        @pl.when(j + 1 < n_tiles)
        def _():
            load(j + 1, (j + 1) % SCATTER_SLOTS).start()

        def issue(g, cc):
            for k in range(SUBLANES):
                r = g * SUBLANES + k
                t = j * TOK_TILE + r
                row_copy(slot, r, d0_ref[t], parity).start()
                row_copy(slot, r, d1_ref[t], parity).start()
            return cc

        lax.fori_loop(0, TOK_TILE // SUBLANES, issue, 0)

        @pl.when(j >= 1)
        def _():
            retire(1 - parity)
        return c

    load(0, 0).start()
    lax.fori_loop(0, n_tiles, tile, 0)
    retire((n_tiles - 1) % 2)
    for_each_zero_copy(lambda cp: cp.wait())


def _scatter_rows(d0, d1, counts, ps, nu, h2, p_rows):
    n, d = h2.shape
    smem = pl.BlockSpec(memory_space=pltpu.SMEM)
    return pl.pallas_call(
        functools.partial(_scatter_kernel, nblk=p_rows // MOE_BLK),
        in_specs=[smem, smem, smem, smem, smem, pl.BlockSpec(memory_space=pl.ANY)],
        out_specs=pl.BlockSpec(memory_space=pl.ANY),
        out_shape=jax.ShapeDtypeStruct((p_rows, d), h2.dtype),
        scratch_shapes=[pltpu.VMEM((MOE_BLK, d), h2.dtype),
                        pltpu.VMEM((SCATTER_SLOTS, TOK_TILE, d), h2.dtype),
                        pltpu.SemaphoreType.DMA((SCATTER_SLOTS,)),
                        pltpu.SemaphoreType.DMA((2,)),
                        pltpu.SemaphoreType.DMA(())],
        compiler_params=_cparams(),
        name="moe_scatter_rows",
    )(d0, d1, counts, ps, nu, h2)


def _expert_kernel(be_ref, nu_ref, x_ref, wg_hbm, wu_hbm, wd_hbm, o_ref,
                   wg_f32, wu_f32, wd_f32, wg_scr, wu_scr, wd_scr, slot_ref, sem):
    i = pl.program_id(0)
    n_used = nu_ref[0]
    expert = be_ref[i]
    new_expert = jnp.logical_or(i == 0, expert != be_ref[jnp.maximum(i - 1, 0)])

    def weight_copies(e, slot):
        return [pltpu.make_async_copy(src.at[e], dst.at[slot], sem.at[slot])
                for src, dst in ((wg_hbm, wg_f32), (wu_hbm, wu_f32), (wd_hbm, wd_f32))]

    @pl.when(i == 0)
    def _():
        slot_ref[0] = 0
        for cp in weight_copies(expert, 0):
            cp.start()

    @pl.when(jnp.logical_and(new_expert, i < n_used))
    def _():
        slot = slot_ref[0]
        for cp in weight_copies(expert, slot):
            cp.wait()
        nxt = lax.while_loop(lambda j: jnp.logical_and(j < n_used, be_ref[jnp.minimum(j, n_used - 1)] == expert),
                             lambda j: j + 1, i + 1)

        @pl.when(nxt < n_used)
        def _():
            for cp in weight_copies(be_ref[nxt], 1 - slot):
                cp.start()

        wg_scr[...] = wg_f32[slot].astype(BF16)
        wu_scr[...] = wu_f32[slot].astype(BF16)
        wd_scr[...] = wd_f32[slot].astype(BF16)
        slot_ref[0] = 1 - slot

    @pl.when(i < n_used)
    def _():
        x_hi, x_lo = _unpack_bf16_pairs(x_ref[...])
        x = jnp.concatenate([x_hi.astype(BF16), x_lo.astype(BF16)], axis=1)
        g = jnp.dot(x, wg_scr[...], preferred_element_type=F32)
        u = jnp.dot(x, wu_scr[...], preferred_element_type=F32)
        act = (g * (1.0 / (1.0 + jnp.exp(-g))) * u).astype(BF16)
        o_ref[...] = _pack_bf16_pairs(jnp.dot(act, wd_scr[...], preferred_element_type=F32))

    @pl.when(i >= nu_ref[0])
    def _():
        o_ref[...] = jnp.zeros(o_ref.shape, o_ref.dtype)


def _experts(be, nu, xb, wg, wu, wd):
    p = xb.shape[0]
    _, d, de = wg.shape
    return pl.pallas_call(
        _expert_kernel,
        grid_spec=pltpu.PrefetchScalarGridSpec(
            num_scalar_prefetch=2,
            grid=(p // MOE_BLK,),
            in_specs=[pl.BlockSpec((MOE_BLK, d // 2), lambda i, be, nu: (i, 0)),
                      pl.BlockSpec(memory_space=pl.ANY),
                      pl.BlockSpec(memory_space=pl.ANY),
                      pl.BlockSpec(memory_space=pl.ANY)],
            out_specs=pl.BlockSpec((MOE_BLK, d // 2), lambda i, be, nu: (i, 0)),
            scratch_shapes=[pltpu.VMEM((2, d, de), F32), pltpu.VMEM((2, d, de), F32),
                            pltpu.VMEM((2, de, d), F32),
                            pltpu.VMEM((d, de), BF16), pltpu.VMEM((d, de), BF16),
                            pltpu.VMEM((de, d), BF16),
                            pltpu.SMEM((1,), I32),
                            pltpu.SemaphoreType.DMA((2,))]),
        out_shape=jax.ShapeDtypeStruct((p, d // 2), I32),
        compiler_params=_cparams(("arbitrary",)),
        name="moe_expert_ffn",
    )(be, nu, xb, wg, wu, wd)


def _combine_kernel(d0_ref, d1_ref, x_ref, info_ref, g2_ref, yb_ref, o_ref, rows_scr, sem):
    i = pl.program_id(0)
    cur = i % 2

    def copy(buf, which, r, src_row, rows=1):
        return pltpu.make_async_copy(yb_ref.at[pl.ds(src_row, rows)],
                                     rows_scr.at[buf, which, pl.ds(r, rows)], sem.at[buf])

    def start_gather(step, buf):
        base = step * TOK_TILE

        def issue(g, c):
            for k in range(SUBLANES):
                r = g * SUBLANES + k
                copy(buf, 0, r, d0_ref[base + r]).start()
                copy(buf, 1, r, d1_ref[base + r]).start()
            return c

        lax.fori_loop(0, TOK_TILE // SUBLANES, issue, 0)

    @pl.when(i == 0)
    def _():
        start_gather(0, 0)

    @pl.when(i + 1 < pl.num_programs(0))
    def _():
        start_gather(i + 1, 1 - cur)

    copy(cur, 0, 0, 0, TOK_TILE).wait()
    copy(cur, 1, 0, 0, TOK_TILE).wait()

    info = info_ref[...]
    half = o_ref.shape[1] // 2
    y0_hi, y0_lo = _unpack_bf16_pairs(rows_scr[cur, 0])
    y1_hi, y1_lo = _unpack_bf16_pairs(rows_scr[cur, 1])
    g2 = g2_ref[0]
    o_ref[:, :half] = x_ref[:, :half] + g2[:, :half] * (info[:, 2:3] * y0_hi + info[:, 3:4] * y1_hi)
    o_ref[:, half:] = x_ref[:, half:] + g2[:, half:] * (info[:, 2:3] * y0_lo + info[:, 3:4] * y1_lo)


def _combine(d0, d1, x1, info, g2, yb, seq):
    n, d = x1.shape
    tpb = seq // TOK_TILE
    return pl.pallas_call(
        _combine_kernel,
        grid_spec=pltpu.PrefetchScalarGridSpec(
            num_scalar_prefetch=2,
            grid=(n // TOK_TILE,),
            in_specs=[pl.BlockSpec((TOK_TILE, d), lambda i, a, b: (i, 0)),
                      pl.BlockSpec((TOK_TILE, LANES), lambda i, a, b: (i, 0)),
                      pl.BlockSpec((1, 1, d), lambda i, a, b: (i // tpb, 0, 0)),
                      pl.BlockSpec(memory_space=pl.ANY)],
            out_specs=pl.BlockSpec((TOK_TILE, d), lambda i, a, b: (i, 0)),
            scratch_shapes=[pltpu.VMEM((2, 2, TOK_TILE, d // 2), I32),
                            pltpu.SemaphoreType.DMA((2,))]),
        out_shape=jax.ShapeDtypeStruct((n, d), F32),
        compiler_params=_cparams(("arbitrary",)),
        name="moe_combine",
    )(d0, d1, x1, info, g2, yb)


def _tri_inclusive_rev(k):
    l = (np.arange(k)[:, None] >= np.arange(k)[None, :]).astype(np.float32)
    half = np.concatenate([l, np.ones((k, k), np.float32)], axis=1)
    return jnp.asarray(np.concatenate([half, half], axis=0), dtype=BF16)


def _tri_strict_lower(k):
    return jnp.asarray((np.arange(k)[None, :] < np.arange(k)[:, None]).astype(np.float32), dtype=BF16)


def kernel(x, c, w_mod, b_mod, ln1_g, w_in, q_norm_g, w_q_up, q_gain, k_gain, rel_bias, gn_a, gn_b,
           w_out, ln2_g, router_g, router_e, w_gate, w_up, w_down):
    batch, seq, d = x.shape
    n = batch * seq
    assert w_mod.shape[0] == 1 and d == D_A + D_B
    assert seq % KB_A == 0 and KB_A % QB_A == 0 and seq % TOK_TILE == 0 and n % 512 == 0
    x2 = x.reshape(n, d)

    c8 = jnp.pad(c, ((0, 8 - batch), (0, 0)))
    mod = _modulation(c8, w_mod.reshape(d, -1), b_mod.reshape(1, -1))[:batch]
    sh1, sc1, g1, sh2, sc2, g2 = [m.reshape(batch, 1, d) for m in jnp.split(mod, 6, axis=-1)]

    wi = w_in.reshape(d, -1).astype(BF16)
    n_a = Q_RANK + 2 * HEAD_DIM + IDX_DIM + IDX_HEADS
    wi = jnp.concatenate([wi[:, n_a:], wi[:, :n_a],
                          jnp.zeros((d, D_IN_PAD - wi.shape[1]), wi.dtype)], axis=1)
    proj = _ln_proj(x2, ln1_g.reshape(1, d), sc1, sh1, wi, seq)

    qa, qidx, ka, va, kidx, widx = _aprep(
        proj, n, q_norm_g.reshape(1, -1), w_q_up.reshape(Q_RANK, -1).astype(BF16),
        q_gain.reshape(1, -1), k_gain.reshape(1, -1))

    bt = _bias_tiles(rel_bias)
    bound = _logit_bound(q_gain.reshape(1, -1), k_gain.reshape(1, -1), rel_bias)[0, :1]
    oa = _attn_a(qidx, widx.T, qa, kidx, ka, va, bt, gn_a.reshape(1, -1), bound, batch, seq)
    ob = _attn_b(proj, _tri_inclusive_rev(QB_B), batch, seq)

    rw = jnp.concatenate([router_e.reshape(d, -1), router_g.reshape(d, -1),
                          jnp.zeros((d, LANES - N_EXPERTS - N_GROUPS), F32)], axis=1)
    x1, h2, lg = _out_proj(x2, oa, ob, gn_b.reshape(1, -1), w_out.reshape(d, d).astype(BF16),
                           g1, ln2_g.reshape(1, d), sc2, sh2, rw, seq)

    info, cnt = _route(lg)
    counts = cnt[0, :N_EXPERTS].astype(I32)
    p_rows = 2 * n + N_EXPERTS * MOE_BLK
    ps, be, nu = _plan(counts, p_rows // MOE_BLK)
    ps_lanes = jnp.pad(ps.astype(F32), (0, LANES - N_EXPERTS)).reshape(1, LANES)
    dinfo = _dest(info, ps_lanes, _tri_strict_lower(512))
    d0 = dinfo[:, 0].astype(I32)
    d1 = dinfo[:, 1].astype(I32)
    xb = _scatter_rows(d0, d1, counts, ps, nu, h2, p_rows)
    yb = _experts(be, nu, xb,
                  w_gate.reshape(N_EXPERTS, d, D_EXPERT),
                  w_up.reshape(N_EXPERTS, d, D_EXPERT),
                  w_down.reshape(N_EXPERTS, D_EXPERT, d))
    out = _combine(d0, d1, x1, info, g2, yb, seq)
    return out.reshape(batch, seq, d)
```

```python
import functools
import math

import numpy as np
import jax
import jax.numpy as jnp
from jax import lax
from jax.experimental import pallas as pl
from jax.experimental.pallas import tpu as pltpu

F32 = jnp.float32
BF16 = jnp.bfloat16
I32 = jnp.int32

HEAD_DIM = 128
N_HEADS_A = 8
N_HEADS_B = 8
D_A = N_HEADS_A * HEAD_DIM
D_B = N_HEADS_B * HEAD_DIM
Q_RANK = 512
IDX_HEADS = 16
IDX_DIM = 64
TOPK_MAX = 256
N_BUCKETS = 32
MAX_DISTANCE = 128
N_GROUPS = 4
EXPERTS_PER_GROUP = 8
N_EXPERTS = N_GROUPS * EXPERTS_PER_GROUP
D_EXPERT = 512
EPS = 1e-6

LANES = 128
SUBLANES = 8
VMEM_LIMIT = 56 * 1024 * 1024
NEG = -1e30
INT_MIN = -(2 ** 31)
EXP2_UNDERFLOW = 150.0
LOG2E = math.log2(math.e)
PLAIN_SOFTMAX_LIMIT = 64.0
GUESS_HALF_WIDTH = 0.1
EXTRACT_BELOW = 3.0

QB_A = 128
KB_A = 256
QB_B = 128
MOE_BLK = 256
TOK_TILE = 256

COL_QB, COL_KB, COL_VB, COL_A = 0, D_B, 2 * D_B, 3 * D_B
A_CQ, A_KA, A_VA, A_KIDX, A_WIDX = 0, 512, 640, 768, 832
A_WIDTH = 1024
D_IN_PAD = COL_A + A_WIDTH


def _cparams(sem=None):
    return pltpu.CompilerParams(dimension_semantics=sem, vmem_limit_bytes=VMEM_LIMIT)


def _rms(x):
    return x * lax.rsqrt(jnp.mean(x * x, axis=-1, keepdims=True) + EPS)


def _pack_bf16_pairs(x):
    c = x.shape[1] // 2
    hi = pltpu.bitcast(x[:, :c].astype(BF16).astype(F32), I32)
    lo = pltpu.bitcast(x[:, c:].astype(BF16).astype(F32), I32)
    return hi | lax.shift_right_logical(lo, 16)


def _unpack_bf16_pairs(u):
    hi = pltpu.bitcast(u & jnp.int32(-65536), F32)
    lo = pltpu.bitcast(lax.shift_left(u, 16), F32)
    return hi, lo


def _mod_kernel(c_ref, w_ref, b_ref, o_ref):
    c = c_ref[...]
    s = c * (1.0 / (1.0 + jnp.exp(-c)))
    o_ref[...] = jnp.dot(s, w_ref[...], preferred_element_type=F32,
                         precision=lax.Precision.HIGHEST) + b_ref[...]


def _modulation(c8, w_mod, b_mod):
    d, n6 = w_mod.shape
    tn = 1024
    return pl.pallas_call(
        _mod_kernel,
        grid=(n6 // tn,),
        in_specs=[pl.BlockSpec((8, d), lambda j: (0, 0)),
                  pl.BlockSpec((d, tn), lambda j: (0, j)),
                  pl.BlockSpec((1, tn), lambda j: (0, j))],
        out_specs=pl.BlockSpec((8, tn), lambda j: (0, j)),
        out_shape=jax.ShapeDtypeStruct((8, n6), F32),
        compiler_params=_cparams(("arbitrary",)),
        name="modulation",
    )(c8, w_mod, b_mod)


def _ln_proj_kernel(x_ref, g_ref, sc_ref, sh_ref, w_ref, o_ref, h_scr, *, q_tiles, chunk):
    i = pl.program_id(0)
    j = pl.program_id(1)

    def normalise_chunk():
        rows = pl.ds(pl.multiple_of(j * chunk, chunk), chunk)
        h = _rms(x_ref[rows, :]) * g_ref[...]
        h = h * (1.0 + sc_ref[0]) + sh_ref[0]
        h_scr[i % 2, rows, :] = h.astype(BF16)

    @pl.when(i == 0)
    def _():
        normalise_chunk()
        o_ref[...] = jnp.zeros(o_ref.shape, o_ref.dtype)

    @pl.when(i > 0)
    def _():
        normalise_chunk()
        col_scale = jnp.where(j < q_tiles, HEAD_DIM ** -0.5 * LOG2E, 1.0)
        acc = jnp.dot(h_scr[(i - 1) % 2], w_ref[...], preferred_element_type=F32)
        o_ref[...] = (acc * col_scale).astype(o_ref.dtype)


def _ln_proj(x2, ln_g, sc, sh, w, seq):
    n, d = x2.shape
    ncol = w.shape[1]
    tm = min(1024, seq)
    tn = 1024
    tpb = seq // tm
    n_i, n_j = n // tm, ncol // tn
    assert COL_QB == 0 and D_B % tn == 0 and tm % (n_j * 16) == 0
    last = n_i - 1
    return pl.pallas_call(
        functools.partial(_ln_proj_kernel, q_tiles=D_B // tn, chunk=tm // n_j),
        grid=(n_i + 1, n_j),
        in_specs=[pl.BlockSpec((tm, d), lambda i, j: (jnp.minimum(i, last), 0)),
                  pl.BlockSpec((1, d), lambda i, j: (0, 0)),
                  pl.BlockSpec((1, 1, d), lambda i, j: (jnp.minimum(i, last) // tpb, 0, 0)),
                  pl.BlockSpec((1, 1, d), lambda i, j: (jnp.minimum(i, last) // tpb, 0, 0)),
                  pl.BlockSpec((d, tn), lambda i, j: (0, j))],
        out_specs=pl.BlockSpec((tm, tn), lambda i, j: (jnp.where(i == 0, n_i, i - 1), j)),
        out_shape=jax.ShapeDtypeStruct((n + tm, ncol), BF16),
        scratch_shapes=[pltpu.VMEM((2, tm, d), BF16)],
        compiler_params=_cparams(("arbitrary", "arbitrary")),
        name="ln_in_proj",
    )(x2, ln_g, sc, sh, w)


def _aprep_kernel(a_ref, qng_ref, wq_ref, qg_ref, kg_ref,
                  qa_ref, qidx_ref, ka_ref, va_ref, kidx_ref, widx_ref):
    cq = a_ref[:, A_CQ:A_CQ + Q_RANK].astype(F32)
    cqn = (_rms(cq) * qng_ref[...]).astype(BF16)
    qup = jnp.dot(cqn, wq_ref[...], preferred_element_type=F32)
    for h in range(N_HEADS_A):
        qh = qup[:, h * HEAD_DIM:(h + 1) * HEAD_DIM]
        qn = _rms(qh) * qg_ref[...] * (HEAD_DIM ** -0.5 * LOG2E)
        qa_ref[:, h * HEAD_DIM:(h + 1) * HEAD_DIM] = qn.astype(BF16)
    for h in range(IDX_HEADS):
        qi = qup[:, D_A + h * IDX_DIM:D_A + (h + 1) * IDX_DIM] * (IDX_DIM ** -0.5)
        qidx_ref[h] = qi.astype(BF16)
    ka = a_ref[:, A_KA:A_KA + HEAD_DIM].astype(F32)
    ka_ref[...] = (_rms(ka) * kg_ref[...]).astype(BF16)
    va_ref[...] = a_ref[:, A_VA:A_VA + HEAD_DIM]
    kidx_ref[...] = a_ref[:, A_KIDX:A_KIDX + IDX_DIM]
    widx_ref[...] = a_ref[:, A_WIDX:A_WIDX + IDX_HEADS].astype(F32) * (IDX_HEADS ** -0.5)


def _aprep(proj, n, q_norm_g, w_q_up, q_gain, k_gain):
    tm = 512
    cblk = COL_A // A_WIDTH
    nup = w_q_up.shape[1]
    return pl.pallas_call(
        _aprep_kernel,
        grid=(n // tm,),
        in_specs=[pl.BlockSpec((tm, A_WIDTH), lambda i: (i, cblk)),
                  pl.BlockSpec((1, Q_RANK), lambda i: (0, 0)),
                  pl.BlockSpec((Q_RANK, nup), lambda i: (0, 0)),
                  pl.BlockSpec((1, HEAD_DIM), lambda i: (0, 0)),
                  pl.BlockSpec((1, HEAD_DIM), lambda i: (0, 0))],
        out_specs=[pl.BlockSpec((tm, D_A), lambda i: (i, 0)),
                   pl.BlockSpec((IDX_HEADS, tm, IDX_DIM), lambda i: (0, i, 0)),
                   pl.BlockSpec((tm, HEAD_DIM), lambda i: (i, 0)),
                   pl.BlockSpec((tm, HEAD_DIM), lambda i: (i, 0)),
                   pl.BlockSpec((tm, IDX_DIM), lambda i: (i, 0)),
                   pl.BlockSpec((tm, IDX_HEADS), lambda i: (i, 0))],
        out_shape=[jax.ShapeDtypeStruct((n, D_A), BF16),
                   jax.ShapeDtypeStruct((IDX_HEADS, n, IDX_DIM), BF16),
                   jax.ShapeDtypeStruct((n, HEAD_DIM), BF16),
                   jax.ShapeDtypeStruct((n, HEAD_DIM), BF16),
                   jax.ShapeDtypeStruct((n, IDX_DIM), BF16),
                   jax.ShapeDtypeStruct((n, IDX_HEADS), F32)],
        compiler_params=_cparams(("parallel",)),
        name="group_a_prep",
    )(proj, q_norm_g, w_q_up, q_gain, k_gain)


def _t5_bucket_starts():
    max_exact = N_BUCKETS // 2
    d = np.arange(0, 4 * MAX_DISTANCE, dtype=np.int64)
    df = np.maximum(d, 1).astype(np.float32)
    large = max_exact + (np.log(df / np.float32(max_exact)) / np.float32(math.log(MAX_DISTANCE / max_exact))
                         * np.float32(N_BUCKETS - max_exact)).astype(np.int32)
    large = np.minimum(large, N_BUCKETS - 1)
    bucket = np.where(d < max_exact, d, large)
    assert np.all(np.diff(bucket) >= 0) and bucket[-1] == N_BUCKETS - 1
    return [int(np.argmax(bucket >= b)) for b in range(N_BUCKETS)]


_BUCKET_START = _t5_bucket_starts()
N_BIAS_TILES = 2 * KB_A // LANES


def _bias_kernel(rb_ref, o_ref):
    di = pl.program_id(0)
    h = pl.program_id(1)
    i = lax.broadcasted_iota(I32, (QB_A, KB_A), 0)
    j = lax.broadcasted_iota(I32, (QB_A, KB_A), 1)
    d = di * LANES + i - j
    val = jnp.full((QB_A, KB_A), rb_ref[0, h], F32)
    for b in range(1, N_BUCKETS):
        val = jnp.where(d >= _BUCKET_START[b], rb_ref[b, h], val)
    o_ref[0, 0] = (val - rb_ref[N_BUCKETS - 1, h]) * LOG2E


def _bias_tiles(rel_bias):
    return pl.pallas_call(
        _bias_kernel,
        grid=(N_BIAS_TILES, N_HEADS_A),
        in_specs=[pl.BlockSpec(memory_space=pltpu.SMEM)],
        out_specs=pl.BlockSpec((1, 1, QB_A, KB_A), lambda a, h: (a, h, 0, 0)),
        out_shape=jax.ShapeDtypeStruct((N_BIAS_TILES, N_HEADS_A, QB_A, KB_A), F32),
        compiler_params=_cparams(("arbitrary", "arbitrary")),
        name="t5_bias_tiles",
    )(rel_bias)


def _bound_kernel(qg_ref, kg_ref, rb_ref, o_ref):
    qmax = jnp.max(jnp.abs(qg_ref[...]), axis=1, keepdims=True)
    kmax = jnp.max(jnp.abs(kg_ref[...]), axis=1, keepdims=True)
    rb = rb_ref[...]
    shifted = jnp.abs(rb - rb[N_BUCKETS - 1:N_BUCKETS, :])
    bmax = jnp.max(jnp.max(shifted, axis=1, keepdims=True), axis=0, keepdims=True)
    bound = qmax * kmax * (math.sqrt(HEAD_DIM) * 1.02) + bmax
    o_ref[...] = jnp.broadcast_to(bound, o_ref.shape)


def _logit_bound(q_gain, k_gain, rel_bias):
    return pl.pallas_call(
        _bound_kernel,
        out_shape=jax.ShapeDtypeStruct((1, LANES), F32),
        name="dsa_logit_bound",
    )(q_gain, k_gain, rel_bias)


def _attn_a_kernel(qidx_ref, w_ref, qa_ref, kidx_ref, ka_ref, va_ref, bt_ref, gn_ref, bound_ref, o_ref,
                   keys_scr, keyt_scr, m_scr, acc_scr, s_scr, mb_scr, *, topk):
    qb = pl.program_id(1)
    t0 = qb * QB_A
    kbl = (t0 + QB_A - 1) // KB_A
    row = t0 + lax.broadcasted_iota(I32, (QB_A, KB_A), 0)
    col0 = lax.broadcasted_iota(I32, (QB_A, KB_A), 1)
    nt = (((1,), (1,)), ((), ()))

    reps = KB_A // LANES

    def sort_key(v):
        bits = pltpu.bitcast(v, I32)
        return jnp.where(bits < 0, bits ^ jnp.int32(0x7FFFFFFF), bits)

    def key_value(k):
        return pltpu.bitcast(jnp.where(k < 0, k ^ jnp.int32(0x7FFFFFFF), k), F32)

    key_pos = lax.broadcasted_iota(I32, (KB_A, QB_A), 0)
    qry_pos = t0 + lax.broadcasted_iota(I32, (KB_A, QB_A), 1)

    def score_tile(kb, carry, diagonal):
        smin, smax, s1, s2 = carry
        kt = kidx_ref[pl.ds(pl.multiple_of(kb * KB_A, KB_A), KB_A), :]
        score = jnp.zeros((KB_A, QB_A), F32)
        for hp in range(IDX_HEADS // 2):
            q2 = qidx_ref[2 * hp:2 * hp + 2].reshape(2 * QB_A, IDX_DIM)
            sc = lax.dot_general(kt, q2, nt, preferred_element_type=F32)
            score = score + w_ref[2 * hp:2 * hp + 1, :] * jnp.maximum(sc[:, :QB_A], 0.0)
            score = score + w_ref[2 * hp + 1:2 * hp + 2, :] * jnp.maximum(sc[:, QB_A:], 0.0)
        key_t = sort_key(score)
        low, high, live = score, score, score
        if diagonal:
            causal = (kb * KB_A + key_pos) <= qry_pos
            key_t = jnp.where(causal, key_t, jnp.int32(INT_MIN))
            low = jnp.where(causal, score, jnp.inf)
            high = jnp.where(causal, score, -jnp.inf)
            live = jnp.where(causal, score, 0.0)
        keyt_scr[kb] = key_t
        keys_scr[kb] = key_t.T
        smin = jnp.minimum(smin, jnp.min(low, axis=0, keepdims=True))
        smax = jnp.maximum(smax, jnp.max(high, axis=0, keepdims=True))
        s1 = s1 + jnp.sum(live, axis=0, keepdims=True)
        s2 = s2 + jnp.sum(live * live, axis=0, keepdims=True)
        return smin, smax, s1, s2

    def score_group(first, count, carry):
        for u in range(count):
            carry = score_tile(first + u, carry, False)
        return carry

    zero_row = jnp.zeros((1, QB_A), F32)
    stats = lax.fori_loop(0, kbl // 4, lambda i, c: score_group(4 * i, 4, c),
                          (jnp.full((1, QB_A), jnp.inf, F32), jnp.full((1, QB_A), -jnp.inf, F32),
                           zero_row, zero_row))
    stats = lax.cond(kbl % 4 >= 2, lambda c: score_group((kbl // 4) * 4, 2, c), lambda c: c, stats)
    stats = lax.cond(kbl % 2 == 1, lambda c: score_tile(kbl - 1, c, False), lambda c: c, stats)
    smin, smax, s1, s2 = score_tile(kbl, stats, True)

    qry1 = t0 + lax.broadcasted_iota(I32, (1, QB_A), 1)
    kf = float(topk)
    acc_rows = 32
    n_causal = (qry1 + 1).astype(F32)

    mean = s1 / n_causal
    dev = jnp.sqrt(jnp.maximum(s2 / n_causal - mean * mean, 0.0))
    frac_top = jnp.minimum(kf / n_causal, 1.0)
    tail = jnp.minimum(frac_top, 1.0 - frac_top)
    tq = jnp.sqrt(-2.0 * jnp.log(jnp.maximum(tail, 1e-6)))
    zq = tq - ((0.010328 * tq + 0.802853) * tq + 2.515517) / (((0.001308 * tq + 0.189269) * tq + 1.432788) * tq + 1.0)
    zq = jnp.where(frac_top > 0.5, -zq, zq)

    def for_each_key_tile(fold, init):
        n_all = kbl + 1
        acc = lax.fori_loop(0, n_all // 2,
                            lambda i, a: fold(fold(a, keyt_scr[2 * i]), keyt_scr[2 * i + 1]), init)
        return lax.cond(n_all % 2 == 1, lambda a: fold(a, keyt_scr[n_all - 1]), lambda a: a, acc)

    def search_pass(p, state):
        lo, hi, clo, chi, open_q = state
        lo_v = key_value(lo)
        hi_v = key_value(hi)
        gap = clo - chi
        frac = (clo - (kf - 0.5)) / gap
        frac = jnp.where(p % 2 == 1, 0.7 * frac + 0.15, frac)
        frac = jnp.where(gap > 16.0, frac, 0.5)
        value = lo_v + (hi_v - lo_v) * frac
        value = jnp.where(p == 0, mean + (zq - GUESS_HALF_WIDTH) * dev, value)
        value = jnp.where(p == 1, mean + (zq + GUESS_HALF_WIDTH) * dev, value)
        cand = sort_key(value)
        cand = jnp.where(p % 8 == 7, lo + lax.shift_right_logical(hi - lo, 1), cand)
        cand = jnp.minimum(jnp.maximum(cand, lo + 1), hi - 1)

        def count(cnt, keys):
            ge = jnp.where(keys >= cand, 1.0, 0.0)
            return cnt + jnp.sum(ge.reshape(KB_A // acc_rows, acc_rows, QB_A), axis=0)

        tot = jnp.sum(for_each_key_tile(count, jnp.zeros((acc_rows, QB_A), F32)), axis=0, keepdims=True)
        ge = tot >= kf
        lo = jnp.where(ge, cand, lo)
        clo = jnp.where(ge, tot, clo)
        hi = jnp.where(ge, hi, cand)
        chi = jnp.where(ge, chi, tot)
        width = hi - lo
        settled = jnp.where(clo == kf, 1.0, jnp.where(width == 1, 1.0, 0.0))
        open_q = jnp.where(settled > 0.5, 0.0, open_q)
        return lo, hi, clo, chi, open_q

    def extract_pass(state):
        lo, hi, clo, chi, open_q = state

        def top_below(best, keys):
            below = jnp.where(keys < hi, keys, jnp.int32(INT_MIN))
            return jnp.maximum(best, jnp.max(below.reshape(KB_A // acc_rows, acc_rows, QB_A), axis=0))

        best = for_each_key_tile(top_below, jnp.full((acc_rows, QB_A), INT_MIN, I32))
        best = jnp.max(best, axis=0, keepdims=True)
        is_open = open_q > 0.5
        last = jnp.logical_and(is_open, kf - chi <= 1.0)
        more = jnp.logical_and(is_open, kf - chi > 1.0)
        lo = jnp.where(last, best, lo)
        clo = jnp.where(last, chi + 1.0, clo)
        hi = jnp.where(more, best, hi)
        chi = jnp.where(more, chi + 1.0, chi)
        open_q = jnp.where(last, 0.0, open_q)
        return lo, hi, clo, chi, open_q

    def missing(state):
        return jnp.max(jnp.where(state[4] > 0.5, kf - state[3], 0.0))

    def search_step(carry):
        p, state, lacking = carry
        state = lax.cond(lacking <= EXTRACT_BELOW,
                         lambda st: extract_pass(st),
                         lambda st: search_pass(p + 1, search_pass(p, st)), state)
        return p + 2, state, missing(state)

    open0 = jnp.where(qry1 >= topk, 1.0, 0.0)
    lo0 = sort_key(smin)
    hi0 = sort_key(smax) + 1
    open0 = jnp.where(hi0 - lo0 == 1, 0.0, open0)
    state0 = (lo0, hi0, n_causal, zero_row, open0)
    _, (lo, _, _, _, _), _ = lax.while_loop(
        lambda c: c[2] > 0.0, search_step, (jnp.int32(0), state0, missing(state0)))
    thr = jnp.where(qry1 >= topk, lo, jnp.int32(INT_MIN))
    thr = jnp.broadcast_to(thr, (QB_A, QB_A)).T
    thrb = jnp.tile(thr, (1, reps))

    m_scr[...] = jnp.full(m_scr.shape, NEG, F32)
    acc_scr[...] = jnp.zeros(acc_scr.shape, F32)

    plain = bound_ref[0] <= PLAIN_SOFTMAX_LIMIT

    def attend(kb, near, online):
        mb = jnp.where(keys_scr[kb] >= thrb, 0.0, NEG)
        if near:
            mb = jnp.where((kb * KB_A + col0) <= row, mb, NEG)
            di = (t0 - kb * KB_A) // LANES
        mb_scr[...] = mb
        start = pl.multiple_of(kb * KB_A, KB_A)
        kt = ka_ref[pl.ds(start, KB_A), :]
        vt = jnp.concatenate([va_ref[pl.ds(start, KB_A), :], jnp.ones((KB_A, LANES), BF16)], axis=1)
        for h in range(N_HEADS_A):
            q = qa_ref[:, h * HEAD_DIM:(h + 1) * HEAD_DIM]
            s_scr[h] = lax.dot_general(q, kt, nt, preferred_element_type=F32)
        for h in range(N_HEADS_A):
            s = s_scr[h] + mb_scr[...]
            if near:
                s = s + bt_ref[di, h]
            if online:
                m_prev = m_scr[h]
                m_new = jnp.maximum(m_prev, jnp.max(s, axis=1, keepdims=True))
                alpha = jnp.exp2(m_prev - m_new)
                p = jnp.exp2(s - jnp.tile(m_new, (1, reps)))
                acc_scr[h] = (jnp.tile(alpha, (1, 2)) * acc_scr[h]
                              + jnp.dot(p.astype(BF16), vt, preferred_element_type=F32))
                m_scr[h] = m_new
            else:
                acc_scr[h] += jnp.dot(jnp.exp2(s).astype(BF16), vt, preferred_element_type=F32)

    def attend_all(online):
        prev_near = jnp.logical_and(kbl >= 1, t0 - kbl * KB_A < _BUCKET_START[-1] - 1)
        n_far = jnp.where(prev_near, kbl - 1, kbl)
        group = 2 if online else 4

        def far_group(i, carry):
            for u in range(group):
                attend(group * i + u, False, online)
            return carry

        lax.fori_loop(0, n_far // group, far_group, 0)

        if group == 4:
            @pl.when(n_far % 4 >= 2)
            def _():
                attend((n_far // 4) * 4, False, online)
                attend((n_far // 4) * 4 + 1, False, online)

        @pl.when(n_far % 2 == 1)
        def _():
            attend(n_far - 1, False, online)

        @pl.when(prev_near)
        def _():
            attend(kbl - 1, True, online)

        attend(kbl, True, online)

    @pl.when(plain)
    def _():
        attend_all(False)

    @pl.when(jnp.logical_not(plain))
    def _():
        attend_all(True)

    ssq = jnp.zeros((QB_A, LANES), F32)
    for h in range(N_HEADS_A):
        oh = acc_scr[h, :, :HEAD_DIM] / acc_scr[h, :, HEAD_DIM:]
        acc_scr[h, :, :HEAD_DIM] = oh
        ssq = ssq + jnp.sum(oh * oh, axis=1, keepdims=True)
    inv = lax.rsqrt(ssq * (1.0 / D_A) + EPS)
    for h in range(N_HEADS_A):
        sl = slice(h * HEAD_DIM, (h + 1) * HEAD_DIM)
        o_ref[:, sl] = (acc_scr[h, :, :HEAD_DIM] * inv * gn_ref[:, sl]).astype(o_ref.dtype)


def _attn_a(qidx, widx, qa, kidx, ka, va, bt, gn_a, bound, batch, seq):
    n = qa.shape[0]
    nq = seq // QB_A
    nkt = seq // KB_A
    topk = min(TOPK_MAX, seq // 4)
    return pl.pallas_call(
        functools.partial(_attn_a_kernel, topk=topk),
        grid=(batch, nq),
        in_specs=[pl.BlockSpec((IDX_HEADS, QB_A, IDX_DIM), lambda b, q: (0, b * nq + q, 0)),
                  pl.BlockSpec((IDX_HEADS, QB_A), lambda b, q: (0, b * nq + q)),
                  pl.BlockSpec((QB_A, D_A), lambda b, q: (b * nq + q, 0)),
                  pl.BlockSpec((seq, IDX_DIM), lambda b, q: (b, 0)),
                  pl.BlockSpec((seq, HEAD_DIM), lambda b, q: (b, 0)),
                  pl.BlockSpec((seq, HEAD_DIM), lambda b, q: (b, 0)),
                  pl.BlockSpec(bt.shape, lambda b, q: (0, 0, 0, 0)),
                  pl.BlockSpec((1, D_A), lambda b, q: (0, 0)),
                  pl.BlockSpec(memory_space=pltpu.SMEM)],
        out_specs=pl.BlockSpec((QB_A, D_A), lambda b, q: (b * nq + q, 0)),
        out_shape=jax.ShapeDtypeStruct((n, D_A), BF16),
        scratch_shapes=[pltpu.VMEM((nkt, QB_A, KB_A), I32),
                        pltpu.VMEM((nkt, KB_A, QB_A), I32),
                        pltpu.VMEM((N_HEADS_A, QB_A, LANES), F32),
                        pltpu.VMEM((N_HEADS_A, QB_A, 2 * HEAD_DIM), F32),
                        pltpu.VMEM((N_HEADS_A, QB_A, KB_A), F32),
                        pltpu.VMEM((QB_A, KB_A), F32)],
        compiler_params=_cparams(("parallel", "arbitrary")),
        name="dsa_attention",
    )(qidx, widx, qa, kidx, ka, va, bt, gn_a, bound)


def _attn_b_kernel(q_ref, k_ref, v_ref, tri_ref, o_ref, rest_scr, z_scr, cs_scr):
    qb = pl.program_id(1)
    row = lax.broadcasted_iota(I32, (QB_B, QB_B), 0)
    col = lax.broadcasted_iota(I32, (QB_B, QB_B), 1)
    strict = col < row
    nt = (((1,), (1,)), ((), ()))
    def step(kb, diag):
        start = pl.multiple_of(kb * QB_B, QB_B)
        heads = [slice(h * HEAD_DIM, (h + 1) * HEAD_DIM) for h in range(N_HEADS_B)]
        for h, sl in enumerate(heads):
            kt = k_ref[pl.ds(start, QB_B), sl]
            z_scr[h] = lax.dot_general(q_ref[:, sl], kt, nt, preferred_element_type=F32)
        for h, sl in enumerate(heads):
            z = z_scr[h]
            sp = jnp.maximum(z, 0.0) + jnp.log(1.0 + jnp.exp2(-jnp.abs(z))) * LOG2E
            if diag:
                sp = jnp.where(strict, sp, 0.0)
            hi = sp.astype(BF16)
            lo = (sp - hi.astype(F32)).astype(BF16)
            cs_scr[h] = jnp.dot(jnp.concatenate([hi, lo], axis=1), tri_ref[...],
                                preferred_element_type=F32)
        worst = None
        for h, sl in enumerate(heads):
            vt = v_ref[pl.ds(start, QB_B), sl]
            z = z_scr[h]
            cs = cs_scr[h, :, :QB_B]
            tot = cs_scr[h, :, QB_B:]
            if diag:
                a = jnp.where(strict, jnp.exp2(z - cs), 0.0)
                o_ref[:, sl] = jnp.dot(a.astype(BF16), vt, preferred_element_type=F32)
                rest = tot
            else:
                rest = rest_scr[h]
                a = jnp.exp2(z - cs - rest)
                o_ref[:, sl] += jnp.dot(a.astype(BF16), vt, preferred_element_type=F32)
                rest = rest + tot
            rest_scr[h] = rest
            worst = rest if worst is None else jnp.minimum(worst, rest)
        return jnp.min(worst)

    def more(kb, smallest):
        return jnp.logical_and(kb >= 0, smallest < EXP2_UNDERFLOW)

    def body(carry):
        kb, _ = carry
        return kb - 1, more(kb - 1, step(kb, False))

    def first_tiles(have_previous):
        smallest = step(qb, True)
        if have_previous:
            smallest = step(qb - 1, False)
        return smallest

    smallest = lax.cond(qb >= 1, lambda: first_tiles(True), lambda: first_tiles(False))
    lax.while_loop(lambda c: c[1], body, (qb - 2, more(qb - 2, smallest)))


def _attn_b(proj, tri, batch, seq):
    n = batch * seq
    nq = seq // QB_B
    return pl.pallas_call(
        _attn_b_kernel,
        grid=(batch, nq),
        in_specs=[pl.BlockSpec((QB_B, D_B), lambda b, q: (b * nq + q, COL_QB // D_B)),
                  pl.BlockSpec((seq, D_B), lambda b, q: (b, COL_KB // D_B)),
                  pl.BlockSpec((seq, D_B), lambda b, q: (b, COL_VB // D_B)),
                  pl.BlockSpec(tri.shape, lambda b, q: (0, 0))],
        out_specs=pl.BlockSpec((QB_B, D_B), lambda b, q: (b * nq + q, 0)),
        out_shape=jax.ShapeDtypeStruct((n, D_B), F32),
        scratch_shapes=[pltpu.VMEM((N_HEADS_B, QB_B, QB_B), F32),
                        pltpu.VMEM((N_HEADS_B, QB_B, QB_B), F32),
                        pltpu.VMEM((N_HEADS_B, QB_B, 2 * QB_B), F32)],
        compiler_params=_cparams(("parallel", "arbitrary")),
        name="stick_breaking_attention",
    )(proj, proj, proj, tri)


def _out_proj_kernel(x_ref, oa_ref, ob_ref, gnb_ref, w_ref, g1_ref, ln_ref, sc_ref, sh_ref, rw_ref,
                     x1_ref, h2_ref, lg_ref):
    obn = (_rms(ob_ref[...]) * gnb_ref[...]).astype(BF16)
    y = jnp.dot(oa_ref[...], w_ref[0:D_A, :], preferred_element_type=F32)
    y = y + jnp.dot(obn, w_ref[D_A:D_A + D_B, :], preferred_element_type=F32)
    x1 = x_ref[...] + g1_ref[0] * y
    x1_ref[...] = x1
    h2 = _rms(x1) * ln_ref[...]
    h2 = h2 * (1.0 + sc_ref[0]) + sh_ref[0]
    h2_ref[...] = _pack_bf16_pairs(h2)
    hh = h2.astype(BF16)
    hl = (h2 - hh.astype(F32)).astype(BF16)
    rw = rw_ref[...]
    rh = rw.astype(BF16)
    rl = (rw - rh.astype(F32)).astype(BF16)
    both = jnp.dot(hh, jnp.concatenate([rh, rl], axis=1), preferred_element_type=F32)
    lg_ref[...] = both[:, :LANES] + both[:, LANES:] + jnp.dot(hl, rh, preferred_element_type=F32)


def _out_proj(x2, oa, ob, gn_b, w_out, g1, ln_g, sc, sh, rw, seq):
    n, d = x2.shape
    tm = 256
    tpb = seq // tm
    row = lambda i: (i, 0)
    fixed = lambda i: (0, 0)
    perb = lambda i: (i // tpb, 0, 0)
    return pl.pallas_call(
        _out_proj_kernel,
        grid=(n // tm,),
        in_specs=[pl.BlockSpec((tm, d), row),
                  pl.BlockSpec((tm, D_A), row),
                  pl.BlockSpec((tm, D_B), row),
                  pl.BlockSpec((1, D_B), fixed),
                  pl.BlockSpec(w_out.shape, fixed),
                  pl.BlockSpec((1, 1, d), perb),
                  pl.BlockSpec((1, d), fixed),
                  pl.BlockSpec((1, 1, d), perb),
                  pl.BlockSpec((1, 1, d), perb),
                  pl.BlockSpec(rw.shape, fixed)],
        out_specs=[pl.BlockSpec((tm, d), row),
                   pl.BlockSpec((tm, d // 2), row),
                   pl.BlockSpec((tm, LANES), row)],
        out_shape=[jax.ShapeDtypeStruct((n, d), F32),
                   jax.ShapeDtypeStruct((n, d // 2), I32),
                   jax.ShapeDtypeStruct((n, LANES), F32)],
        compiler_params=_cparams(("parallel",)),
        name="out_proj_ln2_router",
    )(x2, oa, ob, gn_b, w_out, g1, ln_g, sc, sh, rw)


def _route_kernel(lg_ref, info_ref, cnt_ref):
    @pl.when(pl.program_id(0) == 0)
    def _():
        cnt_ref[...] = jnp.zeros(cnt_ref.shape, F32)

    lg = lg_ref[...]
    lane = lax.broadcasted_iota(I32, lg.shape, 1)
    lanef = lane.astype(F32)
    big = float(4 * LANES)
    gm = jnp.where(lane >= N_EXPERTS, jnp.where(lane < N_EXPERTS + N_GROUPS, 1.0, 0.0), 0.0) > 0.5
    lgm = jnp.where(gm, lg, NEG)
    mg = jnp.max(lgm, axis=1, keepdims=True)
    eg = jnp.where(gm, jnp.exp(lgm - mg), 0.0)
    pg = eg / jnp.sum(eg, axis=1, keepdims=True)
    gw = jnp.max(pg, axis=1, keepdims=True)
    gidx = jnp.min(jnp.where(gm, jnp.where(pg == gw, lanef - N_EXPERTS, big), big), axis=1, keepdims=True)
    lane_group = (lane // EXPERTS_PER_GROUP).astype(F32)
    em = jnp.where(lane < N_EXPERTS, jnp.where(lane_group == gidx, 1.0, 0.0), 0.0) > 0.5
    lem = jnp.where(em, lg, NEG)
    me = jnp.max(lem, axis=1, keepdims=True)
    ee = jnp.where(em, jnp.exp(lem - me), 0.0)
    pe = jnp.where(em, ee / jnp.sum(ee, axis=1, keepdims=True), -1.0)
    p1 = jnp.max(pe, axis=1, keepdims=True)
    i1 = jnp.min(jnp.where(pe == p1, lanef, big), axis=1, keepdims=True)
    pe2 = jnp.where(lanef == i1, -1.0, pe)
    p2 = jnp.max(pe2, axis=1, keepdims=True)
    i2 = jnp.min(jnp.where(pe2 == p2, lanef, big), axis=1, keepdims=True)
    den = p1 + p2
    g0 = gw * p1 / den
    g1 = gw * p2 / den
    info = jnp.where(lane == 0, i1, jnp.where(lane == 1, i2,
                     jnp.where(lane == 2, g0, jnp.where(lane == 3, g1, 0.0))))
    info_ref[...] = info
    oh = jnp.where(lanef == i1, 1.0, 0.0) + jnp.where(lanef == i2, 1.0, 0.0)
    cnt_ref[...] += jnp.sum(oh, axis=0, keepdims=True)


def _route(lg):
    n = lg.shape[0]
    tm = min(1024, n)
    return pl.pallas_call(
        _route_kernel,
        grid=(n // tm,),
        in_specs=[pl.BlockSpec((tm, LANES), lambda i: (i, 0))],
        out_specs=[pl.BlockSpec((tm, LANES), lambda i: (i, 0)),
                   pl.BlockSpec((1, LANES), lambda i: (0, 0))],
        out_shape=[jax.ShapeDtypeStruct((n, LANES), F32),
                   jax.ShapeDtypeStruct((1, LANES), F32)],
        compiler_params=_cparams(("arbitrary",)),
        name="moe_route",
    )(lg)


def _plan_kernel(cnt_ref, ps_ref, be_ref, nu_ref, *, nblk):
    def fill(i, c):
        be_ref[i] = N_EXPERTS - 1
        return c

    lax.fori_loop(0, nblk, fill, 0)

    def per_expert(e, pos):
        ps_ref[e] = pos * MOE_BLK
        nb = (cnt_ref[e] + MOE_BLK - 1) // MOE_BLK

        def mark(k, c):
            be_ref[pos + k] = e
            return c

        lax.fori_loop(0, nb, mark, 0)
        return pos + nb

    nu_ref[0] = lax.fori_loop(0, N_EXPERTS, per_expert, jnp.int32(0))


def _plan(counts, nblk):
    smem = pl.BlockSpec(memory_space=pltpu.SMEM)
    return pl.pallas_call(
        functools.partial(_plan_kernel, nblk=nblk),
        in_specs=[smem],
        out_specs=[smem, smem, smem],
        out_shape=[jax.ShapeDtypeStruct((N_EXPERTS,), I32),
                   jax.ShapeDtypeStruct((nblk,), I32),
                   jax.ShapeDtypeStruct((1,), I32)],
        name="moe_block_plan",
    )(counts)


def _dest_kernel(info_ref, ps_ref, tri_ref, o_ref, carry_scr):
    @pl.when(pl.program_id(0) == 0)
    def _():
        carry_scr[...] = jnp.zeros(carry_scr.shape, F32)

    info = info_ref[...]
    lane = lax.broadcasted_iota(I32, info.shape, 1)
    lanef = lane.astype(F32)
    o1 = jnp.where(lanef == info[:, 0:1], 1.0, 0.0)
    o2 = jnp.where(lanef == info[:, 1:2], 1.0, 0.0)
    oh = o1 + o2
    before = jnp.dot(tri_ref[...], oh.astype(BF16), preferred_element_type=F32)
    base = before + carry_scr[...] + ps_ref[...]
    d1 = jnp.sum(o1 * base, axis=1, keepdims=True)
    d2 = jnp.sum(o2 * base, axis=1, keepdims=True)
    o_ref[...] = jnp.where(lane == 0, d1, jnp.where(lane == 1, d2, 0.0))
    carry_scr[...] += jnp.sum(oh, axis=0, keepdims=True)


def _dest(info, ps_lanes, tri):
    n = info.shape[0]
    tm = tri.shape[0]
    return pl.pallas_call(
        _dest_kernel,
        grid=(n // tm,),
        in_specs=[pl.BlockSpec((tm, LANES), lambda i: (i, 0)),
                  pl.BlockSpec((1, LANES), lambda i: (0, 0)),
                  pl.BlockSpec((tm, tm), lambda i: (0, 0))],
        out_specs=pl.BlockSpec((tm, LANES), lambda i: (i, 0)),
        out_shape=jax.ShapeDtypeStruct((n, LANES), F32),
        scratch_shapes=[pltpu.VMEM((1, LANES), F32)],
        compiler_params=_cparams(("arbitrary",)),
        name="moe_dest_rows",
    )(info, ps_lanes, tri)


SCATTER_SLOTS = 3


def _scatter_kernel(d0_ref, d1_ref, cnt_ref, ps_ref, nu_ref, h_ref, xb_ref,
                    zero_scr, stage_scr, load_sem, out_sem, zero_sem, *, nblk):
    n_tiles = h_ref.shape[0] // TOK_TILE
    zero_scr[...] = jnp.zeros(zero_scr.shape, zero_scr.dtype)
    pad_sizes = [s for s in (1 << k for k in range(MOE_BLK.bit_length() - 2, -1, -1))
                 if s >= SUBLANES]

    def zero_copy(rows, dst_row):
        return pltpu.make_async_copy(zero_scr.at[pl.ds(0, rows)], xb_ref.at[pl.ds(dst_row, rows)],
                                     zero_sem)

    def load(j, slot):
        return pltpu.make_async_copy(h_ref.at[pl.ds(j * TOK_TILE, TOK_TILE)], stage_scr.at[slot],
                                     load_sem.at[slot])

    def row_copy(slot, r, dst_row, parity, rows=1):
        return pltpu.make_async_copy(stage_scr.at[slot, pl.ds(r, rows)],
                                     xb_ref.at[pl.ds(dst_row, rows)], out_sem.at[parity])

    def for_each_zero_copy(act):
        def per_expert(e, c):
            pos = ps_ref[e] + cnt_ref[e]
            pad = (MOE_BLK - (cnt_ref[e] & (MOE_BLK - 1))) & (MOE_BLK - 1)
            head = pad & (SUBLANES - 1)
            for k in range(SUBLANES - 1):
                @pl.when(k < head)
                def _():
                    act(zero_copy(1, pos + k))
            pos = pl.multiple_of(pos + head, SUBLANES)
            for size in pad_sizes:
                @pl.when((pad & size) != 0)
                def _():
                    act(zero_copy(size, pos))
                pos = pl.multiple_of(pos + (pad & size), SUBLANES)
            return c

        def per_tail_block(i, c):
            act(zero_copy(MOE_BLK, i * MOE_BLK))
            return c

        lax.fori_loop(0, N_EXPERTS, per_expert, 0)
        lax.fori_loop(nu_ref[0], nblk, per_tail_block, 0)

    for_each_zero_copy(lambda cp: cp.start())

    def retire(parity):
        row_copy(0, 0, 0, parity, TOK_TILE).wait()
        row_copy(0, 0, 0, parity, TOK_TILE).wait()

    def tile(j, c):
        slot = j % SCATTER_SLOTS
        parity = j % 2
        load(j, slot).wait()

        @pl.when(j + 1 < n_tiles)
        def _():
            load(j + 1, (j + 1) % SCATTER_SLOTS).start()

        def issue(g, cc):
            for k in range(SUBLANES):
                r = g * SUBLANES + k
                t = j * TOK_TILE + r
                row_copy(slot, r, d0_ref[t], parity).start()
                row_copy(slot, r, d1_ref[t], parity).start()
            return cc

        lax.fori_loop(0, TOK_TILE // SUBLANES, issue, 0)

        @pl.when(j >= 1)
        def _():
            retire(1 - parity)
        return c

    load(0, 0).start()
    lax.fori_loop(0, n_tiles, tile, 0)
    retire((n_tiles - 1) % 2)
    for_each_zero_copy(lambda cp: cp.wait())


def _scatter_rows(d0, d1, counts, ps, nu, h2, p_rows):
    n, d = h2.shape
    smem = pl.BlockSpec(memory_space=pltpu.SMEM)
    return pl.pallas_call(
        functools.partial(_scatter_kernel, nblk=p_rows // MOE_BLK),
        in_specs=[smem, smem, smem, smem, smem, pl.BlockSpec(memory_space=pl.ANY)],
        out_specs=pl.BlockSpec(memory_space=pl.ANY),
        out_shape=jax.ShapeDtypeStruct((p_rows, d), h2.dtype),
        scratch_shapes=[pltpu.VMEM((MOE_BLK, d), h2.dtype),
                        pltpu.VMEM((SCATTER_SLOTS, TOK_TILE, d), h2.dtype),
                        pltpu.SemaphoreType.DMA((SCATTER_SLOTS,)),
                        pltpu.SemaphoreType.DMA((2,)),
                        pltpu.SemaphoreType.DMA(())],
        compiler_params=_cparams(),
        name="moe_scatter_rows",
    )(d0, d1, counts, ps, nu, h2)


def _expert_kernel(be_ref, nu_ref, x_ref, wg_hbm, wu_hbm, wd_hbm, o_ref,
                   wg_f32, wu_f32, wd_f32, wg_scr, wu_scr, wd_scr, slot_ref, sem):
    i = pl.program_id(0)
    n_used = nu_ref[0]
    expert = be_ref[i]
    new_expert = jnp.logical_or(i == 0, expert != be_ref[jnp.maximum(i - 1, 0)])

    def weight_copies(e, slot):
        return [pltpu.make_async_copy(src.at[e], dst.at[slot], sem.at[slot])
                for src, dst in ((wg_hbm, wg_f32), (wu_hbm, wu_f32), (wd_hbm, wd_f32))]

    @pl.when(i == 0)
    def _():
        slot_ref[0] = 0
        for cp in weight_copies(expert, 0):
            cp.start()

    @pl.when(jnp.logical_and(new_expert, i < n_used))
    def _():
        slot = slot_ref[0]
        for cp in weight_copies(expert, slot):
            cp.wait()
        nxt = lax.while_loop(lambda j: jnp.logical_and(j < n_used, be_ref[jnp.minimum(j, n_used - 1)] == expert),
                             lambda j: j + 1, i + 1)

        @pl.when(nxt < n_used)
        def _():
            for cp in weight_copies(be_ref[nxt], 1 - slot):
                cp.start()

        wg_scr[...] = wg_f32[slot].astype(BF16)
        wu_scr[...] = wu_f32[slot].astype(BF16)
        wd_scr[...] = wd_f32[slot].astype(BF16)
        slot_ref[0] = 1 - slot

    @pl.when(i < n_used)
    def _():
        x_hi, x_lo = _unpack_bf16_pairs(x_ref[...])
        x = jnp.concatenate([x_hi.astype(BF16), x_lo.astype(BF16)], axis=1)
        g = jnp.dot(x, wg_scr[...], preferred_element_type=F32)
        u = jnp.dot(x, wu_scr[...], preferred_element_type=F32)
        act = (g * (1.0 / (1.0 + jnp.exp(-g))) * u).astype(BF16)
        o_ref[...] = _pack_bf16_pairs(jnp.dot(act, wd_scr[...], preferred_element_type=F32))

    @pl.when(i >= nu_ref[0])
    def _():
        o_ref[...] = jnp.zeros(o_ref.shape, o_ref.dtype)


def _experts(be, nu, xb, wg, wu, wd):
    p = xb.shape[0]
    _, d, de = wg.shape
    return pl.pallas_call(
        _expert_kernel,
        grid_spec=pltpu.PrefetchScalarGridSpec(
            num_scalar_prefetch=2,
            grid=(p // MOE_BLK,),
            in_specs=[pl.BlockSpec((MOE_BLK, d // 2), lambda i, be, nu: (i, 0)),
                      pl.BlockSpec(memory_space=pl.ANY),
                      pl.BlockSpec(memory_space=pl.ANY),
                      pl.BlockSpec(memory_space=pl.ANY)],
            out_specs=pl.BlockSpec((MOE_BLK, d // 2), lambda i, be, nu: (i, 0)),
            scratch_shapes=[pltpu.VMEM((2, d, de), F32), pltpu.VMEM((2, d, de), F32),
                            pltpu.VMEM((2, de, d), F32),
                            pltpu.VMEM((d, de), BF16), pltpu.VMEM((d, de), BF16),
                            pltpu.VMEM((de, d), BF16),
                            pltpu.SMEM((1,), I32),
                            pltpu.SemaphoreType.DMA((2,))]),
        out_shape=jax.ShapeDtypeStruct((p, d // 2), I32),
        compiler_params=_cparams(("arbitrary",)),
        name="moe_expert_ffn",
    )(be, nu, xb, wg, wu, wd)


def _combine_kernel(d0_ref, d1_ref, x_ref, info_ref, g2_ref, yb_ref, o_ref, rows_scr, sem):
    i = pl.program_id(0)
    cur = i % 2

    def copy(buf, which, r, src_row, rows=1):
        return pltpu.make_async_copy(yb_ref.at[pl.ds(src_row, rows)],
                                     rows_scr.at[buf, which, pl.ds(r, rows)], sem.at[buf])

    def start_gather(step, buf):
        base = step * TOK_TILE

        def issue(g, c):
            for k in range(SUBLANES):
                r = g * SUBLANES + k
                copy(buf, 0, r, d0_ref[base + r]).start()
                copy(buf, 1, r, d1_ref[base + r]).start()
            return c

        lax.fori_loop(0, TOK_TILE // SUBLANES, issue, 0)

    @pl.when(i == 0)
    def _():
        start_gather(0, 0)

    @pl.when(i + 1 < pl.num_programs(0))
    def _():
        start_gather(i + 1, 1 - cur)

    copy(cur, 0, 0, 0, TOK_TILE).wait()
    copy(cur, 1, 0, 0, TOK_TILE).wait()

    info = info_ref[...]
    half = o_ref.shape[1] // 2
    y0_hi, y0_lo = _unpack_bf16_pairs(rows_scr[cur, 0])
    y1_hi, y1_lo = _unpack_bf16_pairs(rows_scr[cur, 1])
    g2 = g2_ref[0]
    o_ref[:, :half] = x_ref[:, :half] + g2[:, :half] * (info[:, 2:3] * y0_hi + info[:, 3:4] * y1_hi)
    o_ref[:, half:] = x_ref[:, half:] + g2[:, half:] * (info[:, 2:3] * y0_lo + info[:, 3:4] * y1_lo)


def _combine(d0, d1, x1, info, g2, yb, seq):
    n, d = x1.shape
    tpb = seq // TOK_TILE
    return pl.pallas_call(
        _combine_kernel,
        grid_spec=pltpu.PrefetchScalarGridSpec(
            num_scalar_prefetch=2,
            grid=(n // TOK_TILE,),
            in_specs=[pl.BlockSpec((TOK_TILE, d), lambda i, a, b: (i, 0)),
                      pl.BlockSpec((TOK_TILE, LANES), lambda i, a, b: (i, 0)),
                      pl.BlockSpec((1, 1, d), lambda i, a, b: (i // tpb, 0, 0)),
                      pl.BlockSpec(memory_space=pl.ANY)],
            out_specs=pl.BlockSpec((TOK_TILE, d), lambda i, a, b: (i, 0)),
            scratch_shapes=[pltpu.VMEM((2, 2, TOK_TILE, d // 2), I32),
                            pltpu.SemaphoreType.DMA((2,))]),
        out_shape=jax.ShapeDtypeStruct((n, d), F32),
        compiler_params=_cparams(("arbitrary",)),
        name="moe_combine",
    )(d0, d1, x1, info, g2, yb)


def _tri_inclusive_rev(k):
    l = (np.arange(k)[:, None] >= np.arange(k)[None, :]).astype(np.float32)
    half = np.concatenate([l, np.ones((k, k), np.float32)], axis=1)
    return jnp.asarray(np.concatenate([half, half], axis=0), dtype=BF16)


def _tri_strict_lower(k):
    return jnp.asarray((np.arange(k)[None, :] < np.arange(k)[:, None]).astype(np.float32), dtype=BF16)


def kernel(x, c, w_mod, b_mod, ln1_g, w_in, q_norm_g, w_q_up, q_gain, k_gain, rel_bias, gn_a, gn_b,
           w_out, ln2_g, router_g, router_e, w_gate, w_up, w_down):
    batch, seq, d = x.shape
    n = batch * seq
    assert w_mod.shape[0] == 1 and d == D_A + D_B
    assert seq % KB_A == 0 and KB_A % QB_A == 0 and seq % TOK_TILE == 0 and n % 512 == 0
    x2 = x.reshape(n, d)

    c8 = jnp.pad(c, ((0, 8 - batch), (0, 0)))
    mod = _modulation(c8, w_mod.reshape(d, -1), b_mod.reshape(1, -1))[:batch]
    sh1, sc1, g1, sh2, sc2, g2 = [m.reshape(batch, 1, d) for m in jnp.split(mod, 6, axis=-1)]

    wi = w_in.reshape(d, -1).astype(BF16)
    n_a = Q_RANK + 2 * HEAD_DIM + IDX_DIM + IDX_HEADS
    wi = jnp.concatenate([wi[:, n_a:], wi[:, :n_a],
                          jnp.zeros((d, D_IN_PAD - wi.shape[1]), wi.dtype)], axis=1)
    proj = _ln_proj(x2, ln1_g.reshape(1, d), sc1, sh1, wi, seq)

    qa, qidx, ka, va, kidx, widx = _aprep(
        proj, n, q_norm_g.reshape(1, -1), w_q_up.reshape(Q_RANK, -1).astype(BF16),
        q_gain.reshape(1, -1), k_gain.reshape(1, -1))

    bt = _bias_tiles(rel_bias)
    bound = _logit_bound(q_gain.reshape(1, -1), k_gain.reshape(1, -1), rel_bias)[0, :1]
    oa = _attn_a(qidx, widx.T, qa, kidx, ka, va, bt, gn_a.reshape(1, -1), bound, batch, seq)
    ob = _attn_b(proj, _tri_inclusive_rev(QB_B), batch, seq)

    rw = jnp.concatenate([router_e.reshape(d, -1), router_g.reshape(d, -1),
                          jnp.zeros((d, LANES - N_EXPERTS - N_GROUPS), F32)], axis=1)
    x1, h2, lg = _out_proj(x2, oa, ob, gn_b.reshape(1, -1), w_out.reshape(d, d).astype(BF16),
                           g1, ln2_g.reshape(1, d), sc2, sh2, rw, seq)

    info, cnt = _route(lg)
    counts = cnt[0, :N_EXPERTS].astype(I32)
    p_rows = 2 * n + N_EXPERTS * MOE_BLK
    ps, be, nu = _plan(counts, p_rows // MOE_BLK)
    ps_lanes = jnp.pad(ps.astype(F32), (0, LANES - N_EXPERTS)).reshape(1, LANES)
    dinfo = _dest(info, ps_lanes, _tri_strict_lower(512))
    d0 = dinfo[:, 0].astype(I32)
    d1 = dinfo[:, 1].astype(I32)
    xb = _scatter_rows(d0, d1, counts, ps, nu, h2, p_rows)
    yb = _experts(be, nu, xb,
                  w_gate.reshape(N_EXPERTS, d, D_EXPERT),
                  w_up.reshape(N_EXPERTS, d, D_EXPERT),
                  w_down.reshape(N_EXPERTS, D_EXPERT, d))
    out = _combine(d0, d1, x1, info, g2, yb, seq)
    return out.reshape(batch, seq, d)
```

```python
import functools
import math

import numpy as np
import jax
import jax.numpy as jnp
from jax import lax
from jax.experimental import pallas as pl
from jax.experimental.pallas import tpu as pltpu

F32 = jnp.float32
BF16 = jnp.bfloat16
I32 = jnp.int32

HEAD_DIM = 128
N_HEADS_A = 8
N_HEADS_B = 8
D_A = N_HEADS_A * HEAD_DIM
D_B = N_HEADS_B * HEAD_DIM
Q_RANK = 512
IDX_HEADS = 16
IDX_DIM = 64
TOPK_MAX = 256
N_BUCKETS = 32
MAX_DISTANCE = 128
N_GROUPS = 4
EXPERTS_PER_GROUP = 8
N_EXPERTS = N_GROUPS * EXPERTS_PER_GROUP
D_EXPERT = 512
EPS = 1e-6

LANES = 128
SUBLANES = 8
VMEM_LIMIT = 56 * 1024 * 1024
NEG = -1e30
INT_MIN = -(2 ** 31)
EXP2_UNDERFLOW = 150.0
LOG2E = math.log2(math.e)
PLAIN_SOFTMAX_LIMIT = 64.0
GUESS_HALF_WIDTH = 0.1
EXTRACT_BELOW = 3.0

QB_A = 128
KB_A = 256
QB_B = 128
MOE_BLK = 256
TOK_TILE = 512

COL_QB, COL_KB, COL_VB, COL_A = 0, D_B, 2 * D_B, 3 * D_B
A_CQ, A_KA, A_VA, A_KIDX, A_WIDX = 0, 512, 640, 768, 832
A_WIDTH = 1024
D_IN_PAD = COL_A + A_WIDTH


def _cparams(sem=None):
    return pltpu.CompilerParams(dimension_semantics=sem, vmem_limit_bytes=VMEM_LIMIT)


def _rms(x):
    return x * lax.rsqrt(jnp.mean(x * x, axis=-1, keepdims=True) + EPS)


def _pack_bf16_pairs(x):
    c = x.shape[1] // 2
    hi = pltpu.bitcast(x[:, :c].astype(BF16).astype(F32), I32)
    lo = pltpu.bitcast(x[:, c:].astype(BF16).astype(F32), I32)
    return hi | lax.shift_right_logical(lo, 16)


def _unpack_bf16_pairs(u):
    hi = pltpu.bitcast(u & jnp.int32(-65536), F32)
    lo = pltpu.bitcast(lax.shift_left(u, 16), F32)
    return hi, lo


def _mod_kernel(c_ref, w_ref, b_ref, o_ref):
    c = c_ref[...]
    s = c * (1.0 / (1.0 + jnp.exp(-c)))
    o_ref[...] = jnp.dot(s, w_ref[...], preferred_element_type=F32,
                         precision=lax.Precision.HIGHEST) + b_ref[...]


def _modulation(c8, w_mod, b_mod):
    d, n6 = w_mod.shape
    tn = 1024
    return pl.pallas_call(
        _mod_kernel,
        grid=(n6 // tn,),
        in_specs=[pl.BlockSpec((8, d), lambda j: (0, 0)),
                  pl.BlockSpec((d, tn), lambda j: (0, j)),
                  pl.BlockSpec((1, tn), lambda j: (0, j))],
        out_specs=pl.BlockSpec((8, tn), lambda j: (0, j)),
        out_shape=jax.ShapeDtypeStruct((8, n6), F32),
        compiler_params=_cparams(("arbitrary",)),
        name="modulation",
    )(c8, w_mod, b_mod)


def _ln_proj_kernel(x_ref, g_ref, sc_ref, sh_ref, w_ref, o_ref, h_scr, *, q_tiles, chunk):
    i = pl.program_id(0)
    j = pl.program_id(1)

    def normalise_chunk():
        rows = pl.ds(pl.multiple_of(j * chunk, chunk), chunk)
        h = _rms(x_ref[rows, :]) * g_ref[...]
        h = h * (1.0 + sc_ref[0]) + sh_ref[0]
        h_scr[i % 2, rows, :] = h.astype(BF16)

    @pl.when(i == 0)
    def _():
        normalise_chunk()
        o_ref[...] = jnp.zeros(o_ref.shape, o_ref.dtype)

    @pl.when(i > 0)
    def _():
        normalise_chunk()
        col_scale = jnp.where(j < q_tiles, HEAD_DIM ** -0.5 * LOG2E, 1.0)
        acc = jnp.dot(h_scr[(i - 1) % 2], w_ref[...], preferred_element_type=F32)
        o_ref[...] = (acc * col_scale).astype(o_ref.dtype)


def _ln_proj(x2, ln_g, sc, sh, w, seq):
    n, d = x2.shape
    ncol = w.shape[1]
    tm = min(1024, seq)
    tn = 1024
    tpb = seq // tm
    n_i, n_j = n // tm, ncol // tn
    assert COL_QB == 0 and D_B % tn == 0 and tm % (n_j * 16) == 0
    last = n_i - 1
    return pl.pallas_call(
        functools.partial(_ln_proj_kernel, q_tiles=D_B // tn, chunk=tm // n_j),
        grid=(n_i + 1, n_j),
        in_specs=[pl.BlockSpec((tm, d), lambda i, j: (jnp.minimum(i, last), 0)),
                  pl.BlockSpec((1, d), lambda i, j: (0, 0)),
                  pl.BlockSpec((1, 1, d), lambda i, j: (jnp.minimum(i, last) // tpb, 0, 0)),
                  pl.BlockSpec((1, 1, d), lambda i, j: (jnp.minimum(i, last) // tpb, 0, 0)),
                  pl.BlockSpec((d, tn), lambda i, j: (0, j))],
        out_specs=pl.BlockSpec((tm, tn), lambda i, j: (jnp.where(i == 0, n_i, i - 1), j)),
        out_shape=jax.ShapeDtypeStruct((n + tm, ncol), BF16),
        scratch_shapes=[pltpu.VMEM((2, tm, d), BF16)],
        compiler_params=_cparams(("arbitrary", "arbitrary")),
        name="ln_in_proj",
    )(x2, ln_g, sc, sh, w)


def _aprep_kernel(a_ref, qng_ref, wq_ref, qg_ref, kg_ref,
                  qa_ref, qidx_ref, ka_ref, va_ref, kidx_ref, widx_ref):
    cq = a_ref[:, A_CQ:A_CQ + Q_RANK].astype(F32)
    cqn = (_rms(cq) * qng_ref[...]).astype(BF16)
    qup = jnp.dot(cqn, wq_ref[...], preferred_element_type=F32)
    for h in range(N_HEADS_A):
        qh = qup[:, h * HEAD_DIM:(h + 1) * HEAD_DIM]
        qn = _rms(qh) * qg_ref[...] * (HEAD_DIM ** -0.5 * LOG2E)
        qa_ref[:, h * HEAD_DIM:(h + 1) * HEAD_DIM] = qn.astype(BF16)
    for h in range(IDX_HEADS):
        qi = qup[:, D_A + h * IDX_DIM:D_A + (h + 1) * IDX_DIM] * (IDX_DIM ** -0.5)
        qidx_ref[h] = qi.astype(BF16)
    ka = a_ref[:, A_KA:A_KA + HEAD_DIM].astype(F32)
    ka_ref[...] = (_rms(ka) * kg_ref[...]).astype(BF16)
    va_ref[...] = a_ref[:, A_VA:A_VA + HEAD_DIM]
    kidx_ref[...] = a_ref[:, A_KIDX:A_KIDX + IDX_DIM]
    widx_ref[...] = a_ref[:, A_WIDX:A_WIDX + IDX_HEADS].astype(F32) * (IDX_HEADS ** -0.5)


def _aprep(proj, n, q_norm_g, w_q_up, q_gain, k_gain):
    tm = 512
    cblk = COL_A // A_WIDTH
    nup = w_q_up.shape[1]
    return pl.pallas_call(
        _aprep_kernel,
        grid=(n // tm,),
        in_specs=[pl.BlockSpec((tm, A_WIDTH), lambda i: (i, cblk)),
                  pl.BlockSpec((1, Q_RANK), lambda i: (0, 0)),
                  pl.BlockSpec((Q_RANK, nup), lambda i: (0, 0)),
                  pl.BlockSpec((1, HEAD_DIM), lambda i: (0, 0)),
                  pl.BlockSpec((1, HEAD_DIM), lambda i: (0, 0))],
        out_specs=[pl.BlockSpec((tm, D_A), lambda i: (i, 0)),
                   pl.BlockSpec((IDX_HEADS, tm, IDX_DIM), lambda i: (0, i, 0)),
                   pl.BlockSpec((tm, HEAD_DIM), lambda i: (i, 0)),
                   pl.BlockSpec((tm, HEAD_DIM), lambda i: (i, 0)),
                   pl.BlockSpec((tm, IDX_DIM), lambda i: (i, 0)),
                   pl.BlockSpec((tm, IDX_HEADS), lambda i: (i, 0))],
        out_shape=[jax.ShapeDtypeStruct((n, D_A), BF16),
                   jax.ShapeDtypeStruct((IDX_HEADS, n, IDX_DIM), BF16),
                   jax.ShapeDtypeStruct((n, HEAD_DIM), BF16),
                   jax.ShapeDtypeStruct((n, HEAD_DIM), BF16),
                   jax.ShapeDtypeStruct((n, IDX_DIM), BF16),
                   jax.ShapeDtypeStruct((n, IDX_HEADS), F32)],
        compiler_params=_cparams(("parallel",)),
        name="group_a_prep",
    )(proj, q_norm_g, w_q_up, q_gain, k_gain)


def _t5_bucket_starts():
    max_exact = N_BUCKETS // 2
    d = np.arange(0, 4 * MAX_DISTANCE, dtype=np.int64)
    df = np.maximum(d, 1).astype(np.float32)
    large = max_exact + (np.log(df / np.float32(max_exact)) / np.float32(math.log(MAX_DISTANCE / max_exact))
                         * np.float32(N_BUCKETS - max_exact)).astype(np.int32)
    large = np.minimum(large, N_BUCKETS - 1)
    bucket = np.where(d < max_exact, d, large)
    assert np.all(np.diff(bucket) >= 0) and bucket[-1] == N_BUCKETS - 1
    return [int(np.argmax(bucket >= b)) for b in range(N_BUCKETS)]


_BUCKET_START = _t5_bucket_starts()
N_BIAS_TILES = 2 * KB_A // LANES


def _bias_kernel(rb_ref, o_ref):
    di = pl.program_id(0)
    h = pl.program_id(1)
    i = lax.broadcasted_iota(I32, (QB_A, KB_A), 0)
    j = lax.broadcasted_iota(I32, (QB_A, KB_A), 1)
    d = di * LANES + i - j
    val = jnp.full((QB_A, KB_A), rb_ref[0, h], F32)
    for b in range(1, N_BUCKETS):
        val = jnp.where(d >= _BUCKET_START[b], rb_ref[b, h], val)
    o_ref[0, 0] = (val - rb_ref[N_BUCKETS - 1, h]) * LOG2E


def _bias_tiles(rel_bias):
    return pl.pallas_call(
        _bias_kernel,
        grid=(N_BIAS_TILES, N_HEADS_A),
        in_specs=[pl.BlockSpec(memory_space=pltpu.SMEM)],
        out_specs=pl.BlockSpec((1, 1, QB_A, KB_A), lambda a, h: (a, h, 0, 0)),
        out_shape=jax.ShapeDtypeStruct((N_BIAS_TILES, N_HEADS_A, QB_A, KB_A), F32),
        compiler_params=_cparams(("arbitrary", "arbitrary")),
        name="t5_bias_tiles",
    )(rel_bias)


def _bound_kernel(qg_ref, kg_ref, rb_ref, o_ref):
    qmax = jnp.max(jnp.abs(qg_ref[...]), axis=1, keepdims=True)
    kmax = jnp.max(jnp.abs(kg_ref[...]), axis=1, keepdims=True)
    rb = rb_ref[...]
    shifted = jnp.abs(rb - rb[N_BUCKETS - 1:N_BUCKETS, :])
    bmax = jnp.max(jnp.max(shifted, axis=1, keepdims=True), axis=0, keepdims=True)
    bound = qmax * kmax * (math.sqrt(HEAD_DIM) * 1.02) + bmax
    o_ref[...] = jnp.broadcast_to(bound, o_ref.shape)


def _logit_bound(q_gain, k_gain, rel_bias):
    return pl.pallas_call(
        _bound_kernel,
        out_shape=jax.ShapeDtypeStruct((1, LANES), F32),
        name="dsa_logit_bound",
    )(q_gain, k_gain, rel_bias)


def _attn_a_kernel(qidx_ref, w_ref, qa_ref, kidx_ref, ka_ref, va_ref, bt_ref, gn_ref, bound_ref, o_ref,
                   keys_scr, keyt_scr, m_scr, acc_scr, s_scr, mb_scr, *, topk):
    qb = pl.program_id(1)
    t0 = qb * QB_A
    kbl = (t0 + QB_A - 1) // KB_A
    row = t0 + lax.broadcasted_iota(I32, (QB_A, KB_A), 0)
    col0 = lax.broadcasted_iota(I32, (QB_A, KB_A), 1)
    nt = (((1,), (1,)), ((), ()))

    reps = KB_A // LANES

    def sort_key(v):
        bits = pltpu.bitcast(v, I32)
        return jnp.where(bits < 0, bits ^ jnp.int32(0x7FFFFFFF), bits)

    def key_value(k):
        return pltpu.bitcast(jnp.where(k < 0, k ^ jnp.int32(0x7FFFFFFF), k), F32)

    key_pos = lax.broadcasted_iota(I32, (KB_A, QB_A), 0)
    qry_pos = t0 + lax.broadcasted_iota(I32, (KB_A, QB_A), 1)

    def score_tile(kb, carry, diagonal):
        smin, smax, s1, s2 = carry
        kt = kidx_ref[pl.ds(pl.multiple_of(kb * KB_A, KB_A), KB_A), :]
        score = jnp.zeros((KB_A, QB_A), F32)
        for hp in range(IDX_HEADS // 2):
            q2 = qidx_ref[2 * hp:2 * hp + 2].reshape(2 * QB_A, IDX_DIM)
            sc = lax.dot_general(kt, q2, nt, preferred_element_type=F32)
            score = score + w_ref[2 * hp:2 * hp + 1, :] * jnp.maximum(sc[:, :QB_A], 0.0)
            score = score + w_ref[2 * hp + 1:2 * hp + 2, :] * jnp.maximum(sc[:, QB_A:], 0.0)
        key_t = sort_key(score)
        low, high, live = score, score, score
        if diagonal:
            causal = (kb * KB_A + key_pos) <= qry_pos
            key_t = jnp.where(causal, key_t, jnp.int32(INT_MIN))
            low = jnp.where(causal, score, jnp.inf)
            high = jnp.where(causal, score, -jnp.inf)
            live = jnp.where(causal, score, 0.0)
        keyt_scr[kb] = key_t
        keys_scr[kb] = key_t.T
        smin = jnp.minimum(smin, jnp.min(low, axis=0, keepdims=True))
        smax = jnp.maximum(smax, jnp.max(high, axis=0, keepdims=True))
        s1 = s1 + jnp.sum(live, axis=0, keepdims=True)
        s2 = s2 + jnp.sum(live * live, axis=0, keepdims=True)
        return smin, smax, s1, s2

    def score_group(first, count, carry):
        for u in range(count):
            carry = score_tile(first + u, carry, False)
        return carry

    zero_row = jnp.zeros((1, QB_A), F32)
    stats = lax.fori_loop(0, kbl // 4, lambda i, c: score_group(4 * i, 4, c),
                          (jnp.full((1, QB_A), jnp.inf, F32), jnp.full((1, QB_A), -jnp.inf, F32),
                           zero_row, zero_row))
    stats = lax.cond(kbl % 4 >= 2, lambda c: score_group((kbl // 4) * 4, 2, c), lambda c: c, stats)
    stats = lax.cond(kbl % 2 == 1, lambda c: score_tile(kbl - 1, c, False), lambda c: c, stats)
    smin, smax, s1, s2 = score_tile(kbl, stats, True)

    qry1 = t0 + lax.broadcasted_iota(I32, (1, QB_A), 1)
    kf = float(topk)
    acc_rows = 32
    n_causal = (qry1 + 1).astype(F32)

    mean = s1 / n_causal
    dev = jnp.sqrt(jnp.maximum(s2 / n_causal - mean * mean, 0.0))
    frac_top = jnp.minimum(kf / n_causal, 1.0)
    tail = jnp.minimum(frac_top, 1.0 - frac_top)
    tq = jnp.sqrt(-2.0 * jnp.log(jnp.maximum(tail, 1e-6)))
    zq = tq - ((0.010328 * tq + 0.802853) * tq + 2.515517) / (((0.001308 * tq + 0.189269) * tq + 1.432788) * tq + 1.0)
    zq = jnp.where(frac_top > 0.5, -zq, zq)

    def for_each_key_tile(fold, init):
        n_all = kbl + 1
        acc = lax.fori_loop(0, n_all // 2,
                            lambda i, a: fold(fold(a, keyt_scr[2 * i]), keyt_scr[2 * i + 1]), init)
        return lax.cond(n_all % 2 == 1, lambda a: fold(a, keyt_scr[n_all - 1]), lambda a: a, acc)

    def search_pass(p, state):
        lo, hi, clo, chi, open_q = state
        lo_v = key_value(lo)
        hi_v = key_value(hi)
        gap = clo - chi
        frac = (clo - (kf - 0.5)) / gap
        frac = jnp.where(p % 2 == 1, 0.7 * frac + 0.15, frac)
        frac = jnp.where(gap > 16.0, frac, 0.5)
        value = lo_v + (hi_v - lo_v) * frac
        value = jnp.where(p == 0, mean + (zq - GUESS_HALF_WIDTH) * dev, value)
        value = jnp.where(p == 1, mean + (zq + GUESS_HALF_WIDTH) * dev, value)
        cand = sort_key(value)
        cand = jnp.where(p % 8 == 7, lo + lax.shift_right_logical(hi - lo, 1), cand)
        cand = jnp.minimum(jnp.maximum(cand, lo + 1), hi - 1)

        def count(cnt, keys):
            ge = jnp.where(keys >= cand, 1.0, 0.0)
            return cnt + jnp.sum(ge.reshape(KB_A // acc_rows, acc_rows, QB_A), axis=0)

        tot = jnp.sum(for_each_key_tile(count, jnp.zeros((acc_rows, QB_A), F32)), axis=0, keepdims=True)
        ge = tot >= kf
        lo = jnp.where(ge, cand, lo)
        clo = jnp.where(ge, tot, clo)
        hi = jnp.where(ge, hi, cand)
        chi = jnp.where(ge, chi, tot)
        width = hi - lo
        settled = jnp.where(clo == kf, 1.0, jnp.where(width == 1, 1.0, 0.0))
        open_q = jnp.where(settled > 0.5, 0.0, open_q)
        return lo, hi, clo, chi, open_q

    def extract_pass(state):
        lo, hi, clo, chi, open_q = state

        def top_below(best, keys):
            below = jnp.where(keys < hi, keys, jnp.int32(INT_MIN))
            return jnp.maximum(best, jnp.max(below.reshape(KB_A // acc_rows, acc_rows, QB_A), axis=0))

        best = for_each_key_tile(top_below, jnp.full((acc_rows, QB_A), INT_MIN, I32))
        best = jnp.max(best, axis=0, keepdims=True)
        is_open = open_q > 0.5
        last = jnp.logical_and(is_open, kf - chi <= 1.0)
        more = jnp.logical_and(is_open, kf - chi > 1.0)
        lo = jnp.where(last, best, lo)
        clo = jnp.where(last, chi + 1.0, clo)
        hi = jnp.where(more, best, hi)
        chi = jnp.where(more, chi + 1.0, chi)
        open_q = jnp.where(last, 0.0, open_q)
        return lo, hi, clo, chi, open_q

    def missing(state):
        return jnp.max(jnp.where(state[4] > 0.5, kf - state[3], 0.0))

    def search_step(carry):
        p, state, lacking = carry
        state = lax.cond(lacking <= EXTRACT_BELOW,
                         lambda st: extract_pass(st),
                         lambda st: search_pass(p + 1, search_pass(p, st)), state)
        return p + 2, state, missing(state)

    open0 = jnp.where(qry1 >= topk, 1.0, 0.0)
    lo0 = sort_key(smin)
    hi0 = sort_key(smax) + 1
    open0 = jnp.where(hi0 - lo0 == 1, 0.0, open0)
    state0 = (lo0, hi0, n_causal, zero_row, open0)
    _, (lo, _, _, _, _), _ = lax.while_loop(
        lambda c: c[2] > 0.0, search_step, (jnp.int32(0), state0, missing(state0)))
    thr = jnp.where(qry1 >= topk, lo, jnp.int32(INT_MIN))
    thr = jnp.broadcast_to(thr, (QB_A, QB_A)).T
    thrb = jnp.tile(thr, (1, reps))

    m_scr[...] = jnp.full(m_scr.shape, NEG, F32)
    acc_scr[...] = jnp.zeros(acc_scr.shape, F32)

    plain = bound_ref[0] <= PLAIN_SOFTMAX_LIMIT

    def attend(kb, near, online):
        mb = jnp.where(keys_scr[kb] >= thrb, 0.0, NEG)
        if near:
            mb = jnp.where((kb * KB_A + col0) <= row, mb, NEG)
            di = (t0 - kb * KB_A) // LANES
        mb_scr[...] = mb
        start = pl.multiple_of(kb * KB_A, KB_A)
        kt = ka_ref[pl.ds(start, KB_A), :]
        vt = jnp.concatenate([va_ref[pl.ds(start, KB_A), :], jnp.ones((KB_A, LANES), BF16)], axis=1)
        for h in range(N_HEADS_A):
            q = qa_ref[:, h * HEAD_DIM:(h + 1) * HEAD_DIM]
            s_scr[h] = lax.dot_general(q, kt, nt, preferred_element_type=F32)
        for h in range(N_HEADS_A):
            s = s_scr[h] + mb_scr[...]
            if near:
                s = s + bt_ref[di, h]
            if online:
                m_prev = m_scr[h]
                m_new = jnp.maximum(m_prev, jnp.max(s, axis=1, keepdims=True))
                alpha = jnp.exp2(m_prev - m_new)
                p = jnp.exp2(s - jnp.tile(m_new, (1, reps)))
                acc_scr[h] = (jnp.tile(alpha, (1, 2)) * acc_scr[h]
                              + jnp.dot(p.astype(BF16), vt, preferred_element_type=F32))
                m_scr[h] = m_new
            else:
                acc_scr[h] += jnp.dot(jnp.exp2(s).astype(BF16), vt, preferred_element_type=F32)

    def attend_all(online):
        prev_near = jnp.logical_and(kbl >= 1, t0 - kbl * KB_A < _BUCKET_START[-1] - 1)
        n_far = jnp.where(prev_near, kbl - 1, kbl)
        group = 2 if online else 4

        def far_group(i, carry):
            for u in range(group):
                attend(group * i + u, False, online)
            return carry

        lax.fori_loop(0, n_far // group, far_group, 0)

        if group == 4:
            @pl.when(n_far % 4 >= 2)
            def _():
                attend((n_far // 4) * 4, False, online)
                attend((n_far // 4) * 4 + 1, False, online)

        @pl.when(n_far % 2 == 1)
        def _():
            attend(n_far - 1, False, online)

        @pl.when(prev_near)
        def _():
            attend(kbl - 1, True, online)
            attend(kbl, True, online)

        @pl.when(jnp.logical_not(prev_near))
        def _():
            attend(kbl, True, online)

    @pl.when(plain)
    def _():
        attend_all(False)

    @pl.when(jnp.logical_not(plain))
    def _():
        attend_all(True)

    ssq = jnp.zeros((QB_A, LANES), F32)
    for h in range(N_HEADS_A):
        oh = acc_scr[h, :, :HEAD_DIM] / acc_scr[h, :, HEAD_DIM:]
        acc_scr[h, :, :HEAD_DIM] = oh
        ssq = ssq + jnp.sum(oh * oh, axis=1, keepdims=True)
    inv = lax.rsqrt(ssq * (1.0 / D_A) + EPS)
    for h in range(N_HEADS_A):
        sl = slice(h * HEAD_DIM, (h + 1) * HEAD_DIM)
        o_ref[:, sl] = (acc_scr[h, :, :HEAD_DIM] * inv * gn_ref[:, sl]).astype(o_ref.dtype)


def _attn_a(qidx, widx, qa, kidx, ka, va, bt, gn_a, bound, batch, seq):
    n = qa.shape[0]
    nq = seq // QB_A
    nkt = seq // KB_A
    topk = min(TOPK_MAX, seq // 4)
    return pl.pallas_call(
        functools.partial(_attn_a_kernel, topk=topk),
        grid=(batch, nq),
        in_specs=[pl.BlockSpec((IDX_HEADS, QB_A, IDX_DIM), lambda b, q: (0, b * nq + q, 0)),
                  pl.BlockSpec((IDX_HEADS, QB_A), lambda b, q: (0, b * nq + q)),
                  pl.BlockSpec((QB_A, D_A), lambda b, q: (b * nq + q, 0)),
                  pl.BlockSpec((seq, IDX_DIM), lambda b, q: (b, 0)),
                  pl.BlockSpec((seq, HEAD_DIM), lambda b, q: (b, 0)),
                  pl.BlockSpec((seq, HEAD_DIM), lambda b, q: (b, 0)),
                  pl.BlockSpec(bt.shape, lambda b, q: (0, 0, 0, 0)),
                  pl.BlockSpec((1, D_A), lambda b, q: (0, 0)),
                  pl.BlockSpec(memory_space=pltpu.SMEM)],
        out_specs=pl.BlockSpec((QB_A, D_A), lambda b, q: (b * nq + q, 0)),
        out_shape=jax.ShapeDtypeStruct((n, D_A), BF16),
        scratch_shapes=[pltpu.VMEM((nkt, QB_A, KB_A), I32),
                        pltpu.VMEM((nkt, KB_A, QB_A), I32),
                        pltpu.VMEM((N_HEADS_A, QB_A, LANES), F32),
                        pltpu.VMEM((N_HEADS_A, QB_A, 2 * HEAD_DIM), F32),
                        pltpu.VMEM((N_HEADS_A, QB_A, KB_A), F32),
                        pltpu.VMEM((QB_A, KB_A), F32)],
        compiler_params=_cparams(("parallel", "arbitrary")),
        name="dsa_attention",
    )(qidx, widx, qa, kidx, ka, va, bt, gn_a, bound)


def _attn_b_kernel(q_ref, k_ref, v_ref, tri_ref, o_ref, rest_scr, z_scr, cs_scr):
    qb = pl.program_id(1)
    row = lax.broadcasted_iota(I32, (QB_B, QB_B), 0)
    col = lax.broadcasted_iota(I32, (QB_B, QB_B), 1)
    strict = col < row
    nt = (((1,), (1,)), ((), ()))
    def step(kb, diag):
        start = pl.multiple_of(kb * QB_B, QB_B)
        heads = [slice(h * HEAD_DIM, (h + 1) * HEAD_DIM) for h in range(N_HEADS_B)]
        for h, sl in enumerate(heads):
            kt = k_ref[pl.ds(start, QB_B), sl]
            z_scr[h] = lax.dot_general(q_ref[:, sl], kt, nt, preferred_element_type=F32)
        for h, sl in enumerate(heads):
            z = z_scr[h]
            sp = jnp.maximum(z, 0.0) + jnp.log(1.0 + jnp.exp2(-jnp.abs(z))) * LOG2E
            if diag:
                sp = jnp.where(strict, sp, 0.0)
            hi = sp.astype(BF16)
            lo = (sp - hi.astype(F32)).astype(BF16)
            cs_scr[h] = jnp.dot(jnp.concatenate([hi, lo], axis=1), tri_ref[...],
                                preferred_element_type=F32)
        worst = None
        for h, sl in enumerate(heads):
            vt = v_ref[pl.ds(start, QB_B), sl]
            z = z_scr[h]
            cs = cs_scr[h, :, :QB_B]
            tot = cs_scr[h, :, QB_B:]
            if diag:
                a = jnp.where(strict, jnp.exp2(z - cs), 0.0)
                o_ref[:, sl] = jnp.dot(a.astype(BF16), vt, preferred_element_type=F32)
                rest = tot
            else:
                rest = rest_scr[h]
                a = jnp.exp2(z - cs - rest)
                o_ref[:, sl] += jnp.dot(a.astype(BF16), vt, preferred_element_type=F32)
                rest = rest + tot
            rest_scr[h] = rest
            worst = rest if worst is None else jnp.minimum(worst, rest)
        return jnp.min(worst)

    def more(kb, smallest):
        return jnp.logical_and(kb >= 0, smallest < EXP2_UNDERFLOW)

    def body(carry):
        kb, _ = carry
        return kb - 1, more(kb - 1, step(kb, False))

    def first_tiles(have_previous):
        smallest = step(qb, True)
        if have_previous:
            smallest = step(qb - 1, False)
        return smallest

    smallest = lax.cond(qb >= 1, lambda: first_tiles(True), lambda: first_tiles(False))
    lax.while_loop(lambda c: c[1], body, (qb - 2, more(qb - 2, smallest)))


def _attn_b(proj, tri, batch, seq):
    n = batch * seq
    nq = seq // QB_B
    return pl.pallas_call(
        _attn_b_kernel,
        grid=(batch, nq),
        in_specs=[pl.BlockSpec((QB_B, D_B), lambda b, q: (b * nq + q, COL_QB // D_B)),
                  pl.BlockSpec((seq, D_B), lambda b, q: (b, COL_KB // D_B)),
                  pl.BlockSpec((seq, D_B), lambda b, q: (b, COL_VB // D_B)),
                  pl.BlockSpec(tri.shape, lambda b, q: (0, 0))],
        out_specs=pl.BlockSpec((QB_B, D_B), lambda b, q: (b * nq + q, 0)),
        out_shape=jax.ShapeDtypeStruct((n, D_B), F32),
        scratch_shapes=[pltpu.VMEM((N_HEADS_B, QB_B, QB_B), F32),
                        pltpu.VMEM((N_HEADS_B, QB_B, QB_B), F32),
                        pltpu.VMEM((N_HEADS_B, QB_B, 2 * QB_B), F32)],
        compiler_params=_cparams(("parallel", "arbitrary")),
        name="stick_breaking_attention",
    )(proj, proj, proj, tri)


def _out_proj_kernel(x_ref, oa_ref, ob_ref, gnb_ref, w_ref, g1_ref, ln_ref, sc_ref, sh_ref, rw_ref,
                     x1_ref, h2_ref, lg_ref):
    obn = (_rms(ob_ref[...]) * gnb_ref[...]).astype(BF16)
    y = jnp.dot(oa_ref[...], w_ref[0:D_A, :], preferred_element_type=F32)
    y = y + jnp.dot(obn, w_ref[D_A:D_A + D_B, :], preferred_element_type=F32)
    x1 = x_ref[...] + g1_ref[0] * y
    x1_ref[...] = x1
    h2 = _rms(x1) * ln_ref[...]
    h2 = h2 * (1.0 + sc_ref[0]) + sh_ref[0]
    h2_ref[...] = _pack_bf16_pairs(h2)
    hh = h2.astype(BF16)
    hl = (h2 - hh.astype(F32)).astype(BF16)
    rw = rw_ref[...]
    rh = rw.astype(BF16)
    rl = (rw - rh.astype(F32)).astype(BF16)
    both = jnp.dot(hh, jnp.concatenate([rh, rl], axis=1), preferred_element_type=F32)
    lg_ref[...] = both[:, :LANES] + both[:, LANES:] + jnp.dot(hl, rh, preferred_element_type=F32)


def _out_proj(x2, oa, ob, gn_b, w_out, g1, ln_g, sc, sh, rw, seq):
    n, d = x2.shape
    tm = 256
    tpb = seq // tm
    row = lambda i: (i, 0)
    fixed = lambda i: (0, 0)
    perb = lambda i: (i // tpb, 0, 0)
    return pl.pallas_call(
        _out_proj_kernel,
        grid=(n // tm,),
        in_specs=[pl.BlockSpec((tm, d), row),
                  pl.BlockSpec((tm, D_A), row),
                  pl.BlockSpec((tm, D_B), row),
                  pl.BlockSpec((1, D_B), fixed),
                  pl.BlockSpec(w_out.shape, fixed),
                  pl.BlockSpec((1, 1, d), perb),
                  pl.BlockSpec((1, d), fixed),
                  pl.BlockSpec((1, 1, d), perb),
                  pl.BlockSpec((1, 1, d), perb),
                  pl.BlockSpec(rw.shape, fixed)],
        out_specs=[pl.BlockSpec((tm, d), row),
                   pl.BlockSpec((tm, d // 2), row),
                   pl.BlockSpec((tm, LANES), row)],
        out_shape=[jax.ShapeDtypeStruct((n, d), F32),
                   jax.ShapeDtypeStruct((n, d // 2), I32),
                   jax.ShapeDtypeStruct((n, LANES), F32)],
        compiler_params=_cparams(("parallel",)),
        name="out_proj_ln2_router",
    )(x2, oa, ob, gn_b, w_out, g1, ln_g, sc, sh, rw)


def _route_kernel(lg_ref, info_ref, cnt_ref):
    @pl.when(pl.program_id(0) == 0)
    def _():
        cnt_ref[...] = jnp.zeros(cnt_ref.shape, F32)

    lg = lg_ref[...]
    lane = lax.broadcasted_iota(I32, lg.shape, 1)
    lanef = lane.astype(F32)
    big = float(4 * LANES)
    gm = jnp.where(lane >= N_EXPERTS, jnp.where(lane < N_EXPERTS + N_GROUPS, 1.0, 0.0), 0.0) > 0.5
    lgm = jnp.where(gm, lg, NEG)
    mg = jnp.max(lgm, axis=1, keepdims=True)
    eg = jnp.where(gm, jnp.exp(lgm - mg), 0.0)
    pg = eg / jnp.sum(eg, axis=1, keepdims=True)
    gw = jnp.max(pg, axis=1, keepdims=True)
    gidx = jnp.min(jnp.where(gm, jnp.where(pg == gw, lanef - N_EXPERTS, big), big), axis=1, keepdims=True)
    lane_group = (lane // EXPERTS_PER_GROUP).astype(F32)
    em = jnp.where(lane < N_EXPERTS, jnp.where(lane_group == gidx, 1.0, 0.0), 0.0) > 0.5
    lem = jnp.where(em, lg, NEG)
    me = jnp.max(lem, axis=1, keepdims=True)
    ee = jnp.where(em, jnp.exp(lem - me), 0.0)
    pe = jnp.where(em, ee / jnp.sum(ee, axis=1, keepdims=True), -1.0)
    p1 = jnp.max(pe, axis=1, keepdims=True)
    i1 = jnp.min(jnp.where(pe == p1, lanef, big), axis=1, keepdims=True)
    pe2 = jnp.where(lanef == i1, -1.0, pe)
    p2 = jnp.max(pe2, axis=1, keepdims=True)
    i2 = jnp.min(jnp.where(pe2 == p2, lanef, big), axis=1, keepdims=True)
    den = p1 + p2
    g0 = gw * p1 / den
    g1 = gw * p2 / den
    info = jnp.where(lane == 0, i1, jnp.where(lane == 1, i2,
                     jnp.where(lane == 2, g0, jnp.where(lane == 3, g1, 0.0))))
    info_ref[...] = info
    oh = jnp.where(lanef == i1, 1.0, 0.0) + jnp.where(lanef == i2, 1.0, 0.0)
    cnt_ref[...] += jnp.sum(oh, axis=0, keepdims=True)


def _route(lg):
    n = lg.shape[0]
    tm = min(1024, n)
    return pl.pallas_call(
        _route_kernel,
        grid=(n // tm,),
        in_specs=[pl.BlockSpec((tm, LANES), lambda i: (i, 0))],
        out_specs=[pl.BlockSpec((tm, LANES), lambda i: (i, 0)),
                   pl.BlockSpec((1, LANES), lambda i: (0, 0))],
        out_shape=[jax.ShapeDtypeStruct((n, LANES), F32),
                   jax.ShapeDtypeStruct((1, LANES), F32)],
        compiler_params=_cparams(("arbitrary",)),
        name="moe_route",
    )(lg)


def _plan_kernel(cnt_ref, ps_ref, be_ref, nu_ref, *, nblk):
    def fill(i, c):
        be_ref[i] = N_EXPERTS - 1
        return c

    lax.fori_loop(0, nblk, fill, 0)

    def per_expert(e, pos):
        ps_ref[e] = pos * MOE_BLK
        nb = (cnt_ref[e] + MOE_BLK - 1) // MOE_BLK

        def mark(k, c):
            be_ref[pos + k] = e
            return c

        lax.fori_loop(0, nb, mark, 0)
        return pos + nb

    nu_ref[0] = lax.fori_loop(0, N_EXPERTS, per_expert, jnp.int32(0))


def _plan(counts, nblk):
    smem = pl.BlockSpec(memory_space=pltpu.SMEM)
    return pl.pallas_call(
        functools.partial(_plan_kernel, nblk=nblk),
        in_specs=[smem],
        out_specs=[smem, smem, smem],
        out_shape=[jax.ShapeDtypeStruct((N_EXPERTS,), I32),
                   jax.ShapeDtypeStruct((nblk,), I32),
                   jax.ShapeDtypeStruct((1,), I32)],
        name="moe_block_plan",
    )(counts)


def _dest_kernel(info_ref, ps_ref, tri_ref, o_ref, carry_scr):
    @pl.when(pl.program_id(0) == 0)
    def _():
        carry_scr[...] = jnp.zeros(carry_scr.shape, F32)

    info = info_ref[...]
    lane = lax.broadcasted_iota(I32, info.shape, 1)
    lanef = lane.astype(F32)
    o1 = jnp.where(lanef == info[:, 0:1], 1.0, 0.0)
    o2 = jnp.where(lanef == info[:, 1:2], 1.0, 0.0)
    oh = o1 + o2
    before = jnp.dot(tri_ref[...], oh.astype(BF16), preferred_element_type=F32)
    base = before + carry_scr[...] + ps_ref[...]
    d1 = jnp.sum(o1 * base, axis=1, keepdims=True)
    d2 = jnp.sum(o2 * base, axis=1, keepdims=True)
    o_ref[...] = jnp.where(lane == 0, d1, jnp.where(lane == 1, d2, 0.0))
    carry_scr[...] += jnp.sum(oh, axis=0, keepdims=True)


def _dest(info, ps_lanes, tri):
    n = info.shape[0]
    tm = tri.shape[0]
    return pl.pallas_call(
        _dest_kernel,
        grid=(n // tm,),
        in_specs=[pl.BlockSpec((tm, LANES), lambda i: (i, 0)),
                  pl.BlockSpec((1, LANES), lambda i: (0, 0)),
                  pl.BlockSpec((tm, tm), lambda i: (0, 0))],
        out_specs=pl.BlockSpec((tm, LANES), lambda i: (i, 0)),
        out_shape=jax.ShapeDtypeStruct((n, LANES), F32),
        scratch_shapes=[pltpu.VMEM((1, LANES), F32)],
        compiler_params=_cparams(("arbitrary",)),
        name="moe_dest_rows",
    )(info, ps_lanes, tri)


SCATTER_SLOTS = 3


def _scatter_kernel(d0_ref, d1_ref, cnt_ref, ps_ref, nu_ref, h_ref, xb_ref,
                    zero_scr, stage_scr, load_sem, out_sem, zero_sem, *, nblk):
    n_tiles = h_ref.shape[0] // TOK_TILE
    zero_scr[...] = jnp.zeros(zero_scr.shape, zero_scr.dtype)
    pad_sizes = [s for s in (1 << k for k in range(MOE_BLK.bit_length() - 2, -1, -1))
                 if s >= SUBLANES]

    def zero_copy(rows, dst_row):
        return pltpu.make_async_copy(zero_scr.at[pl.ds(0, rows)], xb_ref.at[pl.ds(dst_row, rows)],
                                     zero_sem)

    def load(j, slot):
        return pltpu.make_async_copy(h_ref.at[pl.ds(j * TOK_TILE, TOK_TILE)], stage_scr.at[slot],
                                     load_sem.at[slot])

    def row_copy(slot, r, dst_row, parity, rows=1):
        return pltpu.make_async_copy(stage_scr.at[slot, pl.ds(r, rows)],
                                     xb_ref.at[pl.ds(dst_row, rows)], out_sem.at[parity])

    def for_each_zero_copy(act):
        def per_expert(e, c):
            pos = ps_ref[e] + cnt_ref[e]
            pad = (MOE_BLK - (cnt_ref[e] & (MOE_BLK - 1))) & (MOE_BLK - 1)
            head = pad & (SUBLANES - 1)
            for k in range(SUBLANES - 1):
                @pl.when(k < head)
                def _():
                    act(zero_copy(1, pos + k))
            pos = pl.multiple_of(pos + head, SUBLANES)
            for size in pad_sizes:
                @pl.when((pad & size) != 0)
                def _():
                    act(zero_copy(size, pos))
                pos = pl.multiple_of(pos + (pad & size), SUBLANES)
            return c

        def per_tail_block(i, c):
            act(zero_copy(MOE_BLK, i * MOE_BLK))
            return c

        lax.fori_loop(0, N_EXPERTS, per_expert, 0)
        lax.fori_loop(nu_ref[0], nblk, per_tail_block, 0)

    for_each_zero_copy(lambda cp: cp.start())

    def retire(parity):
        row_copy(0, 0, 0, parity, TOK_TILE).wait()
        row_copy(0, 0, 0, parity, TOK_TILE).wait()

    def tile(j, c):
        slot = j % SCATTER_SLOTS
        parity = j % 2
        load(j, slot).wait()

        @pl.when(j + 1 < n_tiles)
        def _():
            load(j + 1, (j + 1) % SCATTER_SLOTS).start()

        def issue(g, cc):
            for k in range(SUBLANES):
                r = g * SUBLANES + k
                t = j * TOK_TILE + r
                row_copy(slot, r, d0_ref[t], parity).start()
                row_copy(slot, r, d1_ref[t], parity).start()
            return cc

        lax.fori_loop(0, TOK_TILE // SUBLANES, issue, 0)

        @pl.when(j >= 1)
        def _():
            retire(1 - parity)
        return c

    load(0, 0).start()
    lax.fori_loop(0, n_tiles, tile, 0)
    retire((n_tiles - 1) % 2)
    for_each_zero_copy(lambda cp: cp.wait())


def _scatter_rows(d0, d1, counts, ps, nu, h2, p_rows):
    n, d = h2.shape
    smem = pl.BlockSpec(memory_space=pltpu.SMEM)
    return pl.pallas_call(
        functools.partial(_scatter_kernel, nblk=p_rows // MOE_BLK),
        in_specs=[smem, smem, smem, smem, smem, pl.BlockSpec(memory_space=pl.ANY)],
        out_specs=pl.BlockSpec(memory_space=pl.ANY),
        out_shape=jax.ShapeDtypeStruct((p_rows, d), h2.dtype),
        scratch_shapes=[pltpu.VMEM((MOE_BLK, d), h2.dtype),
                        pltpu.VMEM((SCATTER_SLOTS, TOK_TILE, d), h2.dtype),
                        pltpu.SemaphoreType.DMA((SCATTER_SLOTS,)),
                        pltpu.SemaphoreType.DMA((2,)),
                        pltpu.SemaphoreType.DMA(())],
        compiler_params=_cparams(),
        name="moe_scatter_rows",
    )(d0, d1, counts, ps, nu, h2)


def _expert_kernel(be_ref, nu_ref, x_ref, wg_hbm, wu_hbm, wd_hbm, o_ref,
                   wg_f32, wu_f32, wd_f32, wg_scr, wu_scr, wd_scr, slot_ref, sem):
    i = pl.program_id(0)
    n_used = nu_ref[0]
    expert = be_ref[i]
    new_expert = jnp.logical_or(i == 0, expert != be_ref[jnp.maximum(i - 1, 0)])

    def weight_copies(e, slot):
        return [pltpu.make_async_copy(src.at[e], dst.at[slot], sem.at[slot])
                for src, dst in ((wg_hbm, wg_f32), (wu_hbm, wu_f32), (wd_hbm, wd_f32))]

    @pl.when(i == 0)
    def _():
        slot_ref[0] = 0
        for cp in weight_copies(expert, 0):
            cp.start()

    @pl.when(jnp.logical_and(new_expert, i < n_used))
    def _():
        slot = slot_ref[0]
        for cp in weight_copies(expert, slot):
            cp.wait()
        nxt = lax.while_loop(lambda j: jnp.logical_and(j < n_used, be_ref[jnp.minimum(j, n_used - 1)] == expert),
                             lambda j: j + 1, i + 1)

        @pl.when(nxt < n_used)
        def _():
            for cp in weight_copies(be_ref[nxt], 1 - slot):
                cp.start()

        wg_scr[...] = wg_f32[slot].astype(BF16)
        wu_scr[...] = wu_f32[slot].astype(BF16)
        wd_scr[...] = wd_f32[slot].astype(BF16)
        slot_ref[0] = 1 - slot

    @pl.when(i < n_used)
    def _():
        x_hi, x_lo = _unpack_bf16_pairs(x_ref[...])
        x = jnp.concatenate([x_hi.astype(BF16), x_lo.astype(BF16)], axis=1)
        g = jnp.dot(x, wg_scr[...], preferred_element_type=F32)
        u = jnp.dot(x, wu_scr[...], preferred_element_type=F32)
        act = (g * (1.0 / (1.0 + jnp.exp(-g))) * u).astype(BF16)
        o_ref[...] = _pack_bf16_pairs(jnp.dot(act, wd_scr[...], preferred_element_type=F32))

    @pl.when(i >= nu_ref[0])
    def _():
        o_ref[...] = jnp.zeros(o_ref.shape, o_ref.dtype)


def _experts(be, nu, xb, wg, wu, wd):
    p = xb.shape[0]
    _, d, de = wg.shape
    return pl.pallas_call(
        _expert_kernel,
        grid_spec=pltpu.PrefetchScalarGridSpec(
            num_scalar_prefetch=2,
            grid=(p // MOE_BLK,),
            in_specs=[pl.BlockSpec((MOE_BLK, d // 2), lambda i, be, nu: (i, 0)),
                      pl.BlockSpec(memory_space=pl.ANY),
                      pl.BlockSpec(memory_space=pl.ANY),
                      pl.BlockSpec(memory_space=pl.ANY)],
            out_specs=pl.BlockSpec((MOE_BLK, d // 2), lambda i, be, nu: (i, 0)),
            scratch_shapes=[pltpu.VMEM((2, d, de), F32), pltpu.VMEM((2, d, de), F32),
                            pltpu.VMEM((2, de, d), F32),
                            pltpu.VMEM((d, de), BF16), pltpu.VMEM((d, de), BF16),
                            pltpu.VMEM((de, d), BF16),
                            pltpu.SMEM((1,), I32),
                            pltpu.SemaphoreType.DMA((2,))]),
        out_shape=jax.ShapeDtypeStruct((p, d // 2), I32),
        compiler_params=_cparams(("arbitrary",)),
        name="moe_expert_ffn",
    )(be, nu, xb, wg, wu, wd)


def _combine_kernel(d0_ref, d1_ref, x_ref, info_ref, g2_ref, yb_ref, o_ref, rows_scr, sem):
    i = pl.program_id(0)
    cur = i % 2

    def copy(buf, which, r, src_row, rows=1):
        return pltpu.make_async_copy(yb_ref.at[pl.ds(src_row, rows)],
                                     rows_scr.at[buf, which, pl.ds(r, rows)], sem.at[buf])

    def start_gather(step, buf):
        base = step * TOK_TILE

        def issue(g, c):
            for k in range(SUBLANES):
                r = g * SUBLANES + k
                copy(buf, 0, r, d0_ref[base + r]).start()
                copy(buf, 1, r, d1_ref[base + r]).start()
            return c

        lax.fori_loop(0, TOK_TILE // SUBLANES, issue, 0)

    @pl.when(i == 0)
    def _():
        start_gather(0, 0)

    @pl.when(i + 1 < pl.num_programs(0))
    def _():
        start_gather(i + 1, 1 - cur)

    copy(cur, 0, 0, 0, TOK_TILE).wait()
    copy(cur, 1, 0, 0, TOK_TILE).wait()

    info = info_ref[...]
    half = o_ref.shape[1] // 2
    y0_hi, y0_lo = _unpack_bf16_pairs(rows_scr[cur, 0])
    y1_hi, y1_lo = _unpack_bf16_pairs(rows_scr[cur, 1])
    g2 = g2_ref[0]
    o_ref[:, :half] = x_ref[:, :half] + g2[:, :half] * (info[:, 2:3] * y0_hi + info[:, 3:4] * y1_hi)
    o_ref[:, half:] = x_ref[:, half:] + g2[:, half:] * (info[:, 2:3] * y0_lo + info[:, 3:4] * y1_lo)


def _combine(d0, d1, x1, info, g2, yb, seq):
    n, d = x1.shape
    tpb = seq // TOK_TILE
    return pl.pallas_call(
        _combine_kernel,
        grid_spec=pltpu.PrefetchScalarGridSpec(
            num_scalar_prefetch=2,
            grid=(n // TOK_TILE,),
            in_specs=[pl.BlockSpec((TOK_TILE, d), lambda i, a, b: (i, 0)),
                      pl.BlockSpec((TOK_TILE, LANES), lambda i, a, b: (i, 0)),
                      pl.BlockSpec((1, 1, d), lambda i, a, b: (i // tpb, 0, 0)),
                      pl.BlockSpec(memory_space=pl.ANY)],
            out_specs=pl.BlockSpec((TOK_TILE, d), lambda i, a, b: (i, 0)),
            scratch_shapes=[pltpu.VMEM((2, 2, TOK_TILE, d // 2), I32),
                            pltpu.SemaphoreType.DMA((2,))]),
        out_shape=jax.ShapeDtypeStruct((n, d), F32),
        compiler_params=_cparams(("arbitrary",)),
        name="moe_combine",
    )(d0, d1, x1, info, g2, yb)


def _tri_inclusive_rev(k):
    l = (np.arange(k)[:, None] >= np.arange(k)[None, :]).astype(np.float32)
    half = np.concatenate([l, np.ones((k, k), np.float32)], axis=1)
    return jnp.asarray(np.concatenate([half, half], axis=0), dtype=BF16)


def _tri_strict_lower(k):
    return jnp.asarray((np.arange(k)[None, :] < np.arange(k)[:, None]).astype(np.float32), dtype=BF16)


def kernel(x, c, w_mod, b_mod, ln1_g, w_in, q_norm_g, w_q_up, q_gain, k_gain, rel_bias, gn_a, gn_b,
           w_out, ln2_g, router_g, router_e, w_gate, w_up, w_down):
    batch, seq, d = x.shape
    n = batch * seq
    assert w_mod.shape[0] == 1 and d == D_A + D_B
    assert seq % KB_A == 0 and KB_A % QB_A == 0 and seq % TOK_TILE == 0 and n % 512 == 0
    x2 = x.reshape(n, d)

    c8 = jnp.pad(c, ((0, 8 - batch), (0, 0)))
    mod = _modulation(c8, w_mod.reshape(d, -1), b_mod.reshape(1, -1))[:batch]
    sh1, sc1, g1, sh2, sc2, g2 = [m.reshape(batch, 1, d) for m in jnp.split(mod, 6, axis=-1)]

    wi = w_in.reshape(d, -1).astype(BF16)
    n_a = Q_RANK + 2 * HEAD_DIM + IDX_DIM + IDX_HEADS
    wi = jnp.concatenate([wi[:, n_a:], wi[:, :n_a],
                          jnp.zeros((d, D_IN_PAD - wi.shape[1]), wi.dtype)], axis=1)
    proj = _ln_proj(x2, ln1_g.reshape(1, d), sc1, sh1, wi, seq)

    qa, qidx, ka, va, kidx, widx = _aprep(
        proj, n, q_norm_g.reshape(1, -1), w_q_up.reshape(Q_RANK, -1).astype(BF16),
        q_gain.reshape(1, -1), k_gain.reshape(1, -1))

    bt = _bias_tiles(rel_bias)
    bound = _logit_bound(q_gain.reshape(1, -1), k_gain.reshape(1, -1), rel_bias)[0, :1]
    oa = _attn_a(qidx, widx.T, qa, kidx, ka, va, bt, gn_a.reshape(1, -1), bound, batch, seq)
    ob = _attn_b(proj, _tri_inclusive_rev(QB_B), batch, seq)

    rw = jnp.concatenate([router_e.reshape(d, -1), router_g.reshape(d, -1),
                          jnp.zeros((d, LANES - N_EXPERTS - N_GROUPS), F32)], axis=1)
    x1, h2, lg = _out_proj(x2, oa, ob, gn_b.reshape(1, -1), w_out.reshape(d, d).astype(BF16),
                           g1, ln2_g.reshape(1, d), sc2, sh2, rw, seq)

    info, cnt = _route(lg)
    counts = cnt[0, :N_EXPERTS].astype(I32)
    p_rows = 2 * n + N_EXPERTS * MOE_BLK
    ps, be, nu = _plan(counts, p_rows // MOE_BLK)
    ps_lanes = jnp.pad(ps.astype(F32), (0, LANES - N_EXPERTS)).reshape(1, LANES)
    dinfo = _dest(info, ps_lanes, _tri_strict_lower(512))
    d0 = dinfo[:, 0].astype(I32)
    d1 = dinfo[:, 1].astype(I32)
    xb = _scatter_rows(d0, d1, counts, ps, nu, h2, p_rows)
    yb = _experts(be, nu, xb,
                  w_gate.reshape(N_EXPERTS, d, D_EXPERT),
                  w_up.reshape(N_EXPERTS, d, D_EXPERT),
                  w_down.reshape(N_EXPERTS, D_EXPERT, d))
    out = _combine(d0, d1, x1, info, g2, yb, seq)
    return out.reshape(batch, seq, d)
```

```python
import functools
import math

import numpy as np
import jax
import jax.numpy as jnp
from jax import lax
from jax.experimental import pallas as pl
from jax.experimental.pallas import tpu as pltpu

F32 = jnp.float32
BF16 = jnp.bfloat16
I32 = jnp.int32

HEAD_DIM = 128
N_HEADS_A = 8
N_HEADS_B = 8
D_A = N_HEADS_A * HEAD_DIM
D_B = N_HEADS_B * HEAD_DIM
Q_RANK = 512
IDX_HEADS = 16
IDX_DIM = 64
TOPK_MAX = 256
N_BUCKETS = 32
MAX_DISTANCE = 128
N_GROUPS = 4
EXPERTS_PER_GROUP = 8
N_EXPERTS = N_GROUPS * EXPERTS_PER_GROUP
D_EXPERT = 512
EPS = 1e-6

LANES = 128
SUBLANES = 8
VMEM_LIMIT = 56 * 1024 * 1024
NEG = -1e30
INT_MIN = -(2 ** 31)
EXP2_UNDERFLOW = 150.0
LOG2E = math.log2(math.e)
PLAIN_SOFTMAX_LIMIT = 64.0
GUESS_HALF_WIDTH = 0.1
EXTRACT_BELOW = 3.0

QB_A = 128
KB_A = 256
QB_B = 128
MOE_BLK = 256
TOK_TILE = 512

COL_QB, COL_KB, COL_VB, COL_A = 0, D_B, 2 * D_B, 3 * D_B
A_CQ, A_KA, A_VA, A_KIDX, A_WIDX = 0, 512, 640, 768, 832
A_WIDTH = 1024
D_IN_PAD = COL_A + A_WIDTH


def _cparams(sem=None):
    return pltpu.CompilerParams(dimension_semantics=sem, vmem_limit_bytes=VMEM_LIMIT)


def _rms(x):
    return x * lax.rsqrt(jnp.mean(x * x, axis=-1, keepdims=True) + EPS)


def _pack_bf16_pairs(x):
    c = x.shape[1] // 2
    hi = pltpu.bitcast(x[:, :c].astype(BF16).astype(F32), I32)
    lo = pltpu.bitcast(x[:, c:].astype(BF16).astype(F32), I32)
    return hi | lax.shift_right_logical(lo, 16)


def _unpack_bf16_pairs(u):
    hi = pltpu.bitcast(u & jnp.int32(-65536), F32)
    lo = pltpu.bitcast(lax.shift_left(u, 16), F32)
    return hi, lo


def _mod_kernel(c_ref, w_ref, b_ref, o_ref):
    c = c_ref[...]
    s = c * (1.0 / (1.0 + jnp.exp(-c)))
    o_ref[...] = jnp.dot(s, w_ref[...], preferred_element_type=F32,
                         precision=lax.Precision.HIGHEST) + b_ref[...]


def _modulation(c8, w_mod, b_mod):
    d, n6 = w_mod.shape
    tn = 1024
    return pl.pallas_call(
        _mod_kernel,
        grid=(n6 // tn,),
        in_specs=[pl.BlockSpec((8, d), lambda j: (0, 0)),
                  pl.BlockSpec((d, tn), lambda j: (0, j)),
                  pl.BlockSpec((1, tn), lambda j: (0, j))],
        out_specs=pl.BlockSpec((8, tn), lambda j: (0, j)),
        out_shape=jax.ShapeDtypeStruct((8, n6), F32),
        compiler_params=_cparams(("arbitrary",)),
        name="modulation",
    )(c8, w_mod, b_mod)


def _ln_proj_kernel(x_ref, g_ref, sc_ref, sh_ref, w_ref, o_ref, h_scr, *, q_tiles, chunk):
    i = pl.program_id(0)
    j = pl.program_id(1)

    def normalise_chunk():
        rows = pl.ds(pl.multiple_of(j * chunk, chunk), chunk)
        h = _rms(x_ref[rows, :]) * g_ref[...]
        h = h * (1.0 + sc_ref[0]) + sh_ref[0]
        h_scr[i % 2, rows, :] = h.astype(BF16)

    @pl.when(i == 0)
    def _():
        normalise_chunk()
        o_ref[...] = jnp.zeros(o_ref.shape, o_ref.dtype)

    @pl.when(i > 0)
    def _():
        normalise_chunk()
        col_scale = jnp.where(j < q_tiles, HEAD_DIM ** -0.5 * LOG2E, 1.0)
        acc = jnp.dot(h_scr[(i - 1) % 2], w_ref[...], preferred_element_type=F32)
        o_ref[...] = (acc * col_scale).astype(o_ref.dtype)


def _ln_proj(x2, ln_g, sc, sh, w, seq):
    n, d = x2.shape
    ncol = w.shape[1]
    tm = min(1024, seq)
    tn = 1024
    tpb = seq // tm
    n_i, n_j = n // tm, ncol // tn
    assert COL_QB == 0 and D_B % tn == 0 and tm % (n_j * 16) == 0
    last = n_i - 1
    return pl.pallas_call(
        functools.partial(_ln_proj_kernel, q_tiles=D_B // tn, chunk=tm // n_j),
        grid=(n_i + 1, n_j),
        in_specs=[pl.BlockSpec((tm, d), lambda i, j: (jnp.minimum(i, last), 0)),
                  pl.BlockSpec((1, d), lambda i, j: (0, 0)),
                  pl.BlockSpec((1, 1, d), lambda i, j: (jnp.minimum(i, last) // tpb, 0, 0)),
                  pl.BlockSpec((1, 1, d), lambda i, j: (jnp.minimum(i, last) // tpb, 0, 0)),
                  pl.BlockSpec((d, tn), lambda i, j: (0, j))],
        out_specs=pl.BlockSpec((tm, tn), lambda i, j: (jnp.where(i == 0, n_i, i - 1), j)),
        out_shape=jax.ShapeDtypeStruct((n + tm, ncol), BF16),
        scratch_shapes=[pltpu.VMEM((2, tm, d), BF16)],
        compiler_params=_cparams(("arbitrary", "arbitrary")),
        name="ln_in_proj",
    )(x2, ln_g, sc, sh, w)


def _aprep_kernel(a_ref, qng_ref, wq_ref, qg_ref, kg_ref,
                  qa_ref, qidx_ref, ka_ref, va_ref, kidx_ref, widx_ref):
    cq = a_ref[:, A_CQ:A_CQ + Q_RANK].astype(F32)
    cqn = (_rms(cq) * qng_ref[...]).astype(BF16)
    qup = jnp.dot(cqn, wq_ref[...], preferred_element_type=F32)
    for h in range(N_HEADS_A):
        qh = qup[:, h * HEAD_DIM:(h + 1) * HEAD_DIM]
        qn = _rms(qh) * qg_ref[...] * (HEAD_DIM ** -0.5 * LOG2E)
        qa_ref[:, h * HEAD_DIM:(h + 1) * HEAD_DIM] = qn.astype(BF16)
    for h in range(IDX_HEADS):
        qi = qup[:, D_A + h * IDX_DIM:D_A + (h + 1) * IDX_DIM] * (IDX_DIM ** -0.5)
        qidx_ref[h] = qi.astype(BF16)
    ka = a_ref[:, A_KA:A_KA + HEAD_DIM].astype(F32)
    ka_ref[...] = (_rms(ka) * kg_ref[...]).astype(BF16)
    va_ref[...] = a_ref[:, A_VA:A_VA + HEAD_DIM]
    kidx_ref[...] = a_ref[:, A_KIDX:A_KIDX + IDX_DIM]
    widx_ref[...] = a_ref[:, A_WIDX:A_WIDX + IDX_HEADS].astype(F32) * (IDX_HEADS ** -0.5)


def _aprep(proj, n, q_norm_g, w_q_up, q_gain, k_gain):
    tm = 512
    cblk = COL_A // A_WIDTH
    nup = w_q_up.shape[1]
    return pl.pallas_call(
        _aprep_kernel,
        grid=(n // tm,),
        in_specs=[pl.BlockSpec((tm, A_WIDTH), lambda i: (i, cblk)),
                  pl.BlockSpec((1, Q_RANK), lambda i: (0, 0)),
                  pl.BlockSpec((Q_RANK, nup), lambda i: (0, 0)),
                  pl.BlockSpec((1, HEAD_DIM), lambda i: (0, 0)),
                  pl.BlockSpec((1, HEAD_DIM), lambda i: (0, 0))],
        out_specs=[pl.BlockSpec((tm, D_A), lambda i: (i, 0)),
                   pl.BlockSpec((IDX_HEADS, tm, IDX_DIM), lambda i: (0, i, 0)),
                   pl.BlockSpec((tm, HEAD_DIM), lambda i: (i, 0)),
                   pl.BlockSpec((tm, HEAD_DIM), lambda i: (i, 0)),
                   pl.BlockSpec((tm, IDX_DIM), lambda i: (i, 0)),
                   pl.BlockSpec((tm, IDX_HEADS), lambda i: (i, 0))],
        out_shape=[jax.ShapeDtypeStruct((n, D_A), BF16),
                   jax.ShapeDtypeStruct((IDX_HEADS, n, IDX_DIM), BF16),
                   jax.ShapeDtypeStruct((n, HEAD_DIM), BF16),
                   jax.ShapeDtypeStruct((n, HEAD_DIM), BF16),
                   jax.ShapeDtypeStruct((n, IDX_DIM), BF16),
                   jax.ShapeDtypeStruct((n, IDX_HEADS), F32)],
        compiler_params=_cparams(("parallel",)),
        name="group_a_prep",
    )(proj, q_norm_g, w_q_up, q_gain, k_gain)


def _t5_bucket_starts():
    max_exact = N_BUCKETS // 2
    d = np.arange(0, 4 * MAX_DISTANCE, dtype=np.int64)
    df = np.maximum(d, 1).astype(np.float32)
    large = max_exact + (np.log(df / np.float32(max_exact)) / np.float32(math.log(MAX_DISTANCE / max_exact))
                         * np.float32(N_BUCKETS - max_exact)).astype(np.int32)
    large = np.minimum(large, N_BUCKETS - 1)
    bucket = np.where(d < max_exact, d, large)
    assert np.all(np.diff(bucket) >= 0) and bucket[-1] == N_BUCKETS - 1
    return [int(np.argmax(bucket >= b)) for b in range(N_BUCKETS)]


_BUCKET_START = _t5_bucket_starts()
N_BIAS_TILES = 2 * KB_A // LANES


def _bias_kernel(rb_ref, o_ref):
    di = pl.program_id(0)
    h = pl.program_id(1)
    i = lax.broadcasted_iota(I32, (QB_A, KB_A), 0)
    j = lax.broadcasted_iota(I32, (QB_A, KB_A), 1)
    d = di * LANES + i - j
    val = jnp.full((QB_A, KB_A), rb_ref[0, h], F32)
    for b in range(1, N_BUCKETS):
        val = jnp.where(d >= _BUCKET_START[b], rb_ref[b, h], val)
    o_ref[0, 0] = (val - rb_ref[N_BUCKETS - 1, h]) * LOG2E


def _bias_tiles(rel_bias):
    return pl.pallas_call(
        _bias_kernel,
        grid=(N_BIAS_TILES, N_HEADS_A),
        in_specs=[pl.BlockSpec(memory_space=pltpu.SMEM)],
        out_specs=pl.BlockSpec((1, 1, QB_A, KB_A), lambda a, h: (a, h, 0, 0)),
        out_shape=jax.ShapeDtypeStruct((N_BIAS_TILES, N_HEADS_A, QB_A, KB_A), F32),
        compiler_params=_cparams(("arbitrary", "arbitrary")),
        name="t5_bias_tiles",
    )(rel_bias)


def _bound_kernel(qg_ref, kg_ref, rb_ref, o_ref):
    qmax = jnp.max(jnp.abs(qg_ref[...]), axis=1, keepdims=True)
    kmax = jnp.max(jnp.abs(kg_ref[...]), axis=1, keepdims=True)
    rb = rb_ref[...]
    shifted = jnp.abs(rb - rb[N_BUCKETS - 1:N_BUCKETS, :])
    bmax = jnp.max(jnp.max(shifted, axis=1, keepdims=True), axis=0, keepdims=True)
    bound = qmax * kmax * (math.sqrt(HEAD_DIM) * 1.02) + bmax
    o_ref[...] = jnp.broadcast_to(bound, o_ref.shape)


def _logit_bound(q_gain, k_gain, rel_bias):
    return pl.pallas_call(
        _bound_kernel,
        out_shape=jax.ShapeDtypeStruct((1, LANES), F32),
        name="dsa_logit_bound",
    )(q_gain, k_gain, rel_bias)


def _attn_a_kernel(qidx_ref, w_ref, qa_ref, kidx_ref, ka_ref, va_ref, bt_ref, gn_ref, bound_ref, o_ref,
                   keys_scr, keyt_scr, m_scr, acc_scr, s_scr, mb_scr, *, topk):
    qb = pl.program_id(1)
    t0 = qb * QB_A
    kbl = (t0 + QB_A - 1) // KB_A
    row = t0 + lax.broadcasted_iota(I32, (QB_A, KB_A), 0)
    col0 = lax.broadcasted_iota(I32, (QB_A, KB_A), 1)
    nt = (((1,), (1,)), ((), ()))

    reps = KB_A // LANES

    def sort_key(v):
        bits = pltpu.bitcast(v, I32)
        return jnp.where(bits < 0, bits ^ jnp.int32(0x7FFFFFFF), bits)

    def key_value(k):
        return pltpu.bitcast(jnp.where(k < 0, k ^ jnp.int32(0x7FFFFFFF), k), F32)

    key_pos = lax.broadcasted_iota(I32, (KB_A, QB_A), 0)
    qry_pos = t0 + lax.broadcasted_iota(I32, (KB_A, QB_A), 1)

    def score_tile(kb, carry, diagonal):
        smin, smax, s1, s2 = carry
        kt = kidx_ref[pl.ds(pl.multiple_of(kb * KB_A, KB_A), KB_A), :]
        score = jnp.zeros((KB_A, QB_A), F32)
        for hp in range(IDX_HEADS // 2):
            q2 = qidx_ref[2 * hp:2 * hp + 2].reshape(2 * QB_A, IDX_DIM)
            sc = lax.dot_general(kt, q2, nt, preferred_element_type=F32)
            score = score + w_ref[2 * hp:2 * hp + 1, :] * jnp.maximum(sc[:, :QB_A], 0.0)
            score = score + w_ref[2 * hp + 1:2 * hp + 2, :] * jnp.maximum(sc[:, QB_A:], 0.0)
        key_t = sort_key(score)
        low, high, live = score, score, score
        if diagonal:
            causal = (kb * KB_A + key_pos) <= qry_pos
            key_t = jnp.where(causal, key_t, jnp.int32(INT_MIN))
            low = jnp.where(causal, score, jnp.inf)
            high = jnp.where(causal, score, -jnp.inf)
            live = jnp.where(causal, score, 0.0)
        keyt_scr[kb] = key_t
        keys_scr[kb] = key_t.T
        smin = jnp.minimum(smin, jnp.min(low, axis=0, keepdims=True))
        smax = jnp.maximum(smax, jnp.max(high, axis=0, keepdims=True))
        s1 = s1 + jnp.sum(live, axis=0, keepdims=True)
        s2 = s2 + jnp.sum(live * live, axis=0, keepdims=True)
        return smin, smax, s1, s2

    def score_group(first, count, carry):
        for u in range(count):
            carry = score_tile(first + u, carry, False)
        return carry

    zero_row = jnp.zeros((1, QB_A), F32)
    stats = lax.fori_loop(0, kbl // 4, lambda i, c: score_group(4 * i, 4, c),
                          (jnp.full((1, QB_A), jnp.inf, F32), jnp.full((1, QB_A), -jnp.inf, F32),
                           zero_row, zero_row))
    stats = lax.cond(kbl % 4 >= 2, lambda c: score_group((kbl // 4) * 4, 2, c), lambda c: c, stats)
    stats = lax.cond(kbl % 2 == 1, lambda c: score_tile(kbl - 1, c, False), lambda c: c, stats)
    smin, smax, s1, s2 = score_tile(kbl, stats, True)

    qry1 = t0 + lax.broadcasted_iota(I32, (1, QB_A), 1)
    kf = float(topk)
    acc_rows = 32
    n_causal = (qry1 + 1).astype(F32)

    mean = s1 / n_causal
    dev = jnp.sqrt(jnp.maximum(s2 / n_causal - mean * mean, 0.0))
    frac_top = jnp.minimum(kf / n_causal, 1.0)
    tail = jnp.minimum(frac_top, 1.0 - frac_top)
    tq = jnp.sqrt(-2.0 * jnp.log(jnp.maximum(tail, 1e-6)))
    zq = tq - ((0.010328 * tq + 0.802853) * tq + 2.515517) / (((0.001308 * tq + 0.189269) * tq + 1.432788) * tq + 1.0)
    zq = jnp.where(frac_top > 0.5, -zq, zq)

    def for_each_key_tile(fold, init):
        n_all = kbl + 1
        acc = lax.fori_loop(0, n_all // 2,
                            lambda i, a: fold(fold(a, keyt_scr[2 * i]), keyt_scr[2 * i + 1]), init)
        return lax.cond(n_all % 2 == 1, lambda a: fold(a, keyt_scr[n_all - 1]), lambda a: a, acc)

    def search_pass(p, state):
        lo, hi, clo, chi, open_q = state
        lo_v = key_value(lo)
        hi_v = key_value(hi)
        gap = clo - chi
        frac = (clo - (kf - 0.5)) / gap
        frac = jnp.where(p % 2 == 1, 0.7 * frac + 0.15, frac)
        frac = jnp.where(gap > 16.0, frac, 0.5)
        value = lo_v + (hi_v - lo_v) * frac
        value = jnp.where(p == 0, mean + (zq - GUESS_HALF_WIDTH) * dev, value)
        value = jnp.where(p == 1, mean + (zq + GUESS_HALF_WIDTH) * dev, value)
        cand = sort_key(value)
        cand = jnp.where(p % 8 == 7, lo + lax.shift_right_logical(hi - lo, 1), cand)
        cand = jnp.minimum(jnp.maximum(cand, lo + 1), hi - 1)

        def count(cnt, keys):
            ge = jnp.where(keys >= cand, 1.0, 0.0)
            return cnt + jnp.sum(ge.reshape(KB_A // acc_rows, acc_rows, QB_A), axis=0)

        tot = jnp.sum(for_each_key_tile(count, jnp.zeros((acc_rows, QB_A), F32)), axis=0, keepdims=True)
        ge = tot >= kf
        lo = jnp.where(ge, cand, lo)
        clo = jnp.where(ge, tot, clo)
        hi = jnp.where(ge, hi, cand)
        chi = jnp.where(ge, chi, tot)
        width = hi - lo
        settled = jnp.where(clo == kf, 1.0, jnp.where(width == 1, 1.0, 0.0))
        open_q = jnp.where(settled > 0.5, 0.0, open_q)
        return lo, hi, clo, chi, open_q

    def extract_pass(state):
        lo, hi, clo, chi, open_q = state

        def top_below(best, keys):
            below = jnp.where(keys < hi, keys, jnp.int32(INT_MIN))
            return jnp.maximum(best, jnp.max(below.reshape(KB_A // acc_rows, acc_rows, QB_A), axis=0))

        best = for_each_key_tile(top_below, jnp.full((acc_rows, QB_A), INT_MIN, I32))
        best = jnp.max(best, axis=0, keepdims=True)
        is_open = open_q > 0.5
        last = jnp.logical_and(is_open, kf - chi <= 1.0)
        more = jnp.logical_and(is_open, kf - chi > 1.0)
        lo = jnp.where(last, best, lo)
        clo = jnp.where(last, chi + 1.0, clo)
        hi = jnp.where(more, best, hi)
        chi = jnp.where(more, chi + 1.0, chi)
        open_q = jnp.where(last, 0.0, open_q)
        return lo, hi, clo, chi, open_q

    def missing(state):
        return jnp.max(jnp.where(state[4] > 0.5, kf - state[3], 0.0))

    def search_step(carry):
        p, state, lacking = carry
        state = lax.cond(lacking <= EXTRACT_BELOW,
                         lambda st: extract_pass(st),
                         lambda st: search_pass(p + 1, search_pass(p, st)), state)
        return p + 2, state, missing(state)

    open0 = jnp.where(qry1 >= topk, 1.0, 0.0)
    lo0 = sort_key(smin)
    hi0 = sort_key(smax) + 1
    open0 = jnp.where(hi0 - lo0 == 1, 0.0, open0)
    state0 = (lo0, hi0, n_causal, zero_row, open0)
    _, (lo, _, _, _, _), _ = lax.while_loop(
        lambda c: c[2] > 0.0, search_step, (jnp.int32(0), state0, missing(state0)))
    thr = jnp.where(qry1 >= topk, lo, jnp.int32(INT_MIN))
    thr = jnp.broadcast_to(thr, (QB_A, QB_A)).T
    thrb = jnp.tile(thr, (1, reps))

    m_scr[...] = jnp.full(m_scr.shape, NEG, F32)
    acc_scr[...] = jnp.zeros(acc_scr.shape, F32)

    plain = bound_ref[0] <= PLAIN_SOFTMAX_LIMIT

    def attend(kb, near, online):
        mb = jnp.where(keys_scr[kb] >= thrb, 0.0, NEG)
        if near:
            mb = jnp.where((kb * KB_A + col0) <= row, mb, NEG)
            di = (t0 - kb * KB_A) // LANES
        mb_scr[...] = mb
        start = pl.multiple_of(kb * KB_A, KB_A)
        kt = ka_ref[pl.ds(start, KB_A), :]
        vt = jnp.concatenate([va_ref[pl.ds(start, KB_A), :], jnp.ones((KB_A, LANES), BF16)], axis=1)
        for h in range(N_HEADS_A):
            q = qa_ref[:, h * HEAD_DIM:(h + 1) * HEAD_DIM]
            s_scr[h] = lax.dot_general(q, kt, nt, preferred_element_type=F32)
        for h in range(N_HEADS_A):
            s = s_scr[h] + mb_scr[...]
            if near:
                s = s + bt_ref[di, h]
            if online:
                m_prev = m_scr[h]
                m_new = jnp.maximum(m_prev, jnp.max(s, axis=1, keepdims=True))
                alpha = jnp.exp2(m_prev - m_new)
                p = jnp.exp2(s - jnp.tile(m_new, (1, reps)))
                acc_scr[h] = (jnp.tile(alpha, (1, 2)) * acc_scr[h]
                              + jnp.dot(p.astype(BF16), vt, preferred_element_type=F32))
                m_scr[h] = m_new
            else:
                acc_scr[h] += jnp.dot(jnp.exp2(s).astype(BF16), vt, preferred_element_type=F32)

    def attend_all(online):
        prev_near = jnp.logical_and(kbl >= 1, t0 - kbl * KB_A < _BUCKET_START[-1] - 1)
        n_far = jnp.where(prev_near, kbl - 1, kbl)
        group = 2 if online else 4

        def far_group(i, carry):
            for u in range(group):
                attend(group * i + u, False, online)
            return carry

        lax.fori_loop(0, n_far // group, far_group, 0)

        if group == 4:
            @pl.when(n_far % 4 >= 2)
            def _():
                attend((n_far // 4) * 4, False, online)
                attend((n_far // 4) * 4 + 1, False, online)

        @pl.when(n_far % 2 == 1)
        def _():
            attend(n_far - 1, False, online)

        @pl.when(prev_near)
        def _():
            attend(kbl - 1, True, online)
            attend(kbl, True, online)

        @pl.when(jnp.logical_not(prev_near))
        def _():
            attend(kbl, True, online)

    @pl.when(plain)
    def _():
        attend_all(False)

    @pl.when(jnp.logical_not(plain))
    def _():
        attend_all(True)

    ssq = jnp.zeros((QB_A, LANES), F32)
    for h in range(N_HEADS_A):
        oh = acc_scr[h, :, :HEAD_DIM] / acc_scr[h, :, HEAD_DIM:]
        acc_scr[h, :, :HEAD_DIM] = oh
        ssq = ssq + jnp.sum(oh * oh, axis=1, keepdims=True)
    inv = lax.rsqrt(ssq * (1.0 / D_A) + EPS)
    for h in range(N_HEADS_A):
        sl = slice(h * HEAD_DIM, (h + 1) * HEAD_DIM)
        o_ref[:, sl] = (acc_scr[h, :, :HEAD_DIM] * inv * gn_ref[:, sl]).astype(o_ref.dtype)


def _attn_a(qidx, widx, qa, kidx, ka, va, bt, gn_a, bound, batch, seq):
    n = qa.shape[0]
    nq = seq // QB_A
    nkt = seq // KB_A
    topk = min(TOPK_MAX, seq // 4)
    return pl.pallas_call(
        functools.partial(_attn_a_kernel, topk=topk),
        grid=(batch, nq),
        in_specs=[pl.BlockSpec((IDX_HEADS, QB_A, IDX_DIM), lambda b, q: (0, b * nq + q, 0)),
                  pl.BlockSpec((IDX_HEADS, QB_A), lambda b, q: (0, b * nq + q)),
                  pl.BlockSpec((QB_A, D_A), lambda b, q: (b * nq + q, 0)),
                  pl.BlockSpec((seq, IDX_DIM), lambda b, q: (b, 0)),
                  pl.BlockSpec((seq, HEAD_DIM), lambda b, q: (b, 0)),
                  pl.BlockSpec((seq, HEAD_DIM), lambda b, q: (b, 0)),
                  pl.BlockSpec(bt.shape, lambda b, q: (0, 0, 0, 0)),
                  pl.BlockSpec((1, D_A), lambda b, q: (0, 0)),
                  pl.BlockSpec(memory_space=pltpu.SMEM)],
        out_specs=pl.BlockSpec((QB_A, D_A), lambda b, q: (b * nq + q, 0)),
        out_shape=jax.ShapeDtypeStruct((n, D_A), BF16),
        scratch_shapes=[pltpu.VMEM((nkt, QB_A, KB_A), I32),
                        pltpu.VMEM((nkt, KB_A, QB_A), I32),
                        pltpu.VMEM((N_HEADS_A, QB_A, LANES), F32),
                        pltpu.VMEM((N_HEADS_A, QB_A, 2 * HEAD_DIM), F32),
                        pltpu.VMEM((N_HEADS_A, QB_A, KB_A), F32),
                        pltpu.VMEM((QB_A, KB_A), F32)],
        compiler_params=_cparams(("parallel", "arbitrary")),
        name="dsa_attention",
    )(qidx, widx, qa, kidx, ka, va, bt, gn_a, bound)


def _attn_b_kernel(q_ref, k_ref, v_ref, tri_ref, o_ref, rest_scr, z_scr, cs_scr):
    qb = pl.program_id(1)
    row = lax.broadcasted_iota(I32, (QB_B, QB_B), 0)
    col = lax.broadcasted_iota(I32, (QB_B, QB_B), 1)
    strict = col < row
    nt = (((1,), (1,)), ((), ()))
    def step(kb, diag):
        start = pl.multiple_of(kb * QB_B, QB_B)
        heads = [slice(h * HEAD_DIM, (h + 1) * HEAD_DIM) for h in range(N_HEADS_B)]
        for h, sl in enumerate(heads):
            kt = k_ref[pl.ds(start, QB_B), sl]
            z_scr[h] = lax.dot_general(q_ref[:, sl], kt, nt, preferred_element_type=F32)
        for h, sl in enumerate(heads):
            z = z_scr[h]
            sp = jnp.maximum(z, 0.0) + jnp.log(1.0 + jnp.exp2(-jnp.abs(z))) * LOG2E
            if diag:
                sp = jnp.where(strict, sp, 0.0)
            hi = sp.astype(BF16)
            lo = (sp - hi.astype(F32)).astype(BF16)
            cs_scr[h] = jnp.dot(jnp.concatenate([hi, lo], axis=1), tri_ref[...],
                                preferred_element_type=F32)
        worst = None
        for h, sl in enumerate(heads):
            vt = v_ref[pl.ds(start, QB_B), sl]
            z = z_scr[h]
            cs = cs_scr[h, :, :QB_B]
            tot = cs_scr[h, :, QB_B:]
            if diag:
                a = jnp.where(strict, jnp.exp2(z - cs), 0.0)
                o_ref[:, sl] = jnp.dot(a.astype(BF16), vt, preferred_element_type=F32)
                rest = tot
            else:
                rest = rest_scr[h]
                a = jnp.exp2(z - cs - rest)
                o_ref[:, sl] += jnp.dot(a.astype(BF16), vt, preferred_element_type=F32)
                rest = rest + tot
            rest_scr[h] = rest
            worst = rest if worst is None else jnp.minimum(worst, rest)
        return jnp.min(worst)

    def more(kb, smallest):
        return jnp.logical_and(kb >= 0, smallest < EXP2_UNDERFLOW)

    def body(carry):
        kb, _ = carry
        return kb - 1, more(kb - 1, step(kb, False))

    def first_tiles(have_previous):
        smallest = step(qb, True)
        if have_previous:
            smallest = step(qb - 1, False)
        return smallest

    smallest = lax.cond(qb >= 1, lambda: first_tiles(True), lambda: first_tiles(False))
    lax.while_loop(lambda c: c[1], body, (qb - 2, more(qb - 2, smallest)))


def _attn_b(proj, tri, batch, seq):
    n = batch * seq
    nq = seq // QB_B
    return pl.pallas_call(
        _attn_b_kernel,
        grid=(batch, nq),
        in_specs=[pl.BlockSpec((QB_B, D_B), lambda b, q: (b * nq + q, COL_QB // D_B)),
                  pl.BlockSpec((seq, D_B), lambda b, q: (b, COL_KB // D_B)),
                  pl.BlockSpec((seq, D_B), lambda b, q: (b, COL_VB // D_B)),
                  pl.BlockSpec(tri.shape, lambda b, q: (0, 0))],
        out_specs=pl.BlockSpec((QB_B, D_B), lambda b, q: (b * nq + q, 0)),
        out_shape=jax.ShapeDtypeStruct((n, D_B), F32),
        scratch_shapes=[pltpu.VMEM((N_HEADS_B, QB_B, QB_B), F32),
                        pltpu.VMEM((N_HEADS_B, QB_B, QB_B), F32),
                        pltpu.VMEM((N_HEADS_B, QB_B, 2 * QB_B), F32)],
        compiler_params=_cparams(("parallel", "arbitrary")),
        name="stick_breaking_attention",
    )(proj, proj, proj, tri)


def _out_proj_kernel(x_ref, oa_ref, ob_ref, gnb_ref, w_ref, g1_ref, ln_ref, sc_ref, sh_ref, rw_ref,
                     x1_ref, h2_ref, lg_ref):
    obn = (_rms(ob_ref[...]) * gnb_ref[...]).astype(BF16)
    y = jnp.dot(oa_ref[...], w_ref[0:D_A, :], preferred_element_type=F32)
    y = y + jnp.dot(obn, w_ref[D_A:D_A + D_B, :], preferred_element_type=F32)
    x1 = x_ref[...] + g1_ref[0] * y
    x1_ref[...] = x1
    h2 = _rms(x1) * ln_ref[...]
    h2 = h2 * (1.0 + sc_ref[0]) + sh_ref[0]
    h2_ref[...] = _pack_bf16_pairs(h2)
    hh = h2.astype(BF16)
    hl = (h2 - hh.astype(F32)).astype(BF16)
    rw = rw_ref[...]
    rh = rw.astype(BF16)
    rl = (rw - rh.astype(F32)).astype(BF16)
    both = jnp.dot(hh, jnp.concatenate([rh, rl], axis=1), preferred_element_type=F32)
    lg_ref[...] = both[:, :LANES] + both[:, LANES:] + jnp.dot(hl, rh, preferred_element_type=F32)


def _out_proj(x2, oa, ob, gn_b, w_out, g1, ln_g, sc, sh, rw, seq):
    n, d = x2.shape
    tm = 512
    tpb = seq // tm
    row = lambda i: (i, 0)
    fixed = lambda i: (0, 0)
    perb = lambda i: (i // tpb, 0, 0)
    return pl.pallas_call(
        _out_proj_kernel,
        grid=(n // tm,),
        in_specs=[pl.BlockSpec((tm, d), row),
                  pl.BlockSpec((tm, D_A), row),
                  pl.BlockSpec((tm, D_B), row),
                  pl.BlockSpec((1, D_B), fixed),
                  pl.BlockSpec(w_out.shape, fixed),
                  pl.BlockSpec((1, 1, d), perb),
                  pl.BlockSpec((1, d), fixed),
                  pl.BlockSpec((1, 1, d), perb),
                  pl.BlockSpec((1, 1, d), perb),
                  pl.BlockSpec(rw.shape, fixed)],
        out_specs=[pl.BlockSpec((tm, d), row),
                   pl.BlockSpec((tm, d // 2), row),
                   pl.BlockSpec((tm, LANES), row)],
        out_shape=[jax.ShapeDtypeStruct((n, d), F32),
                   jax.ShapeDtypeStruct((n, d // 2), I32),
                   jax.ShapeDtypeStruct((n, LANES), F32)],
        compiler_params=_cparams(("parallel",)),
        name="out_proj_ln2_router",
    )(x2, oa, ob, gn_b, w_out, g1, ln_g, sc, sh, rw)


def _route_kernel(lg_ref, info_ref, cnt_ref):
    @pl.when(pl.program_id(0) == 0)
    def _():
        cnt_ref[...] = jnp.zeros(cnt_ref.shape, F32)

    lg = lg_ref[...]
    lane = lax.broadcasted_iota(I32, lg.shape, 1)
    lanef = lane.astype(F32)
    big = float(4 * LANES)
    gm = jnp.where(lane >= N_EXPERTS, jnp.where(lane < N_EXPERTS + N_GROUPS, 1.0, 0.0), 0.0) > 0.5
    lgm = jnp.where(gm, lg, NEG)
    mg = jnp.max(lgm, axis=1, keepdims=True)
    eg = jnp.where(gm, jnp.exp(lgm - mg), 0.0)
    pg = eg / jnp.sum(eg, axis=1, keepdims=True)
    gw = jnp.max(pg, axis=1, keepdims=True)
    gidx = jnp.min(jnp.where(gm, jnp.where(pg == gw, lanef - N_EXPERTS, big), big), axis=1, keepdims=True)
    lane_group = (lane // EXPERTS_PER_GROUP).astype(F32)
    em = jnp.where(lane < N_EXPERTS, jnp.where(lane_group == gidx, 1.0, 0.0), 0.0) > 0.5
    lem = jnp.where(em, lg, NEG)
    me = jnp.max(lem, axis=1, keepdims=True)
    ee = jnp.where(em, jnp.exp(lem - me), 0.0)
    pe = jnp.where(em, ee / jnp.sum(ee, axis=1, keepdims=True), -1.0)
    p1 = jnp.max(pe, axis=1, keepdims=True)
    i1 = jnp.min(jnp.where(pe == p1, lanef, big), axis=1, keepdims=True)
    pe2 = jnp.where(lanef == i1, -1.0, pe)
    p2 = jnp.max(pe2, axis=1, keepdims=True)
    i2 = jnp.min(jnp.where(pe2 == p2, lanef, big), axis=1, keepdims=True)
    den = p1 + p2
    g0 = gw * p1 / den
    g1 = gw * p2 / den
    info = jnp.where(lane == 0, i1, jnp.where(lane == 1, i2,
                     jnp.where(lane == 2, g0, jnp.where(lane == 3, g1, 0.0))))
    info_ref[...] = info
    oh = jnp.where(lanef == i1, 1.0, 0.0) + jnp.where(lanef == i2, 1.0, 0.0)
    cnt_ref[...] += jnp.sum(oh, axis=0, keepdims=True)


def _route(lg):
    n = lg.shape[0]
    tm = min(1024, n)
    return pl.pallas_call(
        _route_kernel,
        grid=(n // tm,),
        in_specs=[pl.BlockSpec((tm, LANES), lambda i: (i, 0))],
        out_specs=[pl.BlockSpec((tm, LANES), lambda i: (i, 0)),
                   pl.BlockSpec((1, LANES), lambda i: (0, 0))],
        out_shape=[jax.ShapeDtypeStruct((n, LANES), F32),
                   jax.ShapeDtypeStruct((1, LANES), F32)],
        compiler_params=_cparams(("arbitrary",)),
        name="moe_route",
    )(lg)


def _plan_kernel(cnt_ref, ps_ref, be_ref, nu_ref, *, nblk):
    def fill(i, c):
        be_ref[i] = N_EXPERTS - 1
        return c

    lax.fori_loop(0, nblk, fill, 0)

    def per_expert(e, pos):
        ps_ref[e] = pos * MOE_BLK
        nb = (cnt_ref[e] + MOE_BLK - 1) // MOE_BLK

        def mark(k, c):
            be_ref[pos + k] = e
            return c

        lax.fori_loop(0, nb, mark, 0)
        return pos + nb

    nu_ref[0] = lax.fori_loop(0, N_EXPERTS, per_expert, jnp.int32(0))


def _plan(counts, nblk):
    smem = pl.BlockSpec(memory_space=pltpu.SMEM)
    return pl.pallas_call(
        functools.partial(_plan_kernel, nblk=nblk),
        in_specs=[smem],
        out_specs=[smem, smem, smem],
        out_shape=[jax.ShapeDtypeStruct((N_EXPERTS,), I32),
                   jax.ShapeDtypeStruct((nblk,), I32),
                   jax.ShapeDtypeStruct((1,), I32)],
        name="moe_block_plan",
    )(counts)


def _dest_kernel(info_ref, ps_ref, tri_ref, o_ref, carry_scr):
    @pl.when(pl.program_id(0) == 0)
    def _():
        carry_scr[...] = jnp.zeros(carry_scr.shape, F32)

    info = info_ref[...]
    lane = lax.broadcasted_iota(I32, info.shape, 1)
    lanef = lane.astype(F32)
    o1 = jnp.where(lanef == info[:, 0:1], 1.0, 0.0)
    o2 = jnp.where(lanef == info[:, 1:2], 1.0, 0.0)
    oh = o1 + o2
    before = jnp.dot(tri_ref[...], oh.astype(BF16), preferred_element_type=F32)
    base = before + carry_scr[...] + ps_ref[...]
    d1 = jnp.sum(o1 * base, axis=1, keepdims=True)
    d2 = jnp.sum(o2 * base, axis=1, keepdims=True)
    o_ref[...] = jnp.where(lane == 0, d1, jnp.where(lane == 1, d2, 0.0))
    carry_scr[...] += jnp.sum(oh, axis=0, keepdims=True)


def _dest(info, ps_lanes, tri):
    n = info.shape[0]
    tm = tri.shape[0]
    return pl.pallas_call(
        _dest_kernel,
        grid=(n // tm,),
        in_specs=[pl.BlockSpec((tm, LANES), lambda i: (i, 0)),
                  pl.BlockSpec((1, LANES), lambda i: (0, 0)),
                  pl.BlockSpec((tm, tm), lambda i: (0, 0))],
        out_specs=pl.BlockSpec((tm, LANES), lambda i: (i, 0)),
        out_shape=jax.ShapeDtypeStruct((n, LANES), F32),
        scratch_shapes=[pltpu.VMEM((1, LANES), F32)],
        compiler_params=_cparams(("arbitrary",)),
        name="moe_dest_rows",
    )(info, ps_lanes, tri)


SCATTER_SLOTS = 3


def _scatter_kernel(d0_ref, d1_ref, cnt_ref, ps_ref, nu_ref, h_ref, xb_ref,
                    zero_scr, stage_scr, load_sem, out_sem, zero_sem, *, nblk):
    n_tiles = h_ref.shape[0] // TOK_TILE
    zero_scr[...] = jnp.zeros(zero_scr.shape, zero_scr.dtype)
    pad_sizes = [s for s in (1 << k for k in range(MOE_BLK.bit_length() - 2, -1, -1))
                 if s >= SUBLANES]

    def zero_copy(rows, dst_row):
        return pltpu.make_async_copy(zero_scr.at[pl.ds(0, rows)], xb_ref.at[pl.ds(dst_row, rows)],
                                     zero_sem)

    def load(j, slot):
        return pltpu.make_async_copy(h_ref.at[pl.ds(j * TOK_TILE, TOK_TILE)], stage_scr.at[slot],
                                     load_sem.at[slot])

    def row_copy(slot, r, dst_row, parity, rows=1):
        return pltpu.make_async_copy(stage_scr.at[slot, pl.ds(r, rows)],
                                     xb_ref.at[pl.ds(dst_row, rows)], out_sem.at[parity])

    def for_each_zero_copy(act):
        def per_expert(e, c):
            pos = ps_ref[e] + cnt_ref[e]
            pad = (MOE_BLK - (cnt_ref[e] & (MOE_BLK - 1))) & (MOE_BLK - 1)
            head = pad & (SUBLANES - 1)
            for k in range(SUBLANES - 1):
                @pl.when(k < head)
                def _():
                    act(zero_copy(1, pos + k))
            pos = pl.multiple_of(pos + head, SUBLANES)
            for size in pad_sizes:
                @pl.when((pad & size) != 0)
                def _():
                    act(zero_copy(size, pos))
                pos = pl.multiple_of(pos + (pad & size), SUBLANES)
            return c

        def per_tail_block(i, c):
            act(zero_copy(MOE_BLK, i * MOE_BLK))
            return c

        lax.fori_loop(0, N_EXPERTS, per_expert, 0)
        lax.fori_loop(nu_ref[0], nblk, per_tail_block, 0)

    for_each_zero_copy(lambda cp: cp.start())

    def retire(parity):
        row_copy(0, 0, 0, parity, TOK_TILE).wait()
        row_copy(0, 0, 0, parity, TOK_TILE).wait()

    def tile(j, c):
        slot = j % SCATTER_SLOTS
        parity = j % 2
        load(j, slot).wait()

        @pl.when(j + 1 < n_tiles)
        def _():
            load(j + 1, (j + 1) % SCATTER_SLOTS).start()

        def issue(g, cc):
            for k in range(SUBLANES):
                r = g * SUBLANES + k
                t = j * TOK_TILE + r
                row_copy(slot, r, d0_ref[t], parity).start()
                row_copy(slot, r, d1_ref[t], parity).start()
            return cc

        lax.fori_loop(0, TOK_TILE // SUBLANES, issue, 0)

        @pl.when(j >= 1)
        def _():
            retire(1 - parity)
        return c

    load(0, 0).start()
    lax.fori_loop(0, n_tiles, tile, 0)
    retire((n_tiles - 1) % 2)
    for_each_zero_copy(lambda cp: cp.wait())


def _scatter_rows(d0, d1, counts, ps, nu, h2, p_rows):
    n, d = h2.shape
    smem = pl.BlockSpec(memory_space=pltpu.SMEM)
    return pl.pallas_call(
        functools.partial(_scatter_kernel, nblk=p_rows // MOE_BLK),
        in_specs=[smem, smem, smem, smem, smem, pl.BlockSpec(memory_space=pl.ANY)],
        out_specs=pl.BlockSpec(memory_space=pl.ANY),
        out_shape=jax.ShapeDtypeStruct((p_rows, d), h2.dtype),
        scratch_shapes=[pltpu.VMEM((MOE_BLK, d), h2.dtype),
                        pltpu.VMEM((SCATTER_SLOTS, TOK_TILE, d), h2.dtype),
                        pltpu.SemaphoreType.DMA((SCATTER_SLOTS,)),
                        pltpu.SemaphoreType.DMA((2,)),
                        pltpu.SemaphoreType.DMA(())],
        compiler_params=_cparams(),
        name="moe_scatter_rows",
    )(d0, d1, counts, ps, nu, h2)


def _expert_kernel(be_ref, nu_ref, x_ref, wg_hbm, wu_hbm, wd_hbm, o_ref,
                   wg_f32, wu_f32, wd_f32, wg_scr, wu_scr, wd_scr, slot_ref, sem):
    i = pl.program_id(0)
    n_used = nu_ref[0]
    expert = be_ref[i]
    new_expert = jnp.logical_or(i == 0, expert != be_ref[jnp.maximum(i - 1, 0)])

    def weight_copies(e, slot):
        return [pltpu.make_async_copy(src.at[e], dst.at[slot], sem.at[slot])
                for src, dst in ((wg_hbm, wg_f32), (wu_hbm, wu_f32), (wd_hbm, wd_f32))]

    @pl.when(i == 0)
    def _():
        slot_ref[0] = 0
        for cp in weight_copies(expert, 0):
            cp.start()

    @pl.when(jnp.logical_and(new_expert, i < n_used))
    def _():
        slot = slot_ref[0]
        for cp in weight_copies(expert, slot):
            cp.wait()
        nxt = lax.while_loop(lambda j: jnp.logical_and(j < n_used, be_ref[jnp.minimum(j, n_used - 1)] == expert),
                             lambda j: j + 1, i + 1)

        @pl.when(nxt < n_used)
        def _():
            for cp in weight_copies(be_ref[nxt], 1 - slot):
                cp.start()

        wg_scr[...] = wg_f32[slot].astype(BF16)
        wu_scr[...] = wu_f32[slot].astype(BF16)
        wd_scr[...] = wd_f32[slot].astype(BF16)
        slot_ref[0] = 1 - slot

    @pl.when(i < n_used)
    def _():
        x_hi, x_lo = _unpack_bf16_pairs(x_ref[...])
        x = jnp.concatenate([x_hi.astype(BF16), x_lo.astype(BF16)], axis=1)
        g = jnp.dot(x, wg_scr[...], preferred_element_type=F32)
        u = jnp.dot(x, wu_scr[...], preferred_element_type=F32)
        act = (g * (1.0 / (1.0 + jnp.exp(-g))) * u).astype(BF16)
        o_ref[...] = _pack_bf16_pairs(jnp.dot(act, wd_scr[...], preferred_element_type=F32))

    @pl.when(i >= nu_ref[0])
    def _():
        o_ref[...] = jnp.zeros(o_ref.shape, o_ref.dtype)


def _experts(be, nu, xb, wg, wu, wd):
    p = xb.shape[0]
    _, d, de = wg.shape
    return pl.pallas_call(
        _expert_kernel,
        grid_spec=pltpu.PrefetchScalarGridSpec(
            num_scalar_prefetch=2,
            grid=(p // MOE_BLK,),
            in_specs=[pl.BlockSpec((MOE_BLK, d // 2), lambda i, be, nu: (i, 0)),
                      pl.BlockSpec(memory_space=pl.ANY),
                      pl.BlockSpec(memory_space=pl.ANY),
                      pl.BlockSpec(memory_space=pl.ANY)],
            out_specs=pl.BlockSpec((MOE_BLK, d // 2), lambda i, be, nu: (i, 0)),
            scratch_shapes=[pltpu.VMEM((2, d, de), F32), pltpu.VMEM((2, d, de), F32),
                            pltpu.VMEM((2, de, d), F32),
                            pltpu.VMEM((d, de), BF16), pltpu.VMEM((d, de), BF16),
                            pltpu.VMEM((de, d), BF16),
                            pltpu.SMEM((1,), I32),
                            pltpu.SemaphoreType.DMA((2,))]),
        out_shape=jax.ShapeDtypeStruct((p, d // 2), I32),
        compiler_params=_cparams(("arbitrary",)),
        name="moe_expert_ffn",
    )(be, nu, xb, wg, wu, wd)


def _combine_kernel(d0_ref, d1_ref, x_ref, info_ref, g2_ref, yb_ref, o_ref, rows_scr, sem):
    i = pl.program_id(0)
    cur = i % 2

    def copy(buf, which, r, src_row, rows=1):
        return pltpu.make_async_copy(yb_ref.at[pl.ds(src_row, rows)],
                                     rows_scr.at[buf, which, pl.ds(r, rows)], sem.at[buf])

    def start_gather(step, buf):
        base = step * TOK_TILE

        def issue(g, c):
            for k in range(SUBLANES):
                r = g * SUBLANES + k
                copy(buf, 0, r, d0_ref[base + r]).start()
                copy(buf, 1, r, d1_ref[base + r]).start()
            return c

        lax.fori_loop(0, TOK_TILE // SUBLANES, issue, 0)

    @pl.when(i == 0)
    def _():
        start_gather(0, 0)

    @pl.when(i + 1 < pl.num_programs(0))
    def _():
        start_gather(i + 1, 1 - cur)

    copy(cur, 0, 0, 0, TOK_TILE).wait()
    copy(cur, 1, 0, 0, TOK_TILE).wait()

    info = info_ref[...]
    half = o_ref.shape[1] // 2
    y0_hi, y0_lo = _unpack_bf16_pairs(rows_scr[cur, 0])
    y1_hi, y1_lo = _unpack_bf16_pairs(rows_scr[cur, 1])
    g2 = g2_ref[0]
    o_ref[:, :half] = x_ref[:, :half] + g2[:, :half] * (info[:, 2:3] * y0_hi + info[:, 3:4] * y1_hi)
    o_ref[:, half:] = x_ref[:, half:] + g2[:, half:] * (info[:, 2:3] * y0_lo + info[:, 3:4] * y1_lo)


def _combine(d0, d1, x1, info, g2, yb, seq):
    n, d = x1.shape
    tpb = seq // TOK_TILE
    return pl.pallas_call(
        _combine_kernel,
        grid_spec=pltpu.PrefetchScalarGridSpec(
            num_scalar_prefetch=2,
            grid=(n // TOK_TILE,),
            in_specs=[pl.BlockSpec((TOK_TILE, d), lambda i, a, b: (i, 0)),
                      pl.BlockSpec((TOK_TILE, LANES), lambda i, a, b: (i, 0)),
                      pl.BlockSpec((1, 1, d), lambda i, a, b: (i // tpb, 0, 0)),
                      pl.BlockSpec(memory_space=pl.ANY)],
            out_specs=pl.BlockSpec((TOK_TILE, d), lambda i, a, b: (i, 0)),
            scratch_shapes=[pltpu.VMEM((2, 2, TOK_TILE, d // 2), I32),
                            pltpu.SemaphoreType.DMA((2,))]),
        out_shape=jax.ShapeDtypeStruct((n, d), F32),
        compiler_params=_cparams(("arbitrary",)),
        name="moe_combine",
    )(d0, d1, x1, info, g2, yb)


def _tri_inclusive_rev(k):
    l = (np.arange(k)[:, None] >= np.arange(k)[None, :]).astype(np.float32)
    half = np.concatenate([l, np.ones((k, k), np.float32)], axis=1)
    return jnp.asarray(np.concatenate([half, half], axis=0), dtype=BF16)


def _tri_strict_lower(k):
    return jnp.asarray((np.arange(k)[None, :] < np.arange(k)[:, None]).astype(np.float32), dtype=BF16)


def kernel(x, c, w_mod, b_mod, ln1_g, w_in, q_norm_g, w_q_up, q_gain, k_gain, rel_bias, gn_a, gn_b,
           w_out, ln2_g, router_g, router_e, w_gate, w_up, w_down):
    batch, seq, d = x.shape
    n = batch * seq
    assert w_mod.shape[0] == 1 and d == D_A + D_B
    assert seq % KB_A == 0 and KB_A % QB_A == 0 and seq % TOK_TILE == 0 and n % 512 == 0
    x2 = x.reshape(n, d)

    c8 = jnp.pad(c, ((0, 8 - batch), (0, 0)))
    mod = _modulation(c8, w_mod.reshape(d, -1), b_mod.reshape(1, -1))[:batch]
    sh1, sc1, g1, sh2, sc2, g2 = [m.reshape(batch, 1, d) for m in jnp.split(mod, 6, axis=-1)]

    wi = w_in.reshape(d, -1).astype(BF16)
    n_a = Q_RANK + 2 * HEAD_DIM + IDX_DIM + IDX_HEADS
    wi = jnp.concatenate([wi[:, n_a:], wi[:, :n_a],
                          jnp.zeros((d, D_IN_PAD - wi.shape[1]), wi.dtype)], axis=1)
    proj = _ln_proj(x2, ln1_g.reshape(1, d), sc1, sh1, wi, seq)

    qa, qidx, ka, va, kidx, widx = _aprep(
        proj, n, q_norm_g.reshape(1, -1), w_q_up.reshape(Q_RANK, -1).astype(BF16),
        q_gain.reshape(1, -1), k_gain.reshape(1, -1))

    bt = _bias_tiles(rel_bias)
    bound = _logit_bound(q_gain.reshape(1, -1), k_gain.reshape(1, -1), rel_bias)[0, :1]
    oa = _attn_a(qidx, widx.T, qa, kidx, ka, va, bt, gn_a.reshape(1, -1), bound, batch, seq)
    ob = _attn_b(proj, _tri_inclusive_rev(QB_B), batch, seq)

    rw = jnp.concatenate([router_e.reshape(d, -1), router_g.reshape(d, -1),
                          jnp.zeros((d, LANES - N_EXPERTS - N_GROUPS), F32)], axis=1)
    x1, h2, lg = _out_proj(x2, oa, ob, gn_b.reshape(1, -1), w_out.reshape(d, d).astype(BF16),
                           g1, ln2_g.reshape(1, d), sc2, sh2, rw, seq)

    info, cnt = _route(lg)
    counts = cnt[0, :N_EXPERTS].astype(I32)
    p_rows = 2 * n + N_EXPERTS * MOE_BLK
    ps, be, nu = _plan(counts, p_rows // MOE_BLK)
    ps_lanes = jnp.pad(ps.astype(F32), (0, LANES - N_EXPERTS)).reshape(1, LANES)
    dinfo = _dest(info, ps_lanes, _tri_strict_lower(512))
    d0 = dinfo[:, 0].astype(I32)
    d1 = dinfo[:, 1].astype(I32)
    xb = _scatter_rows(d0, d1, counts, ps, nu, h2, p_rows)
    yb = _experts(be, nu, xb,
                  w_gate.reshape(N_EXPERTS, d, D_EXPERT),
                  w_up.reshape(N_EXPERTS, d, D_EXPERT),
                  w_down.reshape(N_EXPERTS, D_EXPERT, d))
    out = _combine(d0, d1, x1, info, g2, yb, seq)
    return out.reshape(batch, seq, d)
```

```python
import functools
import math

import numpy as np
import jax
import jax.numpy as jnp
from jax import lax
from jax.experimental import pallas as pl
from jax.experimental.pallas import tpu as pltpu

F32 = jnp.float32
BF16 = jnp.bfloat16
I32 = jnp.int32

HEAD_DIM = 128
N_HEADS_A = 8
N_HEADS_B = 8
D_A = N_HEADS_A * HEAD_DIM
D_B = N_HEADS_B * HEAD_DIM
Q_RANK = 512
IDX_HEADS = 16
IDX_DIM = 64
TOPK_MAX = 256
N_BUCKETS = 32
MAX_DISTANCE = 128
N_GROUPS = 4
EXPERTS_PER_GROUP = 8
N_EXPERTS = N_GROUPS * EXPERTS_PER_GROUP
D_EXPERT = 512
EPS = 1e-6

LANES = 128
SUBLANES = 8
VMEM_LIMIT = 56 * 1024 * 1024
NEG = -1e30
INT_MIN = -(2 ** 31)
EXP2_UNDERFLOW = 150.0
LOG2E = math.log2(math.e)
PLAIN_SOFTMAX_LIMIT = 64.0
GUESS_HALF_WIDTH = 0.1
EXTRACT_BELOW = 3.0

QB_A = 128
KB_A = 256
QB_B = 128
MOE_BLK = 256
TOK_TILE = 512

COL_QB, COL_KB, COL_VB, COL_A = 0, D_B, 2 * D_B, 3 * D_B
A_CQ, A_KA, A_VA, A_KIDX, A_WIDX = 0, 512, 640, 768, 832
A_WIDTH = 1024
D_IN_PAD = COL_A + A_WIDTH


def _cparams(sem=None):
    return pltpu.CompilerParams(dimension_semantics=sem, vmem_limit_bytes=VMEM_LIMIT)


def _rms(x):
    return x * lax.rsqrt(jnp.mean(x * x, axis=-1, keepdims=True) + EPS)


def _pack_bf16_pairs(x):
    c = x.shape[1] // 2
    hi = pltpu.bitcast(x[:, :c].astype(BF16).astype(F32), I32)
    lo = pltpu.bitcast(x[:, c:].astype(BF16).astype(F32), I32)
    return hi | lax.shift_right_logical(lo, 16)


def _unpack_bf16_pairs(u):
    hi = pltpu.bitcast(u & jnp.int32(-65536), F32)
    lo = pltpu.bitcast(lax.shift_left(u, 16), F32)
    return hi, lo


def _mod_kernel(c_ref, w_ref, b_ref, o_ref):
    c = c_ref[...]
    s = c * (1.0 / (1.0 + jnp.exp(-c)))
    o_ref[...] = jnp.dot(s, w_ref[...], preferred_element_type=F32,
                         precision=lax.Precision.HIGHEST) + b_ref[...]


def _modulation(c8, w_mod, b_mod):
    d, n6 = w_mod.shape
    tn = 1024
    return pl.pallas_call(
        _mod_kernel,
        grid=(n6 // tn,),
        in_specs=[pl.BlockSpec((8, d), lambda j: (0, 0)),
                  pl.BlockSpec((d, tn), lambda j: (0, j)),
                  pl.BlockSpec((1, tn), lambda j: (0, j))],
        out_specs=pl.BlockSpec((8, tn), lambda j: (0, j)),
        out_shape=jax.ShapeDtypeStruct((8, n6), F32),
        compiler_params=_cparams(("arbitrary",)),
        name="modulation",
    )(c8, w_mod, b_mod)


def _ln_proj_kernel(x_ref, g_ref, sc_ref, sh_ref, w_ref, o_ref, h_scr, *, q_tiles, chunk):
    i = pl.program_id(0)
    j = pl.program_id(1)

    def normalise_chunk():
        rows = pl.ds(pl.multiple_of(j * chunk, chunk), chunk)
        h = _rms(x_ref[rows, :]) * g_ref[...]
        h = h * (1.0 + sc_ref[0]) + sh_ref[0]
        h_scr[i % 2, rows, :] = h.astype(BF16)

    @pl.when(i == 0)
    def _():
        normalise_chunk()
        o_ref[...] = jnp.zeros(o_ref.shape, o_ref.dtype)

    @pl.when(i > 0)
    def _():
        normalise_chunk()
        col_scale = jnp.where(j < q_tiles, HEAD_DIM ** -0.5 * LOG2E, 1.0)
        acc = jnp.dot(h_scr[(i - 1) % 2], w_ref[...], preferred_element_type=F32)
        o_ref[...] = (acc * col_scale).astype(o_ref.dtype)


def _ln_proj(x2, ln_g, sc, sh, w, seq):
    n, d = x2.shape
    ncol = w.shape[1]
    tm = min(1024, seq)
    tn = 1024
    tpb = seq // tm
    n_i, n_j = n // tm, ncol // tn
    assert COL_QB == 0 and D_B % tn == 0 and tm % (n_j * 16) == 0
    last = n_i - 1
    return pl.pallas_call(
        functools.partial(_ln_proj_kernel, q_tiles=D_B // tn, chunk=tm // n_j),
        grid=(n_i + 1, n_j),
        in_specs=[pl.BlockSpec((tm, d), lambda i, j: (jnp.minimum(i, last), 0)),
                  pl.BlockSpec((1, d), lambda i, j: (0, 0)),
                  pl.BlockSpec((1, 1, d), lambda i, j: (jnp.minimum(i, last) // tpb, 0, 0)),
                  pl.BlockSpec((1, 1, d), lambda i, j: (jnp.minimum(i, last) // tpb, 0, 0)),
                  pl.BlockSpec((d, tn), lambda i, j: (0, j))],
        out_specs=pl.BlockSpec((tm, tn), lambda i, j: (jnp.where(i == 0, n_i, i - 1), j)),
        out_shape=jax.ShapeDtypeStruct((n + tm, ncol), BF16),
        scratch_shapes=[pltpu.VMEM((2, tm, d), BF16)],
        compiler_params=_cparams(("arbitrary", "arbitrary")),
        name="ln_in_proj",
    )(x2, ln_g, sc, sh, w)


def _aprep_kernel(a_ref, qng_ref, wq_ref, qg_ref, kg_ref,
                  qa_ref, qidx_ref, ka_ref, va_ref, kidx_ref, widx_ref):
    cq = a_ref[:, A_CQ:A_CQ + Q_RANK].astype(F32)
    cqn = (_rms(cq) * qng_ref[...]).astype(BF16)
    qup = jnp.dot(cqn, wq_ref[...], preferred_element_type=F32)
    for h in range(N_HEADS_A):
        qh = qup[:, h * HEAD_DIM:(h + 1) * HEAD_DIM]
        qn = _rms(qh) * qg_ref[...] * (HEAD_DIM ** -0.5 * LOG2E)
        qa_ref[:, h * HEAD_DIM:(h + 1) * HEAD_DIM] = qn.astype(BF16)
    for h in range(IDX_HEADS):
        qi = qup[:, D_A + h * IDX_DIM:D_A + (h + 1) * IDX_DIM] * (IDX_DIM ** -0.5)
        qidx_ref[h] = qi.astype(BF16)
    ka = a_ref[:, A_KA:A_KA + HEAD_DIM].astype(F32)
    ka_ref[...] = (_rms(ka) * kg_ref[...]).astype(BF16)
    va_ref[...] = a_ref[:, A_VA:A_VA + HEAD_DIM]
    kidx_ref[...] = a_ref[:, A_KIDX:A_KIDX + IDX_DIM]
    widx_ref[...] = a_ref[:, A_WIDX:A_WIDX + IDX_HEADS].astype(F32) * (IDX_HEADS ** -0.5)


def _aprep(proj, n, q_norm_g, w_q_up, q_gain, k_gain):
    tm = 512
    cblk = COL_A // A_WIDTH
    nup = w_q_up.shape[1]
    return pl.pallas_call(
        _aprep_kernel,
        grid=(n // tm,),
        in_specs=[pl.BlockSpec((tm, A_WIDTH), lambda i: (i, cblk)),
                  pl.BlockSpec((1, Q_RANK), lambda i: (0, 0)),
                  pl.BlockSpec((Q_RANK, nup), lambda i: (0, 0)),
                  pl.BlockSpec((1, HEAD_DIM), lambda i: (0, 0)),
                  pl.BlockSpec((1, HEAD_DIM), lambda i: (0, 0))],
        out_specs=[pl.BlockSpec((tm, D_A), lambda i: (i, 0)),
                   pl.BlockSpec((IDX_HEADS, tm, IDX_DIM), lambda i: (0, i, 0)),
                   pl.BlockSpec((tm, HEAD_DIM), lambda i: (i, 0)),
                   pl.BlockSpec((tm, HEAD_DIM), lambda i: (i, 0)),
                   pl.BlockSpec((tm, IDX_DIM), lambda i: (i, 0)),
                   pl.BlockSpec((tm, IDX_HEADS), lambda i: (i, 0))],
        out_shape=[jax.ShapeDtypeStruct((n, D_A), BF16),
                   jax.ShapeDtypeStruct((IDX_HEADS, n, IDX_DIM), BF16),
                   jax.ShapeDtypeStruct((n, HEAD_DIM), BF16),
                   jax.ShapeDtypeStruct((n, HEAD_DIM), BF16),
                   jax.ShapeDtypeStruct((n, IDX_DIM), BF16),
                   jax.ShapeDtypeStruct((n, IDX_HEADS), F32)],
        compiler_params=_cparams(("parallel",)),
        name="group_a_prep",
    )(proj, q_norm_g, w_q_up, q_gain, k_gain)


def _t5_bucket_starts():
    max_exact = N_BUCKETS // 2
    d = np.arange(0, 4 * MAX_DISTANCE, dtype=np.int64)
    df = np.maximum(d, 1).astype(np.float32)
    large = max_exact + (np.log(df / np.float32(max_exact)) / np.float32(math.log(MAX_DISTANCE / max_exact))
                         * np.float32(N_BUCKETS - max_exact)).astype(np.int32)
    large = np.minimum(large, N_BUCKETS - 1)
    bucket = np.where(d < max_exact, d, large)
    assert np.all(np.diff(bucket) >= 0) and bucket[-1] == N_BUCKETS - 1
    return [int(np.argmax(bucket >= b)) for b in range(N_BUCKETS)]


_BUCKET_START = _t5_bucket_starts()
N_BIAS_TILES = 2 * KB_A // LANES


def _bias_kernel(rb_ref, o_ref):
    di = pl.program_id(0)
    h = pl.program_id(1)
    i = lax.broadcasted_iota(I32, (QB_A, KB_A), 0)
    j = lax.broadcasted_iota(I32, (QB_A, KB_A), 1)
    d = di * LANES + i - j
    val = jnp.full((QB_A, KB_A), rb_ref[0, h], F32)
    for b in range(1, N_BUCKETS):
        val = jnp.where(d >= _BUCKET_START[b], rb_ref[b, h], val)
    o_ref[0, 0] = (val - rb_ref[N_BUCKETS - 1, h]) * LOG2E


def _bias_tiles(rel_bias):
    return pl.pallas_call(
        _bias_kernel,
        grid=(N_BIAS_TILES, N_HEADS_A),
        in_specs=[pl.BlockSpec(memory_space=pltpu.SMEM)],
        out_specs=pl.BlockSpec((1, 1, QB_A, KB_A), lambda a, h: (a, h, 0, 0)),
        out_shape=jax.ShapeDtypeStruct((N_BIAS_TILES, N_HEADS_A, QB_A, KB_A), F32),
        compiler_params=_cparams(("arbitrary", "arbitrary")),
        name="t5_bias_tiles",
    )(rel_bias)


def _bound_kernel(qg_ref, kg_ref, rb_ref, o_ref):
    qmax = jnp.max(jnp.abs(qg_ref[...]), axis=1, keepdims=True)
    kmax = jnp.max(jnp.abs(kg_ref[...]), axis=1, keepdims=True)
    rb = rb_ref[...]
    shifted = jnp.abs(rb - rb[N_BUCKETS - 1:N_BUCKETS, :])
    bmax = jnp.max(jnp.max(shifted, axis=1, keepdims=True), axis=0, keepdims=True)
    bound = qmax * kmax * (math.sqrt(HEAD_DIM) * 1.02) + bmax
    o_ref[...] = jnp.broadcast_to(bound, o_ref.shape)


def _logit_bound(q_gain, k_gain, rel_bias):
    return pl.pallas_call(
        _bound_kernel,
        out_shape=jax.ShapeDtypeStruct((1, LANES), F32),
        name="dsa_logit_bound",
    )(q_gain, k_gain, rel_bias)


def _attn_a_kernel(qidx_ref, w_ref, qa_ref, kidx_ref, ka_ref, va_ref, bt_ref, gn_ref, bound_ref, o_ref,
                   keys_scr, keyt_scr, m_scr, acc_scr, s_scr, mb_scr, *, topk):
    qb = pl.program_id(1)
    t0 = qb * QB_A
    kbl = (t0 + QB_A - 1) // KB_A
    row = t0 + lax.broadcasted_iota(I32, (QB_A, KB_A), 0)
    col0 = lax.broadcasted_iota(I32, (QB_A, KB_A), 1)
    nt = (((1,), (1,)), ((), ()))

    reps = KB_A // LANES

    def sort_key(v):
        bits = pltpu.bitcast(v, I32)
        return jnp.where(bits < 0, bits ^ jnp.int32(0x7FFFFFFF), bits)

    def key_value(k):
        return pltpu.bitcast(jnp.where(k < 0, k ^ jnp.int32(0x7FFFFFFF), k), F32)

    key_pos = lax.broadcasted_iota(I32, (KB_A, QB_A), 0)
    qry_pos = t0 + lax.broadcasted_iota(I32, (KB_A, QB_A), 1)

    def score_tile(kb, carry, diagonal):
        smin, smax, s1, s2 = carry
        kt = kidx_ref[pl.ds(pl.multiple_of(kb * KB_A, KB_A), KB_A), :]
        score = jnp.zeros((KB_A, QB_A), F32)
        for hp in range(IDX_HEADS // 2):
            q2 = qidx_ref[2 * hp:2 * hp + 2].reshape(2 * QB_A, IDX_DIM)
            sc = lax.dot_general(kt, q2, nt, preferred_element_type=F32)
            score = score + w_ref[2 * hp:2 * hp + 1, :] * jnp.maximum(sc[:, :QB_A], 0.0)
            score = score + w_ref[2 * hp + 1:2 * hp + 2, :] * jnp.maximum(sc[:, QB_A:], 0.0)
        key_t = sort_key(score)
        low, high, live = score, score, score
        if diagonal:
            causal = (kb * KB_A + key_pos) <= qry_pos
            key_t = jnp.where(causal, key_t, jnp.int32(INT_MIN))
            low = jnp.where(causal, score, jnp.inf)
            high = jnp.where(causal, score, -jnp.inf)
            live = jnp.where(causal, score, 0.0)
        keyt_scr[kb] = key_t
        keys_scr[kb] = key_t.T
        smin = jnp.minimum(smin, jnp.min(low, axis=0, keepdims=True))
        smax = jnp.maximum(smax, jnp.max(high, axis=0, keepdims=True))
        s1 = s1 + jnp.sum(live, axis=0, keepdims=True)
        s2 = s2 + jnp.sum(live * live, axis=0, keepdims=True)
        return smin, smax, s1, s2

    def score_group(first, count, carry):
        for u in range(count):
            carry = score_tile(first + u, carry, False)
        return carry

    zero_row = jnp.zeros((1, QB_A), F32)
    stats = lax.fori_loop(0, kbl // 4, lambda i, c: score_group(4 * i, 4, c),
                          (jnp.full((1, QB_A), jnp.inf, F32), jnp.full((1, QB_A), -jnp.inf, F32),
                           zero_row, zero_row))
    stats = lax.cond(kbl % 4 >= 2, lambda c: score_group((kbl // 4) * 4, 2, c), lambda c: c, stats)
    stats = lax.cond(kbl % 2 == 1, lambda c: score_tile(kbl - 1, c, False), lambda c: c, stats)
    smin, smax, s1, s2 = score_tile(kbl, stats, True)

    qry1 = t0 + lax.broadcasted_iota(I32, (1, QB_A), 1)
    kf = float(topk)
    acc_rows = 32
    n_causal = (qry1 + 1).astype(F32)

    mean = s1 / n_causal
    dev = jnp.sqrt(jnp.maximum(s2 / n_causal - mean * mean, 0.0))
    frac_top = jnp.minimum(kf / n_causal, 1.0)
    tail = jnp.minimum(frac_top, 1.0 - frac_top)
    tq = jnp.sqrt(-2.0 * jnp.log(jnp.maximum(tail, 1e-6)))
    zq = tq - ((0.010328 * tq + 0.802853) * tq + 2.515517) / (((0.001308 * tq + 0.189269) * tq + 1.432788) * tq + 1.0)
    zq = jnp.where(frac_top > 0.5, -zq, zq)

    def for_each_key_tile(fold, init):
        n_all = kbl + 1
        acc = lax.fori_loop(0, n_all // 2,
                            lambda i, a: fold(fold(a, keyt_scr[2 * i]), keyt_scr[2 * i + 1]), init)
        return lax.cond(n_all % 2 == 1, lambda a: fold(a, keyt_scr[n_all - 1]), lambda a: a, acc)

    def search_pass(p, state):
        lo, hi, clo, chi, open_q = state
        lo_v = key_value(lo)
        hi_v = key_value(hi)
        gap = clo - chi
        frac = (clo - (kf - 0.5)) / gap
        frac = jnp.where(p % 2 == 1, 0.7 * frac + 0.15, frac)
        frac = jnp.where(gap > 16.0, frac, 0.5)
        value = lo_v + (hi_v - lo_v) * frac
        value = jnp.where(p == 0, mean + (zq - GUESS_HALF_WIDTH) * dev, value)
        value = jnp.where(p == 1, mean + (zq + GUESS_HALF_WIDTH) * dev, value)
        cand = sort_key(value)
        cand = jnp.where(p % 8 == 7, lo + lax.shift_right_logical(hi - lo, 1), cand)
        cand = jnp.minimum(jnp.maximum(cand, lo + 1), hi - 1)

        def count(cnt, keys):
            ge = jnp.where(keys >= cand, 1.0, 0.0)
            return cnt + jnp.sum(ge.reshape(KB_A // acc_rows, acc_rows, QB_A), axis=0)

        tot = jnp.sum(for_each_key_tile(count, jnp.zeros((acc_rows, QB_A), F32)), axis=0, keepdims=True)
        ge = tot >= kf
        lo = jnp.where(ge, cand, lo)
        clo = jnp.where(ge, tot, clo)
        hi = jnp.where(ge, hi, cand)
        chi = jnp.where(ge, chi, tot)
        width = hi - lo
        settled = jnp.where(clo == kf, 1.0, jnp.where(width == 1, 1.0, 0.0))
        open_q = jnp.where(settled > 0.5, 0.0, open_q)
        return lo, hi, clo, chi, open_q

    def extract_pass(state):
        lo, hi, clo, chi, open_q = state

        def top_below(best, keys):
            below = jnp.where(keys < hi, keys, jnp.int32(INT_MIN))
            return jnp.maximum(best, jnp.max(below.reshape(KB_A // acc_rows, acc_rows, QB_A), axis=0))

        best = for_each_key_tile(top_below, jnp.full((acc_rows, QB_A), INT_MIN, I32))
        best = jnp.max(best, axis=0, keepdims=True)
        is_open = open_q > 0.5
        last = jnp.logical_and(is_open, kf - chi <= 1.0)
        more = jnp.logical_and(is_open, kf - chi > 1.0)
        lo = jnp.where(last, best, lo)
        clo = jnp.where(last, chi + 1.0, clo)
        hi = jnp.where(more, best, hi)
        chi = jnp.where(more, chi + 1.0, chi)
        open_q = jnp.where(last, 0.0, open_q)
        return lo, hi, clo, chi, open_q

    def missing(state):
        return jnp.max(jnp.where(state[4] > 0.5, kf - state[3], 0.0))

    def search_step(carry):
        p, state, lacking = carry

        def searching(st):
            st = search_pass(p + 1, search_pass(p, st))
            return lax.cond(p == 0, lambda s: search_pass(3, search_pass(2, s)), lambda s: s, st)

        state = lax.cond(lacking <= EXTRACT_BELOW, extract_pass, searching, state)
        return jnp.where(p == 0, 4, p + 2), state, missing(state)

    open0 = jnp.where(qry1 >= topk, 1.0, 0.0)
    lo0 = sort_key(smin)
    hi0 = sort_key(smax) + 1
    open0 = jnp.where(hi0 - lo0 == 1, 0.0, open0)
    state0 = (lo0, hi0, n_causal, zero_row, open0)
    _, (lo, _, _, _, _), _ = lax.while_loop(
        lambda c: c[2] > 0.0, search_step, (jnp.int32(0), state0, missing(state0)))
    thr = jnp.where(qry1 >= topk, lo, jnp.int32(INT_MIN))
    thr = jnp.broadcast_to(thr, (QB_A, QB_A)).T
    thrb = jnp.tile(thr, (1, reps))

    m_scr[...] = jnp.full(m_scr.shape, NEG, F32)
    acc_scr[...] = jnp.zeros(acc_scr.shape, F32)

    plain = bound_ref[0] <= PLAIN_SOFTMAX_LIMIT

    def attend(kb, near, online):
        mb = jnp.where(keys_scr[kb] >= thrb, 0.0, NEG)
        if near:
            mb = jnp.where((kb * KB_A + col0) <= row, mb, NEG)
            di = (t0 - kb * KB_A) // LANES
        mb_scr[...] = mb
        start = pl.multiple_of(kb * KB_A, KB_A)
        kt = ka_ref[pl.ds(start, KB_A), :]
        vt = jnp.concatenate([va_ref[pl.ds(start, KB_A), :], jnp.ones((KB_A, LANES), BF16)], axis=1)
        for h in range(N_HEADS_A):
            q = qa_ref[:, h * HEAD_DIM:(h + 1) * HEAD_DIM]
            s_scr[h] = lax.dot_general(q, kt, nt, preferred_element_type=F32)
        for h in range(N_HEADS_A):
            s = s_scr[h] + mb_scr[...]
            if near:
                s = s + bt_ref[di, h]
            if online:
                m_prev = m_scr[h]
                m_new = jnp.maximum(m_prev, jnp.max(s, axis=1, keepdims=True))
                alpha = jnp.exp2(m_prev - m_new)
                p = jnp.exp2(s - jnp.tile(m_new, (1, reps)))
                acc_scr[h] = (jnp.tile(alpha, (1, 2)) * acc_scr[h]
                              + jnp.dot(p.astype(BF16), vt, preferred_element_type=F32))
                m_scr[h] = m_new
            else:
                acc_scr[h] += jnp.dot(jnp.exp2(s).astype(BF16), vt, preferred_element_type=F32)

    def attend_all(online):
        prev_near = jnp.logical_and(kbl >= 1, t0 - kbl * KB_A < _BUCKET_START[-1] - 1)
        n_far = jnp.where(prev_near, kbl - 1, kbl)
        group = 2 if online else 4

        def far_group(i, carry):
            for u in range(group):
                attend(group * i + u, False, online)
            return carry

        lax.fori_loop(0, n_far // group, far_group, 0)

        if group == 4:
            @pl.when(n_far % 4 >= 2)
            def _():
                attend((n_far // 4) * 4, False, online)
                attend((n_far // 4) * 4 + 1, False, online)

        @pl.when(n_far % 2 == 1)
        def _():
            attend(n_far - 1, False, online)

        @pl.when(prev_near)
        def _():
            attend(kbl - 1, True, online)
            attend(kbl, True, online)

        @pl.when(jnp.logical_not(prev_near))
        def _():
            attend(kbl, True, online)

    @pl.when(plain)
    def _():
        attend_all(False)

    @pl.when(jnp.logical_not(plain))
    def _():
        attend_all(True)

    ssq = jnp.zeros((QB_A, LANES), F32)
    for h in range(N_HEADS_A):
        oh = acc_scr[h, :, :HEAD_DIM] / acc_scr[h, :, HEAD_DIM:]
        acc_scr[h, :, :HEAD_DIM] = oh
        ssq = ssq + jnp.sum(oh * oh, axis=1, keepdims=True)
    inv = lax.rsqrt(ssq * (1.0 / D_A) + EPS)
    for h in range(N_HEADS_A):
        sl = slice(h * HEAD_DIM, (h + 1) * HEAD_DIM)
        o_ref[:, sl] = (acc_scr[h, :, :HEAD_DIM] * inv * gn_ref[:, sl]).astype(o_ref.dtype)


def _attn_a(qidx, widx, qa, kidx, ka, va, bt, gn_a, bound, batch, seq):
    n = qa.shape[0]
    nq = seq // QB_A
    nkt = seq // KB_A
    topk = min(TOPK_MAX, seq // 4)
    return pl.pallas_call(
        functools.partial(_attn_a_kernel, topk=topk),
        grid=(batch, nq),
        in_specs=[pl.BlockSpec((IDX_HEADS, QB_A, IDX_DIM), lambda b, q: (0, b * nq + q, 0)),
                  pl.BlockSpec((IDX_HEADS, QB_A), lambda b, q: (0, b * nq + q)),
                  pl.BlockSpec((QB_A, D_A), lambda b, q: (b * nq + q, 0)),
                  pl.BlockSpec((seq, IDX_DIM), lambda b, q: (b, 0)),
                  pl.BlockSpec((seq, HEAD_DIM), lambda b, q: (b, 0)),
                  pl.BlockSpec((seq, HEAD_DIM), lambda b, q: (b, 0)),
                  pl.BlockSpec(bt.shape, lambda b, q: (0, 0, 0, 0)),
                  pl.BlockSpec((1, D_A), lambda b, q: (0, 0)),
                  pl.BlockSpec(memory_space=pltpu.SMEM)],
        out_specs=pl.BlockSpec((QB_A, D_A), lambda b, q: (b * nq + q, 0)),
        out_shape=jax.ShapeDtypeStruct((n, D_A), BF16),
        scratch_shapes=[pltpu.VMEM((nkt, QB_A, KB_A), I32),
                        pltpu.VMEM((nkt, KB_A, QB_A), I32),
                        pltpu.VMEM((N_HEADS_A, QB_A, LANES), F32),
                        pltpu.VMEM((N_HEADS_A, QB_A, 2 * HEAD_DIM), F32),
                        pltpu.VMEM((N_HEADS_A, QB_A, KB_A), F32),
                        pltpu.VMEM((QB_A, KB_A), F32)],
        compiler_params=_cparams(("parallel", "arbitrary")),
        name="dsa_attention",
    )(qidx, widx, qa, kidx, ka, va, bt, gn_a, bound)


def _attn_b_kernel(q_ref, k_ref, v_ref, tri_ref, o_ref, rest_scr, z_scr, cs_scr):
    qb = pl.program_id(1)
    row = lax.broadcasted_iota(I32, (QB_B, QB_B), 0)
    col = lax.broadcasted_iota(I32, (QB_B, QB_B), 1)
    strict = col < row
    nt = (((1,), (1,)), ((), ()))
    def step(kb, diag):
        start = pl.multiple_of(kb * QB_B, QB_B)
        heads = [slice(h * HEAD_DIM, (h + 1) * HEAD_DIM) for h in range(N_HEADS_B)]
        for h, sl in enumerate(heads):
            kt = k_ref[pl.ds(start, QB_B), sl]
            z_scr[h] = lax.dot_general(q_ref[:, sl], kt, nt, preferred_element_type=F32)
        for h, sl in enumerate(heads):
            z = z_scr[h]
            sp = jnp.maximum(z, 0.0) + jnp.log(1.0 + jnp.exp2(-jnp.abs(z))) * LOG2E
            if diag:
                sp = jnp.where(strict, sp, 0.0)
            hi = sp.astype(BF16)
            lo = (sp - hi.astype(F32)).astype(BF16)
            cs_scr[h] = jnp.dot(jnp.concatenate([hi, lo], axis=1), tri_ref[...],
                                preferred_element_type=F32)
        worst = None
        for h, sl in enumerate(heads):
            vt = v_ref[pl.ds(start, QB_B), sl]
            z = z_scr[h]
            cs = cs_scr[h, :, :QB_B]
            tot = cs_scr[h, :, QB_B:]
            if diag:
                a = jnp.where(strict, jnp.exp2(z - cs), 0.0)
                o_ref[:, sl] = jnp.dot(a.astype(BF16), vt, preferred_element_type=F32)
                rest = tot
            else:
                rest = rest_scr[h]
                a = jnp.exp2(z - cs - rest)
                o_ref[:, sl] += jnp.dot(a.astype(BF16), vt, preferred_element_type=F32)
                rest = rest + tot
            rest_scr[h] = rest
            worst = rest if worst is None else jnp.minimum(worst, rest)
        return jnp.min(worst)

    def more(kb, smallest):
        return jnp.logical_and(kb >= 0, smallest < EXP2_UNDERFLOW)

    def body(carry):
        kb, _ = carry
        return kb - 1, more(kb - 1, step(kb, False))

    def first_tiles(have_previous):
        smallest = step(qb, True)
        if have_previous:
            smallest = step(qb - 1, False)
        return smallest

    smallest = lax.cond(qb >= 1, lambda: first_tiles(True), lambda: first_tiles(False))
    lax.while_loop(lambda c: c[1], body, (qb - 2, more(qb - 2, smallest)))


def _attn_b(proj, tri, batch, seq):
    n = batch * seq
    nq = seq // QB_B
    return pl.pallas_call(
        _attn_b_kernel,
        grid=(batch, nq),
        in_specs=[pl.BlockSpec((QB_B, D_B), lambda b, q: (b * nq + q, COL_QB // D_B)),
                  pl.BlockSpec((seq, D_B), lambda b, q: (b, COL_KB // D_B)),
                  pl.BlockSpec((seq, D_B), lambda b, q: (b, COL_VB // D_B)),
                  pl.BlockSpec(tri.shape, lambda b, q: (0, 0))],
        out_specs=pl.BlockSpec((QB_B, D_B), lambda b, q: (b * nq + q, 0)),
        out_shape=jax.ShapeDtypeStruct((n, D_B), F32),
        scratch_shapes=[pltpu.VMEM((N_HEADS_B, QB_B, QB_B), F32),
                        pltpu.VMEM((N_HEADS_B, QB_B, QB_B), F32),
                        pltpu.VMEM((N_HEADS_B, QB_B, 2 * QB_B), F32)],
        compiler_params=_cparams(("parallel", "arbitrary")),
        name="stick_breaking_attention",
    )(proj, proj, proj, tri)


def _out_proj_kernel(x_ref, oa_ref, ob_ref, gnb_ref, w_ref, g1_ref, ln_ref, sc_ref, sh_ref, rw_ref,
                     x1_ref, h2_ref, lg_ref):
    obn = (_rms(ob_ref[...]) * gnb_ref[...]).astype(BF16)
    y = jnp.dot(oa_ref[...], w_ref[0:D_A, :], preferred_element_type=F32)
    y = y + jnp.dot(obn, w_ref[D_A:D_A + D_B, :], preferred_element_type=F32)
    x1 = x_ref[...] + g1_ref[0] * y
    x1_ref[...] = x1
    h2 = _rms(x1) * ln_ref[...]
    h2 = h2 * (1.0 + sc_ref[0]) + sh_ref[0]
    h2_ref[...] = _pack_bf16_pairs(h2)
    hh = h2.astype(BF16)
    hl = (h2 - hh.astype(F32)).astype(BF16)
    rw = rw_ref[...]
    rh = rw.astype(BF16)
    rl = (rw - rh.astype(F32)).astype(BF16)
    both = jnp.dot(hh, jnp.concatenate([rh, rl], axis=1), preferred_element_type=F32)
    lg_ref[...] = both[:, :LANES] + both[:, LANES:] + jnp.dot(hl, rh, preferred_element_type=F32)


def _out_proj(x2, oa, ob, gn_b, w_out, g1, ln_g, sc, sh, rw, seq):
    n, d = x2.shape
    tm = 512
    tpb = seq // tm
    row = lambda i: (i, 0)
    fixed = lambda i: (0, 0)
    perb = lambda i: (i // tpb, 0, 0)
    return pl.pallas_call(
        _out_proj_kernel,
        grid=(n // tm,),
        in_specs=[pl.BlockSpec((tm, d), row),
                  pl.BlockSpec((tm, D_A), row),
                  pl.BlockSpec((tm, D_B), row),
                  pl.BlockSpec((1, D_B), fixed),
                  pl.BlockSpec(w_out.shape, fixed),
                  pl.BlockSpec((1, 1, d), perb),
                  pl.BlockSpec((1, d), fixed),
                  pl.BlockSpec((1, 1, d), perb),
                  pl.BlockSpec((1, 1, d), perb),
                  pl.BlockSpec(rw.shape, fixed)],
        out_specs=[pl.BlockSpec((tm, d), row),
                   pl.BlockSpec((tm, d // 2), row),
                   pl.BlockSpec((tm, LANES), row)],
        out_shape=[jax.ShapeDtypeStruct((n, d), F32),
                   jax.ShapeDtypeStruct((n, d // 2), I32),
                   jax.ShapeDtypeStruct((n, LANES), F32)],
        compiler_params=_cparams(("parallel",)),
        name="out_proj_ln2_router",
    )(x2, oa, ob, gn_b, w_out, g1, ln_g, sc, sh, rw)


def _route_kernel(lg_ref, info_ref, cnt_ref):
    @pl.when(pl.program_id(0) == 0)
    def _():
        cnt_ref[...] = jnp.zeros(cnt_ref.shape, F32)

    lg = lg_ref[...]
    lane = lax.broadcasted_iota(I32, lg.shape, 1)
    lanef = lane.astype(F32)
    big = float(4 * LANES)
    gm = jnp.where(lane >= N_EXPERTS, jnp.where(lane < N_EXPERTS + N_GROUPS, 1.0, 0.0), 0.0) > 0.5
    lgm = jnp.where(gm, lg, NEG)
    mg = jnp.max(lgm, axis=1, keepdims=True)
    eg = jnp.where(gm, jnp.exp(lgm - mg), 0.0)
    pg = eg / jnp.sum(eg, axis=1, keepdims=True)
    gw = jnp.max(pg, axis=1, keepdims=True)
    gidx = jnp.min(jnp.where(gm, jnp.where(pg == gw, lanef - N_EXPERTS, big), big), axis=1, keepdims=True)
    lane_group = (lane // EXPERTS_PER_GROUP).astype(F32)
    em = jnp.where(lane < N_EXPERTS, jnp.where(lane_group == gidx, 1.0, 0.0), 0.0) > 0.5
    lem = jnp.where(em, lg, NEG)
    me = jnp.max(lem, axis=1, keepdims=True)
    ee = jnp.where(em, jnp.exp(lem - me), 0.0)
    pe = jnp.where(em, ee / jnp.sum(ee, axis=1, keepdims=True), -1.0)
    p1 = jnp.max(pe, axis=1, keepdims=True)
    i1 = jnp.min(jnp.where(pe == p1, lanef, big), axis=1, keepdims=True)
    pe2 = jnp.where(lanef == i1, -1.0, pe)
    p2 = jnp.max(pe2, axis=1, keepdims=True)
    i2 = jnp.min(jnp.where(pe2 == p2, lanef, big), axis=1, keepdims=True)
    den = p1 + p2
    g0 = gw * p1 / den
    g1 = gw * p2 / den
    info = jnp.where(lane == 0, i1, jnp.where(lane == 1, i2,
                     jnp.where(lane == 2, g0, jnp.where(lane == 3, g1, 0.0))))
    info_ref[...] = info
    oh = jnp.where(lanef == i1, 1.0, 0.0) + jnp.where(lanef == i2, 1.0, 0.0)
    cnt_ref[...] += jnp.sum(oh, axis=0, keepdims=True)


def _route(lg):
    n = lg.shape[0]
    tm = min(1024, n)
    return pl.pallas_call(
        _route_kernel,
        grid=(n // tm,),
        in_specs=[pl.BlockSpec((tm, LANES), lambda i: (i, 0))],
        out_specs=[pl.BlockSpec((tm, LANES), lambda i: (i, 0)),
                   pl.BlockSpec((1, LANES), lambda i: (0, 0))],
        out_shape=[jax.ShapeDtypeStruct((n, LANES), F32),
                   jax.ShapeDtypeStruct((1, LANES), F32)],
        compiler_params=_cparams(("arbitrary",)),
        name="moe_route",
    )(lg)


def _plan_kernel(cnt_ref, ps_ref, be_ref, nu_ref, *, nblk):
    def fill(i, c):
        be_ref[i] = N_EXPERTS - 1
        return c

    lax.fori_loop(0, nblk, fill, 0)

    def per_expert(e, pos):
        ps_ref[e] = pos * MOE_BLK
        nb = (cnt_ref[e] + MOE_BLK - 1) // MOE_BLK

        def mark(k, c):
            be_ref[pos + k] = e
            return c

        lax.fori_loop(0, nb, mark, 0)
        return pos + nb

    nu_ref[0] = lax.fori_loop(0, N_EXPERTS, per_expert, jnp.int32(0))


def _plan(counts, nblk):
    smem = pl.BlockSpec(memory_space=pltpu.SMEM)
    return pl.pallas_call(
        functools.partial(_plan_kernel, nblk=nblk),
        in_specs=[smem],
        out_specs=[smem, smem, smem],
        out_shape=[jax.ShapeDtypeStruct((N_EXPERTS,), I32),
                   jax.ShapeDtypeStruct((nblk,), I32),
                   jax.ShapeDtypeStruct((1,), I32)],
        name="moe_block_plan",
    )(counts)


def _dest_kernel(info_ref, ps_ref, tri_ref, o_ref, carry_scr):
    @pl.when(pl.program_id(0) == 0)
    def _():
        carry_scr[...] = jnp.zeros(carry_scr.shape, F32)

    info = info_ref[...]
    lane = lax.broadcasted_iota(I32, info.shape, 1)
    lanef = lane.astype(F32)
    o1 = jnp.where(lanef == info[:, 0:1], 1.0, 0.0)
    o2 = jnp.where(lanef == info[:, 1:2], 1.0, 0.0)
    oh = o1 + o2
    before = jnp.dot(tri_ref[...], oh.astype(BF16), preferred_element_type=F32)
    base = before + carry_scr[...] + ps_ref[...]
    d1 = jnp.sum(o1 * base, axis=1, keepdims=True)
    d2 = jnp.sum(o2 * base, axis=1, keepdims=True)
    o_ref[...] = jnp.where(lane == 0, d1, jnp.where(lane == 1, d2, 0.0))
    carry_scr[...] += jnp.sum(oh, axis=0, keepdims=True)


def _dest(info, ps_lanes, tri):
    n = info.shape[0]
    tm = tri.shape[0]
    return pl.pallas_call(
        _dest_kernel,
        grid=(n // tm,),
        in_specs=[pl.BlockSpec((tm, LANES), lambda i: (i, 0)),
                  pl.BlockSpec((1, LANES), lambda i: (0, 0)),
                  pl.BlockSpec((tm, tm), lambda i: (0, 0))],
        out_specs=pl.BlockSpec((tm, LANES), lambda i: (i, 0)),
        out_shape=jax.ShapeDtypeStruct((n, LANES), F32),
        scratch_shapes=[pltpu.VMEM((1, LANES), F32)],
        compiler_params=_cparams(("arbitrary",)),
        name="moe_dest_rows",
    )(info, ps_lanes, tri)


SCATTER_SLOTS = 3


def _scatter_kernel(d0_ref, d1_ref, cnt_ref, ps_ref, nu_ref, h_ref, xb_ref,
                    zero_scr, stage_scr, load_sem, out_sem, zero_sem, *, nblk):
    n_tiles = h_ref.shape[0] // TOK_TILE
    zero_scr[...] = jnp.zeros(zero_scr.shape, zero_scr.dtype)
    pad_sizes = [s for s in (1 << k for k in range(MOE_BLK.bit_length() - 2, -1, -1))
                 if s >= SUBLANES]

    def zero_copy(rows, dst_row):
        return pltpu.make_async_copy(zero_scr.at[pl.ds(0, rows)], xb_ref.at[pl.ds(dst_row, rows)],
                                     zero_sem)

    def load(j, slot):
        return pltpu.make_async_copy(h_ref.at[pl.ds(j * TOK_TILE, TOK_TILE)], stage_scr.at[slot],
                                     load_sem.at[slot])

    def row_copy(slot, r, dst_row, parity, rows=1):
        return pltpu.make_async_copy(stage_scr.at[slot, pl.ds(r, rows)],
                                     xb_ref.at[pl.ds(dst_row, rows)], out_sem.at[parity])

    def for_each_zero_copy(act):
        def per_expert(e, c):
            pos = ps_ref[e] + cnt_ref[e]
            pad = (MOE_BLK - (cnt_ref[e] & (MOE_BLK - 1))) & (MOE_BLK - 1)
            head = pad & (SUBLANES - 1)
            for k in range(SUBLANES - 1):
                @pl.when(k < head)
                def _():
                    act(zero_copy(1, pos + k))
            pos = pl.multiple_of(pos + head, SUBLANES)
            for size in pad_sizes:
                @pl.when((pad & size) != 0)
                def _():
                    act(zero_copy(size, pos))
                pos = pl.multiple_of(pos + (pad & size), SUBLANES)
            return c

        def per_tail_block(i, c):
            act(zero_copy(MOE_BLK, i * MOE_BLK))
            return c

        lax.fori_loop(0, N_EXPERTS, per_expert, 0)
        lax.fori_loop(nu_ref[0], nblk, per_tail_block, 0)

    for_each_zero_copy(lambda cp: cp.start())

    def retire(parity):
        row_copy(0, 0, 0, parity, TOK_TILE).wait()
        row_copy(0, 0, 0, parity, TOK_TILE).wait()

    def tile(j, c):
        slot = j % SCATTER_SLOTS
        parity = j % 2
        load(j, slot).wait()

        @pl.when(j + 1 < n_tiles)
        def _():
            load(j + 1, (j + 1) % SCATTER_SLOTS).start()

        def issue(g, cc):
            for k in range(SUBLANES):
                r = g * SUBLANES + k
                t = j * TOK_TILE + r
                row_copy(slot, r, d0_ref[t], parity).start()
                row_copy(slot, r, d1_ref[t], parity).start()
            return cc

        lax.fori_loop(0, TOK_TILE // SUBLANES, issue, 0)

        @pl.when(j >= 1)
        def _():
            retire(1 - parity)
        return c

    load(0, 0).start()
    lax.fori_loop(0, n_tiles, tile, 0)
    retire((n_tiles - 1) % 2)
    for_each_zero_copy(lambda cp: cp.wait())


def _scatter_rows(d0, d1, counts, ps, nu, h2, p_rows):
    n, d = h2.shape
    smem = pl.BlockSpec(memory_space=pltpu.SMEM)
    return pl.pallas_call(
        functools.partial(_scatter_kernel, nblk=p_rows // MOE_BLK),
        in_specs=[smem, smem, smem, smem, smem, pl.BlockSpec(memory_space=pl.ANY)],
        out_specs=pl.BlockSpec(memory_space=pl.ANY),
        out_shape=jax.ShapeDtypeStruct((p_rows, d), h2.dtype),
        scratch_shapes=[pltpu.VMEM((MOE_BLK, d), h2.dtype),
                        pltpu.VMEM((SCATTER_SLOTS, TOK_TILE, d), h2.dtype),
                        pltpu.SemaphoreType.DMA((SCATTER_SLOTS,)),
                        pltpu.SemaphoreType.DMA((2,)),
                        pltpu.SemaphoreType.DMA(())],
        compiler_params=_cparams(),
        name="moe_scatter_rows",
    )(d0, d1, counts, ps, nu, h2)


def _expert_kernel(be_ref, nu_ref, x_ref, wg_hbm, wu_hbm, wd_hbm, o_ref,
                   wg_f32, wu_f32, wd_f32, wg_scr, wu_scr, wd_scr, slot_ref, sem):
    i = pl.program_id(0)
    n_used = nu_ref[0]
    expert = be_ref[i]
    new_expert = jnp.logical_or(i == 0, expert != be_ref[jnp.maximum(i - 1, 0)])

    def weight_copies(e, slot):
        return [pltpu.make_async_copy(src.at[e], dst.at[slot], sem.at[slot])
                for src, dst in ((wg_hbm, wg_f32), (wu_hbm, wu_f32), (wd_hbm, wd_f32))]

    @pl.when(i == 0)
    def _():
        slot_ref[0] = 0
        for cp in weight_copies(expert, 0):
            cp.start()

    @pl.when(jnp.logical_and(new_expert, i < n_used))
    def _():
        slot = slot_ref[0]
        for cp in weight_copies(expert, slot):
            cp.wait()
        nxt = lax.while_loop(lambda j: jnp.logical_and(j < n_used, be_ref[jnp.minimum(j, n_used - 1)] == expert),
                             lambda j: j + 1, i + 1)

        @pl.when(nxt < n_used)
        def _():
            for cp in weight_copies(be_ref[nxt], 1 - slot):
                cp.start()

        wg_scr[...] = wg_f32[slot].astype(BF16)
        wu_scr[...] = wu_f32[slot].astype(BF16)
        wd_scr[...] = wd_f32[slot].astype(BF16)
        slot_ref[0] = 1 - slot

    @pl.when(i < n_used)
    def _():
        x_hi, x_lo = _unpack_bf16_pairs(x_ref[...])
        x = jnp.concatenate([x_hi.astype(BF16), x_lo.astype(BF16)], axis=1)
        g = jnp.dot(x, wg_scr[...], preferred_element_type=F32)
        u = jnp.dot(x, wu_scr[...], preferred_element_type=F32)
        act = (g * (1.0 / (1.0 + jnp.exp(-g))) * u).astype(BF16)
        o_ref[...] = _pack_bf16_pairs(jnp.dot(act, wd_scr[...], preferred_element_type=F32))

    @pl.when(i >= nu_ref[0])
    def _():
        o_ref[...] = jnp.zeros(o_ref.shape, o_ref.dtype)


def _experts(be, nu, xb, wg, wu, wd):
    p = xb.shape[0]
    _, d, de = wg.shape
    return pl.pallas_call(
        _expert_kernel,
        grid_spec=pltpu.PrefetchScalarGridSpec(
            num_scalar_prefetch=2,
            grid=(p // MOE_BLK,),
            in_specs=[pl.BlockSpec((MOE_BLK, d // 2), lambda i, be, nu: (i, 0)),
                      pl.BlockSpec(memory_space=pl.ANY),
                      pl.BlockSpec(memory_space=pl.ANY),
                      pl.BlockSpec(memory_space=pl.ANY)],
            out_specs=pl.BlockSpec((MOE_BLK, d // 2), lambda i, be, nu: (i, 0)),
            scratch_shapes=[pltpu.VMEM((2, d, de), F32), pltpu.VMEM((2, d, de), F32),
                            pltpu.VMEM((2, de, d), F32),
                            pltpu.VMEM((d, de), BF16), pltpu.VMEM((d, de), BF16),
                            pltpu.VMEM((de, d), BF16),
                            pltpu.SMEM((1,), I32),
                            pltpu.SemaphoreType.DMA((2,))]),
        out_shape=jax.ShapeDtypeStruct((p, d // 2), I32),
        compiler_params=_cparams(("arbitrary",)),
        name="moe_expert_ffn",
    )(be, nu, xb, wg, wu, wd)


def _combine_kernel(d0_ref, d1_ref, x_ref, info_ref, g2_ref, yb_ref, o_ref, rows_scr, sem):
    i = pl.program_id(0)
    cur = i % 2

    def copy(buf, which, r, src_row, rows=1):
        return pltpu.make_async_copy(yb_ref.at[pl.ds(src_row, rows)],
                                     rows_scr.at[buf, which, pl.ds(r, rows)], sem.at[buf])

    def start_gather(step, buf):
        base = step * TOK_TILE

        def issue(g, c):
            for k in range(SUBLANES):
                r = g * SUBLANES + k
                copy(buf, 0, r, d0_ref[base + r]).start()
                copy(buf, 1, r, d1_ref[base + r]).start()
            return c

        lax.fori_loop(0, TOK_TILE // SUBLANES, issue, 0)

    @pl.when(i == 0)
    def _():
        start_gather(0, 0)

    @pl.when(i + 1 < pl.num_programs(0))
    def _():
        start_gather(i + 1, 1 - cur)

    copy(cur, 0, 0, 0, TOK_TILE).wait()
    copy(cur, 1, 0, 0, TOK_TILE).wait()

    info = info_ref[...]
    half = o_ref.shape[1] // 2
    y0_hi, y0_lo = _unpack_bf16_pairs(rows_scr[cur, 0])
    y1_hi, y1_lo = _unpack_bf16_pairs(rows_scr[cur, 1])
    g2 = g2_ref[0]
    o_ref[:, :half] = x_ref[:, :half] + g2[:, :half] * (info[:, 2:3] * y0_hi + info[:, 3:4] * y1_hi)
    o_ref[:, half:] = x_ref[:, half:] + g2[:, half:] * (info[:, 2:3] * y0_lo + info[:, 3:4] * y1_lo)


def _combine(d0, d1, x1, info, g2, yb, seq):
    n, d = x1.shape
    tpb = seq // TOK_TILE
    return pl.pallas_call(
        _combine_kernel,
        grid_spec=pltpu.PrefetchScalarGridSpec(
            num_scalar_prefetch=2,
            grid=(n // TOK_TILE,),
            in_specs=[pl.BlockSpec((TOK_TILE, d), lambda i, a, b: (i, 0)),
                      pl.BlockSpec((TOK_TILE, LANES), lambda i, a, b: (i, 0)),
                      pl.BlockSpec((1, 1, d), lambda i, a, b: (i // tpb, 0, 0)),
                      pl.BlockSpec(memory_space=pl.ANY)],
            out_specs=pl.BlockSpec((TOK_TILE, d), lambda i, a, b: (i, 0)),
            scratch_shapes=[pltpu.VMEM((2, 2, TOK_TILE, d // 2), I32),
                            pltpu.SemaphoreType.DMA((2,))]),
        out_shape=jax.ShapeDtypeStruct((n, d), F32),
        compiler_params=_cparams(("arbitrary",)),
        name="moe_combine",
    )(d0, d1, x1, info, g2, yb)


def _tri_inclusive_rev(k):
    l = (np.arange(k)[:, None] >= np.arange(k)[None, :]).astype(np.float32)
    half = np.concatenate([l, np.ones((k, k), np.float32)], axis=1)
    return jnp.asarray(np.concatenate([half, half], axis=0), dtype=BF16)


def _tri_strict_lower(k):
    return jnp.asarray((np.arange(k)[None, :] < np.arange(k)[:, None]).astype(np.float32), dtype=BF16)


def kernel(x, c, w_mod, b_mod, ln1_g, w_in, q_norm_g, w_q_up, q_gain, k_gain, rel_bias, gn_a, gn_b,
           w_out, ln2_g, router_g, router_e, w_gate, w_up, w_down):
    batch, seq, d = x.shape
    n = batch * seq
    assert w_mod.shape[0] == 1 and d == D_A + D_B
    assert seq % KB_A == 0 and KB_A % QB_A == 0 and seq % TOK_TILE == 0 and n % 512 == 0
    x2 = x.reshape(n, d)

    c8 = jnp.pad(c, ((0, 8 - batch), (0, 0)))
    mod = _modulation(c8, w_mod.reshape(d, -1), b_mod.reshape(1, -1))[:batch]
    sh1, sc1, g1, sh2, sc2, g2 = [m.reshape(batch, 1, d) for m in jnp.split(mod, 6, axis=-1)]

    wi = w_in.reshape(d, -1).astype(BF16)
    n_a = Q_RANK + 2 * HEAD_DIM + IDX_DIM + IDX_HEADS
    wi = jnp.concatenate([wi[:, n_a:], wi[:, :n_a],
                          jnp.zeros((d, D_IN_PAD - wi.shape[1]), wi.dtype)], axis=1)
    proj = _ln_proj(x2, ln1_g.reshape(1, d), sc1, sh1, wi, seq)

    qa, qidx, ka, va, kidx, widx = _aprep(
        proj, n, q_norm_g.reshape(1, -1), w_q_up.reshape(Q_RANK, -1).astype(BF16),
        q_gain.reshape(1, -1), k_gain.reshape(1, -1))

    bt = _bias_tiles(rel_bias)
    bound = _logit_bound(q_gain.reshape(1, -1), k_gain.reshape(1, -1), rel_bias)[0, :1]
    oa = _attn_a(qidx, widx.T, qa, kidx, ka, va, bt, gn_a.reshape(1, -1), bound, batch, seq)
    ob = _attn_b(proj, _tri_inclusive_rev(QB_B), batch, seq)

    rw = jnp.concatenate([router_e.reshape(d, -1), router_g.reshape(d, -1),
                          jnp.zeros((d, LANES - N_EXPERTS - N_GROUPS), F32)], axis=1)
    x1, h2, lg = _out_proj(x2, oa, ob, gn_b.reshape(1, -1), w_out.reshape(d, d).astype(BF16),
                           g1, ln2_g.reshape(1, d), sc2, sh2, rw, seq)

    info, cnt = _route(lg)
    counts = cnt[0, :N_EXPERTS].astype(I32)
    p_rows = 2 * n + N_EXPERTS * MOE_BLK
    ps, be, nu = _plan(counts, p_rows // MOE_BLK)
    ps_lanes = jnp.pad(ps.astype(F32), (0, LANES - N_EXPERTS)).reshape(1, LANES)
    dinfo = _dest(info, ps_lanes, _tri_strict_lower(512))
    d0 = dinfo[:, 0].astype(I32)
    d1 = dinfo[:, 1].astype(I32)
    xb = _scatter_rows(d0, d1, counts, ps, nu, h2, p_rows)
    yb = _experts(be, nu, xb,
                  w_gate.reshape(N_EXPERTS, d, D_EXPERT),
                  w_up.reshape(N_EXPERTS, d, D_EXPERT),
                  w_down.reshape(N_EXPERTS, D_EXPERT, d))
    out = _combine(d0, d1, x1, info, g2, yb, seq)
    return out.reshape(batch, seq, d)
```

```python
import functools
import math

import numpy as np
import jax
import jax.numpy as jnp
from jax import lax
from jax.experimental import pallas as pl
from jax.experimental.pallas import tpu as pltpu

F32 = jnp.float32
BF16 = jnp.bfloat16
I32 = jnp.int32

HEAD_DIM = 128
N_HEADS_A = 8
N_HEADS_B = 8
D_A = N_HEADS_A * HEAD_DIM
D_B = N_HEADS_B * HEAD_DIM
Q_RANK = 512
IDX_HEADS = 16
IDX_DIM = 64
TOPK_MAX = 256
N_BUCKETS = 32
MAX_DISTANCE = 128
N_GROUPS = 4
EXPERTS_PER_GROUP = 8
N_EXPERTS = N_GROUPS * EXPERTS_PER_GROUP
D_EXPERT = 512
EPS = 1e-6

LANES = 128
SUBLANES = 8
VMEM_LIMIT = 56 * 1024 * 1024
NEG = -1e30
INT_MIN = -(2 ** 31)
EXP2_UNDERFLOW = 150.0
LOG2E = math.log2(math.e)
PLAIN_SOFTMAX_LIMIT = 64.0
GUESS_HALF_WIDTH = 0.1
EXTRACT_BELOW = 3.0

QB_A = 128
KB_A = 256
QB_B = 128
MOE_BLK = 256
TOK_TILE = 512

COL_QB, COL_KB, COL_VB, COL_A = 0, D_B, 2 * D_B, 3 * D_B
A_CQ, A_KA, A_VA, A_KIDX, A_WIDX = 0, 512, 640, 768, 832
A_WIDTH = 1024
D_IN_PAD = COL_A + A_WIDTH


def _cparams(sem=None):
    return pltpu.CompilerParams(dimension_semantics=sem, vmem_limit_bytes=VMEM_LIMIT)


def _rms(x):
    return x * lax.rsqrt(jnp.mean(x * x, axis=-1, keepdims=True) + EPS)


def _pack_bf16_pairs(x):
    c = x.shape[1] // 2
    hi = pltpu.bitcast(x[:, :c].astype(BF16).astype(F32), I32)
    lo = pltpu.bitcast(x[:, c:].astype(BF16).astype(F32), I32)
    return hi | lax.shift_right_logical(lo, 16)


def _unpack_bf16_pairs(u):
    hi = pltpu.bitcast(u & jnp.int32(-65536), F32)
    lo = pltpu.bitcast(lax.shift_left(u, 16), F32)
    return hi, lo


def _mod_kernel(c_ref, w_ref, b_ref, o_ref):
    c = c_ref[...]
    s = c * (1.0 / (1.0 + jnp.exp(-c)))
    o_ref[...] = jnp.dot(s, w_ref[...], preferred_element_type=F32,
                         precision=lax.Precision.HIGHEST) + b_ref[...]


def _modulation(c8, w_mod, b_mod):
    d, n6 = w_mod.shape
    tn = 1024
    return pl.pallas_call(
        _mod_kernel,
        grid=(n6 // tn,),
        in_specs=[pl.BlockSpec((8, d), lambda j: (0, 0)),
                  pl.BlockSpec((d, tn), lambda j: (0, j)),
                  pl.BlockSpec((1, tn), lambda j: (0, j))],
        out_specs=pl.BlockSpec((8, tn), lambda j: (0, j)),
        out_shape=jax.ShapeDtypeStruct((8, n6), F32),
        compiler_params=_cparams(("arbitrary",)),
        name="modulation",
    )(c8, w_mod, b_mod)


def _ln_proj_kernel(x_ref, g_ref, sc_ref, sh_ref, w_ref, o_ref, h_scr, *, q_tiles, chunk):
    i = pl.program_id(0)
    j = pl.program_id(1)

    def normalise_chunk():
        rows = pl.ds(pl.multiple_of(j * chunk, chunk), chunk)
        h = _rms(x_ref[rows, :]) * g_ref[...]
        h = h * (1.0 + sc_ref[0]) + sh_ref[0]
        h_scr[i % 2, rows, :] = h.astype(BF16)

    @pl.when(i == 0)
    def _():
        normalise_chunk()
        o_ref[...] = jnp.zeros(o_ref.shape, o_ref.dtype)

    @pl.when(i > 0)
    def _():
        normalise_chunk()
        col_scale = jnp.where(j < q_tiles, HEAD_DIM ** -0.5 * LOG2E, 1.0)
        acc = jnp.dot(h_scr[(i - 1) % 2], w_ref[...], preferred_element_type=F32)
        o_ref[...] = (acc * col_scale).astype(o_ref.dtype)


def _ln_proj(x2, ln_g, sc, sh, w, seq):
    n, d = x2.shape
    ncol = w.shape[1]
    tm = min(1024, seq)
    tn = 1024
    tpb = seq // tm
    n_i, n_j = n // tm, ncol // tn
    assert COL_QB == 0 and D_B % tn == 0 and tm % (n_j * 16) == 0
    last = n_i - 1
    return pl.pallas_call(
        functools.partial(_ln_proj_kernel, q_tiles=D_B // tn, chunk=tm // n_j),
        grid=(n_i + 1, n_j),
        in_specs=[pl.BlockSpec((tm, d), lambda i, j: (jnp.minimum(i, last), 0)),
                  pl.BlockSpec((1, d), lambda i, j: (0, 0)),
                  pl.BlockSpec((1, 1, d), lambda i, j: (jnp.minimum(i, last) // tpb, 0, 0)),
                  pl.BlockSpec((1, 1, d), lambda i, j: (jnp.minimum(i, last) // tpb, 0, 0)),
                  pl.BlockSpec((d, tn), lambda i, j: (0, j))],
        out_specs=pl.BlockSpec((tm, tn), lambda i, j: (jnp.where(i == 0, n_i, i - 1), j)),
        out_shape=jax.ShapeDtypeStruct((n + tm, ncol), BF16),
        scratch_shapes=[pltpu.VMEM((2, tm, d), BF16)],
        compiler_params=_cparams(("arbitrary", "arbitrary")),
        name="ln_in_proj",
    )(x2, ln_g, sc, sh, w)


def _aprep_kernel(a_ref, qng_ref, wq_ref, qg_ref, kg_ref,
                  qa_ref, qidx_ref, ka_ref, va_ref, kidx_ref, widx_ref):
    cq = a_ref[:, A_CQ:A_CQ + Q_RANK].astype(F32)
    cqn = (_rms(cq) * qng_ref[...]).astype(BF16)
    qup = jnp.dot(cqn, wq_ref[...], preferred_element_type=F32)
    for h in range(N_HEADS_A):
        qh = qup[:, h * HEAD_DIM:(h + 1) * HEAD_DIM]
        qn = _rms(qh) * qg_ref[...] * (HEAD_DIM ** -0.5 * LOG2E)
        qa_ref[:, h * HEAD_DIM:(h + 1) * HEAD_DIM] = qn.astype(BF16)
    for h in range(IDX_HEADS):
        qi = qup[:, D_A + h * IDX_DIM:D_A + (h + 1) * IDX_DIM] * (IDX_DIM ** -0.5)
        qidx_ref[h] = qi.astype(BF16)
    ka = a_ref[:, A_KA:A_KA + HEAD_DIM].astype(F32)
    ka_ref[...] = (_rms(ka) * kg_ref[...]).astype(BF16)
    va_ref[...] = a_ref[:, A_VA:A_VA + HEAD_DIM]
    kidx_ref[...] = a_ref[:, A_KIDX:A_KIDX + IDX_DIM]
    widx_ref[...] = a_ref[:, A_WIDX:A_WIDX + IDX_HEADS].astype(F32) * (IDX_HEADS ** -0.5)


def _aprep(proj, n, q_norm_g, w_q_up, q_gain, k_gain):
    tm = 512
    cblk = COL_A // A_WIDTH
    nup = w_q_up.shape[1]
    return pl.pallas_call(
        _aprep_kernel,
        grid=(n // tm,),
        in_specs=[pl.BlockSpec((tm, A_WIDTH), lambda i: (i, cblk)),
                  pl.BlockSpec((1, Q_RANK), lambda i: (0, 0)),
                  pl.BlockSpec((Q_RANK, nup), lambda i: (0, 0)),
                  pl.BlockSpec((1, HEAD_DIM), lambda i: (0, 0)),
                  pl.BlockSpec((1, HEAD_DIM), lambda i: (0, 0))],
        out_specs=[pl.BlockSpec((tm, D_A), lambda i: (i, 0)),
                   pl.BlockSpec((IDX_HEADS, tm, IDX_DIM), lambda i: (0, i, 0)),
                   pl.BlockSpec((tm, HEAD_DIM), lambda i: (i, 0)),
                   pl.BlockSpec((tm, HEAD_DIM), lambda i: (i, 0)),
                   pl.BlockSpec((tm, IDX_DIM), lambda i: (i, 0)),
                   pl.BlockSpec((tm, IDX_HEADS), lambda i: (i, 0))],
        out_shape=[jax.ShapeDtypeStruct((n, D_A), BF16),
                   jax.ShapeDtypeStruct((IDX_HEADS, n, IDX_DIM), BF16),
                   jax.ShapeDtypeStruct((n, HEAD_DIM), BF16),
                   jax.ShapeDtypeStruct((n, HEAD_DIM), BF16),
                   jax.ShapeDtypeStruct((n, IDX_DIM), BF16),
                   jax.ShapeDtypeStruct((n, IDX_HEADS), F32)],
        compiler_params=_cparams(("parallel",)),
        name="group_a_prep",
    )(proj, q_norm_g, w_q_up, q_gain, k_gain)


def _t5_bucket_starts():
    max_exact = N_BUCKETS // 2
    d = np.arange(0, 4 * MAX_DISTANCE, dtype=np.int64)
    df = np.maximum(d, 1).astype(np.float32)
    large = max_exact + (np.log(df / np.float32(max_exact)) / np.float32(math.log(MAX_DISTANCE / max_exact))
                         * np.float32(N_BUCKETS - max_exact)).astype(np.int32)
    large = np.minimum(large, N_BUCKETS - 1)
    bucket = np.where(d < max_exact, d, large)
    assert np.all(np.diff(bucket) >= 0) and bucket[-1] == N_BUCKETS - 1
    return [int(np.argmax(bucket >= b)) for b in range(N_BUCKETS)]


_BUCKET_START = _t5_bucket_starts()
N_BIAS_TILES = 2 * KB_A // LANES


def _bias_kernel(rb_ref, o_ref):
    di = pl.program_id(0)
    h = pl.program_id(1)
    i = lax.broadcasted_iota(I32, (QB_A, KB_A), 0)
    j = lax.broadcasted_iota(I32, (QB_A, KB_A), 1)
    d = di * LANES + i - j
    val = jnp.full((QB_A, KB_A), rb_ref[0, h], F32)
    for b in range(1, N_BUCKETS):
        val = jnp.where(d >= _BUCKET_START[b], rb_ref[b, h], val)
    o_ref[0, 0] = (val - rb_ref[N_BUCKETS - 1, h]) * LOG2E


def _bias_tiles(rel_bias):
    return pl.pallas_call(
        _bias_kernel,
        grid=(N_BIAS_TILES, N_HEADS_A),
        in_specs=[pl.BlockSpec(memory_space=pltpu.SMEM)],
        out_specs=pl.BlockSpec((1, 1, QB_A, KB_A), lambda a, h: (a, h, 0, 0)),
        out_shape=jax.ShapeDtypeStruct((N_BIAS_TILES, N_HEADS_A, QB_A, KB_A), F32),
        compiler_params=_cparams(("arbitrary", "arbitrary")),
        name="t5_bias_tiles",
    )(rel_bias)


def _bound_kernel(qg_ref, kg_ref, rb_ref, o_ref):
    qmax = jnp.max(jnp.abs(qg_ref[...]), axis=1, keepdims=True)
    kmax = jnp.max(jnp.abs(kg_ref[...]), axis=1, keepdims=True)
    rb = rb_ref[...]
    shifted = jnp.abs(rb - rb[N_BUCKETS - 1:N_BUCKETS, :])
    bmax = jnp.max(jnp.max(shifted, axis=1, keepdims=True), axis=0, keepdims=True)
    bound = qmax * kmax * (math.sqrt(HEAD_DIM) * 1.02) + bmax
    o_ref[...] = jnp.broadcast_to(bound, o_ref.shape)


def _logit_bound(q_gain, k_gain, rel_bias):
    return pl.pallas_call(
        _bound_kernel,
        out_shape=jax.ShapeDtypeStruct((1, LANES), F32),
        name="dsa_logit_bound",
    )(q_gain, k_gain, rel_bias)


def _attn_a_kernel(qidx_ref, w_ref, qa_ref, kidx_ref, ka_ref, va_ref, bt_ref, gn_ref, bound_ref, o_ref,
                   keys_scr, keyt_scr, m_scr, acc_scr, s_scr, mb_scr, *, topk):
    qb = pl.program_id(1)
    t0 = qb * QB_A
    kbl = (t0 + QB_A - 1) // KB_A
    row = t0 + lax.broadcasted_iota(I32, (QB_A, KB_A), 0)
    col0 = lax.broadcasted_iota(I32, (QB_A, KB_A), 1)
    nt = (((1,), (1,)), ((), ()))

    reps = KB_A // LANES

    def sort_key(v):
        bits = pltpu.bitcast(v, I32)
        return jnp.where(bits < 0, bits ^ jnp.int32(0x7FFFFFFF), bits)

    def key_value(k):
        return pltpu.bitcast(jnp.where(k < 0, k ^ jnp.int32(0x7FFFFFFF), k), F32)

    key_pos = lax.broadcasted_iota(I32, (KB_A, QB_A), 0)
    qry_pos = t0 + lax.broadcasted_iota(I32, (KB_A, QB_A), 1)

    def score_tile(kb, carry, diagonal):
        smin, smax, s1, s2 = carry
        kt = kidx_ref[pl.ds(pl.multiple_of(kb * KB_A, KB_A), KB_A), :]
        score = jnp.zeros((KB_A, QB_A), F32)
        for hp in range(IDX_HEADS // 2):
            q2 = qidx_ref[2 * hp:2 * hp + 2].reshape(2 * QB_A, IDX_DIM)
            sc = lax.dot_general(kt, q2, nt, preferred_element_type=F32)
            score = score + w_ref[2 * hp:2 * hp + 1, :] * jnp.maximum(sc[:, :QB_A], 0.0)
            score = score + w_ref[2 * hp + 1:2 * hp + 2, :] * jnp.maximum(sc[:, QB_A:], 0.0)
        key_t = sort_key(score)
        low, high, live = score, score, score
        if diagonal:
            causal = (kb * KB_A + key_pos) <= qry_pos
            key_t = jnp.where(causal, key_t, jnp.int32(INT_MIN))
            low = jnp.where(causal, score, jnp.inf)
            high = jnp.where(causal, score, -jnp.inf)
            live = jnp.where(causal, score, 0.0)
        keyt_scr[kb] = key_t
        keys_scr[kb] = key_t.T
        smin = jnp.minimum(smin, jnp.min(low, axis=0, keepdims=True))
        smax = jnp.maximum(smax, jnp.max(high, axis=0, keepdims=True))
        s1 = s1 + jnp.sum(live, axis=0, keepdims=True)
        s2 = s2 + jnp.sum(live * live, axis=0, keepdims=True)
        return smin, smax, s1, s2

    def score_group(first, count, carry):
        for u in range(count):
            carry = score_tile(first + u, carry, False)
        return carry

    zero_row = jnp.zeros((1, QB_A), F32)
    stats = lax.fori_loop(0, kbl // 4, lambda i, c: score_group(4 * i, 4, c),
                          (jnp.full((1, QB_A), jnp.inf, F32), jnp.full((1, QB_A), -jnp.inf, F32),
                           zero_row, zero_row))
    stats = lax.cond(kbl % 4 >= 2, lambda c: score_group((kbl // 4) * 4, 2, c), lambda c: c, stats)
    stats = lax.cond(kbl % 2 == 1, lambda c: score_tile(kbl - 1, c, False), lambda c: c, stats)
    smin, smax, s1, s2 = score_tile(kbl, stats, True)

    qry1 = t0 + lax.broadcasted_iota(I32, (1, QB_A), 1)
    kf = float(topk)
    acc_rows = 32
    n_causal = (qry1 + 1).astype(F32)

    mean = s1 / n_causal
    dev = jnp.sqrt(jnp.maximum(s2 / n_causal - mean * mean, 0.0))
    frac_top = jnp.minimum(kf / n_causal, 1.0)
    tail = jnp.minimum(frac_top, 1.0 - frac_top)
    tq = jnp.sqrt(-2.0 * jnp.log(jnp.maximum(tail, 1e-6)))
    zq = tq - ((0.010328 * tq + 0.802853) * tq + 2.515517) / (((0.001308 * tq + 0.189269) * tq + 1.432788) * tq + 1.0)
    zq = jnp.where(frac_top > 0.5, -zq, zq)

    def for_each_key_tile(fold, init):
        n_all = kbl + 1
        acc = lax.fori_loop(0, n_all // 2,
                            lambda i, a: fold(fold(a, keyt_scr[2 * i]), keyt_scr[2 * i + 1]), init)
        return lax.cond(n_all % 2 == 1, lambda a: fold(a, keyt_scr[n_all - 1]), lambda a: a, acc)

    def search_pass(p, state):
        lo, hi, clo, chi, open_q = state
        lo_v = key_value(lo)
        hi_v = key_value(hi)
        gap = clo - chi
        frac = (clo - (kf - 0.5)) / gap
        frac = jnp.where(p % 2 == 1, 0.7 * frac + 0.15, frac)
        frac = jnp.where(gap > 16.0, frac, 0.5)
        value = lo_v + (hi_v - lo_v) * frac
        value = jnp.where(p == 0, mean + (zq - GUESS_HALF_WIDTH) * dev, value)
        value = jnp.where(p == 1, mean + (zq + GUESS_HALF_WIDTH) * dev, value)
        cand = sort_key(value)
        cand = jnp.where(p % 8 == 7, lo + lax.shift_right_logical(hi - lo, 1), cand)
        cand = jnp.minimum(jnp.maximum(cand, lo + 1), hi - 1)

        def count(cnt, keys):
            ge = jnp.where(keys >= cand, 1.0, 0.0)
            return cnt + jnp.sum(ge.reshape(KB_A // acc_rows, acc_rows, QB_A), axis=0)

        tot = jnp.sum(for_each_key_tile(count, jnp.zeros((acc_rows, QB_A), F32)), axis=0, keepdims=True)
        ge = tot >= kf
        lo = jnp.where(ge, cand, lo)
        clo = jnp.where(ge, tot, clo)
        hi = jnp.where(ge, hi, cand)
        chi = jnp.where(ge, chi, tot)
        width = hi - lo
        settled = jnp.where(clo == kf, 1.0, jnp.where(width == 1, 1.0, 0.0))
        open_q = jnp.where(settled > 0.5, 0.0, open_q)
        return lo, hi, clo, chi, open_q

    def extract_pass(state):
        lo, hi, clo, chi, open_q = state

        def top_below(best, keys):
            below = jnp.where(keys < hi, keys, jnp.int32(INT_MIN))
            return jnp.maximum(best, jnp.max(below.reshape(KB_A // acc_rows, acc_rows, QB_A), axis=0))

        best = for_each_key_tile(top_below, jnp.full((acc_rows, QB_A), INT_MIN, I32))
        best = jnp.max(best, axis=0, keepdims=True)
        is_open = open_q > 0.5
        last = jnp.logical_and(is_open, kf - chi <= 1.0)
        more = jnp.logical_and(is_open, kf - chi > 1.0)
        lo = jnp.where(last, best, lo)
        clo = jnp.where(last, chi + 1.0, clo)
        hi = jnp.where(more, best, hi)
        chi = jnp.where(more, chi + 1.0, chi)
        open_q = jnp.where(last, 0.0, open_q)
        return lo, hi, clo, chi, open_q

    def missing(state):
        return jnp.max(jnp.where(state[4] > 0.5, kf - state[3], 0.0))

    def search_step(carry):
        p, state, lacking = carry

        def searching(st):
            st = search_pass(p + 1, search_pass(p, st))
            return lax.cond(p == 0, lambda s: search_pass(3, search_pass(2, s)), lambda s: s, st)

        state = lax.cond(lacking <= EXTRACT_BELOW, lambda st: extract_pass(extract_pass(st)),
                         searching, state)
        return jnp.where(p == 0, 4, p + 2), state, missing(state)

    open0 = jnp.where(qry1 >= topk, 1.0, 0.0)
    lo0 = sort_key(smin)
    hi0 = sort_key(smax) + 1
    open0 = jnp.where(hi0 - lo0 == 1, 0.0, open0)
    state0 = (lo0, hi0, n_causal, zero_row, open0)
    _, (lo, _, _, _, _), _ = lax.while_loop(
        lambda c: c[2] > 0.0, search_step, (jnp.int32(0), state0, missing(state0)))
    thr = jnp.where(qry1 >= topk, lo, jnp.int32(INT_MIN))
    thr = jnp.broadcast_to(thr, (QB_A, QB_A)).T
    thrb = jnp.tile(thr, (1, reps))

    m_scr[...] = jnp.full(m_scr.shape, NEG, F32)
    acc_scr[...] = jnp.zeros(acc_scr.shape, F32)

    plain = bound_ref[0] <= PLAIN_SOFTMAX_LIMIT

    def attend(kb, near, online):
        mb = jnp.where(keys_scr[kb] >= thrb, 0.0, NEG)
        if near:
            mb = jnp.where((kb * KB_A + col0) <= row, mb, NEG)
            di = (t0 - kb * KB_A) // LANES
        mb_scr[...] = mb
        start = pl.multiple_of(kb * KB_A, KB_A)
        kt = ka_ref[pl.ds(start, KB_A), :]
        vt = jnp.concatenate([va_ref[pl.ds(start, KB_A), :], jnp.ones((KB_A, LANES), BF16)], axis=1)
        for h in range(N_HEADS_A):
            q = qa_ref[:, h * HEAD_DIM:(h + 1) * HEAD_DIM]
            s_scr[h] = lax.dot_general(q, kt, nt, preferred_element_type=F32)
        for h in range(N_HEADS_A):
            s = s_scr[h] + mb_scr[...]
            if near:
                s = s + bt_ref[di, h]
            if online:
                m_prev = m_scr[h]
                m_new = jnp.maximum(m_prev, jnp.max(s, axis=1, keepdims=True))
                alpha = jnp.exp2(m_prev - m_new)
                p = jnp.exp2(s - jnp.tile(m_new, (1, reps)))
                acc_scr[h] = (jnp.tile(alpha, (1, 2)) * acc_scr[h]
                              + jnp.dot(p.astype(BF16), vt, preferred_element_type=F32))
                m_scr[h] = m_new
            else:
                acc_scr[h] += jnp.dot(jnp.exp2(s).astype(BF16), vt, preferred_element_type=F32)

    def attend_all(online):
        prev_near = jnp.logical_and(kbl >= 1, t0 - kbl * KB_A < _BUCKET_START[-1] - 1)
        n_far = jnp.where(prev_near, kbl - 1, kbl)
        group = 2 if online else 4

        def far_group(i, carry):
            for u in range(group):
                attend(group * i + u, False, online)
            return carry

        lax.fori_loop(0, n_far // group, far_group, 0)

        if group == 4:
            @pl.when(n_far % 4 >= 2)
            def _():
                attend((n_far // 4) * 4, False, online)
                attend((n_far // 4) * 4 + 1, False, online)

        @pl.when(n_far % 2 == 1)
        def _():
            attend(n_far - 1, False, online)

        @pl.when(prev_near)
        def _():
            attend(kbl - 1, True, online)
            attend(kbl, True, online)

        @pl.when(jnp.logical_not(prev_near))
        def _():
            attend(kbl, True, online)

    @pl.when(plain)
    def _():
        attend_all(False)

    @pl.when(jnp.logical_not(plain))
    def _():
        attend_all(True)

    ssq = jnp.zeros((QB_A, LANES), F32)
    for h in range(N_HEADS_A):
        oh = acc_scr[h, :, :HEAD_DIM] / acc_scr[h, :, HEAD_DIM:]
        acc_scr[h, :, :HEAD_DIM] = oh
        ssq = ssq + jnp.sum(oh * oh, axis=1, keepdims=True)
    inv = lax.rsqrt(ssq * (1.0 / D_A) + EPS)
    for h in range(N_HEADS_A):
        sl = slice(h * HEAD_DIM, (h + 1) * HEAD_DIM)
        o_ref[:, sl] = (acc_scr[h, :, :HEAD_DIM] * inv * gn_ref[:, sl]).astype(o_ref.dtype)


def _attn_a(qidx, widx, qa, kidx, ka, va, bt, gn_a, bound, batch, seq):
    n = qa.shape[0]
    nq = seq // QB_A
    nkt = seq // KB_A
    topk = min(TOPK_MAX, seq // 4)
    return pl.pallas_call(
        functools.partial(_attn_a_kernel, topk=topk),
        grid=(batch, nq),
        in_specs=[pl.BlockSpec((IDX_HEADS, QB_A, IDX_DIM), lambda b, q: (0, b * nq + q, 0)),
                  pl.BlockSpec((IDX_HEADS, QB_A), lambda b, q: (0, b * nq + q)),
                  pl.BlockSpec((QB_A, D_A), lambda b, q: (b * nq + q, 0)),
                  pl.BlockSpec((seq, IDX_DIM), lambda b, q: (b, 0)),
                  pl.BlockSpec((seq, HEAD_DIM), lambda b, q: (b, 0)),
                  pl.BlockSpec((seq, HEAD_DIM), lambda b, q: (b, 0)),
                  pl.BlockSpec(bt.shape, lambda b, q: (0, 0, 0, 0)),
                  pl.BlockSpec((1, D_A), lambda b, q: (0, 0)),
                  pl.BlockSpec(memory_space=pltpu.SMEM)],
        out_specs=pl.BlockSpec((QB_A, D_A), lambda b, q: (b * nq + q, 0)),
        out_shape=jax.ShapeDtypeStruct((n, D_A), BF16),
        scratch_shapes=[pltpu.VMEM((nkt, QB_A, KB_A), I32),
                        pltpu.VMEM((nkt, KB_A, QB_A), I32),
                        pltpu.VMEM((N_HEADS_A, QB_A, LANES), F32),
                        pltpu.VMEM((N_HEADS_A, QB_A, 2 * HEAD_DIM), F32),
                        pltpu.VMEM((N_HEADS_A, QB_A, KB_A), F32),
                        pltpu.VMEM((QB_A, KB_A), F32)],
        compiler_params=_cparams(("parallel", "arbitrary")),
        name="dsa_attention",
    )(qidx, widx, qa, kidx, ka, va, bt, gn_a, bound)


def _attn_b_kernel(q_ref, k_ref, v_ref, tri_ref, o_ref, rest_scr, z_scr, cs_scr):
    qb = pl.program_id(1)
    row = lax.broadcasted_iota(I32, (QB_B, QB_B), 0)
    col = lax.broadcasted_iota(I32, (QB_B, QB_B), 1)
    strict = col < row
    nt = (((1,), (1,)), ((), ()))
    def step(kb, diag):
        start = pl.multiple_of(kb * QB_B, QB_B)
        heads = [slice(h * HEAD_DIM, (h + 1) * HEAD_DIM) for h in range(N_HEADS_B)]
        for h, sl in enumerate(heads):
            kt = k_ref[pl.ds(start, QB_B), sl]
            z_scr[h] = lax.dot_general(q_ref[:, sl], kt, nt, preferred_element_type=F32)
        for h, sl in enumerate(heads):
            z = z_scr[h]
            sp = jnp.maximum(z, 0.0) + jnp.log(1.0 + jnp.exp2(-jnp.abs(z))) * LOG2E
            if diag:
                sp = jnp.where(strict, sp, 0.0)
            hi = sp.astype(BF16)
            lo = (sp - hi.astype(F32)).astype(BF16)
            cs_scr[h] = jnp.dot(jnp.concatenate([hi, lo], axis=1), tri_ref[...],
                                preferred_element_type=F32)
        worst = None
        for h, sl in enumerate(heads):
            vt = v_ref[pl.ds(start, QB_B), sl]
            z = z_scr[h]
            cs = cs_scr[h, :, :QB_B]
            tot = cs_scr[h, :, QB_B:]
            if diag:
                a = jnp.where(strict, jnp.exp2(z - cs), 0.0)
                o_ref[:, sl] = jnp.dot(a.astype(BF16), vt, preferred_element_type=F32)
                rest = tot
            else:
                rest = rest_scr[h]
                a = jnp.exp2(z - cs - rest)
                o_ref[:, sl] += jnp.dot(a.astype(BF16), vt, preferred_element_type=F32)
                rest = rest + tot
            rest_scr[h] = rest
            worst = rest if worst is None else jnp.minimum(worst, rest)
        return jnp.min(worst)

    def more(kb, smallest):
        return jnp.logical_and(kb >= 0, smallest < EXP2_UNDERFLOW)

    def body(carry):
        kb, _ = carry
        return kb - 1, more(kb - 1, step(kb, False))

    def first_tiles(have_previous):
        smallest = step(qb, True)
        if have_previous:
            smallest = step(qb - 1, False)
        return smallest

    smallest = lax.cond(qb >= 1, lambda: first_tiles(True), lambda: first_tiles(False))
    lax.while_loop(lambda c: c[1], body, (qb - 2, more(qb - 2, smallest)))


def _attn_b(proj, tri, batch, seq):
    n = batch * seq
    nq = seq // QB_B
    return pl.pallas_call(
        _attn_b_kernel,
        grid=(batch, nq),
        in_specs=[pl.BlockSpec((QB_B, D_B), lambda b, q: (b * nq + q, COL_QB // D_B)),
                  pl.BlockSpec((seq, D_B), lambda b, q: (b, COL_KB // D_B)),
                  pl.BlockSpec((seq, D_B), lambda b, q: (b, COL_VB // D_B)),
                  pl.BlockSpec(tri.shape, lambda b, q: (0, 0))],
        out_specs=pl.BlockSpec((QB_B, D_B), lambda b, q: (b * nq + q, 0)),
        out_shape=jax.ShapeDtypeStruct((n, D_B), F32),
        scratch_shapes=[pltpu.VMEM((N_HEADS_B, QB_B, QB_B), F32),
                        pltpu.VMEM((N_HEADS_B, QB_B, QB_B), F32),
                        pltpu.VMEM((N_HEADS_B, QB_B, 2 * QB_B), F32)],
        compiler_params=_cparams(("parallel", "arbitrary")),
        name="stick_breaking_attention",
    )(proj, proj, proj, tri)


def _out_proj_kernel(x_ref, oa_ref, ob_ref, gnb_ref, w_ref, g1_ref, ln_ref, sc_ref, sh_ref, rw_ref,
                     x1_ref, h2_ref, lg_ref):
    obn = (_rms(ob_ref[...]) * gnb_ref[...]).astype(BF16)
    y = jnp.dot(oa_ref[...], w_ref[0:D_A, :], preferred_element_type=F32)
    y = y + jnp.dot(obn, w_ref[D_A:D_A + D_B, :], preferred_element_type=F32)
    x1 = x_ref[...] + g1_ref[0] * y
    x1_ref[...] = x1
    h2 = _rms(x1) * ln_ref[...]
    h2 = h2 * (1.0 + sc_ref[0]) + sh_ref[0]
    h2_ref[...] = _pack_bf16_pairs(h2)
    hh = h2.astype(BF16)
    hl = (h2 - hh.astype(F32)).astype(BF16)
    rw = rw_ref[...]
    rh = rw.astype(BF16)
    rl = (rw - rh.astype(F32)).astype(BF16)
    both = jnp.dot(hh, jnp.concatenate([rh, rl], axis=1), preferred_element_type=F32)
    lg_ref[...] = both[:, :LANES] + both[:, LANES:] + jnp.dot(hl, rh, preferred_element_type=F32)


def _out_proj(x2, oa, ob, gn_b, w_out, g1, ln_g, sc, sh, rw, seq):
    n, d = x2.shape
    tm = 512
    tpb = seq // tm
    row = lambda i: (i, 0)
    fixed = lambda i: (0, 0)
    perb = lambda i: (i // tpb, 0, 0)
    return pl.pallas_call(
        _out_proj_kernel,
        grid=(n // tm,),
        in_specs=[pl.BlockSpec((tm, d), row),
                  pl.BlockSpec((tm, D_A), row),
                  pl.BlockSpec((tm, D_B), row),
                  pl.BlockSpec((1, D_B), fixed),
                  pl.BlockSpec(w_out.shape, fixed),
                  pl.BlockSpec((1, 1, d), perb),
                  pl.BlockSpec((1, d), fixed),
                  pl.BlockSpec((1, 1, d), perb),
                  pl.BlockSpec((1, 1, d), perb),
                  pl.BlockSpec(rw.shape, fixed)],
        out_specs=[pl.BlockSpec((tm, d), row),
                   pl.BlockSpec((tm, d // 2), row),
                   pl.BlockSpec((tm, LANES), row)],
        out_shape=[jax.ShapeDtypeStruct((n, d), F32),
                   jax.ShapeDtypeStruct((n, d // 2), I32),
                   jax.ShapeDtypeStruct((n, LANES), F32)],
        compiler_params=_cparams(("parallel",)),
        name="out_proj_ln2_router",
    )(x2, oa, ob, gn_b, w_out, g1, ln_g, sc, sh, rw)


def _route_kernel(lg_ref, info_ref, cnt_ref):
    @pl.when(pl.program_id(0) == 0)
    def _():
        cnt_ref[...] = jnp.zeros(cnt_ref.shape, F32)

    lg = lg_ref[...]
    lane = lax.broadcasted_iota(I32, lg.shape, 1)
    lanef = lane.astype(F32)
    big = float(4 * LANES)
    gm = jnp.where(lane >= N_EXPERTS, jnp.where(lane < N_EXPERTS + N_GROUPS, 1.0, 0.0), 0.0) > 0.5
    lgm = jnp.where(gm, lg, NEG)
    mg = jnp.max(lgm, axis=1, keepdims=True)
    eg = jnp.where(gm, jnp.exp(lgm - mg), 0.0)
    pg = eg / jnp.sum(eg, axis=1, keepdims=True)
    gw = jnp.max(pg, axis=1, keepdims=True)
    gidx = jnp.min(jnp.where(gm, jnp.where(pg == gw, lanef - N_EXPERTS, big), big), axis=1, keepdims=True)
    lane_group = (lane // EXPERTS_PER_GROUP).astype(F32)
    em = jnp.where(lane < N_EXPERTS, jnp.where(lane_group == gidx, 1.0, 0.0), 0.0) > 0.5
    lem = jnp.where(em, lg, NEG)
    me = jnp.max(lem, axis=1, keepdims=True)
    ee = jnp.where(em, jnp.exp(lem - me), 0.0)
    pe = jnp.where(em, ee / jnp.sum(ee, axis=1, keepdims=True), -1.0)
    p1 = jnp.max(pe, axis=1, keepdims=True)
    i1 = jnp.min(jnp.where(pe == p1, lanef, big), axis=1, keepdims=True)
    pe2 = jnp.where(lanef == i1, -1.0, pe)
    p2 = jnp.max(pe2, axis=1, keepdims=True)
    i2 = jnp.min(jnp.where(pe2 == p2, lanef, big), axis=1, keepdims=True)
    den = p1 + p2
    g0 = gw * p1 / den
    g1 = gw * p2 / den
    info = jnp.where(lane == 0, i1, jnp.where(lane == 1, i2,
                     jnp.where(lane == 2, g0, jnp.where(lane == 3, g1, 0.0))))
    info_ref[...] = info
    oh = jnp.where(lanef == i1, 1.0, 0.0) + jnp.where(lanef == i2, 1.0, 0.0)
    cnt_ref[...] += jnp.sum(oh, axis=0, keepdims=True)


def _route(lg):
    n = lg.shape[0]
    tm = min(1024, n)
    return pl.pallas_call(
        _route_kernel,
        grid=(n // tm,),
        in_specs=[pl.BlockSpec((tm, LANES), lambda i: (i, 0))],
        out_specs=[pl.BlockSpec((tm, LANES), lambda i: (i, 0)),
                   pl.BlockSpec((1, LANES), lambda i: (0, 0))],
        out_shape=[jax.ShapeDtypeStruct((n, LANES), F32),
                   jax.ShapeDtypeStruct((1, LANES), F32)],
        compiler_params=_cparams(("arbitrary",)),
        name="moe_route",
    )(lg)


def _plan_kernel(cnt_ref, ps_ref, be_ref, nu_ref, *, nblk):
    def fill(i, c):
        be_ref[i] = N_EXPERTS - 1
        return c

    lax.fori_loop(0, nblk, fill, 0)

    def per_expert(e, pos):
        ps_ref[e] = pos * MOE_BLK
        nb = (cnt_ref[e] + MOE_BLK - 1) // MOE_BLK

        def mark(k, c):
            be_ref[pos + k] = e
            return c

        lax.fori_loop(0, nb, mark, 0)
        return pos + nb

    nu_ref[0] = lax.fori_loop(0, N_EXPERTS, per_expert, jnp.int32(0))


def _plan(counts, nblk):
    smem = pl.BlockSpec(memory_space=pltpu.SMEM)
    return pl.pallas_call(
        functools.partial(_plan_kernel, nblk=nblk),
        in_specs=[smem],
        out_specs=[smem, smem, smem],
        out_shape=[jax.ShapeDtypeStruct((N_EXPERTS,), I32),
                   jax.ShapeDtypeStruct((nblk,), I32),
                   jax.ShapeDtypeStruct((1,), I32)],
        name="moe_block_plan",
    )(counts)


def _dest_kernel(info_ref, ps_ref, tri_ref, o_ref, carry_scr):
    @pl.when(pl.program_id(0) == 0)
    def _():
        carry_scr[...] = jnp.zeros(carry_scr.shape, F32)

    info = info_ref[...]
    lane = lax.broadcasted_iota(I32, info.shape, 1)
    lanef = lane.astype(F32)
    o1 = jnp.where(lanef == info[:, 0:1], 1.0, 0.0)
    o2 = jnp.where(lanef == info[:, 1:2], 1.0, 0.0)
    oh = o1 + o2
    before = jnp.dot(tri_ref[...], oh.astype(BF16), preferred_element_type=F32)
    base = before + carry_scr[...] + ps_ref[...]
    d1 = jnp.sum(o1 * base, axis=1, keepdims=True)
    d2 = jnp.sum(o2 * base, axis=1, keepdims=True)
    o_ref[...] = jnp.where(lane == 0, d1, jnp.where(lane == 1, d2, 0.0))
    carry_scr[...] += jnp.sum(oh, axis=0, keepdims=True)


def _dest(info, ps_lanes, tri):
    n = info.shape[0]
    tm = tri.shape[0]
    return pl.pallas_call(
        _dest_kernel,
        grid=(n // tm,),
        in_specs=[pl.BlockSpec((tm, LANES), lambda i: (i, 0)),
                  pl.BlockSpec((1, LANES), lambda i: (0, 0)),
                  pl.BlockSpec((tm, tm), lambda i: (0, 0))],
        out_specs=pl.BlockSpec((tm, LANES), lambda i: (i, 0)),
        out_shape=jax.ShapeDtypeStruct((n, LANES), F32),
        scratch_shapes=[pltpu.VMEM((1, LANES), F32)],
        compiler_params=_cparams(("arbitrary",)),
        name="moe_dest_rows",
    )(info, ps_lanes, tri)


SCATTER_SLOTS = 3


def _scatter_kernel(d0_ref, d1_ref, cnt_ref, ps_ref, nu_ref, h_ref, xb_ref,
                    zero_scr, stage_scr, load_sem, out_sem, zero_sem, *, nblk):
    n_tiles = h_ref.shape[0] // TOK_TILE
    zero_scr[...] = jnp.zeros(zero_scr.shape, zero_scr.dtype)
    pad_sizes = [s for s in (1 << k for k in range(MOE_BLK.bit_length() - 2, -1, -1))
                 if s >= SUBLANES]

    def zero_copy(rows, dst_row):
        return pltpu.make_async_copy(zero_scr.at[pl.ds(0, rows)], xb_ref.at[pl.ds(dst_row, rows)],
                                     zero_sem)

    def load(j, slot):
        return pltpu.make_async_copy(h_ref.at[pl.ds(j * TOK_TILE, TOK_TILE)], stage_scr.at[slot],
                                     load_sem.at[slot])

    def row_copy(slot, r, dst_row, parity, rows=1):
        return pltpu.make_async_copy(stage_scr.at[slot, pl.ds(r, rows)],
                                     xb_ref.at[pl.ds(dst_row, rows)], out_sem.at[parity])

    def for_each_zero_copy(act):
        def per_expert(e, c):
            pos = ps_ref[e] + cnt_ref[e]
            pad = (MOE_BLK - (cnt_ref[e] & (MOE_BLK - 1))) & (MOE_BLK - 1)
            head = pad & (SUBLANES - 1)
            for k in range(SUBLANES - 1):
                @pl.when(k < head)
                def _():
                    act(zero_copy(1, pos + k))
            pos = pl.multiple_of(pos + head, SUBLANES)
            for size in pad_sizes:
                @pl.when((pad & size) != 0)
                def _():
                    act(zero_copy(size, pos))
                pos = pl.multiple_of(pos + (pad & size), SUBLANES)
            return c

        def per_tail_block(i, c):
            act(zero_copy(MOE_BLK, i * MOE_BLK))
            return c

        lax.fori_loop(0, N_EXPERTS, per_expert, 0)
        lax.fori_loop(nu_ref[0], nblk, per_tail_block, 0)

    for_each_zero_copy(lambda cp: cp.start())

    def retire(parity):
        row_copy(0, 0, 0, parity, TOK_TILE).wait()
        row_copy(0, 0, 0, parity, TOK_TILE).wait()

    def tile(j, c):
        slot = j % SCATTER_SLOTS
        parity = j % 2
        load(j, slot).wait()

        @pl.when(j + 1 < n_tiles)
        def _():
            load(j + 1, (j + 1) % SCATTER_SLOTS).start()

        def issue(g, cc):
            for k in range(SUBLANES):
                r = g * SUBLANES + k
                t = j * TOK_TILE + r
                row_copy(slot, r, d0_ref[t], parity).start()
                row_copy(slot, r, d1_ref[t], parity).start()
            return cc

        lax.fori_loop(0, TOK_TILE // SUBLANES, issue, 0)

        @pl.when(j >= 1)
        def _():
            retire(1 - parity)
        return c

    load(0, 0).start()
    lax.fori_loop(0, n_tiles, tile, 0)
    retire((n_tiles - 1) % 2)
    for_each_zero_copy(lambda cp: cp.wait())


def _scatter_rows(d0, d1, counts, ps, nu, h2, p_rows):
    n, d = h2.shape
    smem = pl.BlockSpec(memory_space=pltpu.SMEM)
    return pl.pallas_call(
        functools.partial(_scatter_kernel, nblk=p_rows // MOE_BLK),
        in_specs=[smem, smem, smem, smem, smem, pl.BlockSpec(memory_space=pl.ANY)],
        out_specs=pl.BlockSpec(memory_space=pl.ANY),
        out_shape=jax.ShapeDtypeStruct((p_rows, d), h2.dtype),
        scratch_shapes=[pltpu.VMEM((MOE_BLK, d), h2.dtype),
                        pltpu.VMEM((SCATTER_SLOTS, TOK_TILE, d), h2.dtype),
                        pltpu.SemaphoreType.DMA((SCATTER_SLOTS,)),
                        pltpu.SemaphoreType.DMA((2,)),
                        pltpu.SemaphoreType.DMA(())],
        compiler_params=_cparams(),
        name="moe_scatter_rows",
    )(d0, d1, counts, ps, nu, h2)


def _expert_kernel(be_ref, nu_ref, x_ref, wg_hbm, wu_hbm, wd_hbm, o_ref,
                   wg_f32, wu_f32, wd_f32, wg_scr, wu_scr, wd_scr, slot_ref, sem):
    i = pl.program_id(0)
    n_used = nu_ref[0]
    expert = be_ref[i]
    new_expert = jnp.logical_or(i == 0, expert != be_ref[jnp.maximum(i - 1, 0)])

    def weight_copies(e, slot):
        return [pltpu.make_async_copy(src.at[e], dst.at[slot], sem.at[slot])
                for src, dst in ((wg_hbm, wg_f32), (wu_hbm, wu_f32), (wd_hbm, wd_f32))]

    @pl.when(i == 0)
    def _():
        slot_ref[0] = 0
        for cp in weight_copies(expert, 0):
            cp.start()

    @pl.when(jnp.logical_and(new_expert, i < n_used))
    def _():
        slot = slot_ref[0]
        for cp in weight_copies(expert, slot):
            cp.wait()
        nxt = lax.while_loop(lambda j: jnp.logical_and(j < n_used, be_ref[jnp.minimum(j, n_used - 1)] == expert),
                             lambda j: j + 1, i + 1)

        @pl.when(nxt < n_used)
        def _():
            for cp in weight_copies(be_ref[nxt], 1 - slot):
                cp.start()

        wg_scr[...] = wg_f32[slot].astype(BF16)
        wu_scr[...] = wu_f32[slot].astype(BF16)
        wd_scr[...] = wd_f32[slot].astype(BF16)
        slot_ref[0] = 1 - slot

    @pl.when(i < n_used)
    def _():
        x_hi, x_lo = _unpack_bf16_pairs(x_ref[...])
        x = jnp.concatenate([x_hi.astype(BF16), x_lo.astype(BF16)], axis=1)
        g = jnp.dot(x, wg_scr[...], preferred_element_type=F32)
        u = jnp.dot(x, wu_scr[...], preferred_element_type=F32)
        act = (g * (1.0 / (1.0 + jnp.exp(-g))) * u).astype(BF16)
        o_ref[...] = _pack_bf16_pairs(jnp.dot(act, wd_scr[...], preferred_element_type=F32))

    @pl.when(i >= nu_ref[0])
    def _():
        o_ref[...] = jnp.zeros(o_ref.shape, o_ref.dtype)


def _experts(be, nu, xb, wg, wu, wd):
    p = xb.shape[0]
    _, d, de = wg.shape
    return pl.pallas_call(
        _expert_kernel,
        grid_spec=pltpu.PrefetchScalarGridSpec(
            num_scalar_prefetch=2,
            grid=(p // MOE_BLK,),
            in_specs=[pl.BlockSpec((MOE_BLK, d // 2), lambda i, be, nu: (i, 0)),
                      pl.BlockSpec(memory_space=pl.ANY),
                      pl.BlockSpec(memory_space=pl.ANY),
                      pl.BlockSpec(memory_space=pl.ANY)],
            out_specs=pl.BlockSpec((MOE_BLK, d // 2), lambda i, be, nu: (i, 0)),
            scratch_shapes=[pltpu.VMEM((2, d, de), F32), pltpu.VMEM((2, d, de), F32),
                            pltpu.VMEM((2, de, d), F32),
                            pltpu.VMEM((d, de), BF16), pltpu.VMEM((d, de), BF16),
                            pltpu.VMEM((de, d), BF16),
                            pltpu.SMEM((1,), I32),
                            pltpu.SemaphoreType.DMA((2,))]),
        out_shape=jax.ShapeDtypeStruct((p, d // 2), I32),
        compiler_params=_cparams(("arbitrary",)),
        name="moe_expert_ffn",
    )(be, nu, xb, wg, wu, wd)


def _combine_kernel(d0_ref, d1_ref, x_ref, info_ref, g2_ref, yb_ref, o_ref, rows_scr, sem):
    i = pl.program_id(0)
    cur = i % 2

    def copy(buf, which, r, src_row, rows=1):
        return pltpu.make_async_copy(yb_ref.at[pl.ds(src_row, rows)],
                                     rows_scr.at[buf, which, pl.ds(r, rows)], sem.at[buf])

    def start_gather(step, buf):
        base = step * TOK_TILE

        def issue(g, c):
            for k in range(SUBLANES):
                r = g * SUBLANES + k
                copy(buf, 0, r, d0_ref[base + r]).start()
                copy(buf, 1, r, d1_ref[base + r]).start()
            return c

        lax.fori_loop(0, TOK_TILE // SUBLANES, issue, 0)

    @pl.when(i == 0)
    def _():
        start_gather(0, 0)

    @pl.when(i + 1 < pl.num_programs(0))
    def _():
        start_gather(i + 1, 1 - cur)

    copy(cur, 0, 0, 0, TOK_TILE).wait()
    copy(cur, 1, 0, 0, TOK_TILE).wait()

    info = info_ref[...]
    half = o_ref.shape[1] // 2
    y0_hi, y0_lo = _unpack_bf16_pairs(rows_scr[cur, 0])
    y1_hi, y1_lo = _unpack_bf16_pairs(rows_scr[cur, 1])
    g2 = g2_ref[0]
    o_ref[:, :half] = x_ref[:, :half] + g2[:, :half] * (info[:, 2:3] * y0_hi + info[:, 3:4] * y1_hi)
    o_ref[:, half:] = x_ref[:, half:] + g2[:, half:] * (info[:, 2:3] * y0_lo + info[:, 3:4] * y1_lo)


def _combine(d0, d1, x1, info, g2, yb, seq):
    n, d = x1.shape
    tpb = seq // TOK_TILE
    return pl.pallas_call(
        _combine_kernel,
        grid_spec=pltpu.PrefetchScalarGridSpec(
            num_scalar_prefetch=2,
            grid=(n // TOK_TILE,),
            in_specs=[pl.BlockSpec((TOK_TILE, d), lambda i, a, b: (i, 0)),
                      pl.BlockSpec((TOK_TILE, LANES), lambda i, a, b: (i, 0)),
                      pl.BlockSpec((1, 1, d), lambda i, a, b: (i // tpb, 0, 0)),
                      pl.BlockSpec(memory_space=pl.ANY)],
            out_specs=pl.BlockSpec((TOK_TILE, d), lambda i, a, b: (i, 0)),
            scratch_shapes=[pltpu.VMEM((2, 2, TOK_TILE, d // 2), I32),
                            pltpu.SemaphoreType.DMA((2,))]),
        out_shape=jax.ShapeDtypeStruct((n, d), F32),
        compiler_params=_cparams(("arbitrary",)),
        name="moe_combine",
    )(d0, d1, x1, info, g2, yb)


def _tri_inclusive_rev(k):
    l = (np.arange(k)[:, None] >= np.arange(k)[None, :]).astype(np.float32)
    half = np.concatenate([l, np.ones((k, k), np.float32)], axis=1)
    return jnp.asarray(np.concatenate([half, half], axis=0), dtype=BF16)


def _tri_strict_lower(k):
    return jnp.asarray((np.arange(k)[None, :] < np.arange(k)[:, None]).astype(np.float32), dtype=BF16)


def kernel(x, c, w_mod, b_mod, ln1_g, w_in, q_norm_g, w_q_up, q_gain, k_gain, rel_bias, gn_a, gn_b,
           w_out, ln2_g, router_g, router_e, w_gate, w_up, w_down):
    batch, seq, d = x.shape
    n = batch * seq
    assert w_mod.shape[0] == 1 and d == D_A + D_B
    assert seq % KB_A == 0 and KB_A % QB_A == 0 and seq % TOK_TILE == 0 and n % 512 == 0
    x2 = x.reshape(n, d)

    c8 = jnp.pad(c, ((0, 8 - batch), (0, 0)))
    mod = _modulation(c8, w_mod.reshape(d, -1), b_mod.reshape(1, -1))[:batch]
    sh1, sc1, g1, sh2, sc2, g2 = [m.reshape(batch, 1, d) for m in jnp.split(mod, 6, axis=-1)]

    wi = w_in.reshape(d, -1).astype(BF16)
    n_a = Q_RANK + 2 * HEAD_DIM + IDX_DIM + IDX_HEADS
    wi = jnp.concatenate([wi[:, n_a:], wi[:, :n_a],
                          jnp.zeros((d, D_IN_PAD - wi.shape[1]), wi.dtype)], axis=1)
    proj = _ln_proj(x2, ln1_g.reshape(1, d), sc1, sh1, wi, seq)

    qa, qidx, ka, va, kidx, widx = _aprep(
        proj, n, q_norm_g.reshape(1, -1), w_q_up.reshape(Q_RANK, -1).astype(BF16),
        q_gain.reshape(1, -1), k_gain.reshape(1, -1))

    bt = _bias_tiles(rel_bias)
    bound = _logit_bound(q_gain.reshape(1, -1), k_gain.reshape(1, -1), rel_bias)[0, :1]
    oa = _attn_a(qidx, widx.T, qa, kidx, ka, va, bt, gn_a.reshape(1, -1), bound, batch, seq)
    ob = _attn_b(proj, _tri_inclusive_rev(QB_B), batch, seq)

    rw = jnp.concatenate([router_e.reshape(d, -1), router_g.reshape(d, -1),
                          jnp.zeros((d, LANES - N_EXPERTS - N_GROUPS), F32)], axis=1)
    x1, h2, lg = _out_proj(x2, oa, ob, gn_b.reshape(1, -1), w_out.reshape(d, d).astype(BF16),
                           g1, ln2_g.reshape(1, d), sc2, sh2, rw, seq)

    info, cnt = _route(lg)
    counts = cnt[0, :N_EXPERTS].astype(I32)
    p_rows = 2 * n + N_EXPERTS * MOE_BLK
    ps, be, nu = _plan(counts, p_rows // MOE_BLK)
    ps_lanes = jnp.pad(ps.astype(F32), (0, LANES - N_EXPERTS)).reshape(1, LANES)
    dinfo = _dest(info, ps_lanes, _tri_strict_lower(512))
    d0 = dinfo[:, 0].astype(I32)
    d1 = dinfo[:, 1].astype(I32)
    xb = _scatter_rows(d0, d1, counts, ps, nu, h2, p_rows)
    yb = _experts(be, nu, xb,
                  w_gate.reshape(N_EXPERTS, d, D_EXPERT),
                  w_up.reshape(N_EXPERTS, d, D_EXPERT),
                  w_down.reshape(N_EXPERTS, D_EXPERT, d))
    out = _combine(d0, d1, x1, info, g2, yb, seq)
    return out.reshape(batch, seq, d)
```
